```python
import jax, jax.numpy as jnp
from jax import lax
import numpy as np

D_MODEL = 1024
BATCH = 8
SEQ = 8192
DEPTH = 4

N_MIXERS = 4
N_META = 16
Q_BLOCK = 128
EPS = 1e-6
POOL_WINDOWS = (2, 4, 8, 16)
N_POOL_GROUPS = len(POOL_WINDOWS)
POOL_GROUP = D_MODEL // N_POOL_GROUPS
N_HEADS = 16
HEAD_DIM = D_MODEL // N_HEADS
MLA_HEADS = 16
MLA_Q_RANK = 384
MLA_KV_RANK = 256
MLA_NOPE = 64
MLA_ROPE = 32
MLA_V = 64
ROPE_THETA = 10000.0
D_FF = ((-(-8 * D_MODEL // 3) + 255) // 256) * 256

kernel_name = "hybrid_pool_sb_mla_fox_trunk"


def _n_layers_of(m):
    return len(range(m, DEPTH, N_MIXERS))


def rmsnorm(x, g):
    xf = x.astype(jnp.float32)
    y = xf * lax.rsqrt(jnp.mean(xf * xf, axis=-1, keepdims=True) + EPS)
    return (y * g.astype(jnp.float32)).astype(x.dtype)


def swiglu(h, w_gate, w_up, w_down):
    return (jax.nn.silu(h @ w_gate) * (h @ w_up)) @ w_down


def sweep_queries(attend, q_parts, kv_parts):
    L = q_parts[0].shape[1]
    pos = jnp.arange(L)
    meta_out = attend(tuple(a[:, :N_META] for a in q_parts), pos[:N_META],
                      tuple(a[:, :N_META] for a in kv_parts), pos[:N_META])
    n_blk = (L - N_META) // Q_BLOCK

    def body(i):
        start = N_META + i * Q_BLOCK
        qs = tuple(lax.dynamic_slice_in_dim(a, start, Q_BLOCK, axis=1) for a in q_parts)
        return attend(qs, start + jnp.arange(Q_BLOCK), kv_parts, pos)

    out = lax.map(body, jnp.arange(n_blk))
    B = out.shape[1]
    out = jnp.moveaxis(out, 0, 1).reshape((B, n_blk * Q_BLOCK) + out.shape[3:])
    return jnp.concatenate([meta_out, out], axis=1)


def softmax_block(q, k, v, qpos, kpos, scale, q_decay=None, k_decay=None):
    s = jnp.einsum('bqhd,bkhd->bhqk', q, k).astype(jnp.float32) * scale
    if q_decay is not None:
        s = s + (jnp.transpose(q_decay, (0, 2, 1))[:, :, :, None]
                 - jnp.transpose(k_decay, (0, 2, 1))[:, :, None, :]).astype(jnp.float32)
    mask = kpos[None, :] <= qpos[:, None]
    s = jnp.where(mask, s, jnp.finfo(jnp.float32).min)
    p = jax.nn.softmax(s, axis=-1)
    return jnp.einsum('bhqk,bkhd->bqhd', p.astype(v.dtype), v)


def pool_mixer(h, w, scale):
    B, L, _ = h.shape
    hf = h.astype(jnp.float32)
    pos = jnp.arange(L)
    outs = []
    for g, win in enumerate(POOL_WINDOWS):
        xg = hf[..., g * POOL_GROUP:(g + 1) * POOL_GROUP]
        cs = jnp.cumsum(xg, axis=1)
        lag = jnp.pad(cs[:, :-win], ((0, 0), (win, 0), (0, 0)))
        cnt = jnp.minimum(pos + 1, win).astype(jnp.float32)[None, :, None]
        outs.append((cs - lag) / cnt - xg)
    pooled = jnp.stack(outs, axis=2).astype(h.dtype)
    mixed = jnp.einsum('blgc,gcd->blgd', pooled, w).reshape(B, L, D_MODEL)
    return mixed * scale


def _sb_attend(qs, qpos, kvs, kpos):
    (q,) = qs
    k, v = kvs
    z = jnp.einsum('bqhd,bkhd->bhqk', q, k).astype(jnp.float32) * (HEAD_DIM ** -0.5)
    mask = kpos[None, :] < qpos[:, None]
    log_keep = jnp.where(mask, jax.nn.log_sigmoid(-z), 0.0)
    later = lax.cumsum(log_keep, axis=3, reverse=True) - log_keep
    a = jnp.where(mask, jnp.exp(jax.nn.log_sigmoid(z) + later), 0.0)
    return jnp.einsum('bhqk,bkhd->bqhd', a.astype(v.dtype), v)


def sb_mixer(h, w_qkv, w_o):
    B, L, _ = h.shape
    qkv = (h @ w_qkv).reshape(B, L, 3, N_HEADS, HEAD_DIM)
    q, k, v = qkv[:, :, 0], qkv[:, :, 1], qkv[:, :, 2]
    o = sweep_queries(_sb_attend, (q,), (k, v))
    return o.reshape(B, L, N_HEADS * HEAD_DIM) @ w_o


def _rope(x, cos, sin):
    xf = x.astype(jnp.float32)
    half = xf.shape[-1] // 2
    x1, x2 = xf[..., :half], xf[..., half:]
    return jnp.concatenate([x1 * cos - x2 * sin, x2 * cos + x1 * sin], axis=-1).astype(x.dtype)


def _mla_attend(qs, qpos, kvs, kpos):
    (q,) = qs
    k, v = kvs
    return softmax_block(q, k, v, qpos, kpos, (MLA_NOPE + MLA_ROPE) ** -0.5)


def mla_mixer(h, w_down, q_norm, kv_norm, w_uq, w_ukv, w_o):
    B, L, _ = h.shape
    down = h @ w_down
    c_q = rmsnorm(down[..., :MLA_Q_RANK], q_norm)
    c_kv = rmsnorm(down[..., MLA_Q_RANK:MLA_Q_RANK + MLA_KV_RANK], kv_norm)
    k_rope = down[..., MLA_Q_RANK + MLA_KV_RANK:]
    q = (c_q @ w_uq).reshape(B, L, MLA_HEADS, MLA_NOPE + MLA_ROPE)
    kv = (c_kv @ w_ukv).reshape(B, L, MLA_HEADS, MLA_NOPE + MLA_V)
    q_nope, q_rope = q[..., :MLA_NOPE], q[..., MLA_NOPE:]
    k_nope, v = kv[..., :MLA_NOPE], kv[..., MLA_NOPE:]
    inv = ROPE_THETA ** (-jnp.arange(0, MLA_ROPE, 2, dtype=jnp.float32) / MLA_ROPE)
    ang = jnp.arange(L, dtype=jnp.float32)[:, None] * inv[None, :]
    cos, sin = jnp.cos(ang), jnp.sin(ang)
    q_rope = _rope(q_rope, cos[:, None, :], sin[:, None, :])
    k_rope = _rope(k_rope, cos, sin)
    q = jnp.concatenate([q_nope, q_rope], axis=-1)
    k = jnp.concatenate([k_nope, jnp.broadcast_to(k_rope[:, :, None, :], (B, L, MLA_HEADS, MLA_ROPE))], axis=-1)
    o = sweep_queries(_mla_attend, (q,), (k, v))
    return o.reshape(B, L, MLA_HEADS * MLA_V) @ w_o


def _fox_attend(qs, qpos, kvs, kpos):
    q, fq = qs
    k, v, fk = kvs
    return softmax_block(q, k, v, qpos, kpos, HEAD_DIM ** -0.5, fq, fk)


def fox_mixer(h, w_qkvf, b_f, w_o):
    B, L, _ = h.shape
    proj = h @ w_qkvf
    qkv = proj[..., :3 * N_HEADS * HEAD_DIM].reshape(B, L, 3, N_HEADS, HEAD_DIM)
    q, k, v = qkv[:, :, 0], qkv[:, :, 1], qkv[:, :, 2]
    f_logit = proj[..., 3 * N_HEADS * HEAD_DIM:].astype(jnp.float32) + b_f.astype(jnp.float32)
    F = jnp.cumsum(jax.nn.log_sigmoid(f_logit), axis=1)
    o = sweep_queries(_fox_attend, (q, F), (k, v, F))
    return o.reshape(B, L, N_HEADS * HEAD_DIM) @ w_o


def _fwd_setup_inputs(seed: int = 0) -> dict:
    key = jax.random.key(seed)
    ks = jax.random.split(key, 24)
    f32 = jnp.float32

    def w(k, shape, fan_in):
        return jax.random.normal(k, shape, f32) * (fan_in ** -0.5)

    def gain(k, shape):
        return 1.0 + 0.02 * jax.random.normal(k, shape, f32)

    nA, nB, nC, nD = (_n_layers_of(m) for m in range(N_MIXERS))
    D = D_MODEL
    return {
        "x": jax.random.normal(ks[0], (BATCH, SEQ, D), f32),
        "meta": jax.random.normal(ks[1], (N_META, D), f32),
        "norm_mix": gain(ks[2], (DEPTH, D)),
        "norm_ffn": gain(ks[3], (DEPTH, D)),
        "pool_w": w(ks[4], (nA, N_POOL_GROUPS, POOL_GROUP, POOL_GROUP), POOL_GROUP),
        "pool_scale": gain(ks[5], (nA, D)),
        "sb_w_qkv": w(ks[6], (nB, D, 3 * N_HEADS * HEAD_DIM), D),
        "sb_w_o": w(ks[7], (nB, N_HEADS * HEAD_DIM, D), N_HEADS * HEAD_DIM),
        "mla_w_down": w(ks[8], (nC, D, MLA_Q_RANK + MLA_KV_RANK + MLA_ROPE), D),
        "mla_q_norm": gain(ks[9], (nC, MLA_Q_RANK)),
        "mla_kv_norm": gain(ks[10], (nC, MLA_KV_RANK)),
        "mla_w_uq": w(ks[11], (nC, MLA_Q_RANK, MLA_HEADS * (MLA_NOPE + MLA_ROPE)), MLA_Q_RANK),
        "mla_w_ukv": w(ks[12], (nC, MLA_KV_RANK, MLA_HEADS * (MLA_NOPE + MLA_V)), MLA_KV_RANK),
        "mla_w_o": w(ks[13], (nC, MLA_HEADS * MLA_V, D), MLA_HEADS * MLA_V),
        "fox_w_qkvf": w(ks[14], (nD, D, 3 * N_HEADS * HEAD_DIM + N_HEADS), D),
        "fox_b_f": 2.0 + 0.5 * jax.random.normal(ks[15], (nD, N_HEADS), f32),
        "fox_w_o": w(ks[16], (nD, N_HEADS * HEAD_DIM, D), N_HEADS * HEAD_DIM),
        "ffn_w_gate": w(ks[17], (DEPTH, D, D_FF), D),
        "ffn_w_up": w(ks[18], (DEPTH, D, D_FF), D),
        "ffn_w_down": w(ks[19], (DEPTH, D_FF, D), D_FF),
        "final_norm": gain(ks[20], (D,)),
    }


def _fwd_reference(x, meta, norm_mix, norm_ffn, pool_w, pool_scale, sb_w_qkv, sb_w_o,
              mla_w_down, mla_q_norm, mla_kv_norm, mla_w_uq, mla_w_ukv, mla_w_o,
              fox_w_qkvf, fox_b_f, fox_w_o, ffn_w_gate, ffn_w_up, ffn_w_down, final_norm):
    B = x.shape[0]
    meta_b = jnp.broadcast_to(meta[None].astype(x.dtype), (B, N_META, D_MODEL))
    h = jnp.concatenate([meta_b, x], axis=1)
    for i in range(DEPTH):
        m, j = i % N_MIXERS, i // N_MIXERS
        a = rmsnorm(h, norm_mix[i])
        if m == 0:
            mix = pool_mixer(a, pool_w[j], pool_scale[j])
        elif m == 1:
            mix = sb_mixer(a, sb_w_qkv[j], sb_w_o[j])
        elif m == 2:
            mix = mla_mixer(a, mla_w_down[j], mla_q_norm[j], mla_kv_norm[j],
                            mla_w_uq[j], mla_w_ukv[j], mla_w_o[j])
        else:
            mix = fox_mixer(a, fox_w_qkvf[j], fox_b_f[j], fox_w_o[j])
        h = h + mix
        h = h + swiglu(rmsnorm(h, norm_ffn[i]), ffn_w_gate[i], ffn_w_up[i], ffn_w_down[i])
    h = rmsnorm(h, final_norm)
    return h[:, N_META:]


import jax as _jax
import jax.numpy as _jnp

TWIN_FORMAT = 'train_step'
FWD_PARAMS = ['x', 'meta', 'norm_mix', 'norm_ffn', 'pool_w', 'pool_scale', 'sb_w_qkv', 'sb_w_o', 'mla_w_down', 'mla_q_norm', 'mla_kv_norm', 'mla_w_uq', 'mla_w_ukv', 'mla_w_o', 'fox_w_qkvf', 'fox_b_f', 'fox_w_o', 'ffn_w_gate', 'ffn_w_up', 'ffn_w_down', 'final_norm']
TWIN_WEIGHTS = ['meta', 'norm_mix', 'norm_ffn', 'pool_w', 'pool_scale', 'sb_w_qkv', 'sb_w_o', 'mla_w_down', 'mla_q_norm', 'mla_kv_norm', 'mla_w_uq', 'mla_w_ukv', 'mla_w_o', 'fox_w_qkvf', 'fox_b_f', 'fox_w_o', 'ffn_w_gate', 'ffn_w_up', 'ffn_w_down', 'final_norm']
TWIN_DIFF_INPUT = 'x'
TWIN_INPUTS = ['x', 'meta', 'norm_mix', 'norm_ffn', 'pool_w', 'pool_scale', 'sb_w_qkv', 'sb_w_o', 'mla_w_down', 'mla_q_norm', 'mla_kv_norm', 'mla_w_uq', 'mla_w_ukv', 'mla_w_o', 'fox_w_qkvf', 'fox_b_f', 'fox_w_o', 'ffn_w_gate', 'ffn_w_up', 'ffn_w_down', 'final_norm', 'loss_target', 'm_meta', 'm_norm_mix', 'm_norm_ffn', 'm_pool_w', 'm_pool_scale', 'm_sb_w_qkv', 'm_sb_w_o', 'm_mla_w_down', 'm_mla_q_norm', 'm_mla_kv_norm', 'm_mla_w_uq', 'm_mla_w_ukv', 'm_mla_w_o', 'm_fox_w_qkvf', 'm_fox_b_f', 'm_fox_w_o', 'm_ffn_w_gate', 'm_ffn_w_up', 'm_ffn_w_down', 'm_final_norm', 'v_meta', 'v_norm_mix', 'v_norm_ffn', 'v_pool_w', 'v_pool_scale', 'v_sb_w_qkv', 'v_sb_w_o', 'v_mla_w_down', 'v_mla_q_norm', 'v_mla_kv_norm', 'v_mla_w_uq', 'v_mla_w_ukv', 'v_mla_w_o', 'v_fox_w_qkvf', 'v_fox_b_f', 'v_fox_w_o', 'v_ffn_w_gate', 'v_ffn_w_up', 'v_ffn_w_down', 'v_final_norm']
TWIN_OUTPUTS = ['loss', 'grad_x', 'grad_meta', 'grad_norm_mix', 'grad_norm_ffn', 'grad_pool_w', 'grad_pool_scale', 'grad_sb_w_qkv', 'grad_sb_w_o', 'grad_mla_w_down', 'grad_mla_q_norm', 'grad_mla_kv_norm', 'grad_mla_w_uq', 'grad_mla_w_ukv', 'grad_mla_w_o', 'grad_fox_w_qkvf', 'grad_fox_b_f', 'grad_fox_w_o', 'grad_ffn_w_gate', 'grad_ffn_w_up', 'grad_ffn_w_down', 'grad_final_norm', 'delta_meta', 'delta_norm_mix', 'delta_norm_ffn', 'delta_pool_w', 'delta_pool_scale', 'delta_sb_w_qkv', 'delta_sb_w_o', 'delta_mla_w_down', 'delta_mla_q_norm', 'delta_mla_kv_norm', 'delta_mla_w_uq', 'delta_mla_w_ukv', 'delta_mla_w_o', 'delta_fox_w_qkvf', 'delta_fox_b_f', 'delta_fox_w_o', 'delta_ffn_w_gate', 'delta_ffn_w_up', 'delta_ffn_w_down', 'delta_final_norm', 'new_m_meta', 'new_m_norm_mix', 'new_m_norm_ffn', 'new_m_pool_w', 'new_m_pool_scale', 'new_m_sb_w_qkv', 'new_m_sb_w_o', 'new_m_mla_w_down', 'new_m_mla_q_norm', 'new_m_mla_kv_norm', 'new_m_mla_w_uq', 'new_m_mla_w_ukv', 'new_m_mla_w_o', 'new_m_fox_w_qkvf', 'new_m_fox_b_f', 'new_m_fox_w_o', 'new_m_ffn_w_gate', 'new_m_ffn_w_up', 'new_m_ffn_w_down', 'new_m_final_norm', 'new_v_meta', 'new_v_norm_mix', 'new_v_norm_ffn', 'new_v_pool_w', 'new_v_pool_scale', 'new_v_sb_w_qkv', 'new_v_sb_w_o', 'new_v_mla_w_down', 'new_v_mla_q_norm', 'new_v_mla_kv_norm', 'new_v_mla_w_uq', 'new_v_mla_w_ukv', 'new_v_mla_w_o', 'new_v_fox_w_qkvf', 'new_v_fox_b_f', 'new_v_fox_w_o', 'new_v_ffn_w_gate', 'new_v_ffn_w_up', 'new_v_ffn_w_down', 'new_v_final_norm']
TWIN_LEAF_KINDS = {'loss': 'loss', 'grad_x': 'grad_x', 'grad_meta': 'grad_w', 'grad_norm_mix': 'grad_w', 'grad_norm_ffn': 'grad_w', 'grad_pool_w': 'grad_w', 'grad_pool_scale': 'grad_w', 'grad_sb_w_qkv': 'grad_w', 'grad_sb_w_o': 'grad_w', 'grad_mla_w_down': 'grad_w', 'grad_mla_q_norm': 'grad_w', 'grad_mla_kv_norm': 'grad_w', 'grad_mla_w_uq': 'grad_w', 'grad_mla_w_ukv': 'grad_w', 'grad_mla_w_o': 'grad_w', 'grad_fox_w_qkvf': 'grad_w', 'grad_fox_b_f': 'grad_w', 'grad_fox_w_o': 'grad_w', 'grad_ffn_w_gate': 'grad_w', 'grad_ffn_w_up': 'grad_w', 'grad_ffn_w_down': 'grad_w', 'grad_final_norm': 'grad_w', 'delta_meta': 'delta_w', 'delta_norm_mix': 'delta_w', 'delta_norm_ffn': 'delta_w', 'delta_pool_w': 'delta_w', 'delta_pool_scale': 'delta_w', 'delta_sb_w_qkv': 'delta_w', 'delta_sb_w_o': 'delta_w', 'delta_mla_w_down': 'delta_w', 'delta_mla_q_norm': 'delta_w', 'delta_mla_kv_norm': 'delta_w', 'delta_mla_w_uq': 'delta_w', 'delta_mla_w_ukv': 'delta_w', 'delta_mla_w_o': 'delta_w', 'delta_fox_w_qkvf': 'delta_w', 'delta_fox_b_f': 'delta_w', 'delta_fox_w_o': 'delta_w', 'delta_ffn_w_gate': 'delta_w', 'delta_ffn_w_up': 'delta_w', 'delta_ffn_w_down': 'delta_w', 'delta_final_norm': 'delta_w', 'new_m_meta': 'new_m', 'new_m_norm_mix': 'new_m', 'new_m_norm_ffn': 'new_m', 'new_m_pool_w': 'new_m', 'new_m_pool_scale': 'new_m', 'new_m_sb_w_qkv': 'new_m', 'new_m_sb_w_o': 'new_m', 'new_m_mla_w_down': 'new_m', 'new_m_mla_q_norm': 'new_m', 'new_m_mla_kv_norm': 'new_m', 'new_m_mla_w_uq': 'new_m', 'new_m_mla_w_ukv': 'new_m', 'new_m_mla_w_o': 'new_m', 'new_m_fox_w_qkvf': 'new_m', 'new_m_fox_b_f': 'new_m', 'new_m_fox_w_o': 'new_m', 'new_m_ffn_w_gate': 'new_m', 'new_m_ffn_w_up': 'new_m', 'new_m_ffn_w_down': 'new_m', 'new_m_final_norm': 'new_m', 'new_v_meta': 'new_v', 'new_v_norm_mix': 'new_v', 'new_v_norm_ffn': 'new_v', 'new_v_pool_w': 'new_v', 'new_v_pool_scale': 'new_v', 'new_v_sb_w_qkv': 'new_v', 'new_v_sb_w_o': 'new_v', 'new_v_mla_w_down': 'new_v', 'new_v_mla_q_norm': 'new_v', 'new_v_mla_kv_norm': 'new_v', 'new_v_mla_w_uq': 'new_v', 'new_v_mla_w_ukv': 'new_v', 'new_v_mla_w_o': 'new_v', 'new_v_fox_w_qkvf': 'new_v', 'new_v_fox_b_f': 'new_v', 'new_v_fox_w_o': 'new_v', 'new_v_ffn_w_gate': 'new_v', 'new_v_ffn_w_up': 'new_v', 'new_v_ffn_w_down': 'new_v', 'new_v_final_norm': 'new_v'}


def _forward(args):
    return _fwd_reference(*[args[k] for k in FWD_PARAMS])


def _output_shape():
    def fwd():
        inp = _fwd_setup_inputs(0)
        return _fwd_reference(*[inp[k] for k in FWD_PARAMS])
    out = _jax.eval_shape(fwd)
    return out.shape, out.dtype

N_MICROBATCH = 1
ADAM_LR = 0.001
ADAM_B1 = 0.9
ADAM_B2 = 0.999
ADAM_EPS = 1e-08
ADAM_WD = 0.01
ADAM_STEP = 10
PER_EXAMPLE_BATCH_AXIS = {'x': 0, 'loss_target': 0}
SHARED_INPUTS = []
_WEIGHT_DTYPES = {'meta': _jnp.float32, 'norm_mix': _jnp.float32, 'norm_ffn': _jnp.float32, 'pool_w': _jnp.float32, 'pool_scale': _jnp.float32, 'sb_w_qkv': _jnp.float32, 'sb_w_o': _jnp.float32, 'mla_w_down': _jnp.float32, 'mla_q_norm': _jnp.float32, 'mla_kv_norm': _jnp.float32, 'mla_w_uq': _jnp.float32, 'mla_w_ukv': _jnp.float32, 'mla_w_o': _jnp.float32, 'fox_w_qkvf': _jnp.float32, 'fox_b_f': _jnp.float32, 'fox_w_o': _jnp.float32, 'ffn_w_gate': _jnp.float32, 'ffn_w_up': _jnp.float32, 'ffn_w_down': _jnp.float32, 'final_norm': _jnp.float32}
MOMENT_SCALE = {'meta': 7.770540e-03, 'norm_mix': 1.445248e-01, 'norm_ffn': 1.474060e-01, 'pool_w': 2.258723e-01, 'pool_scale': 3.586330e-01, 'sb_w_qkv': 8.551800e-02, 'sb_w_o': 1.222888e-01, 'mla_w_down': 5.828667e-02, 'mla_q_norm': 4.379040e-02, 'mla_kv_norm': 7.773925e-02, 'mla_w_uq': 2.226113e-02, 'mla_w_ukv': 2.642581e-02, 'mla_w_o': 2.932386e-02, 'fox_w_qkvf': 4.875269e-02, 'fox_b_f': 2.221855e-01, 'fox_w_o': 5.613110e-02, 'ffn_w_gate': 6.323362e-02, 'ffn_w_up': 6.120534e-02, 'ffn_w_down': 1.013922e-01, 'final_norm': 6.417353e+01}


def _to_microbatches(a, axis):
    t = _jnp.moveaxis(a, axis, 0)
    t = t.reshape((N_MICROBATCH, t.shape[0] // N_MICROBATCH) + t.shape[1:])
    return _jnp.moveaxis(t, 1, axis + 1)


def setup_inputs(seed: int = 0) -> dict:
    inp = _fwd_setup_inputs(seed)
    key = _jax.random.fold_in(_jax.random.key(seed), 7919)
    shape, _ = _output_shape()
    out = dict(inp)
    out["loss_target"] = _jax.random.normal(_jax.random.fold_in(key, 0), shape, _jnp.float32)
    for i, name in enumerate(TWIN_WEIGHTS):
        w = inp[name].astype(_jnp.float32)
        if MOMENT_SCALE is None:
            s = _jnp.sqrt(_jnp.mean(_jnp.square(w)) + 1e-30)
        else:
            s = MOMENT_SCALE[name]
        km, kv = _jax.random.split(_jax.random.fold_in(key, i + 1))
        out[name] = w
        out["m_" + name] = s * _jax.random.normal(km, w.shape, _jnp.float32)
        out["v_" + name] = (s * s) * _jax.random.uniform(kv, w.shape, _jnp.float32, 0.5, 1.5)
    if N_MICROBATCH > 1:
        for name, axis in PER_EXAMPLE_BATCH_AXIS.items():
            out[name] = _to_microbatches(out[name], axis)
    return {'x': out['x'], 'meta': out['meta'], 'norm_mix': out['norm_mix'], 'norm_ffn': out['norm_ffn'], 'pool_w': out['pool_w'], 'pool_scale': out['pool_scale'], 'sb_w_qkv': out['sb_w_qkv'], 'sb_w_o': out['sb_w_o'], 'mla_w_down': out['mla_w_down'], 'mla_q_norm': out['mla_q_norm'], 'mla_kv_norm': out['mla_kv_norm'], 'mla_w_uq': out['mla_w_uq'], 'mla_w_ukv': out['mla_w_ukv'], 'mla_w_o': out['mla_w_o'], 'fox_w_qkvf': out['fox_w_qkvf'], 'fox_b_f': out['fox_b_f'], 'fox_w_o': out['fox_w_o'], 'ffn_w_gate': out['ffn_w_gate'], 'ffn_w_up': out['ffn_w_up'], 'ffn_w_down': out['ffn_w_down'], 'final_norm': out['final_norm'], 'loss_target': out['loss_target'], 'm_meta': out['m_meta'], 'm_norm_mix': out['m_norm_mix'], 'm_norm_ffn': out['m_norm_ffn'], 'm_pool_w': out['m_pool_w'], 'm_pool_scale': out['m_pool_scale'], 'm_sb_w_qkv': out['m_sb_w_qkv'], 'm_sb_w_o': out['m_sb_w_o'], 'm_mla_w_down': out['m_mla_w_down'], 'm_mla_q_norm': out['m_mla_q_norm'], 'm_mla_kv_norm': out['m_mla_kv_norm'], 'm_mla_w_uq': out['m_mla_w_uq'], 'm_mla_w_ukv': out['m_mla_w_ukv'], 'm_mla_w_o': out['m_mla_w_o'], 'm_fox_w_qkvf': out['m_fox_w_qkvf'], 'm_fox_b_f': out['m_fox_b_f'], 'm_fox_w_o': out['m_fox_w_o'], 'm_ffn_w_gate': out['m_ffn_w_gate'], 'm_ffn_w_up': out['m_ffn_w_up'], 'm_ffn_w_down': out['m_ffn_w_down'], 'm_final_norm': out['m_final_norm'], 'v_meta': out['v_meta'], 'v_norm_mix': out['v_norm_mix'], 'v_norm_ffn': out['v_norm_ffn'], 'v_pool_w': out['v_pool_w'], 'v_pool_scale': out['v_pool_scale'], 'v_sb_w_qkv': out['v_sb_w_qkv'], 'v_sb_w_o': out['v_sb_w_o'], 'v_mla_w_down': out['v_mla_w_down'], 'v_mla_q_norm': out['v_mla_q_norm'], 'v_mla_kv_norm': out['v_mla_kv_norm'], 'v_mla_w_uq': out['v_mla_w_uq'], 'v_mla_w_ukv': out['v_mla_w_ukv'], 'v_mla_w_o': out['v_mla_w_o'], 'v_fox_w_qkvf': out['v_fox_w_qkvf'], 'v_fox_b_f': out['v_fox_b_f'], 'v_fox_w_o': out['v_fox_w_o'], 'v_ffn_w_gate': out['v_ffn_w_gate'], 'v_ffn_w_up': out['v_ffn_w_up'], 'v_ffn_w_down': out['v_ffn_w_down'], 'v_final_norm': out['v_final_norm']}


def _loss(weights, diff, rest, loss_target):
    with _jax.named_scope("forward"):
        args = {**rest, TWIN_DIFF_INPUT: diff, **{k: w.astype(_WEIGHT_DTYPES[k]) for k, w in weights.items()}}
        y = _forward(args)
    with _jax.named_scope("loss_head"):
        err = _jnp.square(y.astype(_jnp.float32) - loss_target)
        return 0.5 * _jnp.sum(_jnp.mean(err, axis=-1)) if err.ndim else 0.5 * err


def _adamw(w, g, m, v):
    m = ADAM_B1 * m + (1.0 - ADAM_B1) * g
    v = ADAM_B2 * v + (1.0 - ADAM_B2) * _jnp.square(g)
    m_hat = m / (1.0 - ADAM_B1 ** ADAM_STEP)
    v_hat = v / (1.0 - ADAM_B2 ** ADAM_STEP)
    delta = -ADAM_LR * (m_hat / (_jnp.sqrt(v_hat) + ADAM_EPS) + ADAM_WD * w)
    return delta, m, v


def reference(x, meta, norm_mix, norm_ffn, pool_w, pool_scale, sb_w_qkv, sb_w_o, mla_w_down, mla_q_norm, mla_kv_norm, mla_w_uq, mla_w_ukv, mla_w_o, fox_w_qkvf, fox_b_f, fox_w_o, ffn_w_gate, ffn_w_up, ffn_w_down, final_norm, loss_target, m_meta, m_norm_mix, m_norm_ffn, m_pool_w, m_pool_scale, m_sb_w_qkv, m_sb_w_o, m_mla_w_down, m_mla_q_norm, m_mla_kv_norm, m_mla_w_uq, m_mla_w_ukv, m_mla_w_o, m_fox_w_qkvf, m_fox_b_f, m_fox_w_o, m_ffn_w_gate, m_ffn_w_up, m_ffn_w_down, m_final_norm, v_meta, v_norm_mix, v_norm_ffn, v_pool_w, v_pool_scale, v_sb_w_qkv, v_sb_w_o, v_mla_w_down, v_mla_q_norm, v_mla_kv_norm, v_mla_w_uq, v_mla_w_ukv, v_mla_w_o, v_fox_w_qkvf, v_fox_b_f, v_fox_w_o, v_ffn_w_gate, v_ffn_w_up, v_ffn_w_down, v_final_norm):
    given = dict(x=x, meta=meta, norm_mix=norm_mix, norm_ffn=norm_ffn, pool_w=pool_w, pool_scale=pool_scale, sb_w_qkv=sb_w_qkv, sb_w_o=sb_w_o, mla_w_down=mla_w_down, mla_q_norm=mla_q_norm, mla_kv_norm=mla_kv_norm, mla_w_uq=mla_w_uq, mla_w_ukv=mla_w_ukv, mla_w_o=mla_w_o, fox_w_qkvf=fox_w_qkvf, fox_b_f=fox_b_f, fox_w_o=fox_w_o, ffn_w_gate=ffn_w_gate, ffn_w_up=ffn_w_up, ffn_w_down=ffn_w_down, final_norm=final_norm, loss_target=loss_target, m_meta=m_meta, m_norm_mix=m_norm_mix, m_norm_ffn=m_norm_ffn, m_pool_w=m_pool_w, m_pool_scale=m_pool_scale, m_sb_w_qkv=m_sb_w_qkv, m_sb_w_o=m_sb_w_o, m_mla_w_down=m_mla_w_down, m_mla_q_norm=m_mla_q_norm, m_mla_kv_norm=m_mla_kv_norm, m_mla_w_uq=m_mla_w_uq, m_mla_w_ukv=m_mla_w_ukv, m_mla_w_o=m_mla_w_o, m_fox_w_qkvf=m_fox_w_qkvf, m_fox_b_f=m_fox_b_f, m_fox_w_o=m_fox_w_o, m_ffn_w_gate=m_ffn_w_gate, m_ffn_w_up=m_ffn_w_up, m_ffn_w_down=m_ffn_w_down, m_final_norm=m_final_norm, v_meta=v_meta, v_norm_mix=v_norm_mix, v_norm_ffn=v_norm_ffn, v_pool_w=v_pool_w, v_pool_scale=v_pool_scale, v_sb_w_qkv=v_sb_w_qkv, v_sb_w_o=v_sb_w_o, v_mla_w_down=v_mla_w_down, v_mla_q_norm=v_mla_q_norm, v_mla_kv_norm=v_mla_kv_norm, v_mla_w_uq=v_mla_w_uq, v_mla_w_ukv=v_mla_w_ukv, v_mla_w_o=v_mla_w_o, v_fox_w_qkvf=v_fox_w_qkvf, v_fox_b_f=v_fox_b_f, v_fox_w_o=v_fox_w_o, v_ffn_w_gate=v_ffn_w_gate, v_ffn_w_up=v_ffn_w_up, v_ffn_w_down=v_ffn_w_down, v_final_norm=v_final_norm)
    weights = {n: given[n] for n in TWIN_WEIGHTS}
    shared = {n: given[n] for n in SHARED_INPUTS}
    per_example = {n: given[n] for n in ['x']}
    grad_fn = _jax.value_and_grad(_loss, argnums=(0, 1))

    def one_microbatch(ex, loss_target):
        ex = dict(ex)
        diff = ex.pop(TWIN_DIFF_INPUT)
        return grad_fn(weights, diff, {**shared, **ex}, loss_target)

    if N_MICROBATCH == 1:
        loss, (grad_w, grad_x) = one_microbatch(per_example, given["loss_target"])
    else:
        def body(carry, xs):
            loss_sum, grad_sum = carry
            l_k, (gw_k, gx_k) = one_microbatch(xs[0], xs[1])
            with _jax.named_scope("update"):
                return (loss_sum + l_k, _jax.tree.map(_jnp.add, grad_sum, gw_k)), gx_k

        init = (_jnp.zeros((), _jnp.float32), _jax.tree.map(_jnp.zeros_like, weights))
        (loss, grad_w), grad_x = _jax.lax.scan(body, init, (per_example, given["loss_target"]))
    with _jax.named_scope("update"):
        delta_w, new_m, new_v = {}, {}, {}
        for n in TWIN_WEIGHTS:
            delta_w[n], new_m[n], new_v[n] = _adamw(weights[n], grad_w[n], given["m_" + n], given["v_" + n])
    return (loss, grad_x, *[grad_w[n] for n in TWIN_WEIGHTS], *[delta_w[n] for n in TWIN_WEIGHTS],
            *[new_m[n] for n in TWIN_WEIGHTS], *[new_v[n] for n in TWIN_WEIGHTS])
```

```python
import functools

import jax
import jax.numpy as jnp
from jax import lax
from jax.experimental import pallas as pl
from jax.experimental.pallas import tpu as pltpu

F32 = jnp.float32
BF16 = jnp.bfloat16

D_MODEL = 1024
N_META = 16
PAD0 = 112
LANES = 128
N_HEADS = 16
HEAD_DIM = 64
POOL_WINDOWS = (2, 4, 8, 16)
POOL_GROUP = 256
POOL_HALO = 16
MLA_Q_RANK = 384
MLA_KV_RANK = 256
MLA_NOPE = 64
MLA_ROPE = 32
MLA_DOWN_PAD = 768
ROPE_THETA = 10000.0
D_FF = 2816
EPS = 1e-6
NEG = -1e30
ADAM_LR = 0.001
ADAM_B1 = 0.9
ADAM_B2 = 0.999
ADAM_EPS = 1e-08
ADAM_WD = 0.01
ADAM_STEP = 10
VMEM_LIMIT = 56 * 1024 * 1024
MESH_AXES = ("x", "y", "c")


def _cp(sem, **kw):
    return pltpu.CompilerParams(dimension_semantics=sem, vmem_limit_bytes=VMEM_LIMIT, **kw)


def _row_tile(m, target):
    best = None
    for t in range(16, min(m, target) + 1, 16):
        if m % t == 0:
            best = t
    return best or m


def _col_tile(n, target):
    best = None
    for t in range(LANES, min(n, target) + 1, LANES):
        if n % t == 0:
            best = t
    return best or n


def _mm(a, b, mode, name, out_dtype=F32, add=None, tm=640, tn=512, tk=2048):
    if mode == "nn":
        (M, K), (K2, N) = a.shape, b.shape
    elif mode == "nt":
        (M, K), (N, K2) = a.shape, b.shape
    else:
        (K, M), (K2, N) = a.shape, b.shape
    assert K == K2, (mode, a.shape, b.shape)
    if mode == "tn":
        tm_ = _col_tile(M, 1408)
        tk_ = _row_tile(K, 640)
    else:
        tm_ = _row_tile(M, tm)
        tk_ = _col_tile(K, tk) if K > tk else K
    tn_ = _col_tile(N, tn)
    nk = K // tk_
    if mode == "nn":
        a_spec = pl.BlockSpec((tm_, tk_), lambda i, j, k: (i, k))
        b_spec = pl.BlockSpec((tk_, tn_), lambda i, j, k: (k, j))
        dims = (((1,), (0,)), ((), ()))
    elif mode == "nt":
        a_spec = pl.BlockSpec((tm_, tk_), lambda i, j, k: (i, k))
        b_spec = pl.BlockSpec((tn_, tk_), lambda i, j, k: (j, k))
        dims = (((1,), (1,)), ((), ()))
    else:
        a_spec = pl.BlockSpec((tk_, tm_), lambda i, j, k: (k, i))
        b_spec = pl.BlockSpec((tk_, tn_), lambda i, j, k: (k, j))
        dims = (((0,), (0,)), ((), ()))
    o_spec = pl.BlockSpec((tm_, tn_), lambda i, j, k: (i, j))
    has_add = add is not None

    def body(*refs):
        if has_add:
            a_ref, b_ref, add_ref, o_ref, acc_ref = refs
        else:
            a_ref, b_ref, o_ref, acc_ref = refs
        k = pl.program_id(2)
        part = lax.dot_general(a_ref[...].astype(BF16), b_ref[...].astype(BF16), dims, preferred_element_type=F32)

        @pl.when(k == 0)
        def _():
            acc_ref[...] = part

        @pl.when(k > 0)
        def _():
            acc_ref[...] += part

        @pl.when(k == nk - 1)
        def _():
            r = acc_ref[...]
            if has_add:
                r = r + add_ref[...]
            o_ref[...] = r.astype(o_ref.dtype)

    ins = [a, b] + ([add] if has_add else [])
    in_specs = [a_spec, b_spec] + ([o_spec] if has_add else [])
    return pl.pallas_call(
        body,
        out_shape=jax.ShapeDtypeStruct((M, N), out_dtype),
        grid=(M // tm_, N // tn_, nk),
        in_specs=in_specs,
        out_specs=o_spec,
        scratch_shapes=[pltpu.VMEM((tm_, tn_), F32)],
        name=name,
        compiler_params=_cp(("parallel", "parallel", "arbitrary")),
    )(*ins)


def _rms_fwd(x, g, out_dtype, name):
    M, C = x.shape
    tm = _row_tile(M, 640)

    def body(x_ref, g_ref, o_ref):
        xf = x_ref[...]
        r = lax.rsqrt(jnp.mean(xf * xf, axis=-1, keepdims=True) + EPS)
        o_ref[...] = ((xf * r) * g_ref[...]).astype(o_ref.dtype)

    return pl.pallas_call(
        body,
        out_shape=jax.ShapeDtypeStruct((M, C), out_dtype),
        grid=(M // tm,),
        in_specs=[pl.BlockSpec((tm, C), lambda i: (i, 0)), pl.BlockSpec((1, C), lambda i: (0, 0))],
        out_specs=pl.BlockSpec((tm, C), lambda i: (i, 0)),
        name=name,
        compiler_params=_cp(("parallel",)),
    )(x, g.reshape(1, C))


def _rms_bwd(x, g, dy, dres, name):
    M, C = x.shape
    tm = _row_tile(M, 640)
    has_res = dres is not None

    def body(*refs):
        if has_res:
            x_ref, g_ref, dy_ref, dres_ref, dx_ref, dg_ref = refs
        else:
            x_ref, g_ref, dy_ref, dx_ref, dg_ref = refs
        xf = x_ref[...]
        r = lax.rsqrt(jnp.mean(xf * xf, axis=-1, keepdims=True) + EPS)
        xhat = xf * r
        dyf = dy_ref[...].astype(F32)

        @pl.when(pl.program_id(0) == 0)
        def _():
            dg_ref[...] = jnp.zeros_like(dg_ref)

        dg_ref[...] += jnp.sum(dyf * xhat, axis=0, keepdims=True)
        dxh = dyf * g_ref[...]
        dx = r * (dxh - xhat * jnp.mean(dxh * xhat, axis=-1, keepdims=True))
        if has_res:
            dx = dx + dres_ref[...]
        dx_ref[...] = dx

    row = pl.BlockSpec((tm, C), lambda i: (i, 0))
    vec = pl.BlockSpec((1, C), lambda i: (0, 0))
    ins = [x, g.reshape(1, C), dy] + ([dres] if has_res else [])
    return pl.pallas_call(
        body,
        out_shape=(jax.ShapeDtypeStruct((M, C), F32), jax.ShapeDtypeStruct((1, C), F32)),
        grid=(M // tm,),
        in_specs=[row, vec, row] + ([row] if has_res else []),
        out_specs=(row, vec),
        name=name,
        compiler_params=_cp(("arbitrary",)),
    )(*ins)


def _sigmoid(x):
    return 1.0 / (1.0 + jnp.exp(-x))


def _swiglu_fwd(g, u, name):
    M, F = g.shape
    tm = _row_tile(M, 320)

    def body(g_ref, u_ref, o_ref):
        gv = g_ref[...]
        o_ref[...] = ((gv * _sigmoid(gv)) * u_ref[...]).astype(o_ref.dtype)

    blk = pl.BlockSpec((tm, F), lambda i: (i, 0))
    return pl.pallas_call(
        body,
        out_shape=jax.ShapeDtypeStruct((M, F), BF16),
        grid=(M // tm,),
        in_specs=[blk, blk],
        out_specs=blk,
        name=name,
        compiler_params=_cp(("parallel",)),
    )(g, u)


def _swiglu_bwd(g, u, dh, name):
    M, F = g.shape
    tm = _row_tile(M, 320)

    def body(g_ref, u_ref, dh_ref, dg_ref, du_ref):
        gv = g_ref[...]
        sg = _sigmoid(gv)
        d = dh_ref[...]
        du_ref[...] = (d * (gv * sg)).astype(du_ref.dtype)
        dg_ref[...] = ((d * u_ref[...]) * (sg * (1.0 + gv * (1.0 - sg)))).astype(dg_ref.dtype)

    blk = pl.BlockSpec((tm, F), lambda i: (i, 0))
    return pl.pallas_call(
        body,
        out_shape=(jax.ShapeDtypeStruct((M, F), BF16), jax.ShapeDtypeStruct((M, F), BF16)),
        grid=(M // tm,),
        in_specs=[blk, blk, blk],
        out_specs=(blk, blk),
        name=name,
        compiler_params=_cp(("parallel",)),
    )(g, u, dh)


def _attn_tile(lp):
    return _col_tile(lp, 640)


def _head_specs(lp, t, offs):
    q_spec = pl.BlockSpec((t, LANES), lambda h, i: (i, offs[0] + h))
    k_spec = pl.BlockSpec((lp, LANES), lambda h, i: (0, offs[1] + h))
    v_spec = pl.BlockSpec((lp, LANES), lambda h, i: (0, offs[2] + h))
    return q_spec, k_spec, v_spec


def _attn_fwd(qa, ka, va, kb, fq, *, nh, offs, scale, name, tile=640):
    lp = qa.shape[0]
    t = _col_tile(lp, tile)
    nq = lp // t
    has_fq = fq is not None

    def body(*refs):
        if has_fq:
            q_ref, k_ref, v_ref, kb_ref, fq_ref, o_ref, lse_ref = refs
        else:
            q_ref, k_ref, v_ref, kb_ref, o_ref, lse_ref = refs
        i = pl.program_id(1)
        q = q_ref[...]
        fqc = fq_ref[:, 0:1] if has_fq else None
        causal = lax.broadcasted_iota(jnp.int32, (t, t), 1) <= lax.broadcasted_iota(jnp.int32, (t, t), 0)

        def step(j, carry, masked):
            m, l, acc = carry
            st = pl.multiple_of(j * t, t)
            k = k_ref[pl.ds(st, t), :]
            v = v_ref[pl.ds(st, t), :]
            s = lax.dot_general(q, k, (((1,), (1,)), ((), ())), preferred_element_type=F32) * scale
            bias = kb_ref[j]
            if has_fq:
                bias = fqc + bias
            s = s + bias
            if masked:
                s = jnp.where(causal, s, NEG)
            m_new = jnp.maximum(m, jnp.max(s, axis=1, keepdims=True))
            p = jnp.exp(s - m_new)
            alpha = jnp.exp(m - m_new)
            l = alpha * l + jnp.sum(p, axis=1, keepdims=True)
            acc = alpha * acc + jnp.dot(p.astype(BF16), v, preferred_element_type=F32)
            return m_new, l, acc

        init = (jnp.full((t, 1), NEG, F32), jnp.zeros((t, 1), F32), jnp.zeros((t, LANES), F32))
        carry = lax.fori_loop(0, i, lambda j, c: step(j, c, False), init)
        m, l, acc = step(i, carry, True)
        valid = (i * t + lax.broadcasted_iota(jnp.int32, (t, 1), 0)) >= PAD0
        o_ref[...] = jnp.where(valid, acc / l, 0.0).astype(o_ref.dtype)
        lse_ref[...] = jnp.broadcast_to(m + jnp.log(l), (t, LANES))

    q_spec, k_spec, v_spec = _head_specs(lp, t, offs)
    kb_spec = pl.BlockSpec((None, nq, 1, t), lambda h, i: (h, 0, 0, 0))
    row_spec = pl.BlockSpec((t, LANES), lambda h, i: (i, h))
    ins = [qa, ka, va, kb] + ([fq] if has_fq else [])
    return pl.pallas_call(
        body,
        out_shape=(jax.ShapeDtypeStruct((lp, nh * LANES), BF16), jax.ShapeDtypeStruct((lp, nh * LANES), F32)),
        grid=(nh, nq),
        in_specs=[q_spec, k_spec, v_spec, kb_spec] + ([row_spec] if has_fq else []),
        out_specs=(row_spec, row_spec),
        name=name,
        compiler_params=_cp(("parallel", "arbitrary")),
    )(*ins)


def _attn_bwd(qa, ka, va, kb, fq, o, do, lse, *, nh, offs, scale, name, tile=640):
    lp = qa.shape[0]
    t = _col_tile(lp, tile)
    nq = lp // t
    has_fq = fq is not None

    def body(*refs):
        if has_fq:
            q_ref, k_ref, v_ref, kb_ref, fq_ref, o_ref, do_ref, lse_ref, dq_ref, dk_ref, dv_ref, dkb_ref, dqb_ref, dk_acc, dv_acc = refs
        else:
            q_ref, k_ref, v_ref, kb_ref, o_ref, do_ref, lse_ref, dq_ref, dk_ref, dv_ref, dk_acc, dv_acc = refs
        i = pl.program_id(1)

        @pl.when(i == 0)
        def _():
            dk_acc[...] = jnp.zeros_like(dk_acc)
            dv_acc[...] = jnp.zeros_like(dv_acc)
            if has_fq:
                dkb_ref[...] = jnp.zeros_like(dkb_ref)

        q = q_ref[...]
        dov = do_ref[...]
        delta = jnp.sum(o_ref[...].astype(F32) * dov.astype(F32), axis=1, keepdims=True)
        lse_c = lse_ref[:, 0:1]
        fqc = fq_ref[:, 0:1] if has_fq else None
        causal = lax.broadcasted_iota(jnp.int32, (t, t), 1) <= lax.broadcasted_iota(jnp.int32, (t, t), 0)

        def step(j, carry, masked):
            dq_acc, rs = carry
            st = pl.multiple_of(j * t, t)
            k = k_ref[pl.ds(st, t), :]
            v = v_ref[pl.ds(st, t), :]
            s = lax.dot_general(q, k, (((1,), (1,)), ((), ())), preferred_element_type=F32) * scale
            bias = kb_ref[j]
            if has_fq:
                bias = fqc + bias
            s = s + bias
            if masked:
                s = jnp.where(causal, s, NEG)
            p = jnp.exp(s - lse_c)
            dp = lax.dot_general(dov, v, (((1,), (1,)), ((), ())), preferred_element_type=F32)
            ds = p * (dp - delta)
            dv_acc[pl.ds(st, t), :] += lax.dot_general(p.astype(BF16), dov, (((0,), (0,)), ((), ())), preferred_element_type=F32)
            dsb = (ds * scale).astype(BF16)
            dk_acc[pl.ds(st, t), :] += lax.dot_general(dsb, q, (((0,), (0,)), ((), ())), preferred_element_type=F32)
            if has_fq:
                dkb_ref[j] += jnp.sum(ds, axis=0, keepdims=True)
                rs = rs + jnp.sum(ds, axis=1, keepdims=True)
            return dq_acc + jnp.dot(dsb, k, preferred_element_type=F32), rs

        carry = (jnp.zeros((t, LANES), F32), jnp.zeros((t, 1), F32))
        carry = lax.fori_loop(0, i, lambda j, c: step(j, c, False), carry)
        dq_acc, rs = step(i, carry, True)
        dq_ref[...] = dq_acc.astype(dq_ref.dtype)
        if has_fq:
            dqb_ref[...] = jnp.broadcast_to(rs, (t, LANES))

        @pl.when(i == nq - 1)
        def _():
            dk_ref[...] = dk_acc[...].astype(dk_ref.dtype)
            dv_ref[...] = dv_acc[...].astype(dv_ref.dtype)

    q_spec, k_spec, v_spec = _head_specs(lp, t, offs)
    kb_spec = pl.BlockSpec((None, nq, 1, t), lambda h, i: (h, 0, 0, 0))
    row_spec = pl.BlockSpec((t, LANES), lambda h, i: (i, h))
    col_spec = pl.BlockSpec((lp, LANES), lambda h, i: (0, h))
    ins = [qa, ka, va, kb] + ([fq] if has_fq else []) + [o, do, lse]
    wide = jax.ShapeDtypeStruct((lp, nh * LANES), BF16)
    extra_shapes = (jax.ShapeDtypeStruct(kb.shape, F32), jax.ShapeDtypeStruct((lp, nh * LANES), F32)) if has_fq else ()
    extra_specs = (kb_spec, row_spec) if has_fq else ()
    return pl.pallas_call(
        body,
        out_shape=(wide, wide, wide) + extra_shapes,
        grid=(nh, nq),
        in_specs=[q_spec, k_spec, v_spec, kb_spec] + ([row_spec] if has_fq else []) + [row_spec, row_spec, row_spec],
        out_specs=(row_spec, col_spec, col_spec) + extra_specs,
        scratch_shapes=[pltpu.VMEM((lp, LANES), F32), pltpu.VMEM((lp, LANES), F32)],
        name=name,
        compiler_params=_cp(("parallel", "arbitrary")),
    )(*ins)


SB_TK = 128


def _split3(x):
    hi = x.astype(BF16)
    r1 = x - hi.astype(F32)
    mid = r1.astype(BF16)
    lo = (r1 - mid.astype(F32)).astype(BF16)
    return hi, mid, lo


def _dot3(x, tri):
    hi, mid, lo = _split3(x)
    d = functools.partial(jnp.dot, preferred_element_type=F32)
    return d(hi, tri) + d(mid, tri) + d(lo, tri)


def _sb_logits(q, k, kbj, rows, cols, masked):
    z = lax.dot_general(q, k, (((1,), (1,)), ((), ())), preferred_element_type=F32) * (HEAD_DIM ** -0.5) + kbj
    if masked:
        z = jnp.where(cols < rows, z, NEG)
    e = jnp.exp(-jnp.abs(z))
    g = jnp.minimum(z, 0.0) - jnp.log(1.0 + e)
    lk = g - z
    return z, e, g, lk


def _sb_fwd(qa, kb, *, nh, name, tq=640):
    lp = qa.shape[0]
    tq = _col_tile(lp, tq)
    tk = SB_TK
    nq = lp // tq
    sub = tq // tk

    def body(q_ref, k_ref, v_ref, kb_ref, o_ref):
        i = pl.program_id(1)
        q = q_ref[...]
        rows = i * tq + lax.broadcasted_iota(jnp.int32, (tq, tk), 0)
        lane = lax.broadcasted_iota(jnp.int32, (tq, tk), 1)
        tri = (lax.broadcasted_iota(jnp.int32, (tk, tk), 0) > lax.broadcasted_iota(jnp.int32, (tk, tk), 1)).astype(BF16)

        def step(j, carry, masked):
            c, acc = carry
            st = pl.multiple_of(j * tk, tk)
            k = k_ref[pl.ds(st, tk), :]
            v = v_ref[pl.ds(st, tk), :]
            _, _, g, lk = _sb_logits(q, k, kb_ref[j], rows, j * tk + lane, masked)
            later = _dot3(lk, tri) + c
            a = jnp.exp(g + later)
            acc = acc + jnp.dot(a.astype(BF16), v, preferred_element_type=F32)
            return c + jnp.sum(lk, axis=1, keepdims=True), acc

        top = (i + 1) * sub - 1
        carry = (jnp.zeros((tq, 1), F32), jnp.zeros((tq, LANES), F32))
        carry = lax.fori_loop(0, sub, lambda r, c: step(top - r, c, True), carry)
        _, acc = lax.fori_loop(0, i * sub, lambda r, c: step(i * sub - 1 - r, c, False), carry)
        o_ref[...] = acc.astype(o_ref.dtype)

    q_spec, k_spec, v_spec = _head_specs(lp, tq, (0, nh, 2 * nh))
    kb_spec = pl.BlockSpec((None, lp // tk, 1, tk), lambda h, i: (h, 0, 0, 0))
    row_spec = pl.BlockSpec((tq, LANES), lambda h, i: (i, h))
    return pl.pallas_call(
        body,
        out_shape=jax.ShapeDtypeStruct((lp, nh * LANES), BF16),
        grid=(nh, nq),
        in_specs=[q_spec, k_spec, v_spec, kb_spec],
        out_specs=row_spec,
        name=name,
        compiler_params=_cp(("parallel", "arbitrary")),
    )(qa, qa, qa, kb)


def _sb_bwd(qa, kb, do, *, nh, name, tq=128):
    lp = qa.shape[0]
    tq = _col_tile(lp, tq)
    tk = SB_TK
    nq = lp // tq
    nk = lp // tk
    sub = tq // tk
    scale = HEAD_DIM ** -0.5

    def body(q_ref, k_ref, v_ref, kb_ref, do_ref, dq_ref, dk_ref, dv_ref, dk_acc, dv_acc, w_scr, b_scr):
        i = pl.program_id(1)

        @pl.when(i == 0)
        def _():
            dk_acc[...] = jnp.zeros_like(dk_acc)
            dv_acc[...] = jnp.zeros_like(dv_acc)

        q = q_ref[...]
        dov = do_ref[...]
        rows = i * tq + lax.broadcasted_iota(jnp.int32, (tq, tk), 0)
        lane = lax.broadcasted_iota(jnp.int32, (tq, tk), 1)
        r_i = lax.broadcasted_iota(jnp.int32, (tk, tk), 0)
        c_i = lax.broadcasted_iota(jnp.int32, (tk, tk), 1)
        tri_gt = (r_i > c_i).astype(BF16)
        tri_lt = (r_i < c_i).astype(BF16)
        nblk = (i + 1) * sub

        def pass1(j, c, masked):
            st = pl.multiple_of(j * tk, tk)
            k = k_ref[pl.ds(st, tk), :]
            v = v_ref[pl.ds(st, tk), :]
            z, e, g, lk = _sb_logits(q, k, kb_ref[j], rows, j * tk + lane, masked)
            a = jnp.exp(g + _dot3(lk, tri_gt) + c)
            da = lax.dot_general(dov, v, (((1,), (1,)), ((), ())), preferred_element_type=F32)
            w_scr[j] = a * da
            b_scr[j] = jnp.where(z >= 0.0, 1.0, e) / (1.0 + e)
            dv_acc[pl.ds(st, tk), :] += lax.dot_general(a.astype(BF16), dov, (((0,), (0,)), ((), ())), preferred_element_type=F32)
            return c + jnp.sum(lk, axis=1, keepdims=True)

        c = lax.fori_loop(0, sub, lambda r, c: pass1(nblk - 1 - r, c, True), jnp.zeros((tq, 1), F32))
        lax.fori_loop(0, i * sub, lambda r, c: pass1(i * sub - 1 - r, c, False), c)

        def pass2(j, carry):
            u, dq_acc = carry
            st = pl.multiple_of(j * tk, tk)
            k = k_ref[pl.ds(st, tk), :]
            w = w_scr[j]
            beta = b_scr[j]
            before = _dot3(w, tri_lt) + u
            dz = ((w * (1.0 - beta) - beta * before) * scale).astype(BF16)
            dk_acc[pl.ds(st, tk), :] += lax.dot_general(dz, q, (((0,), (0,)), ((), ())), preferred_element_type=F32)
            return u + jnp.sum(w, axis=1, keepdims=True), dq_acc + jnp.dot(dz, k, preferred_element_type=F32)

        _, dq_acc = lax.fori_loop(0, nblk, pass2, (jnp.zeros((tq, 1), F32), jnp.zeros((tq, LANES), F32)))
        dq_ref[...] = dq_acc.astype(dq_ref.dtype)

        @pl.when(i == nq - 1)
        def _():
            dk_ref[...] = dk_acc[...].astype(dk_ref.dtype)
            dv_ref[...] = dv_acc[...].astype(dv_ref.dtype)

    q_spec, k_spec, v_spec = _head_specs(lp, tq, (0, nh, 2 * nh))
    kb_spec = pl.BlockSpec((None, nk, 1, tk), lambda h, i: (h, 0, 0, 0))
    row_spec = pl.BlockSpec((tq, LANES), lambda h, i: (i, h))
    col_spec = pl.BlockSpec((lp, LANES), lambda h, i: (0, h))
    wide = jax.ShapeDtypeStruct((lp, nh * LANES), BF16)
    return pl.pallas_call(
        body,
        out_shape=(wide, wide, wide),
        grid=(nh, nq),
        in_specs=[q_spec, k_spec, v_spec, kb_spec, row_spec],
        out_specs=(row_spec, col_spec, col_spec),
        scratch_shapes=[
            pltpu.VMEM((lp, LANES), F32),
            pltpu.VMEM((lp, LANES), F32),
            pltpu.VMEM((nk, tq, tk), F32),
            pltpu.VMEM((nk, tq, tk), F32),
        ],
        name=name,
        compiler_params=_cp(("parallel", "arbitrary")),
    )(qa, qa, qa, kb, do)


def _pool_counts(pos, win):
    return jnp.clip(pos + 1, 1, win).astype(F32)


def _pool_fwd(a, name):
    lp, C = a.shape
    tm = _row_tile(lp, 640)
    hb = tm // POOL_HALO

    def body(prev_ref, cur_ref, o_ref, xs):
        i = pl.program_id(0)
        xs[pl.ds(0, POOL_HALO), :] = jnp.where(i > 0, prev_ref[...], 0.0)
        xs[pl.ds(POOL_HALO, tm), :] = cur_ref[...]
        pos = i * tm + lax.broadcasted_iota(jnp.int32, (tm, 1), 0) - PAD0
        for g, win in enumerate(POOL_WINDOWS):
            cols = pl.ds(g * POOL_GROUP, POOL_GROUP)
            s = xs[pl.ds(POOL_HALO, tm), cols]
            for k in range(1, win):
                s = s + xs[pl.ds(POOL_HALO - k, tm), cols]
            o_ref[:, cols] = (s / _pool_counts(pos, win) - xs[pl.ds(POOL_HALO, tm), cols]).astype(o_ref.dtype)

    return pl.pallas_call(
        body,
        out_shape=jax.ShapeDtypeStruct((lp, C), BF16),
        grid=(lp // tm,),
        in_specs=[
            pl.BlockSpec((POOL_HALO, C), lambda i: (jnp.maximum(i * hb - 1, 0), 0)),
            pl.BlockSpec((tm, C), lambda i: (i, 0)),
        ],
        out_specs=pl.BlockSpec((tm, C), lambda i: (i, 0)),
        scratch_shapes=[pltpu.VMEM((tm + POOL_HALO, C), F32)],
        name=name,
        compiler_params=_cp(("parallel",)),
    )(a, a)


def _pool_bwd(dp, name):
    lp, C = dp.shape
    tm = _row_tile(lp, 640)
    hb = tm // POOL_HALO
    nt = lp // tm
    last_halo = lp // POOL_HALO - 1

    def body(cur_ref, next_ref, o_ref, xs):
        i = pl.program_id(0)
        pos = i * tm + lax.broadcasted_iota(jnp.int32, (tm, 1), 0) - PAD0
        pos_h = (i + 1) * tm + lax.broadcasted_iota(jnp.int32, (POOL_HALO, 1), 0) - PAD0
        for g, win in enumerate(POOL_WINDOWS):
            cols = pl.ds(g * POOL_GROUP, POOL_GROUP)
            cur = cur_ref[:, cols]
            xs[pl.ds(0, tm), cols] = cur / _pool_counts(pos, win)
            xs[pl.ds(tm, POOL_HALO), cols] = jnp.where(i < nt - 1, next_ref[:, cols], 0.0) / _pool_counts(pos_h, win)
            s = xs[pl.ds(0, tm), cols]
            for k in range(1, win):
                s = s + xs[pl.ds(k, tm), cols]
            o_ref[:, cols] = jnp.where(pos >= 0, s - cur, 0.0)

    return pl.pallas_call(
        body,
        out_shape=jax.ShapeDtypeStruct((lp, C), F32),
        grid=(nt,),
        in_specs=[
            pl.BlockSpec((tm, C), lambda i: (i, 0)),
            pl.BlockSpec((POOL_HALO, C), lambda i: (jnp.minimum((i + 1) * hb, last_halo), 0)),
        ],
        out_specs=pl.BlockSpec((tm, C), lambda i: (i, 0)),
        scratch_shapes=[pltpu.VMEM((tm + POOL_HALO, C), F32)],
        name=name,
        compiler_params=_cp(("parallel",)),
    )(dp, dp)


def _scale_add(h, pre, scale, name):
    M, C = h.shape
    tm = _row_tile(M, 640)

    def body(h_ref, p_ref, s_ref, o_ref):
        o_ref[...] = h_ref[...] + p_ref[...] * s_ref[...]

    row = pl.BlockSpec((tm, C), lambda i: (i, 0))
    return pl.pallas_call(
        body,
        out_shape=jax.ShapeDtypeStruct((M, C), F32),
        grid=(M // tm,),
        in_specs=[row, row, pl.BlockSpec((1, C), lambda i: (0, 0))],
        out_specs=row,
        name=name,
        compiler_params=_cp(("parallel",)),
    )(h, pre, scale.reshape(1, C))


def _scale_bwd(dh, pre, scale, name):
    M, C = dh.shape
    tm = _row_tile(M, 640)

    def body(dh_ref, p_ref, s_ref, dp_ref, ds_ref):
        @pl.when(pl.program_id(0) == 0)
        def _():
            ds_ref[...] = jnp.zeros_like(ds_ref)

        d = dh_ref[...]
        ds_ref[...] += jnp.sum(d * p_ref[...], axis=0, keepdims=True)
        dp_ref[...] = (d * s_ref[...]).astype(dp_ref.dtype)

    row = pl.BlockSpec((tm, C), lambda i: (i, 0))
    vec = pl.BlockSpec((1, C), lambda i: (0, 0))
    return pl.pallas_call(
        body,
        out_shape=(jax.ShapeDtypeStruct((M, C), BF16), jax.ShapeDtypeStruct((1, C), F32)),
        grid=(M // tm,),
        in_specs=[row, row, vec],
        out_specs=(row, vec),
        name=name,
        compiler_params=_cp(("arbitrary",)),
    )(dh, pre, scale.reshape(1, C))


def _gate_parts(z):
    e = jnp.exp(-jnp.abs(z))
    return e, jnp.minimum(z, 0.0) - jnp.log(1.0 + e)


def _tri_dot3(tri, x):
    hi, mid, lo = _split3(x)
    d = functools.partial(jnp.dot, preferred_element_type=F32)
    return d(tri, hi) + d(tri, mid) + d(tri, lo)


def _gate_fwd(x, b, name):
    lp, C = x.shape
    tm = _row_tile(lp, 640)

    def body(x_ref, b_ref, o_ref, carry):
        i = pl.program_id(0)

        @pl.when(i == 0)
        def _():
            carry[...] = jnp.zeros_like(carry)

        _, ls = _gate_parts(x_ref[...] + b_ref[...])
        rows = i * tm + lax.broadcasted_iota(jnp.int32, (tm, 1), 0)
        ls = jnp.where(rows >= PAD0, ls, 0.0)
        tri = (lax.broadcasted_iota(jnp.int32, (tm, tm), 0) >= lax.broadcasted_iota(jnp.int32, (tm, tm), 1)).astype(BF16)
        f = _tri_dot3(tri, ls) + carry[...]
        o_ref[...] = f
        carry[...] = f[tm - 1:tm, :]

    return pl.pallas_call(
        body,
        out_shape=jax.ShapeDtypeStruct((lp, C), F32),
        grid=(lp // tm,),
        in_specs=[pl.BlockSpec((tm, C), lambda i: (i, 0)), pl.BlockSpec((1, C), lambda i: (0, 0))],
        out_specs=pl.BlockSpec((tm, C), lambda i: (i, 0)),
        scratch_shapes=[pltpu.VMEM((1, C), F32)],
        name=name,
        compiler_params=_cp(("arbitrary",)),
    )(x, b)


def _gate_bwd(x, b, df, name):
    lp, C = x.shape
    tm = _row_tile(lp, 640)
    nt = lp // tm

    def body(x_ref, b_ref, df_ref, dx_ref, db_ref, carry):
        i = pl.program_id(0)

        @pl.when(i == 0)
        def _():
            carry[...] = jnp.zeros_like(carry)
            db_ref[...] = jnp.zeros_like(db_ref)

        z = x_ref[...] + b_ref[...]
        e, _ = _gate_parts(z)
        tri = (lax.broadcasted_iota(jnp.int32, (tm, tm), 0) <= lax.broadcasted_iota(jnp.int32, (tm, tm), 1)).astype(BF16)
        r = _tri_dot3(tri, df_ref[...]) + carry[...]
        carry[...] = r[0:1, :]
        rows = (nt - 1 - i) * tm + lax.broadcasted_iota(jnp.int32, (tm, 1), 0)
        dx = jnp.where(rows >= PAD0, r * (jnp.where(z >= 0.0, e, 1.0) / (1.0 + e)), 0.0)
        dx_ref[...] = dx
        db_ref[...] += jnp.sum(dx, axis=0, keepdims=True)

    rev = pl.BlockSpec((tm, C), lambda i: (nt - 1 - i, 0))
    vec = pl.BlockSpec((1, C), lambda i: (0, 0))
    return pl.pallas_call(
        body,
        out_shape=(jax.ShapeDtypeStruct((lp, C), F32), jax.ShapeDtypeStruct((1, C), F32)),
        grid=(nt,),
        in_specs=[rev, vec, rev],
        out_specs=(rev, vec),
        scratch_shapes=[pltpu.VMEM((1, C), F32)],
        name=name,
        compiler_params=_cp(("arbitrary",)),
    )(x, b, df)


def _rope_apply(x, c, a, b):
    return x * c + pltpu.roll(x, LANES - 16, 1) * a + pltpu.roll(x, 16, 1) * b


def _rope_transpose(dy, c, a, b):
    return dy * c + pltpu.roll(dy * a, 16, 1) + pltpu.roll(dy * b, LANES - 16, 1)


def _mla_prep_fwd(q, kmat, kr, c, a, b, name):
    lp, W = q.shape
    nh = W // LANES
    tm = _row_tile(lp, 640)

    def body(q_ref, k_ref, kr_ref, c_ref, a_ref, b_ref, qo_ref, ko_ref):
        cv, av, bv = c_ref[...], a_ref[...], b_ref[...]
        qo_ref[...] = _rope_apply(q_ref[...], cv, av, bv).astype(qo_ref.dtype)
        ko_ref[...] = (k_ref[...] + _rope_apply(kr_ref[...], cv, av, bv)).astype(ko_ref.dtype)

    head = pl.BlockSpec((tm, LANES), lambda i, h: (i, h))
    tab = pl.BlockSpec((tm, LANES), lambda i, h: (i, 0))
    wide = jax.ShapeDtypeStruct((lp, W), BF16)
    return pl.pallas_call(
        body,
        out_shape=(wide, wide),
        grid=(lp // tm, nh),
        in_specs=[head, head, tab, tab, tab, tab],
        out_specs=(head, head),
        name=name,
        compiler_params=_cp(("parallel", "parallel")),
    )(q, kmat, kr, c, a, b)


def _mla_prep_bwd(dq, dk, c, a, b, name):
    lp, W = dq.shape
    nh = W // LANES
    tm = _row_tile(lp, 640)

    def body(dq_ref, dk_ref, c_ref, a_ref, b_ref, dqo_ref, dkr_ref):
        cv, av, bv = c_ref[...], a_ref[...], b_ref[...]
        ksum = jnp.zeros((tm, LANES), F32)
        for h in range(nh):
            cols = pl.ds(h * LANES, LANES)
            dqo_ref[:, cols] = _rope_transpose(dq_ref[:, cols].astype(F32), cv, av, bv).astype(dqo_ref.dtype)
            ksum = ksum + dk_ref[:, cols].astype(F32)
        dkr_ref[...] = _rope_transpose(ksum, cv, av, bv)

    wide = pl.BlockSpec((tm, W), lambda i: (i, 0))
    tab = pl.BlockSpec((tm, LANES), lambda i: (i, 0))
    return pl.pallas_call(
        body,
        out_shape=(jax.ShapeDtypeStruct((lp, W), BF16), jax.ShapeDtypeStruct((lp, LANES), F32)),
        grid=(lp // tm,),
        in_specs=[wide, wide, tab, tab, tab],
        out_specs=(wide, tab),
        name=name,
        compiler_params=_cp(("parallel",)),
    )(dq, dk, c, a, b)


def _loss_head(h, g, target, name):
    lp, C = h.shape
    tm = _row_tile(lp, 640)
    nt = lp // tm

    def body(h_ref, g_ref, t_ref, loss_ref, dh_ref, dg_ref, sq):
        i = pl.program_id(0)

        @pl.when(i == 0)
        def _():
            dg_ref[...] = jnp.zeros_like(dg_ref)
            sq[...] = jnp.zeros_like(sq)

        xf = h_ref[...]
        gv = g_ref[...]
        r = lax.rsqrt(jnp.mean(xf * xf, axis=-1, keepdims=True) + EPS)
        xhat = xf * r
        rows = i * tm + lax.broadcasted_iota(jnp.int32, (tm, 1), 0)
        err = jnp.where(rows >= PAD0 + N_META, xhat * gv - t_ref[...], 0.0)
        sq[...] += jnp.sum(err * err, axis=0, keepdims=True)
        dy = err * (1.0 / C)
        dg_ref[...] += jnp.sum(dy * xhat, axis=0, keepdims=True)
        dxh = dy * gv
        dh_ref[...] = r * (dxh - xhat * jnp.mean(dxh * xhat, axis=-1, keepdims=True))

        @pl.when(i == nt - 1)
        def _():
            loss_ref[...] = jnp.broadcast_to(jnp.sum(sq[...], axis=1, keepdims=True) * (0.5 / C), (1, LANES))

    row = pl.BlockSpec((tm, C), lambda i: (i, 0))
    vec = pl.BlockSpec((1, C), lambda i: (0, 0))
    return pl.pallas_call(
        body,
        out_shape=(jax.ShapeDtypeStruct((1, LANES), F32), jax.ShapeDtypeStruct((lp, C), F32), jax.ShapeDtypeStruct((1, C), F32)),
        grid=(nt,),
        in_specs=[row, vec, row],
        out_specs=(pl.BlockSpec((1, LANES), lambda i: (0, 0)), row, vec),
        scratch_shapes=[pltpu.VMEM((1, C), F32)],
        name=name,
        compiler_params=_cp(("arbitrary",)),
    )(h, g.reshape(1, C), target)


def _adamw(w, g, m, v, name):
    shape = w.shape
    C = shape[-1]
    R = w.size // C
    tr = R
    if R % 8 == 0:
        for cand in range(8, R + 1, 8):
            if R % cand == 0 and cand * C * 4 <= (1 << 20):
                tr = cand
    c1 = 1.0 - ADAM_B1 ** ADAM_STEP
    c2 = 1.0 - ADAM_B2 ** ADAM_STEP

    def body(w_ref, g_ref, m_ref, v_ref, d_ref, nm_ref, nv_ref):
        gv = g_ref[...]
        nm = ADAM_B1 * m_ref[...] + (1.0 - ADAM_B1) * gv
        nv = ADAM_B2 * v_ref[...] + (1.0 - ADAM_B2) * (gv * gv)
        nm_ref[...] = nm
        nv_ref[...] = nv
        d_ref[...] = -ADAM_LR * ((nm / c1) / (jnp.sqrt(nv / c2) + ADAM_EPS) + ADAM_WD * w_ref[...])

    blk = pl.BlockSpec((tr, C), lambda i: (i, 0))
    out = jax.ShapeDtypeStruct((R, C), F32)
    outs = pl.pallas_call(
        body,
        out_shape=(out, out, out),
        grid=(R // tr,),
        in_specs=[blk] * 4,
        out_specs=(blk, blk, blk),
        name=name,
        compiler_params=_cp(("parallel",)),
    )(*(t.reshape(R, C) for t in (w, g, m, v)))
    return tuple(t.reshape(shape) for t in outs)


def _exchange(send, axes, same, name):
    na = len(axes)
    n = 1 << na
    _, R, C = send.shape

    def body(send_ref, recv_ref, send_sems, recv_sems, local_sem):
        coords = {ax: lax.axis_index(ax) for ax in MESH_AXES}
        me = 0
        for ax in axes:
            me = me * 2 + coords[ax]

        def member(r):
            dev = dict(coords)
            for b, ax in enumerate(axes):
                if (r >> (na - 1 - b)) & 1:
                    dev[ax] = 1 - dev[ax]
            return tuple(dev[ax] for ax in MESH_AXES)

        def chunk(j):
            return send_ref.at[0] if same else send_ref.at[j]

        own = pltpu.make_async_copy(chunk(me), recv_ref.at[me], local_sem)
        own.start()
        copies = []
        for r in range(1, n):
            peer = me ^ r
            cp = pltpu.make_async_remote_copy(
                src_ref=chunk(peer), dst_ref=recv_ref.at[me], send_sem=send_sems.at[r], recv_sem=recv_sems.at[r],
                device_id=member(r), device_id_type=pl.DeviceIdType.MESH)
            cp.start()
            copies.append(cp)
        for r, cp in zip(range(1, n), copies):
            arrival = pltpu.make_async_remote_copy(
                src_ref=chunk(me), dst_ref=recv_ref.at[me ^ r], send_sem=send_sems.at[r], recv_sem=recv_sems.at[r],
                device_id=member(r), device_id_type=pl.DeviceIdType.MESH)
            arrival.wait_recv()
        for cp in copies:
            cp.wait_send()
        own.wait()

    any_spec = pl.BlockSpec(memory_space=pl.ANY)
    return pl.pallas_call(
        body,
        out_shape=jax.ShapeDtypeStruct((n, R, C), send.dtype),
        in_specs=[any_spec],
        out_specs=any_spec,
        scratch_shapes=[pltpu.SemaphoreType.DMA((n,)), pltpu.SemaphoreType.DMA((n,)), pltpu.SemaphoreType.DMA],
        name=name,
        compiler_params=pltpu.CompilerParams(has_side_effects=True),
    )(send)


def _sum_chunks(x, name):
    n, R, C = x.shape
    tr = _row_tile(R, 512)

    def body(x_ref, o_ref):
        acc = x_ref[0].astype(F32)
        for j in range(1, n):
            acc = acc + x_ref[j].astype(F32)
        o_ref[...] = acc

    return pl.pallas_call(
        body,
        out_shape=jax.ShapeDtypeStruct((R, C), F32),
        grid=(R // tr,),
        in_specs=[pl.BlockSpec((n, tr, C), lambda i: (0, i, 0))],
        out_specs=pl.BlockSpec((tr, C), lambda i: (i, 0)),
        name=name,
        compiler_params=_cp(("parallel",)),
    )(x)


def _pad_heads_cols(w, groups, d):
    k = w.shape[0]
    w = w.reshape(k, groups * N_HEADS, d)
    return jnp.pad(w, ((0, 0), (0, 0), (0, LANES - d))).reshape(k, groups * N_HEADS * LANES)


def _unpad_heads_cols(w, groups, d):
    k = w.shape[0]
    return w.reshape(k, groups * N_HEADS, LANES)[:, :, :d].reshape(k, groups * N_HEADS * d)


def _pad_heads_rows(w, d):
    n = w.shape[1]
    return jnp.pad(w.reshape(N_HEADS, d, n), ((0, 0), (0, LANES - d), (0, 0))).reshape(N_HEADS * LANES, n)


def _unpad_heads_rows(w, d):
    n = w.shape[1]
    return w.reshape(N_HEADS, LANES, n)[:, :d].reshape(N_HEADS * d, n)


def _kernel_weights(W):
    P = dict(W)
    pw = W["pool_w"][0]
    bd = jnp.zeros((D_MODEL, D_MODEL), pw.dtype)
    for g in range(len(POOL_WINDOWS)):
        bd = lax.dynamic_update_slice(bd, pw[g], (g * POOL_GROUP, g * POOL_GROUP))
    P["pool_bd"] = bd
    P["sb_qkv"] = _pad_heads_cols(W["sb_w_qkv"][0], 3, HEAD_DIM)
    P["sb_o"] = _pad_heads_rows(W["sb_w_o"][0], HEAD_DIM)
    nq = 3 * N_HEADS * HEAD_DIM
    P["fox_qkv"] = _pad_heads_cols(W["fox_w_qkvf"][0][:, :nq], 3, HEAD_DIM)
    P["fox_f"] = jnp.pad(W["fox_w_qkvf"][0][:, nq:], ((0, 0), (0, LANES - N_HEADS)))
    P["fox_o"] = _pad_heads_rows(W["fox_w_o"][0], HEAD_DIM)
    P["fox_b"] = jnp.pad(W["fox_b_f"], ((0, 0), (0, LANES - N_HEADS)))
    P["mla_down"] = jnp.pad(W["mla_w_down"][0], ((0, 0), (0, MLA_DOWN_PAD - W["mla_w_down"].shape[2])))
    P["mla_uq"] = _pad_heads_cols(W["mla_w_uq"][0], 1, MLA_NOPE + MLA_ROPE)
    ukv = W["mla_w_ukv"][0].reshape(MLA_KV_RANK, N_HEADS, 2 * HEAD_DIM)
    padk = ((0, 0), (0, 0), (0, LANES - HEAD_DIM))
    P["mla_ukv"] = jnp.concatenate(
        [jnp.pad(ukv[:, :, :MLA_NOPE], padk).reshape(MLA_KV_RANK, -1), jnp.pad(ukv[:, :, MLA_NOPE:], padk).reshape(MLA_KV_RANK, -1)], axis=1)
    P["mla_o"] = _pad_heads_rows(W["mla_w_o"][0], HEAD_DIM)
    return P


def _rope_tables(lp):
    pos = (jnp.arange(lp) - PAD0).astype(F32)
    inv = ROPE_THETA ** (-jnp.arange(0, MLA_ROPE, 2, dtype=F32) / MLA_ROPE)
    ang = pos[:, None] * inv[None, :]
    cos, sin = jnp.cos(ang), jnp.sin(ang)
    half = MLA_ROPE // 2
    z = lambda n: jnp.zeros((lp, n), F32)
    c = jnp.concatenate([jnp.ones((lp, MLA_NOPE), F32), cos, cos, z(LANES - MLA_NOPE - MLA_ROPE)], axis=1)
    a = jnp.concatenate([z(MLA_NOPE), -sin, z(LANES - MLA_NOPE - half)], axis=1)
    b = jnp.concatenate([z(MLA_NOPE + half), sin, z(LANES - MLA_NOPE - MLA_ROPE)], axis=1)
    return c, a, b


def _key_bias(lp, t, per_head=None):
    pad = jnp.arange(lp)[None, :] < PAD0
    body = jnp.zeros((N_HEADS, lp), F32) if per_head is None else per_head
    return jnp.where(pad, NEG, body).reshape(N_HEADS, lp // t, 1, t)


def _ffn_fwd(h, i, P):
    b = _rms_fwd(h, P["norm_ffn"][i], BF16, "ffn_norm")
    g = _mm(b, P["ffn_w_gate"][i], "nn", "ffn_gate")
    u = _mm(b, P["ffn_w_up"][i], "nn", "ffn_up")
    hd = _swiglu_fwd(g, u, "ffn_act")
    return _mm(hd, P["ffn_w_down"][i], "nn", "ffn_down", add=h), (h, b, g, u, hd)


def _ffn_bwd(dh, i, P, saved):
    h, b, g, u, hd = saved
    dwd = _mm(hd, dh, "tn", "ffn_down_dw")
    dhd = _mm(dh, P["ffn_w_down"][i], "nt", "ffn_down_dx")
    dg, du = _swiglu_bwd(g, u, dhd, "ffn_act_bwd")
    dwg = _mm(b, dg, "tn", "ffn_gate_dw")
    dwu = _mm(b, du, "tn", "ffn_up_dw")
    db = _mm(dg, P["ffn_w_gate"][i], "nt", "ffn_gate_dx")
    db = _mm(du, P["ffn_w_up"][i], "nt", "ffn_up_dx", add=db)
    dh_in, dgain = _rms_bwd(h, P["norm_ffn"][i], db, dh, "ffn_norm_bwd")
    return dh_in, dgain, dwg, dwu, dwd


def _pool_layer_fwd(h, P):
    a = _rms_fwd(h, P["norm_mix"][0], F32, "pool_norm")
    pooled = _pool_fwd(a, "pool_window")
    pre = _mm(pooled, P["pool_bd"], "nn", "pool_mix")
    return _scale_add(h, pre, P["pool_scale"][0], "pool_scale_add"), (h, pooled, pre)


def _pool_layer_bwd(dh, P, saved):
    h, pooled, pre = saved
    dpre, dscale = _scale_bwd(dh, pre, P["pool_scale"][0], "pool_scale_bwd")
    dbd = _mm(pooled, dpre, "tn", "pool_mix_dw")
    dpooled = _mm(dpre, P["pool_bd"], "nt", "pool_mix_dx")
    da = _pool_bwd(dpooled, "pool_window_bwd")
    dh_in, dgain = _rms_bwd(h, P["norm_mix"][0], da, dh, "mix_norm_bwd")
    dw = jnp.stack([dbd[g * POOL_GROUP:(g + 1) * POOL_GROUP, g * POOL_GROUP:(g + 1) * POOL_GROUP] for g in range(len(POOL_WINDOWS))])
    return dh_in, {"norm_mix0": dgain, "pool_w": dw[None], "pool_scale": dscale}


def _out_proj_bwd(o, dh, wo, tag):
    return _mm(o, dh, "tn", tag + "_o_dw"), _mm(dh, wo, "nt", tag + "_o_dx", out_dtype=BF16)


def _sb_layer_fwd(h, P):
    lp = h.shape[0]
    a = _rms_fwd(h, P["norm_mix"][1], BF16, "mix_norm")
    qkv = _mm(a, P["sb_qkv"], "nn", "sb_qkv", out_dtype=BF16)
    kb = _key_bias(lp, SB_TK)
    o = _sb_fwd(qkv, kb, nh=N_HEADS, name="sb_attn")
    return _mm(o, P["sb_o"], "nn", "attn_out", add=h), (h, a, qkv, kb, o)


def _sb_layer_bwd(dh, P, saved):
    h, a, qkv, kb, o = saved
    dwo, do = _out_proj_bwd(o, dh, P["sb_o"], "attn")
    dq, dk, dv = _sb_bwd(qkv, kb, do, nh=N_HEADS, name="sb_attn_bwd")
    dqkv = jnp.concatenate([dq, dk, dv], axis=1)
    dw = _mm(a, dqkv, "tn", "qkv_dw")
    da = _mm(dqkv, P["sb_qkv"], "nt", "qkv_dx")
    dh_in, dgain = _rms_bwd(h, P["norm_mix"][1], da, dh, "mix_norm_bwd")
    return dh_in, {"norm_mix1": dgain, "sb_w_qkv": _unpad_heads_cols(dw, 3, HEAD_DIM)[None], "sb_w_o": _unpad_heads_rows(dwo, HEAD_DIM)[None]}


def _fox_layer_fwd(h, P):
    lp = h.shape[0]
    t = _attn_tile(lp)
    a = _rms_fwd(h, P["norm_mix"][3], BF16, "mix_norm")
    qkv = _mm(a, P["fox_qkv"], "nn", "sb_qkv", out_dtype=BF16)
    f = _mm(a, P["fox_f"], "nn", "fox_gate_proj")
    fc = _gate_fwd(f, P["fox_b"], "fox_gate")[:, :N_HEADS]
    kb = _key_bias(lp, t, -fc.T)
    fq = jnp.broadcast_to(fc[:, :, None], (lp, N_HEADS, LANES)).reshape(lp, N_HEADS * LANES)
    o, lse = _attn_fwd(qkv, qkv, qkv, kb, fq, nh=N_HEADS, offs=(0, N_HEADS, 2 * N_HEADS), scale=HEAD_DIM ** -0.5, name="fox_attn")
    return _mm(o, P["fox_o"], "nn", "attn_out", add=h), (h, a, qkv, f, kb, fq, o, lse)


def _fox_layer_bwd(dh, P, saved):
    h, a, qkv, f, kb, fq, o, lse = saved
    lp = h.shape[0]
    dwo, do = _out_proj_bwd(o, dh, P["fox_o"], "attn")
    dq, dk, dv, dkb, dqb = _attn_bwd(qkv, qkv, qkv, kb, fq, o, do, lse, nh=N_HEADS, offs=(0, N_HEADS, 2 * N_HEADS),
                                     scale=HEAD_DIM ** -0.5, name="fox_attn_bwd")
    dfc = jnp.pad(dqb.reshape(lp, N_HEADS, LANES)[:, :, 0] - dkb.reshape(N_HEADS, lp).T, ((0, 0), (0, LANES - N_HEADS)))
    df, dbf = _gate_bwd(f, P["fox_b"], dfc, "fox_gate_bwd")
    dqkv = jnp.concatenate([dq, dk, dv], axis=1)
    dw = _mm(a, dqkv, "tn", "qkv_dw")
    dwf = _mm(a, df, "tn", "fox_gate_dw")
    da = _mm(dqkv, P["fox_qkv"], "nt", "qkv_dx")
    da = _mm(df, P["fox_f"], "nt", "fox_gate_dx", add=da)
    dh_in, dgain = _rms_bwd(h, P["norm_mix"][3], da, dh, "mix_norm_bwd")
    dwqkvf = jnp.concatenate([_unpad_heads_cols(dw, 3, HEAD_DIM), dwf[:, :N_HEADS]], axis=1)
    return dh_in, {"norm_mix3": dgain, "fox_w_qkvf": dwqkvf[None], "fox_b_f": dbf[:, :N_HEADS], "fox_w_o": _unpad_heads_rows(dwo, HEAD_DIM)[None]}


def _mla_layer_fwd(h, P):
    lp = h.shape[0]
    a = _rms_fwd(h, P["norm_mix"][2], BF16, "mix_norm")
    down = _mm(a, P["mla_down"], "nn", "mla_down")
    cq_pre = down[:, :MLA_Q_RANK]
    ckv_pre = down[:, MLA_Q_RANK:MLA_Q_RANK + MLA_KV_RANK]
    kr = jnp.pad(down[:, MLA_Q_RANK + MLA_KV_RANK:MLA_Q_RANK + MLA_KV_RANK + MLA_ROPE], ((0, 0), (MLA_NOPE, LANES - MLA_NOPE - MLA_ROPE)))
    cq = _rms_fwd(cq_pre, P["mla_q_norm"][0], BF16, "mla_q_norm")
    ckv = _rms_fwd(ckv_pre, P["mla_kv_norm"][0], BF16, "mla_kv_norm")
    q = _mm(cq, P["mla_uq"], "nn", "mla_uq")
    kv = _mm(ckv, P["mla_ukv"], "nn", "mla_ukv", out_dtype=BF16)
    tabs = _rope_tables(lp)
    qr, kc = _mla_prep_fwd(q, kv, kr, *tabs, "mla_rope")
    kb = _key_bias(lp, _attn_tile(lp))
    o, lse = _attn_fwd(qr, kc, kv, kb, None, nh=N_HEADS, offs=(0, 0, N_HEADS), scale=(MLA_NOPE + MLA_ROPE) ** -0.5, name="mla_attn")
    return _mm(o, P["mla_o"], "nn", "attn_out", add=h), (h, a, cq_pre, ckv_pre, cq, ckv, qr, kc, kv, tabs, kb, o, lse)


def _mla_layer_bwd(dh, P, saved):
    h, a, cq_pre, ckv_pre, cq, ckv, qr, kc, kv, tabs, kb, o, lse = saved
    lp = h.shape[0]
    dwo, do = _out_proj_bwd(o, dh, P["mla_o"], "attn")
    dqr, dkc, dv = _attn_bwd(qr, kc, kv, kb, None, o, do, lse, nh=N_HEADS, offs=(0, 0, N_HEADS),
                             scale=(MLA_NOPE + MLA_ROPE) ** -0.5, name="mla_attn_bwd")
    dq, dkr = _mla_prep_bwd(dqr, dkc, *tabs, "mla_rope_bwd")
    dkv = jnp.concatenate([dkc, dv], axis=1)
    dwuq = _mm(cq, dq, "tn", "mla_uq_dw")
    dcq = _mm(dq, P["mla_uq"], "nt", "mla_uq_dx")
    dwukv = _mm(ckv, dkv, "tn", "mla_ukv_dw")
    dckv = _mm(dkv, P["mla_ukv"], "nt", "mla_ukv_dx")
    dcq_pre, dqn = _rms_bwd(cq_pre, P["mla_q_norm"][0], dcq, None, "mla_q_norm_bwd")
    dckv_pre, dkvn = _rms_bwd(ckv_pre, P["mla_kv_norm"][0], dckv, None, "mla_kv_norm_bwd")
    used = MLA_Q_RANK + MLA_KV_RANK + MLA_ROPE
    ddown = jnp.concatenate([dcq_pre, dckv_pre, dkr[:, MLA_NOPE:MLA_NOPE + MLA_ROPE], jnp.zeros((lp, MLA_DOWN_PAD - used), F32)], axis=1)
    dwdown = _mm(a, ddown, "tn", "mla_down_dw")
    da = _mm(ddown, P["mla_down"], "nt", "mla_down_dx")
    dh_in, dgain = _rms_bwd(h, P["norm_mix"][2], da, dh, "mix_norm_bwd")
    dukv = dwukv.reshape(MLA_KV_RANK, 2, N_HEADS, LANES)[:, :, :, :HEAD_DIM]
    dukv = jnp.concatenate([dukv[:, 0], dukv[:, 1]], axis=-1).reshape(MLA_KV_RANK, N_HEADS * 2 * HEAD_DIM)
    return dh_in, {
        "norm_mix2": dgain, "mla_w_down": dwdown[:, :used][None], "mla_q_norm": dqn, "mla_kv_norm": dkvn,
        "mla_w_uq": _unpad_heads_cols(dwuq, 1, MLA_NOPE + MLA_ROPE)[None], "mla_w_ukv": dukv[None],
        "mla_w_o": _unpad_heads_rows(dwo, HEAD_DIM)[None]}


_MIXERS = ((_pool_layer_fwd, _pool_layer_bwd), (_sb_layer_fwd, _sb_layer_bwd), (_mla_layer_fwd, _mla_layer_bwd), (_fox_layer_fwd, _fox_layer_bwd))


def _step_local(x, target, W):
    seq = x.shape[0]
    P = _kernel_weights(W)
    h = jnp.concatenate([jnp.zeros((PAD0, D_MODEL), F32), W["meta"], x], axis=0)
    tpad = jnp.pad(target, ((PAD0 + N_META, 0), (0, 0)))
    saved = []
    for i in range(4):
        h, s_mix = _MIXERS[i][0](h, P)
        h, s_ffn = _ffn_fwd(h, i, P)
        saved.append((s_mix, s_ffn))
    loss, dh, dfinal = _loss_head(h, W["final_norm"], tpad, "loss_head")
    grads = {"final_norm": dfinal.reshape(-1)}
    gains_mix, gains_ffn, dwg, dwu, dwd = [None] * 4, [None] * 4, [None] * 4, [None] * 4, [None] * 4
    for i in reversed(range(4)):
        s_mix, s_ffn = saved[i]
        dh, gains_ffn[i], dwg[i], dwu[i], dwd[i] = _ffn_bwd(dh, i, P, s_ffn)
        dh, g = _MIXERS[i][1](dh, P, s_mix)
        gains_mix[i] = g.pop("norm_mix%d" % i)
        grads.update(g)
    grads["norm_mix"] = jnp.concatenate(gains_mix, axis=0)
    grads["norm_ffn"] = jnp.concatenate(gains_ffn, axis=0)
    grads["ffn_w_gate"] = jnp.stack(dwg)
    grads["ffn_w_up"] = jnp.stack(dwu)
    grads["ffn_w_down"] = jnp.stack(dwd)
    grads["meta"] = dh[PAD0:PAD0 + N_META]
    return loss, dh[PAD0 + N_META:], grads


_WEIGHTS = ("meta", "norm_mix", "norm_ffn", "pool_w", "pool_scale", "sb_w_qkv", "sb_w_o", "mla_w_down", "mla_q_norm",
            "mla_kv_norm", "mla_w_uq", "mla_w_ukv", "mla_w_o", "fox_w_qkvf", "fox_b_f", "fox_w_o", "ffn_w_gate",
            "ffn_w_up", "ffn_w_down", "final_norm")
_SHARD_AXIS = {"meta": 1, "pool_w": 2, "sb_w_qkv": 2, "sb_w_o": 1, "mla_w_down": 1, "mla_q_norm": 1, "mla_kv_norm": 1,
               "mla_w_uq": 2, "mla_w_ukv": 2, "mla_w_o": 1, "fox_w_qkvf": 2, "fox_b_f": None, "fox_w_o": 1,
               "ffn_w_gate": 2, "ffn_w_up": 2, "ffn_w_down": 1}
_SHARDED = tuple(n for n in _WEIGHTS if _SHARD_AXIS.get(n) is not None)
_REPLICATED = tuple(n for n in _WEIGHTS if _SHARD_AXIS.get(n) is None)
_EXACT = ("meta", "mla_q_norm", "mla_kv_norm")
N_CHIPS = 4
GRAD_ROW_TILE = 512


def _flat_rows(parts, dtype, row_multiple):
    flat = jnp.concatenate([p.astype(dtype).reshape(-1) for p in parts])
    rows = -(-flat.shape[0] // (LANES * row_multiple)) * row_multiple
    return jnp.pad(flat, (0, rows * LANES - flat.shape[0])).reshape(rows, LANES)


def _split_flat(flat, like):
    flat = flat.reshape(-1)
    out, off = [], 0
    for t in like:
        out.append(flat[off:off + t.size].reshape(t.shape))
        off += t.size
    return out


def _gather_shards(local, names, dtype, name):
    blocks = [local[n] for n in names]
    recv = _exchange(_flat_rows(blocks, dtype, 16)[None], ("x", "y"), True, name)
    per_chip = [_split_flat(recv[s], blocks) for s in range(N_CHIPS)]
    return {n: jnp.concatenate([per_chip[s][k] for s in range(N_CHIPS)], axis=_SHARD_AXIS[n]) for k, n in enumerate(names)}


def _shard_of(g, n, s):
    w = g.shape[_SHARD_AXIS[n]] // N_CHIPS
    return lax.slice_in_dim(g, s * w, (s + 1) * w, axis=_SHARD_AXIS[n])


def _train_step(a):
    local = {n: a[n] for n in _WEIGHTS}
    full = {n: local[n] for n in _REPLICATED}
    full.update(_gather_shards(local, [n for n in _SHARDED if n not in _EXACT], BF16, "gather_weights"))
    full.update(_gather_shards(local, list(_EXACT), F32, "gather_exact"))

    loss, grad_x, grads = _step_local(a["x"][0], a["loss_target"][0], full)

    send = jnp.stack([
        _flat_rows([_shard_of(grads[n], n, s) for n in _SHARDED], BF16, 2 * GRAD_ROW_TILE).reshape(2, -1, LANES)
        for s in range(N_CHIPS)]).reshape(2 * N_CHIPS, -1, LANES)
    mine = _sum_chunks(_exchange(send, MESH_AXES, False, "scatter_grads"), "sum_grads")
    both = _exchange(mine[None], ("c",), True, "pair_grads")
    reduced = dict(zip(_SHARDED, _split_flat(both, [local[n] for n in _SHARDED])))
    small = _flat_rows([grads[n] for n in _REPLICATED], F32, 8)
    small = _sum_chunks(_exchange(small[None], MESH_AXES, True, "gather_small_grads"), "sum_small_grads")
    reduced.update(zip(_REPLICATED, _split_flat(small, [local[n] for n in _REPLICATED])))

    deltas, new_m, new_v = {}, {}, {}
    for n in _WEIGHTS:
        deltas[n], new_m[n], new_v[n] = _adamw(local[n], reduced[n], a["m_" + n], a["v_" + n], "adamw")
    total = lax.psum(loss[0, 0], MESH_AXES)
    return (total, grad_x[None], *[reduced[n] for n in _WEIGHTS], *[deltas[n] for n in _WEIGHTS],
            *[new_m[n] for n in _WEIGHTS], *[new_v[n] for n in _WEIGHTS])


def kernel(x, meta, norm_mix, norm_ffn, pool_w, pool_scale, sb_w_qkv, sb_w_o, mla_w_down, mla_q_norm, mla_kv_norm, mla_w_uq, mla_w_ukv, mla_w_o, fox_w_qkvf, fox_b_f, fox_w_o, ffn_w_gate, ffn_w_up, ffn_w_down, final_norm, loss_target, m_meta, m_norm_mix, m_norm_ffn, m_pool_w, m_pool_scale, m_sb_w_qkv, m_sb_w_o, m_mla_w_down, m_mla_q_norm, m_mla_kv_norm, m_mla_w_uq, m_mla_w_ukv, m_mla_w_o, m_fox_w_qkvf, m_fox_b_f, m_fox_w_o, m_ffn_w_gate, m_ffn_w_up, m_ffn_w_down, m_final_norm, v_meta, v_norm_mix, v_norm_ffn, v_pool_w, v_pool_scale, v_sb_w_qkv, v_sb_w_o, v_mla_w_down, v_mla_q_norm, v_mla_kv_norm, v_mla_w_uq, v_mla_w_ukv, v_mla_w_o, v_fox_w_qkvf, v_fox_b_f, v_fox_w_o, v_ffn_w_gate, v_ffn_w_up, v_ffn_w_down, v_final_norm):
    return _train_step(dict(locals()))
```

```python
import functools

import jax
import jax.numpy as jnp
from jax import lax
from jax.experimental import pallas as pl
from jax.experimental.pallas import tpu as pltpu

F32 = jnp.float32
BF16 = jnp.bfloat16

D_MODEL = 1024
N_META = 16
PAD0 = 112
LANES = 128
N_HEADS = 16
HEAD_DIM = 64
POOL_WINDOWS = (2, 4, 8, 16)
POOL_GROUP = 256
POOL_HALO = 16
MLA_Q_RANK = 384
MLA_KV_RANK = 256
MLA_NOPE = 64
MLA_ROPE = 32
MLA_DOWN_PAD = 768
ROPE_THETA = 10000.0
D_FF = 2816
EPS = 1e-6
NEG = -1e30
ADAM_LR = 0.001
ADAM_B1 = 0.9
ADAM_B2 = 0.999
ADAM_EPS = 1e-08
ADAM_WD = 0.01
ADAM_STEP = 10
VMEM_LIMIT = 56 * 1024 * 1024
MESH_AXES = ("x", "y", "c")


def _cp(sem, **kw):
    return pltpu.CompilerParams(dimension_semantics=sem, vmem_limit_bytes=VMEM_LIMIT, **kw)


def _row_tile(m, target):
    best = None
    for t in range(16, min(m, target) + 1, 16):
        if m % t == 0:
            best = t
    return best or m


def _col_tile(n, target):
    best = None
    for t in range(LANES, min(n, target) + 1, LANES):
        if n % t == 0:
            best = t
    return best or n


def _mm(a, b, mode, name, out_dtype=F32, add=None, tm=640, tn=512, tk=2048):
    if mode == "nn":
        (M, K), (K2, N) = a.shape, b.shape
    elif mode == "nt":
        (M, K), (N, K2) = a.shape, b.shape
    else:
        (K, M), (K2, N) = a.shape, b.shape
    assert K == K2, (mode, a.shape, b.shape)
    if mode == "tn":
        tm_ = _col_tile(M, 1408)
        tk_ = _row_tile(K, 640)
    else:
        tm_ = _row_tile(M, tm)
        tk_ = _col_tile(K, tk) if K > tk else K
    tn_ = _col_tile(N, tn)
    nk = K // tk_
    if mode == "nn":
        a_spec = pl.BlockSpec((tm_, tk_), lambda i, j, k: (i, k))
        b_spec = pl.BlockSpec((tk_, tn_), lambda i, j, k: (k, j))
        dims = (((1,), (0,)), ((), ()))
    elif mode == "nt":
        a_spec = pl.BlockSpec((tm_, tk_), lambda i, j, k: (i, k))
        b_spec = pl.BlockSpec((tn_, tk_), lambda i, j, k: (j, k))
        dims = (((1,), (1,)), ((), ()))
    else:
        a_spec = pl.BlockSpec((tk_, tm_), lambda i, j, k: (k, i))
        b_spec = pl.BlockSpec((tk_, tn_), lambda i, j, k: (k, j))
        dims = (((0,), (0,)), ((), ()))
    o_spec = pl.BlockSpec((tm_, tn_), lambda i, j, k: (i, j))
    has_add = add is not None

    def body(*refs):
        if has_add:
            a_ref, b_ref, add_ref, o_ref, acc_ref = refs
        else:
            a_ref, b_ref, o_ref, acc_ref = refs
        k = pl.program_id(2)
        part = lax.dot_general(a_ref[...].astype(BF16), b_ref[...].astype(BF16), dims, preferred_element_type=F32)

        @pl.when(k == 0)
        def _():
            acc_ref[...] = part

        @pl.when(k > 0)
        def _():
            acc_ref[...] += part

        @pl.when(k == nk - 1)
        def _():
            r = acc_ref[...]
            if has_add:
                r = r + add_ref[...]
            o_ref[...] = r.astype(o_ref.dtype)

    ins = [a, b] + ([add] if has_add else [])
    in_specs = [a_spec, b_spec] + ([o_spec] if has_add else [])
    return pl.pallas_call(
        body,
        out_shape=jax.ShapeDtypeStruct((M, N), out_dtype),
        grid=(M // tm_, N // tn_, nk),
        in_specs=in_specs,
        out_specs=o_spec,
        scratch_shapes=[pltpu.VMEM((tm_, tn_), F32)],
        name=name,
        compiler_params=_cp(("parallel", "parallel", "arbitrary")),
    )(*ins)


def _rms_fwd(x, g, out_dtype, name):
    M, C = x.shape
    tm = _row_tile(M, 640)

    def body(x_ref, g_ref, o_ref):
        xf = x_ref[...]
        r = lax.rsqrt(jnp.mean(xf * xf, axis=-1, keepdims=True) + EPS)
        o_ref[...] = ((xf * r) * g_ref[...]).astype(o_ref.dtype)

    return pl.pallas_call(
        body,
        out_shape=jax.ShapeDtypeStruct((M, C), out_dtype),
        grid=(M // tm,),
        in_specs=[pl.BlockSpec((tm, C), lambda i: (i, 0)), pl.BlockSpec((1, C), lambda i: (0, 0))],
        out_specs=pl.BlockSpec((tm, C), lambda i: (i, 0)),
        name=name,
        compiler_params=_cp(("parallel",)),
    )(x, g.reshape(1, C))


def _rms_bwd(x, g, dy, dres, name):
    M, C = x.shape
    tm = _row_tile(M, 640)
    has_res = dres is not None

    def body(*refs):
        if has_res:
            x_ref, g_ref, dy_ref, dres_ref, dx_ref, dg_ref = refs
        else:
            x_ref, g_ref, dy_ref, dx_ref, dg_ref = refs
        xf = x_ref[...]
        r = lax.rsqrt(jnp.mean(xf * xf, axis=-1, keepdims=True) + EPS)
        xhat = xf * r
        dyf = dy_ref[...].astype(F32)

        @pl.when(pl.program_id(0) == 0)
        def _():
            dg_ref[...] = jnp.zeros_like(dg_ref)

        dg_ref[...] += jnp.sum(dyf * xhat, axis=0, keepdims=True)
        dxh = dyf * g_ref[...]
        dx = r * (dxh - xhat * jnp.mean(dxh * xhat, axis=-1, keepdims=True))
        if has_res:
            dx = dx + dres_ref[...]
        dx_ref[...] = dx

    row = pl.BlockSpec((tm, C), lambda i: (i, 0))
    vec = pl.BlockSpec((1, C), lambda i: (0, 0))
    ins = [x, g.reshape(1, C), dy] + ([dres] if has_res else [])
    return pl.pallas_call(
        body,
        out_shape=(jax.ShapeDtypeStruct((M, C), F32), jax.ShapeDtypeStruct((1, C), F32)),
        grid=(M // tm,),
        in_specs=[row, vec, row] + ([row] if has_res else []),
        out_specs=(row, vec),
        name=name,
        compiler_params=_cp(("arbitrary",)),
    )(*ins)


def _sigmoid(x):
    return 1.0 / (1.0 + jnp.exp(-x))


def _swiglu_fwd(g, u, name):
    M, F = g.shape
    tm = _row_tile(M, 320)

    def body(g_ref, u_ref, o_ref):
        gv = g_ref[...]
        o_ref[...] = ((gv * _sigmoid(gv)) * u_ref[...]).astype(o_ref.dtype)

    blk = pl.BlockSpec((tm, F), lambda i: (i, 0))
    return pl.pallas_call(
        body,
        out_shape=jax.ShapeDtypeStruct((M, F), BF16),
        grid=(M // tm,),
        in_specs=[blk, blk],
        out_specs=blk,
        name=name,
        compiler_params=_cp(("parallel",)),
    )(g, u)


def _swiglu_bwd(g, u, dh, name):
    M, F = g.shape
    tm = _row_tile(M, 320)

    def body(g_ref, u_ref, dh_ref, dg_ref, du_ref):
        gv = g_ref[...]
        sg = _sigmoid(gv)
        d = dh_ref[...]
        du_ref[...] = (d * (gv * sg)).astype(du_ref.dtype)
        dg_ref[...] = ((d * u_ref[...]) * (sg * (1.0 + gv * (1.0 - sg)))).astype(dg_ref.dtype)

    blk = pl.BlockSpec((tm, F), lambda i: (i, 0))
    return pl.pallas_call(
        body,
        out_shape=(jax.ShapeDtypeStruct((M, F), BF16), jax.ShapeDtypeStruct((M, F), BF16)),
        grid=(M // tm,),
        in_specs=[blk, blk, blk],
        out_specs=(blk, blk),
        name=name,
        compiler_params=_cp(("parallel",)),
    )(g, u, dh)


def _attn_tile(lp):
    return _col_tile(lp, 640)


def _head_specs(lp, t, offs):
    q_spec = pl.BlockSpec((t, LANES), lambda h, i: (i, offs[0] + h))
    k_spec = pl.BlockSpec((lp, LANES), lambda h, i: (0, offs[1] + h))
    v_spec = pl.BlockSpec((lp, LANES), lambda h, i: (0, offs[2] + h))
    return q_spec, k_spec, v_spec


def _attn_fwd(qa, ka, va, kb, fq, *, nh, offs, scale, name, tile=640):
    lp = qa.shape[0]
    t = _col_tile(lp, tile)
    nq = lp // t
    has_fq = fq is not None

    def body(*refs):
        if has_fq:
            q_ref, k_ref, v_ref, kb_ref, fq_ref, o_ref, lse_ref = refs
        else:
            q_ref, k_ref, v_ref, kb_ref, o_ref, lse_ref = refs
        i = pl.program_id(1)
        q = q_ref[...]
        fqc = fq_ref[:, 0:1] if has_fq else None
        causal = lax.broadcasted_iota(jnp.int32, (t, t), 1) <= lax.broadcasted_iota(jnp.int32, (t, t), 0)

        def step(j, carry, masked):
            m, l, acc = carry
            st = pl.multiple_of(j * t, t)
            k = k_ref[pl.ds(st, t), :]
            v = v_ref[pl.ds(st, t), :]
            s = lax.dot_general(q, k, (((1,), (1,)), ((), ())), preferred_element_type=F32) * scale
            bias = kb_ref[j]
            if has_fq:
                bias = fqc + bias
            s = s + bias
            if masked:
                s = jnp.where(causal, s, NEG)
            m_new = jnp.maximum(m, jnp.max(s, axis=1, keepdims=True))
            p = jnp.exp(s - m_new)
            alpha = jnp.exp(m - m_new)
            l = alpha * l + jnp.sum(p, axis=1, keepdims=True)
            acc = alpha * acc + jnp.dot(p.astype(BF16), v, preferred_element_type=F32)
            return m_new, l, acc

        init = (jnp.full((t, 1), NEG, F32), jnp.zeros((t, 1), F32), jnp.zeros((t, LANES), F32))
        carry = lax.fori_loop(0, i, lambda j, c: step(j, c, False), init)
        m, l, acc = step(i, carry, True)
        valid = (i * t + lax.broadcasted_iota(jnp.int32, (t, 1), 0)) >= PAD0
        o_ref[...] = jnp.where(valid, acc / l, 0.0).astype(o_ref.dtype)
        lse_ref[...] = jnp.broadcast_to(m + jnp.log(l), (t, LANES))

    q_spec, k_spec, v_spec = _head_specs(lp, t, offs)
    kb_spec = pl.BlockSpec((None, nq, 1, t), lambda h, i: (h, 0, 0, 0))
    row_spec = pl.BlockSpec((t, LANES), lambda h, i: (i, h))
    ins = [qa, ka, va, kb] + ([fq] if has_fq else [])
    return pl.pallas_call(
        body,
        out_shape=(jax.ShapeDtypeStruct((lp, nh * LANES), BF16), jax.ShapeDtypeStruct((lp, nh * LANES), F32)),
        grid=(nh, nq),
        in_specs=[q_spec, k_spec, v_spec, kb_spec] + ([row_spec] if has_fq else []),
        out_specs=(row_spec, row_spec),
        name=name,
        compiler_params=_cp(("parallel", "arbitrary")),
    )(*ins)


def _attn_bwd(qa, ka, va, kb, fq, o, do, lse, *, nh, offs, scale, name, tile=640):
    lp = qa.shape[0]
    t = _col_tile(lp, tile)
    nq = lp // t
    has_fq = fq is not None

    def body(*refs):
        if has_fq:
            q_ref, k_ref, v_ref, kb_ref, fq_ref, o_ref, do_ref, lse_ref, dq_ref, dk_ref, dv_ref, dkb_ref, dqb_ref, dk_acc, dv_acc = refs
        else:
            q_ref, k_ref, v_ref, kb_ref, o_ref, do_ref, lse_ref, dq_ref, dk_ref, dv_ref, dk_acc, dv_acc = refs
        i = pl.program_id(1)

        @pl.when(i == 0)
        def _():
            dk_acc[...] = jnp.zeros_like(dk_acc)
            dv_acc[...] = jnp.zeros_like(dv_acc)
            if has_fq:
                dkb_ref[...] = jnp.zeros_like(dkb_ref)

        q = q_ref[...]
        dov = do_ref[...]
        delta = jnp.sum(o_ref[...].astype(F32) * dov.astype(F32), axis=1, keepdims=True)
        lse_c = lse_ref[:, 0:1]
        fqc = fq_ref[:, 0:1] if has_fq else None
        causal = lax.broadcasted_iota(jnp.int32, (t, t), 1) <= lax.broadcasted_iota(jnp.int32, (t, t), 0)

        def step(j, carry, masked):
            dq_acc, rs = carry
            st = pl.multiple_of(j * t, t)
            k = k_ref[pl.ds(st, t), :]
            v = v_ref[pl.ds(st, t), :]
            s = lax.dot_general(q, k, (((1,), (1,)), ((), ())), preferred_element_type=F32) * scale
            bias = kb_ref[j]
            if has_fq:
                bias = fqc + bias
            s = s + bias
            if masked:
                s = jnp.where(causal, s, NEG)
            p = jnp.exp(s - lse_c)
            dp = lax.dot_general(dov, v, (((1,), (1,)), ((), ())), preferred_element_type=F32)
            ds = p * (dp - delta)
            dv_acc[pl.ds(st, t), :] += lax.dot_general(p.astype(BF16), dov, (((0,), (0,)), ((), ())), preferred_element_type=F32)
            dsb = (ds * scale).astype(BF16)
            dk_acc[pl.ds(st, t), :] += lax.dot_general(dsb, q, (((0,), (0,)), ((), ())), preferred_element_type=F32)
            if has_fq:
                dkb_ref[j] += jnp.sum(ds, axis=0, keepdims=True)
                rs = rs + jnp.sum(ds, axis=1, keepdims=True)
            return dq_acc + jnp.dot(dsb, k, preferred_element_type=F32), rs

        carry = (jnp.zeros((t, LANES), F32), jnp.zeros((t, 1), F32))
        carry = lax.fori_loop(0, i, lambda j, c: step(j, c, False), carry)
        dq_acc, rs = step(i, carry, True)
        dq_ref[...] = dq_acc.astype(dq_ref.dtype)
        if has_fq:
            dqb_ref[...] = jnp.broadcast_to(rs, (t, LANES))

        @pl.when(i == nq - 1)
        def _():
            dk_ref[...] = dk_acc[...].astype(dk_ref.dtype)
            dv_ref[...] = dv_acc[...].astype(dv_ref.dtype)

    q_spec, k_spec, v_spec = _head_specs(lp, t, offs)
    kb_spec = pl.BlockSpec((None, nq, 1, t), lambda h, i: (h, 0, 0, 0))
    row_spec = pl.BlockSpec((t, LANES), lambda h, i: (i, h))
    col_spec = pl.BlockSpec((lp, LANES), lambda h, i: (0, h))
    ins = [qa, ka, va, kb] + ([fq] if has_fq else []) + [o, do, lse]
    wide = jax.ShapeDtypeStruct((lp, nh * LANES), BF16)
    extra_shapes = (jax.ShapeDtypeStruct(kb.shape, F32), jax.ShapeDtypeStruct((lp, nh * LANES), F32)) if has_fq else ()
    extra_specs = (kb_spec, row_spec) if has_fq else ()
    return pl.pallas_call(
        body,
        out_shape=(wide, wide, wide) + extra_shapes,
        grid=(nh, nq),
        in_specs=[q_spec, k_spec, v_spec, kb_spec] + ([row_spec] if has_fq else []) + [row_spec, row_spec, row_spec],
        out_specs=(row_spec, col_spec, col_spec) + extra_specs,
        scratch_shapes=[pltpu.VMEM((lp, LANES), F32), pltpu.VMEM((lp, LANES), F32)],
        name=name,
        compiler_params=_cp(("parallel", "arbitrary")),
    )(*ins)


SB_TK = 128


def _split3(x):
    hi = x.astype(BF16)
    r1 = x - hi.astype(F32)
    mid = r1.astype(BF16)
    lo = (r1 - mid.astype(F32)).astype(BF16)
    return hi, mid, lo


SB_RC = 128


_NT = (((1,), (1,)), ((), ()))
_TN = (((0,), (0,)), ((), ()))


def _sb_logits(zraw, kbj, mask):
    z = zraw * (HEAD_DIM ** -0.5) + kbj
    if mask is not None:
        z = jnp.where(mask, z, NEG)
    e = jnp.exp(-jnp.abs(z))
    g = jnp.minimum(z, 0.0) - jnp.log(1.0 + e)
    lk = g - z
    return z, e, g, lk


def _dot3_parts(parts, tri):
    d = functools.partial(jnp.dot, preferred_element_type=F32)
    return d(parts[0], tri) + d(parts[1], tri) + d(parts[2], tri)


def _sb_diag_chunks(jj, nrc, rc, tk):
    plan = []
    for r in range(nrc):
        lo_row, hi_row = r * rc, (r + 1) * rc - 1
        lo_col, hi_col = jj * tk, (jj + 1) * tk - 1
        if hi_row <= lo_col:
            plan.append(None)
        elif lo_row > hi_col:
            plan.append("all")
        else:
            plan.append(lo_col - lo_row)
    return plan


def _sb_fwd(qa, kb, *, nh, name, tq=640):
    lp = qa.shape[0]
    tq = _col_tile(lp, tq)
    tk = SB_TK
    rc = min(SB_RC, tq)
    nq, sub, nrc = lp // tq, tq // tk, tq // rc

    def body(q_ref, k_ref, v_ref, kb_ref, o_ref, c_scr, acc_scr):
        i = pl.program_id(1)
        c_scr[...] = jnp.zeros_like(c_scr)
        acc_scr[...] = jnp.zeros_like(acc_scr)
        tri = (lax.broadcasted_iota(jnp.int32, (tk, tk), 0) > lax.broadcasted_iota(jnp.int32, (tk, tk), 1)).astype(BF16)
        row_io = lax.broadcasted_iota(jnp.int32, (rc, tk), 0)
        col_io = lax.broadcasted_iota(jnp.int32, (rc, tk), 1)

        def block(j, plan):
            st = pl.multiple_of(j * tk, tk)
            k = k_ref[pl.ds(st, tk), :]
            v = v_ref[pl.ds(st, tk), :]
            kbj = kb_ref[j]
            live = [r for r, what in enumerate(plan) if what is not None]
            rows = [pl.ds(r * rc, rc) for r in live]
            zs = [lax.dot_general(q_ref[rs, :], k, _NT, preferred_element_type=F32) for rs in rows]
            gs, splits, sums = [], [], []
            for r, zraw in zip(live, zs):
                mask = None if plan[r] == "all" else (col_io + plan[r]) < row_io
                _, _, g, lk = _sb_logits(zraw, kbj, mask)
                gs.append(g)
                splits.append(_split3(lk))
                sums.append(jnp.sum(lk, axis=1, keepdims=True))
            laters = [_dot3_parts(p, tri) for p in splits]
            avs = [jnp.exp(g + (later + c_scr[rs, :])).astype(BF16) for g, later, rs in zip(gs, laters, rows)]
            pvs = [jnp.dot(a, v, preferred_element_type=F32) for a in avs]
            for rs, pv, sm in zip(rows, pvs, sums):
                acc_scr[rs, :] += pv
                c_scr[rs, :] += sm

        for jj in reversed(range(sub)):
            block(i * sub + jj, _sb_diag_chunks(jj, nrc, rc, tk))

        def left(r, carry):
            block(i * sub - 1 - r, ["all"] * nrc)
            return carry

        lax.fori_loop(0, i * sub, left, 0)
        o_ref[...] = acc_scr[...].astype(o_ref.dtype)

    q_spec, k_spec, v_spec = _head_specs(lp, tq, (0, nh, 2 * nh))
    kb_spec = pl.BlockSpec((None, lp // tk, 1, tk), lambda h, i: (h, 0, 0, 0))
    row_spec = pl.BlockSpec((tq, LANES), lambda h, i: (i, h))
    return pl.pallas_call(
        body,
        out_shape=jax.ShapeDtypeStruct((lp, nh * LANES), BF16),
        grid=(nh, nq),
        in_specs=[q_spec, k_spec, v_spec, kb_spec],
        out_specs=row_spec,
        scratch_shapes=[pltpu.VMEM((tq, 1), F32), pltpu.VMEM((tq, LANES), F32)],
        name=name,
        compiler_params=_cp(("parallel", "arbitrary")),
    )(qa, qa, qa, kb)


def _sb_bwd(qa, kb, do, *, nh, name, tq=640):
    lp = qa.shape[0]
    tq = _col_tile(lp, tq)
    tk = SB_TK
    rc = min(SB_RC, tq)
    nq, nk, sub, nrc = lp // tq, lp // tk, tq // tk, tq // rc
    scale = HEAD_DIM ** -0.5

    def body(q_ref, k_ref, v_ref, kb_ref, do_ref, dq_ref, dk_ref, dv_ref, dk_acc, dv_acc, w_scr, b_scr, a_scr, dz_scr, c_scr, u_scr, dq_scr):
        i = pl.program_id(1)

        @pl.when(i == 0)
        def _():
            dk_acc[...] = jnp.zeros_like(dk_acc)
            dv_acc[...] = jnp.zeros_like(dv_acc)

        c_scr[...] = jnp.zeros_like(c_scr)
        u_scr[...] = jnp.zeros_like(u_scr)
        dq_scr[...] = jnp.zeros_like(dq_scr)
        r_i = lax.broadcasted_iota(jnp.int32, (tk, tk), 0)
        c_i = lax.broadcasted_iota(jnp.int32, (tk, tk), 1)
        tri_gt = (r_i > c_i).astype(BF16)
        tri_lt = (r_i < c_i).astype(BF16)
        row_io = lax.broadcasted_iota(jnp.int32, (rc, tk), 0)
        col_io = lax.broadcasted_iota(jnp.int32, (rc, tk), 1)

        def pass1(j, plan):
            st = pl.multiple_of(j * tk, tk)
            k = k_ref[pl.ds(st, tk), :]
            v = v_ref[pl.ds(st, tk), :]
            kbj = kb_ref[j]
            live = [r for r, what in enumerate(plan) if what is not None]
            rows = [pl.ds(r * rc, rc) for r in live]
            for r, what in enumerate(plan):
                if what is None:
                    a_scr[pl.ds(r * rc, rc), :] = jnp.zeros((rc, tk), BF16)
            zs = [lax.dot_general(q_ref[rs, :], k, _NT, preferred_element_type=F32) for rs in rows]
            das = [lax.dot_general(do_ref[rs, :], v, _NT, preferred_element_type=F32) for rs in rows]
            gs, splits = [], []
            for r, rs, zraw in zip(live, rows, zs):
                mask = None if plan[r] == "all" else (col_io + plan[r]) < row_io
                z, e, g, lk = _sb_logits(zraw, kbj, mask)
                b_scr[j, rs, :] = (jnp.where(z >= 0.0, 1.0, e) / (1.0 + e)).astype(BF16)
                gs.append(g + c_scr[rs, :])
                splits.append(_split3(lk))
                c_scr[rs, :] += jnp.sum(lk, axis=1, keepdims=True)
            laters = [_dot3_parts(p, tri_gt) for p in splits]
            for rs, g, later, da in zip(rows, gs, laters, das):
                a = jnp.exp(g + later)
                w_scr[j, rs, :] = (a * da).astype(BF16)
                a_scr[rs, :] = a.astype(BF16)
            dv_acc[pl.ds(st, tk), :] += lax.dot_general(a_scr[...], do_ref[...], _TN, preferred_element_type=F32)

        def pass2(j, plan):
            st = pl.multiple_of(j * tk, tk)
            k = k_ref[pl.ds(st, tk), :]
            live = [r for r, what in enumerate(plan) if what is not None]
            rows = [pl.ds(r * rc, rc) for r in live]
            for r, what in enumerate(plan):
                if what is None:
                    dz_scr[pl.ds(r * rc, rc), :] = jnp.zeros((rc, tk), BF16)
            wbs = [w_scr[j, rs, :] for rs in rows]
            befores = [jnp.dot(wb, tri_lt, preferred_element_type=F32) for wb in wbs]
            dzs = []
            for rs, wb, before in zip(rows, wbs, befores):
                w = wb.astype(F32)
                beta = b_scr[j, rs, :].astype(F32)
                dz = ((w * (1.0 - beta) - beta * (before + u_scr[rs, :])) * scale).astype(BF16)
                dz_scr[rs, :] = dz
                dzs.append(dz)
                u_scr[rs, :] += jnp.sum(w, axis=1, keepdims=True)
            dqs = [jnp.dot(dz, k, preferred_element_type=F32) for dz in dzs]
            for rs, dq in zip(rows, dqs):
                dq_scr[rs, :] += dq
            dk_acc[pl.ds(st, tk), :] += lax.dot_general(dz_scr[...], q_ref[...], _TN, preferred_element_type=F32)

        everything = ["all"] * nrc
        for jj in reversed(range(sub)):
            pass1(i * sub + jj, _sb_diag_chunks(jj, nrc, rc, tk))

        def left1(r, carry):
            pass1(i * sub - 1 - r, everything)
            return carry

        lax.fori_loop(0, i * sub, left1, 0)

        def left2(j, carry):
            pass2(j, everything)
            return carry

        lax.fori_loop(0, i * sub, left2, 0)
        for jj in range(sub):
            pass2(i * sub + jj, _sb_diag_chunks(jj, nrc, rc, tk))
        dq_ref[...] = dq_scr[...].astype(dq_ref.dtype)

        @pl.when(i == nq - 1)
        def _():
            dk_ref[...] = dk_acc[...].astype(dk_ref.dtype)
            dv_ref[...] = dv_acc[...].astype(dv_ref.dtype)

    q_spec, k_spec, v_spec = _head_specs(lp, tq, (0, nh, 2 * nh))
    kb_spec = pl.BlockSpec((None, nk, 1, tk), lambda h, i: (h, 0, 0, 0))
    row_spec = pl.BlockSpec((tq, LANES), lambda h, i: (i, h))
    col_spec = pl.BlockSpec((lp, LANES), lambda h, i: (0, h))
    wide = jax.ShapeDtypeStruct((lp, nh * LANES), BF16)
    return pl.pallas_call(
        body,
        out_shape=(wide, wide, wide),
        grid=(nh, nq),
        in_specs=[q_spec, k_spec, v_spec, kb_spec, row_spec],
        out_specs=(row_spec, col_spec, col_spec),
        scratch_shapes=[
            pltpu.VMEM((lp, LANES), F32),
            pltpu.VMEM((lp, LANES), F32),
            pltpu.VMEM((nk, tq, tk), BF16),
            pltpu.VMEM((nk, tq, tk), BF16),
            pltpu.VMEM((tq, tk), BF16),
            pltpu.VMEM((tq, tk), BF16),
            pltpu.VMEM((tq, 1), F32),
            pltpu.VMEM((tq, 1), F32),
            pltpu.VMEM((tq, LANES), F32),
        ],
        name=name,
        compiler_params=_cp(("parallel", "arbitrary")),
    )(qa, qa, qa, kb, do)


def _pool_counts(pos, win):
    return jnp.clip(pos + 1, 1, win).astype(F32)


def _pool_fwd(a, name):
    lp, C = a.shape
    tm = _row_tile(lp, 640)
    hb = tm // POOL_HALO

    def body(prev_ref, cur_ref, o_ref, xs):
        i = pl.program_id(0)
        xs[pl.ds(0, POOL_HALO), :] = jnp.where(i > 0, prev_ref[...], 0.0)
        xs[pl.ds(POOL_HALO, tm), :] = cur_ref[...]
        pos = i * tm + lax.broadcasted_iota(jnp.int32, (tm, 1), 0) - PAD0
        for g, win in enumerate(POOL_WINDOWS):
            cols = pl.ds(g * POOL_GROUP, POOL_GROUP)
            s = xs[pl.ds(POOL_HALO, tm), cols]
            for k in range(1, win):
                s = s + xs[pl.ds(POOL_HALO - k, tm), cols]
            o_ref[:, cols] = (s / _pool_counts(pos, win) - xs[pl.ds(POOL_HALO, tm), cols]).astype(o_ref.dtype)

    return pl.pallas_call(
        body,
        out_shape=jax.ShapeDtypeStruct((lp, C), BF16),
        grid=(lp // tm,),
        in_specs=[
            pl.BlockSpec((POOL_HALO, C), lambda i: (jnp.maximum(i * hb - 1, 0), 0)),
            pl.BlockSpec((tm, C), lambda i: (i, 0)),
        ],
        out_specs=pl.BlockSpec((tm, C), lambda i: (i, 0)),
        scratch_shapes=[pltpu.VMEM((tm + POOL_HALO, C), F32)],
        name=name,
        compiler_params=_cp(("parallel",)),
    )(a, a)


def _pool_bwd(dp, name):
    lp, C = dp.shape
    tm = _row_tile(lp, 640)
    hb = tm // POOL_HALO
    nt = lp // tm
    last_halo = lp // POOL_HALO - 1

    def body(cur_ref, next_ref, o_ref, xs):
        i = pl.program_id(0)
        pos = i * tm + lax.broadcasted_iota(jnp.int32, (tm, 1), 0) - PAD0
        pos_h = (i + 1) * tm + lax.broadcasted_iota(jnp.int32, (POOL_HALO, 1), 0) - PAD0
        for g, win in enumerate(POOL_WINDOWS):
            cols = pl.ds(g * POOL_GROUP, POOL_GROUP)
            cur = cur_ref[:, cols]
            xs[pl.ds(0, tm), cols] = cur / _pool_counts(pos, win)
            xs[pl.ds(tm, POOL_HALO), cols] = jnp.where(i < nt - 1, next_ref[:, cols], 0.0) / _pool_counts(pos_h, win)
            s = xs[pl.ds(0, tm), cols]
            for k in range(1, win):
                s = s + xs[pl.ds(k, tm), cols]
            o_ref[:, cols] = jnp.where(pos >= 0, s - cur, 0.0)

    return pl.pallas_call(
        body,
        out_shape=jax.ShapeDtypeStruct((lp, C), F32),
        grid=(nt,),
        in_specs=[
            pl.BlockSpec((tm, C), lambda i: (i, 0)),
            pl.BlockSpec((POOL_HALO, C), lambda i: (jnp.minimum((i + 1) * hb, last_halo), 0)),
        ],
        out_specs=pl.BlockSpec((tm, C), lambda i: (i, 0)),
        scratch_shapes=[pltpu.VMEM((tm + POOL_HALO, C), F32)],
        name=name,
        compiler_params=_cp(("parallel",)),
    )(dp, dp)


def _scale_add(h, pre, scale, name):
    M, C = h.shape
    tm = _row_tile(M, 640)

    def body(h_ref, p_ref, s_ref, o_ref):
        o_ref[...] = h_ref[...] + p_ref[...] * s_ref[...]

    row = pl.BlockSpec((tm, C), lambda i: (i, 0))
    return pl.pallas_call(
        body,
        out_shape=jax.ShapeDtypeStruct((M, C), F32),
        grid=(M // tm,),
        in_specs=[row, row, pl.BlockSpec((1, C), lambda i: (0, 0))],
        out_specs=row,
        name=name,
        compiler_params=_cp(("parallel",)),
    )(h, pre, scale.reshape(1, C))


def _scale_bwd(dh, pre, scale, name):
    M, C = dh.shape
    tm = _row_tile(M, 640)

    def body(dh_ref, p_ref, s_ref, dp_ref, ds_ref):
        @pl.when(pl.program_id(0) == 0)
        def _():
            ds_ref[...] = jnp.zeros_like(ds_ref)

        d = dh_ref[...]
        ds_ref[...] += jnp.sum(d * p_ref[...], axis=0, keepdims=True)
        dp_ref[...] = (d * s_ref[...]).astype(dp_ref.dtype)

    row = pl.BlockSpec((tm, C), lambda i: (i, 0))
    vec = pl.BlockSpec((1, C), lambda i: (0, 0))
    return pl.pallas_call(
        body,
        out_shape=(jax.ShapeDtypeStruct((M, C), BF16), jax.ShapeDtypeStruct((1, C), F32)),
        grid=(M // tm,),
        in_specs=[row, row, vec],
        out_specs=(row, vec),
        name=name,
        compiler_params=_cp(("arbitrary",)),
    )(dh, pre, scale.reshape(1, C))


def _gate_parts(z):
    e = jnp.exp(-jnp.abs(z))
    return e, jnp.minimum(z, 0.0) - jnp.log(1.0 + e)


def _tri_dot3(tri, x):
    hi, mid, lo = _split3(x)
    d = functools.partial(jnp.dot, preferred_element_type=F32)
    return d(tri, hi) + d(tri, mid) + d(tri, lo)


def _gate_fwd(x, b, name):
    lp, C = x.shape
    tm = _row_tile(lp, 640)

    def body(x_ref, b_ref, o_ref, carry):
        i = pl.program_id(0)

        @pl.when(i == 0)
        def _():
            carry[...] = jnp.zeros_like(carry)

        _, ls = _gate_parts(x_ref[...] + b_ref[...])
        rows = i * tm + lax.broadcasted_iota(jnp.int32, (tm, 1), 0)
        ls = jnp.where(rows >= PAD0, ls, 0.0)
        tri = (lax.broadcasted_iota(jnp.int32, (tm, tm), 0) >= lax.broadcasted_iota(jnp.int32, (tm, tm), 1)).astype(BF16)
        f = _tri_dot3(tri, ls) + carry[...]
        o_ref[...] = f
        carry[...] = f[tm - 1:tm, :]

    return pl.pallas_call(
        body,
        out_shape=jax.ShapeDtypeStruct((lp, C), F32),
        grid=(lp // tm,),
        in_specs=[pl.BlockSpec((tm, C), lambda i: (i, 0)), pl.BlockSpec((1, C), lambda i: (0, 0))],
        out_specs=pl.BlockSpec((tm, C), lambda i: (i, 0)),
        scratch_shapes=[pltpu.VMEM((1, C), F32)],
        name=name,
        compiler_params=_cp(("arbitrary",)),
    )(x, b)


def _gate_bwd(x, b, df, name):
    lp, C = x.shape
    tm = _row_tile(lp, 640)
    nt = lp // tm

    def body(x_ref, b_ref, df_ref, dx_ref, db_ref, carry):
        i = pl.program_id(0)

        @pl.when(i == 0)
        def _():
            carry[...] = jnp.zeros_like(carry)
            db_ref[...] = jnp.zeros_like(db_ref)

        z = x_ref[...] + b_ref[...]
        e, _ = _gate_parts(z)
        tri = (lax.broadcasted_iota(jnp.int32, (tm, tm), 0) <= lax.broadcasted_iota(jnp.int32, (tm, tm), 1)).astype(BF16)
        r = _tri_dot3(tri, df_ref[...]) + carry[...]
        carry[...] = r[0:1, :]
        rows = (nt - 1 - i) * tm + lax.broadcasted_iota(jnp.int32, (tm, 1), 0)
        dx = jnp.where(rows >= PAD0, r * (jnp.where(z >= 0.0, e, 1.0) / (1.0 + e)), 0.0)
        dx_ref[...] = dx
        db_ref[...] += jnp.sum(dx, axis=0, keepdims=True)

    rev = pl.BlockSpec((tm, C), lambda i: (nt - 1 - i, 0))
    vec = pl.BlockSpec((1, C), lambda i: (0, 0))
    return pl.pallas_call(
        body,
        out_shape=(jax.ShapeDtypeStruct((lp, C), F32), jax.ShapeDtypeStruct((1, C), F32)),
        grid=(nt,),
        in_specs=[rev, vec, rev],
        out_specs=(rev, vec),
        scratch_shapes=[pltpu.VMEM((1, C), F32)],
        name=name,
        compiler_params=_cp(("arbitrary",)),
    )(x, b, df)


def _rope_apply(x, c, a, b):
    return x * c + pltpu.roll(x, LANES - 16, 1) * a + pltpu.roll(x, 16, 1) * b


def _rope_transpose(dy, c, a, b):
    return dy * c + pltpu.roll(dy * a, 16, 1) + pltpu.roll(dy * b, LANES - 16, 1)


def _mla_prep_fwd(q, kmat, kr, c, a, b, name):
    lp, W = q.shape
    nh = W // LANES
    tm = _row_tile(lp, 640)

    def body(q_ref, k_ref, kr_ref, c_ref, a_ref, b_ref, qo_ref, ko_ref):
        cv, av, bv = c_ref[...], a_ref[...], b_ref[...]
        qo_ref[...] = _rope_apply(q_ref[...], cv, av, bv).astype(qo_ref.dtype)
        ko_ref[...] = (k_ref[...] + _rope_apply(kr_ref[...], cv, av, bv)).astype(ko_ref.dtype)

    head = pl.BlockSpec((tm, LANES), lambda i, h: (i, h))
    tab = pl.BlockSpec((tm, LANES), lambda i, h: (i, 0))
    wide = jax.ShapeDtypeStruct((lp, W), BF16)
    return pl.pallas_call(
        body,
        out_shape=(wide, wide),
        grid=(lp // tm, nh),
        in_specs=[head, head, tab, tab, tab, tab],
        out_specs=(head, head),
        name=name,
        compiler_params=_cp(("parallel", "parallel")),
    )(q, kmat, kr, c, a, b)


def _mla_prep_bwd(dq, dk, c, a, b, name):
    lp, W = dq.shape
    nh = W // LANES
    tm = _row_tile(lp, 640)

    def body(dq_ref, dk_ref, c_ref, a_ref, b_ref, dqo_ref, dkr_ref):
        cv, av, bv = c_ref[...], a_ref[...], b_ref[...]
        ksum = jnp.zeros((tm, LANES), F32)
        for h in range(nh):
            cols = pl.ds(h * LANES, LANES)
            dqo_ref[:, cols] = _rope_transpose(dq_ref[:, cols].astype(F32), cv, av, bv).astype(dqo_ref.dtype)
            ksum = ksum + dk_ref[:, cols].astype(F32)
        dkr_ref[...] = _rope_transpose(ksum, cv, av, bv)

    wide = pl.BlockSpec((tm, W), lambda i: (i, 0))
    tab = pl.BlockSpec((tm, LANES), lambda i: (i, 0))
    return pl.pallas_call(
        body,
        out_shape=(jax.ShapeDtypeStruct((lp, W), BF16), jax.ShapeDtypeStruct((lp, LANES), F32)),
        grid=(lp // tm,),
        in_specs=[wide, wide, tab, tab, tab],
        out_specs=(wide, tab),
        name=name,
        compiler_params=_cp(("parallel",)),
    )(dq, dk, c, a, b)


def _loss_head(h, g, target, name):
    lp, C = h.shape
    tm = _row_tile(lp, 640)
    nt = lp // tm

    def body(h_ref, g_ref, t_ref, loss_ref, dh_ref, dg_ref, sq):
        i = pl.program_id(0)

        @pl.when(i == 0)
        def _():
            dg_ref[...] = jnp.zeros_like(dg_ref)
            sq[...] = jnp.zeros_like(sq)

        xf = h_ref[...]
        gv = g_ref[...]
        r = lax.rsqrt(jnp.mean(xf * xf, axis=-1, keepdims=True) + EPS)
        xhat = xf * r
        rows = i * tm + lax.broadcasted_iota(jnp.int32, (tm, 1), 0)
        err = jnp.where(rows >= PAD0 + N_META, xhat * gv - t_ref[...], 0.0)
        sq[...] += jnp.sum(err * err, axis=0, keepdims=True)
        dy = err * (1.0 / C)
        dg_ref[...] += jnp.sum(dy * xhat, axis=0, keepdims=True)
        dxh = dy * gv
        dh_ref[...] = r * (dxh - xhat * jnp.mean(dxh * xhat, axis=-1, keepdims=True))

        @pl.when(i == nt - 1)
        def _():
            loss_ref[...] = jnp.broadcast_to(jnp.sum(sq[...], axis=1, keepdims=True) * (0.5 / C), (1, LANES))

    row = pl.BlockSpec((tm, C), lambda i: (i, 0))
    vec = pl.BlockSpec((1, C), lambda i: (0, 0))
    return pl.pallas_call(
        body,
        out_shape=(jax.ShapeDtypeStruct((1, LANES), F32), jax.ShapeDtypeStruct((lp, C), F32), jax.ShapeDtypeStruct((1, C), F32)),
        grid=(nt,),
        in_specs=[row, vec, row],
        out_specs=(pl.BlockSpec((1, LANES), lambda i: (0, 0)), row, vec),
        scratch_shapes=[pltpu.VMEM((1, C), F32)],
        name=name,
        compiler_params=_cp(("arbitrary",)),
    )(h, g.reshape(1, C), target)


def _adamw(w, g, m, v, name):
    shape = w.shape
    C = shape[-1]
    R = w.size // C
    tr = R
    if R % 8 == 0:
        for cand in range(8, R + 1, 8):
            if R % cand == 0 and cand * C * 4 <= (1 << 20):
                tr = cand
    c1 = 1.0 - ADAM_B1 ** ADAM_STEP
    c2 = 1.0 - ADAM_B2 ** ADAM_STEP

    def body(w_ref, g_ref, m_ref, v_ref, d_ref, nm_ref, nv_ref):
        gv = g_ref[...]
        nm = ADAM_B1 * m_ref[...] + (1.0 - ADAM_B1) * gv
        nv = ADAM_B2 * v_ref[...] + (1.0 - ADAM_B2) * (gv * gv)
        nm_ref[...] = nm
        nv_ref[...] = nv
        d_ref[...] = -ADAM_LR * ((nm / c1) / (jnp.sqrt(nv / c2) + ADAM_EPS) + ADAM_WD * w_ref[...])

    blk = pl.BlockSpec((tr, C), lambda i: (i, 0))
    out = jax.ShapeDtypeStruct((R, C), F32)
    outs = pl.pallas_call(
        body,
        out_shape=(out, out, out),
        grid=(R // tr,),
        in_specs=[blk] * 4,
        out_specs=(blk, blk, blk),
        name=name,
        compiler_params=_cp(("parallel",)),
    )(*(t.reshape(R, C) for t in (w, g, m, v)))
    return tuple(t.reshape(shape) for t in outs)


def _exchange(send, axes, same, name):
    na = len(axes)
    n = 1 << na
    _, R, C = send.shape

    def body(send_ref, recv_ref, send_sems, recv_sems, local_sem):
        coords = {ax: lax.axis_index(ax) for ax in MESH_AXES}
        me = 0
        for ax in axes:
            me = me * 2 + coords[ax]

        def member(r):
            dev = dict(coords)
            for b, ax in enumerate(axes):
                if (r >> (na - 1 - b)) & 1:
                    dev[ax] = 1 - dev[ax]
            return tuple(dev[ax] for ax in MESH_AXES)

        def chunk(j):
            return send_ref.at[0] if same else send_ref.at[j]

        own = pltpu.make_async_copy(chunk(me), recv_ref.at[me], local_sem)
        own.start()
        copies = []
        for r in range(1, n):
            peer = me ^ r
            cp = pltpu.make_async_remote_copy(
                src_ref=chunk(peer), dst_ref=recv_ref.at[me], send_sem=send_sems.at[r], recv_sem=recv_sems.at[r],
                device_id=member(r), device_id_type=pl.DeviceIdType.MESH)
            cp.start()
            copies.append(cp)
        for r, cp in zip(range(1, n), copies):
            arrival = pltpu.make_async_remote_copy(
                src_ref=chunk(me), dst_ref=recv_ref.at[me ^ r], send_sem=send_sems.at[r], recv_sem=recv_sems.at[r],
                device_id=member(r), device_id_type=pl.DeviceIdType.MESH)
            arrival.wait_recv()
        for cp in copies:
            cp.wait_send()
        own.wait()

    any_spec = pl.BlockSpec(memory_space=pl.ANY)
    return pl.pallas_call(
        body,
        out_shape=jax.ShapeDtypeStruct((n, R, C), send.dtype),
        in_specs=[any_spec],
        out_specs=any_spec,
        scratch_shapes=[pltpu.SemaphoreType.DMA((n,)), pltpu.SemaphoreType.DMA((n,)), pltpu.SemaphoreType.DMA],
        name=name,
        compiler_params=pltpu.CompilerParams(has_side_effects=True),
    )(send)


def _sum_chunks(x, name):
    n, R, C = x.shape
    tr = _row_tile(R, 512)

    def body(x_ref, o_ref):
        acc = x_ref[0].astype(F32)
        for j in range(1, n):
            acc = acc + x_ref[j].astype(F32)
        o_ref[...] = acc

    return pl.pallas_call(
        body,
        out_shape=jax.ShapeDtypeStruct((R, C), F32),
        grid=(R // tr,),
        in_specs=[pl.BlockSpec((n, tr, C), lambda i: (0, i, 0))],
        out_specs=pl.BlockSpec((tr, C), lambda i: (i, 0)),
        name=name,
        compiler_params=_cp(("parallel",)),
    )(x)


def _pad_heads_cols(w, groups, d):
    k = w.shape[0]
    w = w.reshape(k, groups * N_HEADS, d)
    return jnp.pad(w, ((0, 0), (0, 0), (0, LANES - d))).reshape(k, groups * N_HEADS * LANES)


def _unpad_heads_cols(w, groups, d):
    k = w.shape[0]
    return w.reshape(k, groups * N_HEADS, LANES)[:, :, :d].reshape(k, groups * N_HEADS * d)


def _pad_heads_rows(w, d):
    n = w.shape[1]
    return jnp.pad(w.reshape(N_HEADS, d, n), ((0, 0), (0, LANES - d), (0, 0))).reshape(N_HEADS * LANES, n)


def _unpad_heads_rows(w, d):
    n = w.shape[1]
    return w.reshape(N_HEADS, LANES, n)[:, :d].reshape(N_HEADS * d, n)


def _kernel_weights(W):
    P = dict(W)
    pw = W["pool_w"][0]
    bd = jnp.zeros((D_MODEL, D_MODEL), pw.dtype)
    for g in range(len(POOL_WINDOWS)):
        bd = lax.dynamic_update_slice(bd, pw[g], (g * POOL_GROUP, g * POOL_GROUP))
    P["pool_bd"] = bd
    P["sb_qkv"] = _pad_heads_cols(W["sb_w_qkv"][0], 3, HEAD_DIM)
    P["sb_o"] = _pad_heads_rows(W["sb_w_o"][0], HEAD_DIM)
    nq = 3 * N_HEADS * HEAD_DIM
    P["fox_qkv"] = _pad_heads_cols(W["fox_w_qkvf"][0][:, :nq], 3, HEAD_DIM)
    P["fox_f"] = jnp.pad(W["fox_w_qkvf"][0][:, nq:], ((0, 0), (0, LANES - N_HEADS)))
    P["fox_o"] = _pad_heads_rows(W["fox_w_o"][0], HEAD_DIM)
    P["fox_b"] = jnp.pad(W["fox_b_f"], ((0, 0), (0, LANES - N_HEADS)))
    P["mla_down"] = jnp.pad(W["mla_w_down"][0], ((0, 0), (0, MLA_DOWN_PAD - W["mla_w_down"].shape[2])))
    P["mla_uq"] = _pad_heads_cols(W["mla_w_uq"][0], 1, MLA_NOPE + MLA_ROPE)
    ukv = W["mla_w_ukv"][0].reshape(MLA_KV_RANK, N_HEADS, 2 * HEAD_DIM)
    padk = ((0, 0), (0, 0), (0, LANES - HEAD_DIM))
    P["mla_ukv"] = jnp.concatenate(
        [jnp.pad(ukv[:, :, :MLA_NOPE], padk).reshape(MLA_KV_RANK, -1), jnp.pad(ukv[:, :, MLA_NOPE:], padk).reshape(MLA_KV_RANK, -1)], axis=1)
    P["mla_o"] = _pad_heads_rows(W["mla_w_o"][0], HEAD_DIM)
    return P


def _rope_tables(lp):
    pos = (jnp.arange(lp) - PAD0).astype(F32)
    inv = ROPE_THETA ** (-jnp.arange(0, MLA_ROPE, 2, dtype=F32) / MLA_ROPE)
    ang = pos[:, None] * inv[None, :]
    cos, sin = jnp.cos(ang), jnp.sin(ang)
    half = MLA_ROPE // 2
    z = lambda n: jnp.zeros((lp, n), F32)
    c = jnp.concatenate([jnp.ones((lp, MLA_NOPE), F32), cos, cos, z(LANES - MLA_NOPE - MLA_ROPE)], axis=1)
    a = jnp.concatenate([z(MLA_NOPE), -sin, z(LANES - MLA_NOPE - half)], axis=1)
    b = jnp.concatenate([z(MLA_NOPE + half), sin, z(LANES - MLA_NOPE - MLA_ROPE)], axis=1)
    return c, a, b


def _key_bias(lp, t, per_head=None):
    pad = jnp.arange(lp)[None, :] < PAD0
    body = jnp.zeros((N_HEADS, lp), F32) if per_head is None else per_head
    return jnp.where(pad, NEG, body).reshape(N_HEADS, lp // t, 1, t)


def _ffn_fwd(h, i, P):
    b = _rms_fwd(h, P["norm_ffn"][i], BF16, "ffn_norm")
    g = _mm(b, P["ffn_w_gate"][i], "nn", "ffn_gate")
    u = _mm(b, P["ffn_w_up"][i], "nn", "ffn_up")
    hd = _swiglu_fwd(g, u, "ffn_act")
    return _mm(hd, P["ffn_w_down"][i], "nn", "ffn_down", add=h), (h, b, g, u, hd)


def _ffn_bwd(dh, i, P, saved):
    h, b, g, u, hd = saved
    dwd = _mm(hd, dh, "tn", "ffn_down_dw")
    dhd = _mm(dh, P["ffn_w_down"][i], "nt", "ffn_down_dx")
    dg, du = _swiglu_bwd(g, u, dhd, "ffn_act_bwd")
    dwg = _mm(b, dg, "tn", "ffn_gate_dw")
    dwu = _mm(b, du, "tn", "ffn_up_dw")
    db = _mm(dg, P["ffn_w_gate"][i], "nt", "ffn_gate_dx")
    db = _mm(du, P["ffn_w_up"][i], "nt", "ffn_up_dx", add=db)
    dh_in, dgain = _rms_bwd(h, P["norm_ffn"][i], db, dh, "ffn_norm_bwd")
    return dh_in, dgain, dwg, dwu, dwd


def _pool_layer_fwd(h, P):
    a = _rms_fwd(h, P["norm_mix"][0], F32, "pool_norm")
    pooled = _pool_fwd(a, "pool_window")
    pre = _mm(pooled, P["pool_bd"], "nn", "pool_mix")
    return _scale_add(h, pre, P["pool_scale"][0], "pool_scale_add"), (h, pooled, pre)


def _pool_layer_bwd(dh, P, saved):
    h, pooled, pre = saved
    dpre, dscale = _scale_bwd(dh, pre, P["pool_scale"][0], "pool_scale_bwd")
    dbd = _mm(pooled, dpre, "tn", "pool_mix_dw")
    dpooled = _mm(dpre, P["pool_bd"], "nt", "pool_mix_dx")
    da = _pool_bwd(dpooled, "pool_window_bwd")
    dh_in, dgain = _rms_bwd(h, P["norm_mix"][0], da, dh, "mix_norm_bwd")
    dw = jnp.stack([dbd[g * POOL_GROUP:(g + 1) * POOL_GROUP, g * POOL_GROUP:(g + 1) * POOL_GROUP] for g in range(len(POOL_WINDOWS))])
    return dh_in, {"norm_mix0": dgain, "pool_w": dw[None], "pool_scale": dscale}


def _out_proj_bwd(o, dh, wo, tag):
    return _mm(o, dh, "tn", tag + "_o_dw"), _mm(dh, wo, "nt", tag + "_o_dx", out_dtype=BF16)


def _sb_layer_fwd(h, P):
    lp = h.shape[0]
    a = _rms_fwd(h, P["norm_mix"][1], BF16, "mix_norm")
    qkv = _mm(a, P["sb_qkv"], "nn", "sb_qkv", out_dtype=BF16)
    kb = _key_bias(lp, SB_TK)
    o = _sb_fwd(qkv, kb, nh=N_HEADS, name="sb_attn")
    return _mm(o, P["sb_o"], "nn", "attn_out", add=h), (h, a, qkv, kb, o)


def _sb_layer_bwd(dh, P, saved):
    h, a, qkv, kb, o = saved
    dwo, do = _out_proj_bwd(o, dh, P["sb_o"], "attn")
    dq, dk, dv = _sb_bwd(qkv, kb, do, nh=N_HEADS, name="sb_attn_bwd")
    dqkv = jnp.concatenate([dq, dk, dv], axis=1)
    dw = _mm(a, dqkv, "tn", "qkv_dw")
    da = _mm(dqkv, P["sb_qkv"], "nt", "qkv_dx")
    dh_in, dgain = _rms_bwd(h, P["norm_mix"][1], da, dh, "mix_norm_bwd")
    return dh_in, {"norm_mix1": dgain, "sb_w_qkv": _unpad_heads_cols(dw, 3, HEAD_DIM)[None], "sb_w_o": _unpad_heads_rows(dwo, HEAD_DIM)[None]}


def _fox_layer_fwd(h, P):
    lp = h.shape[0]
    t = _attn_tile(lp)
    a = _rms_fwd(h, P["norm_mix"][3], BF16, "mix_norm")
    qkv = _mm(a, P["fox_qkv"], "nn", "sb_qkv", out_dtype=BF16)
    f = _mm(a, P["fox_f"], "nn", "fox_gate_proj")
    fc = _gate_fwd(f, P["fox_b"], "fox_gate")[:, :N_HEADS]
    kb = _key_bias(lp, t, -fc.T)
    fq = jnp.broadcast_to(fc[:, :, None], (lp, N_HEADS, LANES)).reshape(lp, N_HEADS * LANES)
    o, lse = _attn_fwd(qkv, qkv, qkv, kb, fq, nh=N_HEADS, offs=(0, N_HEADS, 2 * N_HEADS), scale=HEAD_DIM ** -0.5, name="fox_attn")
    return _mm(o, P["fox_o"], "nn", "attn_out", add=h), (h, a, qkv, f, kb, fq, o, lse)


def _fox_layer_bwd(dh, P, saved):
    h, a, qkv, f, kb, fq, o, lse = saved
    lp = h.shape[0]
    dwo, do = _out_proj_bwd(o, dh, P["fox_o"], "attn")
    dq, dk, dv, dkb, dqb = _attn_bwd(qkv, qkv, qkv, kb, fq, o, do, lse, nh=N_HEADS, offs=(0, N_HEADS, 2 * N_HEADS),
                                     scale=HEAD_DIM ** -0.5, name="fox_attn_bwd")
    dfc = jnp.pad(dqb.reshape(lp, N_HEADS, LANES)[:, :, 0] - dkb.reshape(N_HEADS, lp).T, ((0, 0), (0, LANES - N_HEADS)))
    df, dbf = _gate_bwd(f, P["fox_b"], dfc, "fox_gate_bwd")
    dqkv = jnp.concatenate([dq, dk, dv], axis=1)
    dw = _mm(a, dqkv, "tn", "qkv_dw")
    dwf = _mm(a, df, "tn", "fox_gate_dw")
    da = _mm(dqkv, P["fox_qkv"], "nt", "qkv_dx")
    da = _mm(df, P["fox_f"], "nt", "fox_gate_dx", add=da)
    dh_in, dgain = _rms_bwd(h, P["norm_mix"][3], da, dh, "mix_norm_bwd")
    dwqkvf = jnp.concatenate([_unpad_heads_cols(dw, 3, HEAD_DIM), dwf[:, :N_HEADS]], axis=1)
    return dh_in, {"norm_mix3": dgain, "fox_w_qkvf": dwqkvf[None], "fox_b_f": dbf[:, :N_HEADS], "fox_w_o": _unpad_heads_rows(dwo, HEAD_DIM)[None]}


def _mla_layer_fwd(h, P):
    lp = h.shape[0]
    a = _rms_fwd(h, P["norm_mix"][2], BF16, "mix_norm")
    down = _mm(a, P["mla_down"], "nn", "mla_down")
    cq_pre = down[:, :MLA_Q_RANK]
    ckv_pre = down[:, MLA_Q_RANK:MLA_Q_RANK + MLA_KV_RANK]
    kr = jnp.pad(down[:, MLA_Q_RANK + MLA_KV_RANK:MLA_Q_RANK + MLA_KV_RANK + MLA_ROPE], ((0, 0), (MLA_NOPE, LANES - MLA_NOPE - MLA_ROPE)))
    cq = _rms_fwd(cq_pre, P["mla_q_norm"][0], BF16, "mla_q_norm")
    ckv = _rms_fwd(ckv_pre, P["mla_kv_norm"][0], BF16, "mla_kv_norm")
    q = _mm(cq, P["mla_uq"], "nn", "mla_uq")
    kv = _mm(ckv, P["mla_ukv"], "nn", "mla_ukv", out_dtype=BF16)
    tabs = _rope_tables(lp)
    qr, kc = _mla_prep_fwd(q, kv, kr, *tabs, "mla_rope")
    kb = _key_bias(lp, _attn_tile(lp))
    o, lse = _attn_fwd(qr, kc, kv, kb, None, nh=N_HEADS, offs=(0, 0, N_HEADS), scale=(MLA_NOPE + MLA_ROPE) ** -0.5, name="mla_attn")
    return _mm(o, P["mla_o"], "nn", "attn_out", add=h), (h, a, cq_pre, ckv_pre, cq, ckv, qr, kc, kv, tabs, kb, o, lse)


def _mla_layer_bwd(dh, P, saved):
    h, a, cq_pre, ckv_pre, cq, ckv, qr, kc, kv, tabs, kb, o, lse = saved
    lp = h.shape[0]
    dwo, do = _out_proj_bwd(o, dh, P["mla_o"], "attn")
    dqr, dkc, dv = _attn_bwd(qr, kc, kv, kb, None, o, do, lse, nh=N_HEADS, offs=(0, 0, N_HEADS),
                             scale=(MLA_NOPE + MLA_ROPE) ** -0.5, name="mla_attn_bwd")
    dq, dkr = _mla_prep_bwd(dqr, dkc, *tabs, "mla_rope_bwd")
    dkv = jnp.concatenate([dkc, dv], axis=1)
    dwuq = _mm(cq, dq, "tn", "mla_uq_dw")
    dcq = _mm(dq, P["mla_uq"], "nt", "mla_uq_dx")
    dwukv = _mm(ckv, dkv, "tn", "mla_ukv_dw")
    dckv = _mm(dkv, P["mla_ukv"], "nt", "mla_ukv_dx")
    dcq_pre, dqn = _rms_bwd(cq_pre, P["mla_q_norm"][0], dcq, None, "mla_q_norm_bwd")
    dckv_pre, dkvn = _rms_bwd(ckv_pre, P["mla_kv_norm"][0], dckv, None, "mla_kv_norm_bwd")
    used = MLA_Q_RANK + MLA_KV_RANK + MLA_ROPE
    ddown = jnp.concatenate([dcq_pre, dckv_pre, dkr[:, MLA_NOPE:MLA_NOPE + MLA_ROPE], jnp.zeros((lp, MLA_DOWN_PAD - used), F32)], axis=1)
    dwdown = _mm(a, ddown, "tn", "mla_down_dw")
    da = _mm(ddown, P["mla_down"], "nt", "mla_down_dx")
    dh_in, dgain = _rms_bwd(h, P["norm_mix"][2], da, dh, "mix_norm_bwd")
    dukv = dwukv.reshape(MLA_KV_RANK, 2, N_HEADS, LANES)[:, :, :, :HEAD_DIM]
    dukv = jnp.concatenate([dukv[:, 0], dukv[:, 1]], axis=-1).reshape(MLA_KV_RANK, N_HEADS * 2 * HEAD_DIM)
    return dh_in, {
        "norm_mix2": dgain, "mla_w_down": dwdown[:, :used][None], "mla_q_norm": dqn, "mla_kv_norm": dkvn,
        "mla_w_uq": _unpad_heads_cols(dwuq, 1, MLA_NOPE + MLA_ROPE)[None], "mla_w_ukv": dukv[None],
        "mla_w_o": _unpad_heads_rows(dwo, HEAD_DIM)[None]}


_MIXERS = ((_pool_layer_fwd, _pool_layer_bwd), (_sb_layer_fwd, _sb_layer_bwd), (_mla_layer_fwd, _mla_layer_bwd), (_fox_layer_fwd, _fox_layer_bwd))


def _step_local(x, target, W):
    seq = x.shape[0]
    P = _kernel_weights(W)
    h = jnp.concatenate([jnp.zeros((PAD0, D_MODEL), F32), W["meta"], x], axis=0)
    tpad = jnp.pad(target, ((PAD0 + N_META, 0), (0, 0)))
    saved = []
    for i in range(4):
        h, s_mix = _MIXERS[i][0](h, P)
        h, s_ffn = _ffn_fwd(h, i, P)
        saved.append((s_mix, s_ffn))
    loss, dh, dfinal = _loss_head(h, W["final_norm"], tpad, "loss_head")
    grads = {"final_norm": dfinal.reshape(-1)}
    gains_mix, gains_ffn, dwg, dwu, dwd = [None] * 4, [None] * 4, [None] * 4, [None] * 4, [None] * 4
    for i in reversed(range(4)):
        s_mix, s_ffn = saved[i]
        dh, gains_ffn[i], dwg[i], dwu[i], dwd[i] = _ffn_bwd(dh, i, P, s_ffn)
        dh, g = _MIXERS[i][1](dh, P, s_mix)
        gains_mix[i] = g.pop("norm_mix%d" % i)
        grads.update(g)
    grads["norm_mix"] = jnp.concatenate(gains_mix, axis=0)
    grads["norm_ffn"] = jnp.concatenate(gains_ffn, axis=0)
    grads["ffn_w_gate"] = jnp.stack(dwg)
    grads["ffn_w_up"] = jnp.stack(dwu)
    grads["ffn_w_down"] = jnp.stack(dwd)
    grads["meta"] = dh[PAD0:PAD0 + N_META]
    return loss, dh[PAD0 + N_META:], grads


_WEIGHTS = ("meta", "norm_mix", "norm_ffn", "pool_w", "pool_scale", "sb_w_qkv", "sb_w_o", "mla_w_down", "mla_q_norm",
            "mla_kv_norm", "mla_w_uq", "mla_w_ukv", "mla_w_o", "fox_w_qkvf", "fox_b_f", "fox_w_o", "ffn_w_gate",
            "ffn_w_up", "ffn_w_down", "final_norm")
_SHARD_AXIS = {"meta": 1, "pool_w": 2, "sb_w_qkv": 2, "sb_w_o": 1, "mla_w_down": 1, "mla_q_norm": 1, "mla_kv_norm": 1,
               "mla_w_uq": 2, "mla_w_ukv": 2, "mla_w_o": 1, "fox_w_qkvf": 2, "fox_b_f": None, "fox_w_o": 1,
               "ffn_w_gate": 2, "ffn_w_up": 2, "ffn_w_down": 1}
_SHARDED = tuple(n for n in _WEIGHTS if _SHARD_AXIS.get(n) is not None)
_REPLICATED = tuple(n for n in _WEIGHTS if _SHARD_AXIS.get(n) is None)
_EXACT = ("meta", "mla_q_norm", "mla_kv_norm")
N_CHIPS = 4
GRAD_ROW_TILE = 512


def _flat_rows(parts, dtype, row_multiple):
    flat = jnp.concatenate([p.astype(dtype).reshape(-1) for p in parts])
    rows = -(-flat.shape[0] // (LANES * row_multiple)) * row_multiple
    return jnp.pad(flat, (0, rows * LANES - flat.shape[0])).reshape(rows, LANES)


def _split_flat(flat, like):
    flat = flat.reshape(-1)
    out, off = [], 0
    for t in like:
        out.append(flat[off:off + t.size].reshape(t.shape))
        off += t.size
    return out


def _gather_shards(local, names, dtype, name):
    blocks = [local[n] for n in names]
    recv = _exchange(_flat_rows(blocks, dtype, 16)[None], ("x", "y"), True, name)
    per_chip = [_split_flat(recv[s], blocks) for s in range(N_CHIPS)]
    return {n: jnp.concatenate([per_chip[s][k] for s in range(N_CHIPS)], axis=_SHARD_AXIS[n]) for k, n in enumerate(names)}


def _shard_of(g, n, s):
    w = g.shape[_SHARD_AXIS[n]] // N_CHIPS
    return lax.slice_in_dim(g, s * w, (s + 1) * w, axis=_SHARD_AXIS[n])


def _train_step(a):
    local = {n: a[n] for n in _WEIGHTS}
    full = {n: local[n] for n in _REPLICATED}
    full.update(_gather_shards(local, [n for n in _SHARDED if n not in _EXACT], BF16, "gather_weights"))
    full.update(_gather_shards(local, list(_EXACT), F32, "gather_exact"))

    loss, grad_x, grads = _step_local(a["x"][0], a["loss_target"][0], full)

    send = jnp.stack([
        _flat_rows([_shard_of(grads[n], n, s) for n in _SHARDED], BF16, 2 * GRAD_ROW_TILE).reshape(2, -1, LANES)
        for s in range(N_CHIPS)]).reshape(2 * N_CHIPS, -1, LANES)
    mine = _sum_chunks(_exchange(send, MESH_AXES, False, "scatter_grads"), "sum_grads")
    both = _exchange(mine[None], ("c",), True, "pair_grads")
    reduced = dict(zip(_SHARDED, _split_flat(both, [local[n] for n in _SHARDED])))
    small = _flat_rows([grads[n] for n in _REPLICATED], F32, 8)
    small = _sum_chunks(_exchange(small[None], MESH_AXES, True, "gather_small_grads"), "sum_small_grads")
    reduced.update(zip(_REPLICATED, _split_flat(small, [local[n] for n in _REPLICATED])))

    deltas, new_m, new_v = {}, {}, {}
    for n in _WEIGHTS:
        deltas[n], new_m[n], new_v[n] = _adamw(local[n], reduced[n], a["m_" + n], a["v_" + n], "adamw")
    total = lax.psum(loss[0, 0], MESH_AXES)
    return (total, grad_x[None], *[reduced[n] for n in _WEIGHTS], *[deltas[n] for n in _WEIGHTS],
            *[new_m[n] for n in _WEIGHTS], *[new_v[n] for n in _WEIGHTS])


def kernel(x, meta, norm_mix, norm_ffn, pool_w, pool_scale, sb_w_qkv, sb_w_o, mla_w_down, mla_q_norm, mla_kv_norm, mla_w_uq, mla_w_ukv, mla_w_o, fox_w_qkvf, fox_b_f, fox_w_o, ffn_w_gate, ffn_w_up, ffn_w_down, final_norm, loss_target, m_meta, m_norm_mix, m_norm_ffn, m_pool_w, m_pool_scale, m_sb_w_qkv, m_sb_w_o, m_mla_w_down, m_mla_q_norm, m_mla_kv_norm, m_mla_w_uq, m_mla_w_ukv, m_mla_w_o, m_fox_w_qkvf, m_fox_b_f, m_fox_w_o, m_ffn_w_gate, m_ffn_w_up, m_ffn_w_down, m_final_norm, v_meta, v_norm_mix, v_norm_ffn, v_pool_w, v_pool_scale, v_sb_w_qkv, v_sb_w_o, v_mla_w_down, v_mla_q_norm, v_mla_kv_norm, v_mla_w_uq, v_mla_w_ukv, v_mla_w_o, v_fox_w_qkvf, v_fox_b_f, v_fox_w_o, v_ffn_w_gate, v_ffn_w_up, v_ffn_w_down, v_final_norm):
    return _train_step(dict(locals()))
```

```python
import functools

import jax
import jax.numpy as jnp
from jax import lax
from jax.experimental import pallas as pl
from jax.experimental.pallas import tpu as pltpu

F32 = jnp.float32
BF16 = jnp.bfloat16

D_MODEL = 1024
N_META = 16
PAD0 = 112
LANES = 128
N_HEADS = 16
HEAD_DIM = 64
POOL_WINDOWS = (2, 4, 8, 16)
POOL_GROUP = 256
POOL_HALO = 16
MLA_Q_RANK = 384
MLA_KV_RANK = 256
MLA_NOPE = 64
MLA_ROPE = 32
MLA_DOWN_PAD = 768
ROPE_THETA = 10000.0
D_FF = 2816
EPS = 1e-6
NEG = -1e30
ADAM_LR = 0.001
ADAM_B1 = 0.9
ADAM_B2 = 0.999
ADAM_EPS = 1e-08
ADAM_WD = 0.01
ADAM_STEP = 10
VMEM_LIMIT = 56 * 1024 * 1024
MESH_AXES = ("x", "y", "c")


def _cp(sem, **kw):
    return pltpu.CompilerParams(dimension_semantics=sem, vmem_limit_bytes=VMEM_LIMIT, **kw)


def _row_tile(m, target):
    best = None
    for t in range(16, min(m, target) + 1, 16):
        if m % t == 0:
            best = t
    return best or m


def _col_tile(n, target):
    best = None
    for t in range(LANES, min(n, target) + 1, LANES):
        if n % t == 0:
            best = t
    return best or n


def _mm(a, b, mode, name, out_dtype=F32, add=None, tm=640, tn=512, tk=2048):
    if mode == "nn":
        (M, K), (K2, N) = a.shape, b.shape
    elif mode == "nt":
        (M, K), (N, K2) = a.shape, b.shape
    else:
        (K, M), (K2, N) = a.shape, b.shape
    assert K == K2, (mode, a.shape, b.shape)
    if mode == "tn":
        tm_ = _col_tile(M, 1408)
        tk_ = _row_tile(K, 640)
    else:
        tm_ = _row_tile(M, tm)
        tk_ = _col_tile(K, tk) if K > tk else K
    tn_ = _col_tile(N, tn)
    nk = K // tk_
    if mode == "nn":
        a_spec = pl.BlockSpec((tm_, tk_), lambda i, j, k: (i, k))
        b_spec = pl.BlockSpec((tk_, tn_), lambda i, j, k: (k, j))
        dims = (((1,), (0,)), ((), ()))
    elif mode == "nt":
        a_spec = pl.BlockSpec((tm_, tk_), lambda i, j, k: (i, k))
        b_spec = pl.BlockSpec((tn_, tk_), lambda i, j, k: (j, k))
        dims = (((1,), (1,)), ((), ()))
    else:
        a_spec = pl.BlockSpec((tk_, tm_), lambda i, j, k: (k, i))
        b_spec = pl.BlockSpec((tk_, tn_), lambda i, j, k: (k, j))
        dims = (((0,), (0,)), ((), ()))
    o_spec = pl.BlockSpec((tm_, tn_), lambda i, j, k: (i, j))
    has_add = add is not None

    def body(*refs):
        if has_add:
            a_ref, b_ref, add_ref, o_ref, acc_ref = refs
        else:
            a_ref, b_ref, o_ref, acc_ref = refs
        k = pl.program_id(2)
        part = lax.dot_general(a_ref[...].astype(BF16), b_ref[...].astype(BF16), dims, preferred_element_type=F32)

        @pl.when(k == 0)
        def _():
            acc_ref[...] = part

        @pl.when(k > 0)
        def _():
            acc_ref[...] += part

        @pl.when(k == nk - 1)
        def _():
            r = acc_ref[...]
            if has_add:
                r = r + add_ref[...]
            o_ref[...] = r.astype(o_ref.dtype)

    ins = [a, b] + ([add] if has_add else [])
    in_specs = [a_spec, b_spec] + ([o_spec] if has_add else [])
    return pl.pallas_call(
        body,
        out_shape=jax.ShapeDtypeStruct((M, N), out_dtype),
        grid=(M // tm_, N // tn_, nk),
        in_specs=in_specs,
        out_specs=o_spec,
        scratch_shapes=[pltpu.VMEM((tm_, tn_), F32)],
        name=name,
        compiler_params=_cp(("parallel", "parallel", "arbitrary")),
    )(*ins)


def _rms_fwd(x, g, out_dtype, name):
    M, C = x.shape
    tm = _row_tile(M, 640)

    def body(x_ref, g_ref, o_ref):
        xf = x_ref[...]
        r = lax.rsqrt(jnp.mean(xf * xf, axis=-1, keepdims=True) + EPS)
        o_ref[...] = ((xf * r) * g_ref[...]).astype(o_ref.dtype)

    return pl.pallas_call(
        body,
        out_shape=jax.ShapeDtypeStruct((M, C), out_dtype),
        grid=(M // tm,),
        in_specs=[pl.BlockSpec((tm, C), lambda i: (i, 0)), pl.BlockSpec((1, C), lambda i: (0, 0))],
        out_specs=pl.BlockSpec((tm, C), lambda i: (i, 0)),
        name=name,
        compiler_params=_cp(("parallel",)),
    )(x, g.reshape(1, C))


def _rms_bwd(x, g, dy, dres, name):
    M, C = x.shape
    tm = _row_tile(M, 640)
    has_res = dres is not None

    def body(*refs):
        if has_res:
            x_ref, g_ref, dy_ref, dres_ref, dx_ref, dg_ref = refs
        else:
            x_ref, g_ref, dy_ref, dx_ref, dg_ref = refs
        xf = x_ref[...]
        r = lax.rsqrt(jnp.mean(xf * xf, axis=-1, keepdims=True) + EPS)
        xhat = xf * r
        dyf = dy_ref[...].astype(F32)

        @pl.when(pl.program_id(0) == 0)
        def _():
            dg_ref[...] = jnp.zeros_like(dg_ref)

        dg_ref[...] += jnp.sum(dyf * xhat, axis=0, keepdims=True)
        dxh = dyf * g_ref[...]
        dx = r * (dxh - xhat * jnp.mean(dxh * xhat, axis=-1, keepdims=True))
        if has_res:
            dx = dx + dres_ref[...]
        dx_ref[...] = dx

    row = pl.BlockSpec((tm, C), lambda i: (i, 0))
    vec = pl.BlockSpec((1, C), lambda i: (0, 0))
    ins = [x, g.reshape(1, C), dy] + ([dres] if has_res else [])
    return pl.pallas_call(
        body,
        out_shape=(jax.ShapeDtypeStruct((M, C), F32), jax.ShapeDtypeStruct((1, C), F32)),
        grid=(M // tm,),
        in_specs=[row, vec, row] + ([row] if has_res else []),
        out_specs=(row, vec),
        name=name,
        compiler_params=_cp(("arbitrary",)),
    )(*ins)


def _sigmoid(x):
    return 1.0 / (1.0 + jnp.exp(-x))


def _swiglu_fwd(g, u, name):
    M, F = g.shape
    tm = _row_tile(M, 320)

    def body(g_ref, u_ref, o_ref):
        gv = g_ref[...]
        o_ref[...] = ((gv * _sigmoid(gv)) * u_ref[...]).astype(o_ref.dtype)

    blk = pl.BlockSpec((tm, F), lambda i: (i, 0))
    return pl.pallas_call(
        body,
        out_shape=jax.ShapeDtypeStruct((M, F), BF16),
        grid=(M // tm,),
        in_specs=[blk, blk],
        out_specs=blk,
        name=name,
        compiler_params=_cp(("parallel",)),
    )(g, u)


def _swiglu_bwd(g, u, dh, name):
    M, F = g.shape
    tm = _row_tile(M, 320)

    def body(g_ref, u_ref, dh_ref, dg_ref, du_ref):
        gv = g_ref[...]
        sg = _sigmoid(gv)
        d = dh_ref[...]
        du_ref[...] = (d * (gv * sg)).astype(du_ref.dtype)
        dg_ref[...] = ((d * u_ref[...]) * (sg * (1.0 + gv * (1.0 - sg)))).astype(dg_ref.dtype)

    blk = pl.BlockSpec((tm, F), lambda i: (i, 0))
    return pl.pallas_call(
        body,
        out_shape=(jax.ShapeDtypeStruct((M, F), BF16), jax.ShapeDtypeStruct((M, F), BF16)),
        grid=(M // tm,),
        in_specs=[blk, blk, blk],
        out_specs=(blk, blk),
        name=name,
        compiler_params=_cp(("parallel",)),
    )(g, u, dh)


def _attn_tile(lp):
    return _col_tile(lp, 640)


def _head_specs(lp, t, offs):
    q_spec = pl.BlockSpec((t, LANES), lambda h, i: (i, offs[0] + h))
    k_spec = pl.BlockSpec((lp, LANES), lambda h, i: (0, offs[1] + h))
    v_spec = pl.BlockSpec((lp, LANES), lambda h, i: (0, offs[2] + h))
    return q_spec, k_spec, v_spec


def _attn_fwd(qa, ka, va, kb, fq, *, nh, offs, scale, name, tile=640):
    lp = qa.shape[0]
    t = _col_tile(lp, tile)
    nq = lp // t
    has_fq = fq is not None

    def body(*refs):
        if has_fq:
            q_ref, k_ref, v_ref, kb_ref, fq_ref, o_ref, lse_ref = refs
        else:
            q_ref, k_ref, v_ref, kb_ref, o_ref, lse_ref = refs
        i = pl.program_id(1)
        q = q_ref[...]
        fqc = fq_ref[:, 0:1] if has_fq else None
        causal = lax.broadcasted_iota(jnp.int32, (t, t), 1) <= lax.broadcasted_iota(jnp.int32, (t, t), 0)

        def step(j, carry, masked):
            m, l, acc = carry
            st = pl.multiple_of(j * t, t)
            k = k_ref[pl.ds(st, t), :]
            v = v_ref[pl.ds(st, t), :]
            s = lax.dot_general(q, k, (((1,), (1,)), ((), ())), preferred_element_type=F32) * scale
            bias = kb_ref[j]
            if has_fq:
                bias = fqc + bias
            s = s + bias
            if masked:
                s = jnp.where(causal, s, NEG)
            m_new = jnp.maximum(m, jnp.max(s, axis=1, keepdims=True))
            p = jnp.exp(s - m_new)
            alpha = jnp.exp(m - m_new)
            l = alpha * l + jnp.sum(p, axis=1, keepdims=True)
            acc = alpha * acc + jnp.dot(p.astype(BF16), v, preferred_element_type=F32)
            return m_new, l, acc

        init = (jnp.full((t, 1), NEG, F32), jnp.zeros((t, 1), F32), jnp.zeros((t, LANES), F32))
        carry = lax.fori_loop(0, i, lambda j, c: step(j, c, False), init)
        m, l, acc = step(i, carry, True)
        valid = (i * t + lax.broadcasted_iota(jnp.int32, (t, 1), 0)) >= PAD0
        o_ref[...] = jnp.where(valid, acc / l, 0.0).astype(o_ref.dtype)
        lse_ref[...] = jnp.broadcast_to(m + jnp.log(l), (t, LANES))

    q_spec, k_spec, v_spec = _head_specs(lp, t, offs)
    kb_spec = pl.BlockSpec((None, nq, 1, t), lambda h, i: (h, 0, 0, 0))
    row_spec = pl.BlockSpec((t, LANES), lambda h, i: (i, h))
    ins = [qa, ka, va, kb] + ([fq] if has_fq else [])
    return pl.pallas_call(
        body,
        out_shape=(jax.ShapeDtypeStruct((lp, nh * LANES), BF16), jax.ShapeDtypeStruct((lp, nh * LANES), F32)),
        grid=(nh, nq),
        in_specs=[q_spec, k_spec, v_spec, kb_spec] + ([row_spec] if has_fq else []),
        out_specs=(row_spec, row_spec),
        name=name,
        compiler_params=_cp(("parallel", "arbitrary")),
    )(*ins)


def _attn_bwd(qa, ka, va, kb, fq, o, do, lse, *, nh, offs, scale, name, tile=640):
    lp = qa.shape[0]
    t = _col_tile(lp, tile)
    nq = lp // t
    has_fq = fq is not None

    def body(*refs):
        if has_fq:
            q_ref, k_ref, v_ref, kb_ref, fq_ref, o_ref, do_ref, lse_ref, dq_ref, dk_ref, dv_ref, dkb_ref, dqb_ref, dk_acc, dv_acc = refs
        else:
            q_ref, k_ref, v_ref, kb_ref, o_ref, do_ref, lse_ref, dq_ref, dk_ref, dv_ref, dk_acc, dv_acc = refs
        i = pl.program_id(1)

        @pl.when(i == 0)
        def _():
            dk_acc[...] = jnp.zeros_like(dk_acc)
            dv_acc[...] = jnp.zeros_like(dv_acc)
            if has_fq:
                dkb_ref[...] = jnp.zeros_like(dkb_ref)

        q = q_ref[...]
        dov = do_ref[...]
        delta = jnp.sum(o_ref[...].astype(F32) * dov.astype(F32), axis=1, keepdims=True)
        lse_c = lse_ref[:, 0:1]
        fqc = fq_ref[:, 0:1] if has_fq else None
        causal = lax.broadcasted_iota(jnp.int32, (t, t), 1) <= lax.broadcasted_iota(jnp.int32, (t, t), 0)

        def step(j, carry, masked):
            dq_acc, rs = carry
            st = pl.multiple_of(j * t, t)
            k = k_ref[pl.ds(st, t), :]
            v = v_ref[pl.ds(st, t), :]
            s = lax.dot_general(q, k, (((1,), (1,)), ((), ())), preferred_element_type=F32) * scale
            bias = kb_ref[j]
            if has_fq:
                bias = fqc + bias
            s = s + bias
            if masked:
                s = jnp.where(causal, s, NEG)
            p = jnp.exp(s - lse_c)
            dp = lax.dot_general(dov, v, (((1,), (1,)), ((), ())), preferred_element_type=F32)
            ds = p * (dp - delta)
            dv_acc[pl.ds(st, t), :] += lax.dot_general(p.astype(BF16), dov, (((0,), (0,)), ((), ())), preferred_element_type=F32)
            dsb = (ds * scale).astype(BF16)
            dk_acc[pl.ds(st, t), :] += lax.dot_general(dsb, q, (((0,), (0,)), ((), ())), preferred_element_type=F32)
            if has_fq:
                dkb_ref[j] += jnp.sum(ds, axis=0, keepdims=True)
                rs = rs + jnp.sum(ds, axis=1, keepdims=True)
            return dq_acc + jnp.dot(dsb, k, preferred_element_type=F32), rs

        carry = (jnp.zeros((t, LANES), F32), jnp.zeros((t, 1), F32))
        carry = lax.fori_loop(0, i, lambda j, c: step(j, c, False), carry)
        dq_acc, rs = step(i, carry, True)
        dq_ref[...] = dq_acc.astype(dq_ref.dtype)
        if has_fq:
            dqb_ref[...] = jnp.broadcast_to(rs, (t, LANES))

        @pl.when(i == nq - 1)
        def _():
            dk_ref[...] = dk_acc[...].astype(dk_ref.dtype)
            dv_ref[...] = dv_acc[...].astype(dv_ref.dtype)

    q_spec, k_spec, v_spec = _head_specs(lp, t, offs)
    kb_spec = pl.BlockSpec((None, nq, 1, t), lambda h, i: (h, 0, 0, 0))
    row_spec = pl.BlockSpec((t, LANES), lambda h, i: (i, h))
    col_spec = pl.BlockSpec((lp, LANES), lambda h, i: (0, h))
    ins = [qa, ka, va, kb] + ([fq] if has_fq else []) + [o, do, lse]
    wide = jax.ShapeDtypeStruct((lp, nh * LANES), BF16)
    extra_shapes = (jax.ShapeDtypeStruct(kb.shape, F32), jax.ShapeDtypeStruct((lp, nh * LANES), F32)) if has_fq else ()
    extra_specs = (kb_spec, row_spec) if has_fq else ()
    return pl.pallas_call(
        body,
        out_shape=(wide, wide, wide) + extra_shapes,
        grid=(nh, nq),
        in_specs=[q_spec, k_spec, v_spec, kb_spec] + ([row_spec] if has_fq else []) + [row_spec, row_spec, row_spec],
        out_specs=(row_spec, col_spec, col_spec) + extra_specs,
        scratch_shapes=[pltpu.VMEM((lp, LANES), F32), pltpu.VMEM((lp, LANES), F32)],
        name=name,
        compiler_params=_cp(("parallel", "arbitrary")),
    )(*ins)


SB_TK = 128


def _split3(x):
    hi = x.astype(BF16)
    r1 = x - hi.astype(F32)
    mid = r1.astype(BF16)
    lo = (r1 - mid.astype(F32)).astype(BF16)
    return hi, mid, lo


SB_RC = 128


_NT = (((1,), (1,)), ((), ()))
_TN = (((0,), (0,)), ((), ()))


def _sb_logits(zraw, kbj, mask):
    z = zraw * (HEAD_DIM ** -0.5) + kbj
    if mask is not None:
        z = jnp.where(mask, z, NEG)
    e = jnp.exp(-jnp.abs(z))
    g = jnp.minimum(z, 0.0) - jnp.log(1.0 + e)
    lk = g - z
    return z, e, g, lk


def _dot3_parts(parts, tri):
    d = functools.partial(jnp.dot, preferred_element_type=F32)
    return d(parts[0], tri) + d(parts[1], tri) + d(parts[2], tri)


def _split2(x):
    hi = x.astype(BF16)
    return hi, (x - hi.astype(F32)).astype(BF16)


def _dot_parts(parts, m):
    out = jnp.dot(parts[0], m, preferred_element_type=F32)
    for p in parts[1:]:
        out = out + jnp.dot(p, m, preferred_element_type=F32)
    return out


def _tri(n, pred):
    return pred(lax.broadcasted_iota(jnp.int32, (n, n), 0), lax.broadcasted_iota(jnp.int32, (n, n), 1)).astype(BF16)


def _sb_diag_chunks(jj, nrc, rc, tk):
    plan = []
    for r in range(nrc):
        lo_row, hi_row = r * rc, (r + 1) * rc - 1
        lo_col, hi_col = jj * tk, (jj + 1) * tk - 1
        if hi_row <= lo_col:
            plan.append(None)
        elif lo_row > hi_col:
            plan.append("all")
        else:
            plan.append(lo_col - lo_row)
    return plan


def _sb_fwd(qa, kb, *, nh, name, tq=640):
    lp = qa.shape[0]
    tq = _col_tile(lp, tq)
    tk = SB_TK
    rc = min(SB_RC, tq)
    nq, sub, nrc = lp // tq, tq // tk, tq // rc

    def body(q_ref, k_ref, v_ref, kb_ref, o_ref, c_scr, acc_scr):
        i = pl.program_id(1)
        c_scr[...] = jnp.zeros_like(c_scr)
        acc_scr[...] = jnp.zeros_like(acc_scr)
        tri = _tri(tk, lambda r, c: r > c)
        row_io = lax.broadcasted_iota(jnp.int32, (rc, tk), 0)
        col_io = lax.broadcasted_iota(jnp.int32, (rc, tk), 1)

        def scores(j, rows):
            k = k_ref[pl.ds(pl.multiple_of(j * tk, tk), tk), :]
            return [lax.dot_general(q_ref[rs, :], k, _NT, preferred_element_type=F32) for rs in rows]

        def weights(j, rows, masks, zs):
            kbj = kb_ref[j]
            gs, splits, firsts = [], [], []
            for mask, zraw in zip(masks, zs):
                _, _, g, lk = _sb_logits(zraw, kbj, mask)
                gs.append(g)
                firsts.append(lk[:, 0:1])
                splits.append(_split2(lk))
            sums = [_dot_parts(p, tri) for p in splits]
            avs = [jnp.exp(g + (sm + c_scr[rs, :])).astype(BF16) for g, sm, rs in zip(gs, sums, rows)]
            for rs, sm, first in zip(rows, sums, firsts):
                c_scr[rs, :] += jnp.broadcast_to(sm[:, 0:1] + first, (rc, tk))
            return avs

        def values(j, rows, avs):
            v = v_ref[pl.ds(pl.multiple_of(j * tk, tk), tk), :]
            pvs = [jnp.dot(a, v, preferred_element_type=F32) for a in avs]
            for rs, pv in zip(rows, pvs):
                acc_scr[rs, :] += pv

        for jj in reversed(range(sub)):
            plan = _sb_diag_chunks(jj, nrc, rc, tk)
            live = [r for r, what in enumerate(plan) if what is not None]
            rows = [pl.ds(r * rc, rc) for r in live]
            masks = [None if plan[r] == "all" else (col_io + plan[r]) < row_io for r in live]
            j = i * sub + jj
            values(j, rows, weights(j, rows, masks, scores(j, rows)))

        n = i * sub
        rows = [pl.ds(r * rc, rc) for r in range(nrc)]
        nomask = [None] * nrc
        blk = lambda t: jnp.maximum(n - 1 - t, 0)

        def left(m, carry):
            zs, avs = carry
            t0, t1 = 2 * m, 2 * m + 1
            zs_next = scores(blk(t1 + 1), rows)
            zs1 = scores(blk(t1), rows)
            values(blk(t0 - 1), rows, avs)
            av0 = weights(blk(t0), rows, nomask, zs)
            av1 = weights(blk(t1), rows, nomask, zs1)
            values(blk(t0), rows, av0)
            return zs_next, av1

        first = (scores(blk(0), rows), [jnp.zeros((rc, tk), BF16)] * nrc)
        zs, avs = lax.fori_loop(0, n // 2, left, first)
        values(blk(2 * (n // 2) - 1), rows, avs)

        @pl.when(n % 2 == 1)
        def _():
            values(0, rows, weights(0, rows, nomask, zs))

        o_ref[...] = acc_scr[...].astype(o_ref.dtype)

    q_spec, k_spec, v_spec = _head_specs(lp, tq, (0, nh, 2 * nh))
    kb_spec = pl.BlockSpec((None, lp // tk, 1, tk), lambda h, i: (h, 0, 0, 0))
    row_spec = pl.BlockSpec((tq, LANES), lambda h, i: (i, h))
    return pl.pallas_call(
        body,
        out_shape=jax.ShapeDtypeStruct((lp, nh * LANES), BF16),
        grid=(nh, nq),
        in_specs=[q_spec, k_spec, v_spec, kb_spec],
        out_specs=row_spec,
        scratch_shapes=[pltpu.VMEM((tq, tk), F32), pltpu.VMEM((tq, LANES), F32)],
        name=name,
        compiler_params=_cp(("parallel", "arbitrary")),
    )(qa, qa, qa, kb)


def _sb_bwd(qa, kb, do, *, nh, name, tq=640):
    lp = qa.shape[0]
    tq = _col_tile(lp, tq)
    tk = SB_TK
    rc = min(SB_RC, tq)
    nq, nk, sub, nrc = lp // tq, lp // tk, tq // tk, tq // rc
    scale = HEAD_DIM ** -0.5

    def body(q_ref, k_ref, v_ref, kb_ref, do_ref, dq_ref, dk_ref, dv_ref, dkt_acc, dvt_acc, w_scr, b_scr, c_scr, u_scr, dq_scr):
        i = pl.program_id(1)

        @pl.when(i == 0)
        def _():
            dkt_acc[...] = jnp.zeros_like(dkt_acc)
            dvt_acc[...] = jnp.zeros_like(dvt_acc)

        c_scr[...] = jnp.zeros_like(c_scr)
        u_scr[...] = jnp.zeros_like(u_scr)
        dq_scr[...] = jnp.zeros_like(dq_scr)
        tri_gt = _tri(tk, lambda r, c: r > c)
        tri_lt = _tri(tk, lambda r, c: r < c)
        row_io = lax.broadcasted_iota(jnp.int32, (rc, tk), 0)
        col_io = lax.broadcasted_iota(jnp.int32, (rc, tk), 1)
        qt = q_ref[...].astype(F32).T.astype(BF16)
        dot_t = do_ref[...].astype(F32).T.astype(BF16)
        zero_blk = jnp.zeros((rc, tk), BF16)

        def full_rows(plan, parts):
            it = iter(parts)
            return jnp.concatenate([zero_blk if what is None else next(it) for what in plan], axis=0)

        def key_rows(j):
            return pl.ds(pl.multiple_of(j * tk, tk), tk)

        def scores(j, rows):
            k = k_ref[key_rows(j), :]
            v = v_ref[key_rows(j), :]
            zs = [lax.dot_general(q_ref[rs, :], k, _NT, preferred_element_type=F32) for rs in rows]
            das = [lax.dot_general(do_ref[rs, :], v, _NT, preferred_element_type=F32) for rs in rows]
            return zs, das

        def weights(j, rows, masks, zs, das):
            kbj = kb_ref[j]
            gs, splits, firsts = [], [], []
            for rs, mask, zraw in zip(rows, masks, zs):
                z, e, g, lk = _sb_logits(zraw, kbj, mask)
                b_scr[j, rs, :] = (jnp.where(z >= 0.0, 1.0, e) / (1.0 + e)).astype(BF16)
                gs.append(g)
                firsts.append(lk[:, 0:1])
                splits.append(_split2(lk))
            sums = [_dot_parts(p, tri_gt) for p in splits]
            avs = []
            for rs, g, sm, da, first in zip(rows, gs, sums, das, firsts):
                a = jnp.exp(g + (sm + c_scr[rs, :]))
                w_scr[j, rs, :] = (a * da).astype(BF16)
                avs.append(a.astype(BF16))
                c_scr[rs, :] += jnp.broadcast_to(sm[:, 0:1] + first, (rc, tk))
            return avs

        def dv_update(j, plan, avs):
            dvt_acc[j] += jnp.dot(dot_t, full_rows(plan, avs), preferred_element_type=F32)

        for jj in reversed(range(sub)):
            plan = _sb_diag_chunks(jj, nrc, rc, tk)
            live = [r for r, what in enumerate(plan) if what is not None]
            rows = [pl.ds(r * rc, rc) for r in live]
            masks = [None if plan[r] == "all" else (col_io + plan[r]) < row_io for r in live]
            j = i * sub + jj
            dv_update(j, plan, weights(j, rows, masks, *scores(j, rows)))

        n = i * sub
        everything = ["all"] * nrc
        rows = [pl.ds(r * rc, rc) for r in range(nrc)]
        nomask = [None] * nrc
        def left1(m, carry):
            j0, j1 = n - 1 - 2 * m, n - 2 - 2 * m
            sc0, sc1 = scores(j0, rows), scores(j1, rows)
            av0 = weights(j0, rows, nomask, *sc0)
            av1 = weights(j1, rows, nomask, *sc1)
            dv_update(j0, everything, av0)
            dv_update(j1, everything, av1)
            return carry

        lax.fori_loop(0, n // 2, left1, 0)

        @pl.when(n % 2 == 1)
        def _():
            dv_update(0, everything, weights(0, rows, nomask, *scores(0, rows)))

        def prefix(j, rows):
            return [jnp.dot(w_scr[j, rs, :], tri_lt, preferred_element_type=F32) for rs in rows]

        def dlogits(j, rows, sums):
            dzs = []
            for rs, sm in zip(rows, sums):
                w = w_scr[j, rs, :].astype(F32)
                beta = b_scr[j, rs, :].astype(F32)
                dzs.append(((w * (1.0 - beta) - beta * (sm + u_scr[rs, :])) * scale).astype(BF16))
                u_scr[rs, :] += jnp.broadcast_to(sm[:, tk - 1:tk] + w[:, tk - 1:tk], (rc, tk))
            return dzs

        def dqk_update(j, plan, rows, dzs):
            k = k_ref[key_rows(j), :]
            dqs = [jnp.dot(dz, k, preferred_element_type=F32) for dz in dzs]
            for rs, dq in zip(rows, dqs):
                dq_scr[rs, :] += dq
            dkt_acc[j] += jnp.dot(qt, full_rows(plan, dzs), preferred_element_type=F32)

        def left2(m, carry):
            j0, j1 = 2 * m, 2 * m + 1
            s0, s1 = prefix(j0, rows), prefix(j1, rows)
            dz0 = dlogits(j0, rows, s0)
            dz1 = dlogits(j1, rows, s1)
            dqk_update(j0, everything, rows, dz0)
            dqk_update(j1, everything, rows, dz1)
            return carry

        lax.fori_loop(0, n // 2, left2, 0)

        @pl.when(n % 2 == 1)
        def _():
            dqk_update(n - 1, everything, rows, dlogits(n - 1, rows, prefix(n - 1, rows)))
        for jj in range(sub):
            plan = _sb_diag_chunks(jj, nrc, rc, tk)
            live_rows = [pl.ds(r * rc, rc) for r, what in enumerate(plan) if what is not None]
            j = i * sub + jj
            dqk_update(j, plan, live_rows, dlogits(j, live_rows, prefix(j, live_rows)))
        dq_ref[...] = dq_scr[...].astype(dq_ref.dtype)

        @pl.when(i == nq - 1)
        def _():
            def flush(j, carry):
                dk_ref[key_rows(j), :] = dkt_acc[j].T.astype(dk_ref.dtype)
                dv_ref[key_rows(j), :] = dvt_acc[j].T.astype(dv_ref.dtype)
                return carry

            lax.fori_loop(0, nk, flush, 0)

    q_spec, k_spec, v_spec = _head_specs(lp, tq, (0, nh, 2 * nh))
    kb_spec = pl.BlockSpec((None, nk, 1, tk), lambda h, i: (h, 0, 0, 0))
    row_spec = pl.BlockSpec((tq, LANES), lambda h, i: (i, h))
    col_spec = pl.BlockSpec((lp, LANES), lambda h, i: (0, h))
    wide = jax.ShapeDtypeStruct((lp, nh * LANES), BF16)
    return pl.pallas_call(
        body,
        out_shape=(wide, wide, wide),
        grid=(nh, nq),
        in_specs=[q_spec, k_spec, v_spec, kb_spec, row_spec],
        out_specs=(row_spec, col_spec, col_spec),
        scratch_shapes=[
            pltpu.VMEM((nk, LANES, tk), F32),
            pltpu.VMEM((nk, LANES, tk), F32),
            pltpu.VMEM((nk, tq, tk), BF16),
            pltpu.VMEM((nk, tq, tk), BF16),
            pltpu.VMEM((tq, tk), F32),
            pltpu.VMEM((tq, tk), F32),
            pltpu.VMEM((tq, LANES), F32),
        ],
        name=name,
        compiler_params=_cp(("parallel", "arbitrary")),
    )(qa, qa, qa, kb, do)


def _pool_counts(pos, win):
    return jnp.clip(pos + 1, 1, win).astype(F32)


def _pool_fwd(a, name):
    lp, C = a.shape
    tm = _row_tile(lp, 640)
    hb = tm // POOL_HALO

    def body(prev_ref, cur_ref, o_ref, xs):
        i = pl.program_id(0)
        xs[pl.ds(0, POOL_HALO), :] = jnp.where(i > 0, prev_ref[...], 0.0)
        xs[pl.ds(POOL_HALO, tm), :] = cur_ref[...]
        pos = i * tm + lax.broadcasted_iota(jnp.int32, (tm, 1), 0) - PAD0
        for g, win in enumerate(POOL_WINDOWS):
            cols = pl.ds(g * POOL_GROUP, POOL_GROUP)
            s = xs[pl.ds(POOL_HALO, tm), cols]
            for k in range(1, win):
                s = s + xs[pl.ds(POOL_HALO - k, tm), cols]
            o_ref[:, cols] = (s / _pool_counts(pos, win) - xs[pl.ds(POOL_HALO, tm), cols]).astype(o_ref.dtype)

    return pl.pallas_call(
        body,
        out_shape=jax.ShapeDtypeStruct((lp, C), BF16),
        grid=(lp // tm,),
        in_specs=[
            pl.BlockSpec((POOL_HALO, C), lambda i: (jnp.maximum(i * hb - 1, 0), 0)),
            pl.BlockSpec((tm, C), lambda i: (i, 0)),
        ],
        out_specs=pl.BlockSpec((tm, C), lambda i: (i, 0)),
        scratch_shapes=[pltpu.VMEM((tm + POOL_HALO, C), F32)],
        name=name,
        compiler_params=_cp(("parallel",)),
    )(a, a)


def _pool_bwd(dp, name):
    lp, C = dp.shape
    tm = _row_tile(lp, 640)
    hb = tm // POOL_HALO
    nt = lp // tm
    last_halo = lp // POOL_HALO - 1

    def body(cur_ref, next_ref, o_ref, xs):
        i = pl.program_id(0)
        pos = i * tm + lax.broadcasted_iota(jnp.int32, (tm, 1), 0) - PAD0
        pos_h = (i + 1) * tm + lax.broadcasted_iota(jnp.int32, (POOL_HALO, 1), 0) - PAD0
        for g, win in enumerate(POOL_WINDOWS):
            cols = pl.ds(g * POOL_GROUP, POOL_GROUP)
            cur = cur_ref[:, cols]
            xs[pl.ds(0, tm), cols] = cur / _pool_counts(pos, win)
            xs[pl.ds(tm, POOL_HALO), cols] = jnp.where(i < nt - 1, next_ref[:, cols], 0.0) / _pool_counts(pos_h, win)
            s = xs[pl.ds(0, tm), cols]
            for k in range(1, win):
                s = s + xs[pl.ds(k, tm), cols]
            o_ref[:, cols] = jnp.where(pos >= 0, s - cur, 0.0)

    return pl.pallas_call(
        body,
        out_shape=jax.ShapeDtypeStruct((lp, C), F32),
        grid=(nt,),
        in_specs=[
            pl.BlockSpec((tm, C), lambda i: (i, 0)),
            pl.BlockSpec((POOL_HALO, C), lambda i: (jnp.minimum((i + 1) * hb, last_halo), 0)),
        ],
        out_specs=pl.BlockSpec((tm, C), lambda i: (i, 0)),
        scratch_shapes=[pltpu.VMEM((tm + POOL_HALO, C), F32)],
        name=name,
        compiler_params=_cp(("parallel",)),
    )(dp, dp)


def _scale_add(h, pre, scale, name):
    M, C = h.shape
    tm = _row_tile(M, 640)

    def body(h_ref, p_ref, s_ref, o_ref):
        o_ref[...] = h_ref[...] + p_ref[...] * s_ref[...]

    row = pl.BlockSpec((tm, C), lambda i: (i, 0))
    return pl.pallas_call(
        body,
        out_shape=jax.ShapeDtypeStruct((M, C), F32),
        grid=(M // tm,),
        in_specs=[row, row, pl.BlockSpec((1, C), lambda i: (0, 0))],
        out_specs=row,
        name=name,
        compiler_params=_cp(("parallel",)),
    )(h, pre, scale.reshape(1, C))


def _scale_bwd(dh, pre, scale, name):
    M, C = dh.shape
    tm = _row_tile(M, 640)

    def body(dh_ref, p_ref, s_ref, dp_ref, ds_ref):
        @pl.when(pl.program_id(0) == 0)
        def _():
            ds_ref[...] = jnp.zeros_like(ds_ref)

        d = dh_ref[...]
        ds_ref[...] += jnp.sum(d * p_ref[...], axis=0, keepdims=True)
        dp_ref[...] = (d * s_ref[...]).astype(dp_ref.dtype)

    row = pl.BlockSpec((tm, C), lambda i: (i, 0))
    vec = pl.BlockSpec((1, C), lambda i: (0, 0))
    return pl.pallas_call(
        body,
        out_shape=(jax.ShapeDtypeStruct((M, C), BF16), jax.ShapeDtypeStruct((1, C), F32)),
        grid=(M // tm,),
        in_specs=[row, row, vec],
        out_specs=(row, vec),
        name=name,
        compiler_params=_cp(("arbitrary",)),
    )(dh, pre, scale.reshape(1, C))


def _gate_parts(z):
    e = jnp.exp(-jnp.abs(z))
    return e, jnp.minimum(z, 0.0) - jnp.log(1.0 + e)


def _tri_dot3(tri, x):
    hi, mid, lo = _split3(x)
    d = functools.partial(jnp.dot, preferred_element_type=F32)
    return d(tri, hi) + d(tri, mid) + d(tri, lo)


def _gate_fwd(x, b, name):
    lp, C = x.shape
    tm = _row_tile(lp, 640)

    def body(x_ref, b_ref, o_ref, carry):
        i = pl.program_id(0)

        @pl.when(i == 0)
        def _():
            carry[...] = jnp.zeros_like(carry)

        _, ls = _gate_parts(x_ref[...] + b_ref[...])
        rows = i * tm + lax.broadcasted_iota(jnp.int32, (tm, 1), 0)
        ls = jnp.where(rows >= PAD0, ls, 0.0)
        tri = (lax.broadcasted_iota(jnp.int32, (tm, tm), 0) >= lax.broadcasted_iota(jnp.int32, (tm, tm), 1)).astype(BF16)
        f = _tri_dot3(tri, ls) + carry[...]
        o_ref[...] = f
        carry[...] = f[tm - 1:tm, :]

    return pl.pallas_call(
        body,
        out_shape=jax.ShapeDtypeStruct((lp, C), F32),
        grid=(lp // tm,),
        in_specs=[pl.BlockSpec((tm, C), lambda i: (i, 0)), pl.BlockSpec((1, C), lambda i: (0, 0))],
        out_specs=pl.BlockSpec((tm, C), lambda i: (i, 0)),
        scratch_shapes=[pltpu.VMEM((1, C), F32)],
        name=name,
        compiler_params=_cp(("arbitrary",)),
    )(x, b)


def _gate_bwd(x, b, df, name):
    lp, C = x.shape
    tm = _row_tile(lp, 640)
    nt = lp // tm

    def body(x_ref, b_ref, df_ref, dx_ref, db_ref, carry):
        i = pl.program_id(0)

        @pl.when(i == 0)
        def _():
            carry[...] = jnp.zeros_like(carry)
            db_ref[...] = jnp.zeros_like(db_ref)

        z = x_ref[...] + b_ref[...]
        e, _ = _gate_parts(z)
        tri = (lax.broadcasted_iota(jnp.int32, (tm, tm), 0) <= lax.broadcasted_iota(jnp.int32, (tm, tm), 1)).astype(BF16)
        r = _tri_dot3(tri, df_ref[...]) + carry[...]
        carry[...] = r[0:1, :]
        rows = (nt - 1 - i) * tm + lax.broadcasted_iota(jnp.int32, (tm, 1), 0)
        dx = jnp.where(rows >= PAD0, r * (jnp.where(z >= 0.0, e, 1.0) / (1.0 + e)), 0.0)
        dx_ref[...] = dx
        db_ref[...] += jnp.sum(dx, axis=0, keepdims=True)

    rev = pl.BlockSpec((tm, C), lambda i: (nt - 1 - i, 0))
    vec = pl.BlockSpec((1, C), lambda i: (0, 0))
    return pl.pallas_call(
        body,
        out_shape=(jax.ShapeDtypeStruct((lp, C), F32), jax.ShapeDtypeStruct((1, C), F32)),
        grid=(nt,),
        in_specs=[rev, vec, rev],
        out_specs=(rev, vec),
        scratch_shapes=[pltpu.VMEM((1, C), F32)],
        name=name,
        compiler_params=_cp(("arbitrary",)),
    )(x, b, df)


def _rope_apply(x, c, a, b):
    return x * c + pltpu.roll(x, LANES - 16, 1) * a + pltpu.roll(x, 16, 1) * b


def _rope_transpose(dy, c, a, b):
    return dy * c + pltpu.roll(dy * a, 16, 1) + pltpu.roll(dy * b, LANES - 16, 1)


def _mla_prep_fwd(q, kmat, kr, c, a, b, name):
    lp, W = q.shape
    nh = W // LANES
    tm = _row_tile(lp, 640)

    def body(q_ref, k_ref, kr_ref, c_ref, a_ref, b_ref, qo_ref, ko_ref):
        cv, av, bv = c_ref[...], a_ref[...], b_ref[...]
        qo_ref[...] = _rope_apply(q_ref[...], cv, av, bv).astype(qo_ref.dtype)
        ko_ref[...] = (k_ref[...] + _rope_apply(kr_ref[...], cv, av, bv)).astype(ko_ref.dtype)

    head = pl.BlockSpec((tm, LANES), lambda i, h: (i, h))
    tab = pl.BlockSpec((tm, LANES), lambda i, h: (i, 0))
    wide = jax.ShapeDtypeStruct((lp, W), BF16)
    return pl.pallas_call(
        body,
        out_shape=(wide, wide),
        grid=(lp // tm, nh),
        in_specs=[head, head, tab, tab, tab, tab],
        out_specs=(head, head),
        name=name,
        compiler_params=_cp(("parallel", "parallel")),
    )(q, kmat, kr, c, a, b)


def _mla_prep_bwd(dq, dk, c, a, b, name):
    lp, W = dq.shape
    nh = W // LANES
    tm = _row_tile(lp, 640)

    def body(dq_ref, dk_ref, c_ref, a_ref, b_ref, dqo_ref, dkr_ref):
        cv, av, bv = c_ref[...], a_ref[...], b_ref[...]
        ksum = jnp.zeros((tm, LANES), F32)
        for h in range(nh):
            cols = pl.ds(h * LANES, LANES)
            dqo_ref[:, cols] = _rope_transpose(dq_ref[:, cols].astype(F32), cv, av, bv).astype(dqo_ref.dtype)
            ksum = ksum + dk_ref[:, cols].astype(F32)
        dkr_ref[...] = _rope_transpose(ksum, cv, av, bv)

    wide = pl.BlockSpec((tm, W), lambda i: (i, 0))
    tab = pl.BlockSpec((tm, LANES), lambda i: (i, 0))
    return pl.pallas_call(
        body,
        out_shape=(jax.ShapeDtypeStruct((lp, W), BF16), jax.ShapeDtypeStruct((lp, LANES), F32)),
        grid=(lp // tm,),
        in_specs=[wide, wide, tab, tab, tab],
        out_specs=(wide, tab),
        name=name,
        compiler_params=_cp(("parallel",)),
    )(dq, dk, c, a, b)


def _loss_head(h, g, target, name):
    lp, C = h.shape
    tm = _row_tile(lp, 640)
    nt = lp // tm

    def body(h_ref, g_ref, t_ref, loss_ref, dh_ref, dg_ref, sq):
        i = pl.program_id(0)

        @pl.when(i == 0)
        def _():
            dg_ref[...] = jnp.zeros_like(dg_ref)
            sq[...] = jnp.zeros_like(sq)

        xf = h_ref[...]
        gv = g_ref[...]
        r = lax.rsqrt(jnp.mean(xf * xf, axis=-1, keepdims=True) + EPS)
        xhat = xf * r
        rows = i * tm + lax.broadcasted_iota(jnp.int32, (tm, 1), 0)
        err = jnp.where(rows >= PAD0 + N_META, xhat * gv - t_ref[...], 0.0)
        sq[...] += jnp.sum(err * err, axis=0, keepdims=True)
        dy = err * (1.0 / C)
        dg_ref[...] += jnp.sum(dy * xhat, axis=0, keepdims=True)
        dxh = dy * gv
        dh_ref[...] = r * (dxh - xhat * jnp.mean(dxh * xhat, axis=-1, keepdims=True))

        @pl.when(i == nt - 1)
        def _():
            loss_ref[...] = jnp.broadcast_to(jnp.sum(sq[...], axis=1, keepdims=True) * (0.5 / C), (1, LANES))

    row = pl.BlockSpec((tm, C), lambda i: (i, 0))
    vec = pl.BlockSpec((1, C), lambda i: (0, 0))
    return pl.pallas_call(
        body,
        out_shape=(jax.ShapeDtypeStruct((1, LANES), F32), jax.ShapeDtypeStruct((lp, C), F32), jax.ShapeDtypeStruct((1, C), F32)),
        grid=(nt,),
        in_specs=[row, vec, row],
        out_specs=(pl.BlockSpec((1, LANES), lambda i: (0, 0)), row, vec),
        scratch_shapes=[pltpu.VMEM((1, C), F32)],
        name=name,
        compiler_params=_cp(("arbitrary",)),
    )(h, g.reshape(1, C), target)


def _adamw(w, g, m, v, name):
    shape = w.shape
    C = shape[-1]
    R = w.size // C
    tr = R
    if R % 8 == 0:
        for cand in range(8, R + 1, 8):
            if R % cand == 0 and cand * C * 4 <= (1 << 20):
                tr = cand
    c1 = 1.0 - ADAM_B1 ** ADAM_STEP
    c2 = 1.0 - ADAM_B2 ** ADAM_STEP

    def body(w_ref, g_ref, m_ref, v_ref, d_ref, nm_ref, nv_ref):
        gv = g_ref[...]
        nm = ADAM_B1 * m_ref[...] + (1.0 - ADAM_B1) * gv
        nv = ADAM_B2 * v_ref[...] + (1.0 - ADAM_B2) * (gv * gv)
        nm_ref[...] = nm
        nv_ref[...] = nv
        d_ref[...] = -ADAM_LR * ((nm / c1) / (jnp.sqrt(nv / c2) + ADAM_EPS) + ADAM_WD * w_ref[...])

    blk = pl.BlockSpec((tr, C), lambda i: (i, 0))
    out = jax.ShapeDtypeStruct((R, C), F32)
    outs = pl.pallas_call(
        body,
        out_shape=(out, out, out),
        grid=(R // tr,),
        in_specs=[blk] * 4,
        out_specs=(blk, blk, blk),
        name=name,
        compiler_params=_cp(("parallel",)),
    )(*(t.reshape(R, C) for t in (w, g, m, v)))
    return tuple(t.reshape(shape) for t in outs)


def _exchange(send, axes, same, name):
    na = len(axes)
    n = 1 << na
    _, R, C = send.shape
    parts = max(p for p in (8, 4, 2, 1) if R % (16 * p) == 0 or p == 1)
    pr = R // parts

    def body(send_ref, recv_ref, send_sems, recv_sems, local_sem):
        coords = {ax: lax.axis_index(ax) for ax in MESH_AXES}
        me = 0
        for ax in axes:
            me = me * 2 + coords[ax]

        def member(r):
            dev = dict(coords)
            for b, ax in enumerate(axes):
                if (r >> (na - 1 - b)) & 1:
                    dev[ax] = 1 - dev[ax]
            return tuple(dev[ax] for ax in MESH_AXES)

        def chunk(j, p):
            return (send_ref.at[0] if same else send_ref.at[j]).at[pl.ds(p * pr, pr)]

        def slot(j, p):
            return recv_ref.at[j].at[pl.ds(p * pr, pr)]

        own = pltpu.make_async_copy(send_ref.at[0] if same else send_ref.at[me], recv_ref.at[me], local_sem)
        own.start()
        copies = []
        for r in range(1, n):
            peer = me ^ r
            for p in range(parts):
                cp = pltpu.make_async_remote_copy(
                    src_ref=chunk(peer, p), dst_ref=slot(me, p), send_sem=send_sems.at[r, p], recv_sem=recv_sems.at[r, p],
                    device_id=member(r), device_id_type=pl.DeviceIdType.MESH)
                cp.start()
                copies.append(cp)
        for r in range(1, n):
            for p in range(parts):
                arrival = pltpu.make_async_remote_copy(
                    src_ref=chunk(me, p), dst_ref=slot(me ^ r, p), send_sem=send_sems.at[r, p], recv_sem=recv_sems.at[r, p],
                    device_id=member(r), device_id_type=pl.DeviceIdType.MESH)
                arrival.wait_recv()
        for cp in copies:
            cp.wait_send()
        own.wait()

    any_spec = pl.BlockSpec(memory_space=pl.ANY)
    return pl.pallas_call(
        body,
        out_shape=jax.ShapeDtypeStruct((n, R, C), send.dtype),
        in_specs=[any_spec],
        out_specs=any_spec,
        scratch_shapes=[pltpu.SemaphoreType.DMA((n, parts)), pltpu.SemaphoreType.DMA((n, parts)), pltpu.SemaphoreType.DMA],
        name=name,
        compiler_params=pltpu.CompilerParams(has_side_effects=True),
    )(send)


def _sum_chunks(x, name, out_dtype=F32):
    n, R, C = x.shape
    tr = _row_tile(R, 512)

    def body(x_ref, o_ref):
        acc = x_ref[0].astype(F32)
        for j in range(1, n):
            acc = acc + x_ref[j].astype(F32)
        o_ref[...] = acc.astype(o_ref.dtype)

    return pl.pallas_call(
        body,
        out_shape=jax.ShapeDtypeStruct((R, C), out_dtype),
        grid=(R // tr,),
        in_specs=[pl.BlockSpec((n, tr, C), lambda i: (0, i, 0))],
        out_specs=pl.BlockSpec((tr, C), lambda i: (i, 0)),
        name=name,
        compiler_params=_cp(("parallel",)),
    )(x)


def _pad_heads_cols(w, groups, d):
    k = w.shape[0]
    w = w.reshape(k, groups * N_HEADS, d)
    return jnp.pad(w, ((0, 0), (0, 0), (0, LANES - d))).reshape(k, groups * N_HEADS * LANES)


def _unpad_heads_cols(w, groups, d):
    k = w.shape[0]
    return w.reshape(k, groups * N_HEADS, LANES)[:, :, :d].reshape(k, groups * N_HEADS * d)


def _pad_heads_rows(w, d):
    n = w.shape[1]
    return jnp.pad(w.reshape(N_HEADS, d, n), ((0, 0), (0, LANES - d), (0, 0))).reshape(N_HEADS * LANES, n)


def _unpad_heads_rows(w, d):
    n = w.shape[1]
    return w.reshape(N_HEADS, LANES, n)[:, :d].reshape(N_HEADS * d, n)


def _kernel_weights(W):
    P = dict(W)
    pw = W["pool_w"][0]
    bd = jnp.zeros((D_MODEL, D_MODEL), pw.dtype)
    for g in range(len(POOL_WINDOWS)):
        bd = lax.dynamic_update_slice(bd, pw[g], (g * POOL_GROUP, g * POOL_GROUP))
    P["pool_bd"] = bd
    P["sb_qkv"] = _pad_heads_cols(W["sb_w_qkv"][0], 3, HEAD_DIM)
    P["sb_o"] = _pad_heads_rows(W["sb_w_o"][0], HEAD_DIM)
    nq = 3 * N_HEADS * HEAD_DIM
    P["fox_qkv"] = _pad_heads_cols(W["fox_w_qkvf"][0][:, :nq], 3, HEAD_DIM)
    P["fox_f"] = jnp.pad(W["fox_w_qkvf"][0][:, nq:], ((0, 0), (0, LANES - N_HEADS)))
    P["fox_o"] = _pad_heads_rows(W["fox_w_o"][0], HEAD_DIM)
    P["fox_b"] = jnp.pad(W["fox_b_f"], ((0, 0), (0, LANES - N_HEADS)))
    P["mla_down"] = jnp.pad(W["mla_w_down"][0], ((0, 0), (0, MLA_DOWN_PAD - W["mla_w_down"].shape[2])))
    P["mla_uq"] = _pad_heads_cols(W["mla_w_uq"][0], 1, MLA_NOPE + MLA_ROPE)
    ukv = W["mla_w_ukv"][0].reshape(MLA_KV_RANK, N_HEADS, 2 * HEAD_DIM)
    padk = ((0, 0), (0, 0), (0, LANES - HEAD_DIM))
    P["mla_ukv"] = jnp.concatenate(
        [jnp.pad(ukv[:, :, :MLA_NOPE], padk).reshape(MLA_KV_RANK, -1), jnp.pad(ukv[:, :, MLA_NOPE:], padk).reshape(MLA_KV_RANK, -1)], axis=1)
    P["mla_o"] = _pad_heads_rows(W["mla_w_o"][0], HEAD_DIM)
    return P


def _rope_tables(lp):
    pos = (jnp.arange(lp) - PAD0).astype(F32)
    inv = ROPE_THETA ** (-jnp.arange(0, MLA_ROPE, 2, dtype=F32) / MLA_ROPE)
    ang = pos[:, None] * inv[None, :]
    cos, sin = jnp.cos(ang), jnp.sin(ang)
    half = MLA_ROPE // 2
    z = lambda n: jnp.zeros((lp, n), F32)
    c = jnp.concatenate([jnp.ones((lp, MLA_NOPE), F32), cos, cos, z(LANES - MLA_NOPE - MLA_ROPE)], axis=1)
    a = jnp.concatenate([z(MLA_NOPE), -sin, z(LANES - MLA_NOPE - half)], axis=1)
    b = jnp.concatenate([z(MLA_NOPE + half), sin, z(LANES - MLA_NOPE - MLA_ROPE)], axis=1)
    return c, a, b


def _key_bias(lp, t, per_head=None):
    pad = jnp.arange(lp)[None, :] < PAD0
    body = jnp.zeros((N_HEADS, lp), F32) if per_head is None else per_head
    return jnp.where(pad, NEG, body).reshape(N_HEADS, lp // t, 1, t)


def _ffn_fwd(h, i, P):
    b = _rms_fwd(h, P["norm_ffn"][i], BF16, "ffn_norm")
    g = _mm(b, P["ffn_w_gate"][i], "nn", "ffn_gate")
    u = _mm(b, P["ffn_w_up"][i], "nn", "ffn_up")
    hd = _swiglu_fwd(g, u, "ffn_act")
    return _mm(hd, P["ffn_w_down"][i], "nn", "ffn_down", add=h), (h, b, g, u, hd)


def _ffn_bwd(dh, i, P, saved):
    h, b, g, u, hd = saved
    dwd = _mm(hd, dh, "tn", "ffn_down_dw")
    dhd = _mm(dh, P["ffn_w_down"][i], "nt", "ffn_down_dx")
    dg, du = _swiglu_bwd(g, u, dhd, "ffn_act_bwd")
    dwg = _mm(b, dg, "tn", "ffn_gate_dw")
    dwu = _mm(b, du, "tn", "ffn_up_dw")
    db = _mm(dg, P["ffn_w_gate"][i], "nt", "ffn_gate_dx")
    db = _mm(du, P["ffn_w_up"][i], "nt", "ffn_up_dx", add=db)
    dh_in, dgain = _rms_bwd(h, P["norm_ffn"][i], db, dh, "ffn_norm_bwd")
    return dh_in, dgain, dwg, dwu, dwd


def _pool_layer_fwd(h, P):
    a = _rms_fwd(h, P["norm_mix"][0], F32, "pool_norm")
    pooled = _pool_fwd(a, "pool_window")
    pre = _mm(pooled, P["pool_bd"], "nn", "pool_mix")
    return _scale_add(h, pre, P["pool_scale"][0], "pool_scale_add"), (h, pooled, pre)


def _pool_layer_bwd(dh, P, saved):
    h, pooled, pre = saved
    dpre, dscale = _scale_bwd(dh, pre, P["pool_scale"][0], "pool_scale_bwd")
    dbd = _mm(pooled, dpre, "tn", "pool_mix_dw")
    dpooled = _mm(dpre, P["pool_bd"], "nt", "pool_mix_dx")
    da = _pool_bwd(dpooled, "pool_window_bwd")
    dh_in, dgain = _rms_bwd(h, P["norm_mix"][0], da, dh, "mix_norm_bwd")
    dw = jnp.stack([dbd[g * POOL_GROUP:(g + 1) * POOL_GROUP, g * POOL_GROUP:(g + 1) * POOL_GROUP] for g in range(len(POOL_WINDOWS))])
    return dh_in, {"norm_mix0": dgain, "pool_w": dw[None], "pool_scale": dscale}


def _out_proj_bwd(o, dh, wo, tag):
    return _mm(o, dh, "tn", tag + "_o_dw"), _mm(dh, wo, "nt", tag + "_o_dx", out_dtype=BF16)


def _sb_layer_fwd(h, P):
    lp = h.shape[0]
    a = _rms_fwd(h, P["norm_mix"][1], BF16, "mix_norm")
    qkv = _mm(a, P["sb_qkv"], "nn", "sb_qkv", out_dtype=BF16)
    kb = _key_bias(lp, SB_TK)
    o = _sb_fwd(qkv, kb, nh=N_HEADS, name="sb_attn")
    return _mm(o, P["sb_o"], "nn", "attn_out", add=h), (h, a, qkv, kb, o)


def _sb_layer_bwd(dh, P, saved):
    h, a, qkv, kb, o = saved
    dwo, do = _out_proj_bwd(o, dh, P["sb_o"], "attn")
    dq, dk, dv = _sb_bwd(qkv, kb, do, nh=N_HEADS, name="sb_attn_bwd")
    dqkv = jnp.concatenate([dq, dk, dv], axis=1)
    dw = _mm(a, dqkv, "tn", "qkv_dw")
    da = _mm(dqkv, P["sb_qkv"], "nt", "qkv_dx")
    dh_in, dgain = _rms_bwd(h, P["norm_mix"][1], da, dh, "mix_norm_bwd")
    return dh_in, {"norm_mix1": dgain, "sb_w_qkv": _unpad_heads_cols(dw, 3, HEAD_DIM)[None], "sb_w_o": _unpad_heads_rows(dwo, HEAD_DIM)[None]}


def _fox_layer_fwd(h, P):
    lp = h.shape[0]
    t = _attn_tile(lp)
    a = _rms_fwd(h, P["norm_mix"][3], BF16, "mix_norm")
    qkv = _mm(a, P["fox_qkv"], "nn", "sb_qkv", out_dtype=BF16)
    f = _mm(a, P["fox_f"], "nn", "fox_gate_proj")
    fc = _gate_fwd(f, P["fox_b"], "fox_gate")[:, :N_HEADS]
    kb = _key_bias(lp, t, -fc.T)
    fq = jnp.broadcast_to(fc[:, :, None], (lp, N_HEADS, LANES)).reshape(lp, N_HEADS * LANES)
    o, lse = _attn_fwd(qkv, qkv, qkv, kb, fq, nh=N_HEADS, offs=(0, N_HEADS, 2 * N_HEADS), scale=HEAD_DIM ** -0.5, name="fox_attn")
    return _mm(o, P["fox_o"], "nn", "attn_out", add=h), (h, a, qkv, f, kb, fq, o, lse)


def _fox_layer_bwd(dh, P, saved):
    h, a, qkv, f, kb, fq, o, lse = saved
    lp = h.shape[0]
    dwo, do = _out_proj_bwd(o, dh, P["fox_o"], "attn")
    dq, dk, dv, dkb, dqb = _attn_bwd(qkv, qkv, qkv, kb, fq, o, do, lse, nh=N_HEADS, offs=(0, N_HEADS, 2 * N_HEADS),
                                     scale=HEAD_DIM ** -0.5, name="fox_attn_bwd")
    dfc = jnp.pad(dqb.reshape(lp, N_HEADS, LANES)[:, :, 0] - dkb.reshape(N_HEADS, lp).T, ((0, 0), (0, LANES - N_HEADS)))
    df, dbf = _gate_bwd(f, P["fox_b"], dfc, "fox_gate_bwd")
    dqkv = jnp.concatenate([dq, dk, dv], axis=1)
    dw = _mm(a, dqkv, "tn", "qkv_dw")
    dwf = _mm(a, df, "tn", "fox_gate_dw")
    da = _mm(dqkv, P["fox_qkv"], "nt", "qkv_dx")
    da = _mm(df, P["fox_f"], "nt", "fox_gate_dx", add=da)
    dh_in, dgain = _rms_bwd(h, P["norm_mix"][3], da, dh, "mix_norm_bwd")
    dwqkvf = jnp.concatenate([_unpad_heads_cols(dw, 3, HEAD_DIM), dwf[:, :N_HEADS]], axis=1)
    return dh_in, {"norm_mix3": dgain, "fox_w_qkvf": dwqkvf[None], "fox_b_f": dbf[:, :N_HEADS], "fox_w_o": _unpad_heads_rows(dwo, HEAD_DIM)[None]}


def _mla_layer_fwd(h, P):
    lp = h.shape[0]
    a = _rms_fwd(h, P["norm_mix"][2], BF16, "mix_norm")
    down = _mm(a, P["mla_down"], "nn", "mla_down")
    cq_pre = down[:, :MLA_Q_RANK]
    ckv_pre = down[:, MLA_Q_RANK:MLA_Q_RANK + MLA_KV_RANK]
    kr = jnp.pad(down[:, MLA_Q_RANK + MLA_KV_RANK:MLA_Q_RANK + MLA_KV_RANK + MLA_ROPE], ((0, 0), (MLA_NOPE, LANES - MLA_NOPE - MLA_ROPE)))
    cq = _rms_fwd(cq_pre, P["mla_q_norm"][0], BF16, "mla_q_norm")
    ckv = _rms_fwd(ckv_pre, P["mla_kv_norm"][0], BF16, "mla_kv_norm")
    q = _mm(cq, P["mla_uq"], "nn", "mla_uq")
    kv = _mm(ckv, P["mla_ukv"], "nn", "mla_ukv", out_dtype=BF16)
    tabs = _rope_tables(lp)
    qr, kc = _mla_prep_fwd(q, kv, kr, *tabs, "mla_rope")
    kb = _key_bias(lp, _attn_tile(lp))
    o, lse = _attn_fwd(qr, kc, kv, kb, None, nh=N_HEADS, offs=(0, 0, N_HEADS), scale=(MLA_NOPE + MLA_ROPE) ** -0.5, name="mla_attn")
    return _mm(o, P["mla_o"], "nn", "attn_out", add=h), (h, a, cq_pre, ckv_pre, cq, ckv, qr, kc, kv, tabs, kb, o, lse)


def _mla_layer_bwd(dh, P, saved):
    h, a, cq_pre, ckv_pre, cq, ckv, qr, kc, kv, tabs, kb, o, lse = saved
    lp = h.shape[0]
    dwo, do = _out_proj_bwd(o, dh, P["mla_o"], "attn")
    dqr, dkc, dv = _attn_bwd(qr, kc, kv, kb, None, o, do, lse, nh=N_HEADS, offs=(0, 0, N_HEADS),
                             scale=(MLA_NOPE + MLA_ROPE) ** -0.5, name="mla_attn_bwd")
    dq, dkr = _mla_prep_bwd(dqr, dkc, *tabs, "mla_rope_bwd")
    dkv = jnp.concatenate([dkc, dv], axis=1)
    dwuq = _mm(cq, dq, "tn", "mla_uq_dw")
    dcq = _mm(dq, P["mla_uq"], "nt", "mla_uq_dx")
    dwukv = _mm(ckv, dkv, "tn", "mla_ukv_dw")
    dckv = _mm(dkv, P["mla_ukv"], "nt", "mla_ukv_dx")
    dcq_pre, dqn = _rms_bwd(cq_pre, P["mla_q_norm"][0], dcq, None, "mla_q_norm_bwd")
    dckv_pre, dkvn = _rms_bwd(ckv_pre, P["mla_kv_norm"][0], dckv, None, "mla_kv_norm_bwd")
    used = MLA_Q_RANK + MLA_KV_RANK + MLA_ROPE
    ddown = jnp.concatenate([dcq_pre, dckv_pre, dkr[:, MLA_NOPE:MLA_NOPE + MLA_ROPE], jnp.zeros((lp, MLA_DOWN_PAD - used), F32)], axis=1)
    dwdown = _mm(a, ddown, "tn", "mla_down_dw")
    da = _mm(ddown, P["mla_down"], "nt", "mla_down_dx")
    dh_in, dgain = _rms_bwd(h, P["norm_mix"][2], da, dh, "mix_norm_bwd")
    dukv = dwukv.reshape(MLA_KV_RANK, 2, N_HEADS, LANES)[:, :, :, :HEAD_DIM]
    dukv = jnp.concatenate([dukv[:, 0], dukv[:, 1]], axis=-1).reshape(MLA_KV_RANK, N_HEADS * 2 * HEAD_DIM)
    return dh_in, {
        "norm_mix2": dgain, "mla_w_down": dwdown[:, :used][None], "mla_q_norm": dqn, "mla_kv_norm": dkvn,
        "mla_w_uq": _unpad_heads_cols(dwuq, 1, MLA_NOPE + MLA_ROPE)[None], "mla_w_ukv": dukv[None],
        "mla_w_o": _unpad_heads_rows(dwo, HEAD_DIM)[None]}


_MIXERS = ((_pool_layer_fwd, _pool_layer_bwd), (_sb_layer_fwd, _sb_layer_bwd), (_mla_layer_fwd, _mla_layer_bwd), (_fox_layer_fwd, _fox_layer_bwd))


def _step_local(x, target, W):
    seq = x.shape[0]
    P = _kernel_weights(W)
    h = jnp.concatenate([jnp.zeros((PAD0, D_MODEL), F32), W["meta"], x], axis=0)
    tpad = jnp.pad(target, ((PAD0 + N_META, 0), (0, 0)))
    saved = []
    for i in range(4):
        h, s_mix = _MIXERS[i][0](h, P)
        h, s_ffn = _ffn_fwd(h, i, P)
        saved.append((s_mix, s_ffn))
    loss, dh, dfinal = _loss_head(h, W["final_norm"], tpad, "loss_head")
    grads = {"final_norm": dfinal.reshape(-1)}
    gains_mix, gains_ffn, dwg, dwu, dwd = [None] * 4, [None] * 4, [None] * 4, [None] * 4, [None] * 4
    for i in reversed(range(4)):
        s_mix, s_ffn = saved[i]
        dh, gains_ffn[i], dwg[i], dwu[i], dwd[i] = _ffn_bwd(dh, i, P, s_ffn)
        dh, g = _MIXERS[i][1](dh, P, s_mix)
        gains_mix[i] = g.pop("norm_mix%d" % i)
        grads.update(g)
    grads["norm_mix"] = jnp.concatenate(gains_mix, axis=0)
    grads["norm_ffn"] = jnp.concatenate(gains_ffn, axis=0)
    grads["ffn_w_gate"] = jnp.stack(dwg)
    grads["ffn_w_up"] = jnp.stack(dwu)
    grads["ffn_w_down"] = jnp.stack(dwd)
    grads["meta"] = dh[PAD0:PAD0 + N_META]
    return loss, dh[PAD0 + N_META:], grads


_WEIGHTS = ("meta", "norm_mix", "norm_ffn", "pool_w", "pool_scale", "sb_w_qkv", "sb_w_o", "mla_w_down", "mla_q_norm",
            "mla_kv_norm", "mla_w_uq", "mla_w_ukv", "mla_w_o", "fox_w_qkvf", "fox_b_f", "fox_w_o", "ffn_w_gate",
            "ffn_w_up", "ffn_w_down", "final_norm")
_SHARD_AXIS = {"meta": 1, "pool_w": 2, "sb_w_qkv": 2, "sb_w_o": 1, "mla_w_down": 1, "mla_q_norm": 1, "mla_kv_norm": 1,
               "mla_w_uq": 2, "mla_w_ukv": 2, "mla_w_o": 1, "fox_w_qkvf": 2, "fox_b_f": None, "fox_w_o": 1,
               "ffn_w_gate": 2, "ffn_w_up": 2, "ffn_w_down": 1}
_SHARDED = tuple(n for n in _WEIGHTS if _SHARD_AXIS.get(n) is not None)
_REPLICATED = tuple(n for n in _WEIGHTS if _SHARD_AXIS.get(n) is None)
_EXACT = ("meta", "mla_q_norm", "mla_kv_norm")
N_CHIPS = 4
GRAD_ROW_TILE = 512


def _flat_rows(parts, dtype, row_multiple):
    flat = jnp.concatenate([p.astype(dtype).reshape(-1) for p in parts])
    rows = -(-flat.shape[0] // (LANES * row_multiple)) * row_multiple
    return jnp.pad(flat, (0, rows * LANES - flat.shape[0])).reshape(rows, LANES)


def _split_flat(flat, like):
    flat = flat.reshape(-1)
    out, off = [], 0
    for t in like:
        out.append(flat[off:off + t.size].reshape(t.shape))
        off += t.size
    return out


def _gather_shards(local, names, dtype, name, two_level):
    blocks = [local[n] for n in names]
    flat = _flat_rows(blocks, dtype, 32)
    if two_level:
        half_rows = flat.shape[0] // 2
        half = lax.dynamic_slice_in_dim(flat, lax.axis_index("c") * half_rows, half_rows, axis=0)
        halves = _exchange(half[None], ("x", "y"), True, name)
        both = _exchange(halves.reshape(1, N_CHIPS * half_rows, LANES), ("c",), True, name + "_pair")
        recv = both.reshape(2, N_CHIPS, half_rows, LANES).transpose(1, 0, 2, 3).reshape(N_CHIPS, 2 * half_rows, LANES)
    else:
        recv = _exchange(flat[None], ("x", "y"), True, name)
    per_chip = [_split_flat(recv[s], blocks) for s in range(N_CHIPS)]
    return {n: jnp.concatenate([per_chip[s][k] for s in range(N_CHIPS)], axis=_SHARD_AXIS[n]) for k, n in enumerate(names)}


def _shard_of(g, n, s):
    w = g.shape[_SHARD_AXIS[n]] // N_CHIPS
    return lax.slice_in_dim(g, s * w, (s + 1) * w, axis=_SHARD_AXIS[n])


def _train_step(a):
    local = {n: a[n] for n in _WEIGHTS}
    full = {n: local[n] for n in _REPLICATED}
    full.update(_gather_shards(local, [n for n in _SHARDED if n not in _EXACT], BF16, "gather_weights", True))
    full.update(_gather_shards(local, list(_EXACT), F32, "gather_exact", False))

    loss, grad_x, grads = _step_local(a["x"][0], a["loss_target"][0], full)

    send = jnp.stack([
        _flat_rows([_shard_of(grads[n], n, s) for n in _SHARDED], BF16, 2 * GRAD_ROW_TILE).reshape(2, -1, LANES)
        for s in range(N_CHIPS)], axis=1)
    rows = send.shape[2]
    pair = _exchange(send.reshape(2, N_CHIPS * rows, LANES), ("c",), False, "pair_scatter_grads")
    chip_sum = _sum_chunks(pair, "sum_pair_grads", out_dtype=BF16).reshape(N_CHIPS, rows, LANES)
    mine = _sum_chunks(_exchange(chip_sum, ("x", "y"), False, "scatter_grads"), "sum_grads")
    both = _exchange(mine[None], ("c",), True, "pair_grads")
    reduced = dict(zip(_SHARDED, _split_flat(both, [local[n] for n in _SHARDED])))
    small = _flat_rows([grads[n] for n in _REPLICATED], F32, 8)
    small = _sum_chunks(_exchange(small[None], MESH_AXES, True, "gather_small_grads"), "sum_small_grads")
    reduced.update(zip(_REPLICATED, _split_flat(small, [local[n] for n in _REPLICATED])))

    deltas, new_m, new_v = {}, {}, {}
    for n in _WEIGHTS:
        deltas[n], new_m[n], new_v[n] = _adamw(local[n], reduced[n], a["m_" + n], a["v_" + n], "adamw")
    total = lax.psum(loss[0, 0], MESH_AXES)
    return (total, grad_x[None], *[reduced[n] for n in _WEIGHTS], *[deltas[n] for n in _WEIGHTS],
            *[new_m[n] for n in _WEIGHTS], *[new_v[n] for n in _WEIGHTS])


def kernel(x, meta, norm_mix, norm_ffn, pool_w, pool_scale, sb_w_qkv, sb_w_o, mla_w_down, mla_q_norm, mla_kv_norm, mla_w_uq, mla_w_ukv, mla_w_o, fox_w_qkvf, fox_b_f, fox_w_o, ffn_w_gate, ffn_w_up, ffn_w_down, final_norm, loss_target, m_meta, m_norm_mix, m_norm_ffn, m_pool_w, m_pool_scale, m_sb_w_qkv, m_sb_w_o, m_mla_w_down, m_mla_q_norm, m_mla_kv_norm, m_mla_w_uq, m_mla_w_ukv, m_mla_w_o, m_fox_w_qkvf, m_fox_b_f, m_fox_w_o, m_ffn_w_gate, m_ffn_w_up, m_ffn_w_down, m_final_norm, v_meta, v_norm_mix, v_norm_ffn, v_pool_w, v_pool_scale, v_sb_w_qkv, v_sb_w_o, v_mla_w_down, v_mla_q_norm, v_mla_kv_norm, v_mla_w_uq, v_mla_w_ukv, v_mla_w_o, v_fox_w_qkvf, v_fox_b_f, v_fox_w_o, v_ffn_w_gate, v_ffn_w_up, v_ffn_w_down, v_final_norm):
    return _train_step(dict(locals()))
```

```python
import functools

import jax
import jax.numpy as jnp
from jax import lax
from jax.experimental import pallas as pl
from jax.experimental.pallas import tpu as pltpu

F32 = jnp.float32
BF16 = jnp.bfloat16

D_MODEL = 1024
N_META = 16
PAD0 = 112
LANES = 128
N_HEADS = 16
HEAD_DIM = 64
POOL_WINDOWS = (2, 4, 8, 16)
POOL_GROUP = 256
POOL_HALO = 16
MLA_Q_RANK = 384
MLA_KV_RANK = 256
MLA_NOPE = 64
MLA_ROPE = 32
MLA_DOWN_PAD = 768
ROPE_THETA = 10000.0
D_FF = 2816
EPS = 1e-6
NEG = -1e30
ADAM_LR = 0.001
ADAM_B1 = 0.9
ADAM_B2 = 0.999
ADAM_EPS = 1e-08
ADAM_WD = 0.01
ADAM_STEP = 10
VMEM_LIMIT = 56 * 1024 * 1024
MESH_AXES = ("x", "y", "c")


def _cp(sem, **kw):
    return pltpu.CompilerParams(dimension_semantics=sem, vmem_limit_bytes=VMEM_LIMIT, **kw)


def _row_tile(m, target):
    best = None
    for t in range(16, min(m, target) + 1, 16):
        if m % t == 0:
            best = t
    return best or m


def _col_tile(n, target):
    best = None
    for t in range(LANES, min(n, target) + 1, LANES):
        if n % t == 0:
            best = t
    return best or n


def _mm(a, b, mode, name, out_dtype=F32, add=None, tm=640, tn=512, tk=2048):
    if mode == "nn":
        (M, K), (K2, N) = a.shape, b.shape
    elif mode == "nt":
        (M, K), (N, K2) = a.shape, b.shape
    else:
        (K, M), (K2, N) = a.shape, b.shape
    assert K == K2, (mode, a.shape, b.shape)
    if mode == "tn":
        tm_ = _col_tile(M, 1408)
        tk_ = _row_tile(K, 640)
    else:
        tm_ = _row_tile(M, tm)
        tk_ = _col_tile(K, tk) if K > tk else K
    tn_ = _col_tile(N, tn)
    nk = K // tk_
    if mode == "nn":
        a_spec = pl.BlockSpec((tm_, tk_), lambda i, j, k: (i, k))
        b_spec = pl.BlockSpec((tk_, tn_), lambda i, j, k: (k, j))
        dims = (((1,), (0,)), ((), ()))
    elif mode == "nt":
        a_spec = pl.BlockSpec((tm_, tk_), lambda i, j, k: (i, k))
        b_spec = pl.BlockSpec((tn_, tk_), lambda i, j, k: (j, k))
        dims = (((1,), (1,)), ((), ()))
    else:
        a_spec = pl.BlockSpec((tk_, tm_), lambda i, j, k: (k, i))
        b_spec = pl.BlockSpec((tk_, tn_), lambda i, j, k: (k, j))
        dims = (((0,), (0,)), ((), ()))
    o_spec = pl.BlockSpec((tm_, tn_), lambda i, j, k: (i, j))
    has_add = add is not None

    def body(*refs):
        if has_add:
            a_ref, b_ref, add_ref, o_ref, acc_ref = refs
        else:
            a_ref, b_ref, o_ref, acc_ref = refs
        k = pl.program_id(2)
        part = lax.dot_general(a_ref[...].astype(BF16), b_ref[...].astype(BF16), dims, preferred_element_type=F32)

        @pl.when(k == 0)
        def _():
            acc_ref[...] = part

        @pl.when(k > 0)
        def _():
            acc_ref[...] += part

        @pl.when(k == nk - 1)
        def _():
            r = acc_ref[...]
            if has_add:
                r = r + add_ref[...]
            o_ref[...] = r.astype(o_ref.dtype)

    ins = [a, b] + ([add] if has_add else [])
    in_specs = [a_spec, b_spec] + ([o_spec] if has_add else [])
    return pl.pallas_call(
        body,
        out_shape=jax.ShapeDtypeStruct((M, N), out_dtype),
        grid=(M // tm_, N // tn_, nk),
        in_specs=in_specs,
        out_specs=o_spec,
        scratch_shapes=[pltpu.VMEM((tm_, tn_), F32)],
        name=name,
        compiler_params=_cp(("parallel", "parallel", "arbitrary")),
    )(*ins)


def _rms_fwd(x, g, out_dtype, name):
    M, C = x.shape
    tm = _row_tile(M, 640)

    def body(x_ref, g_ref, o_ref):
        xf = x_ref[...]
        r = lax.rsqrt(jnp.mean(xf * xf, axis=-1, keepdims=True) + EPS)
        o_ref[...] = ((xf * r) * g_ref[...]).astype(o_ref.dtype)

    return pl.pallas_call(
        body,
        out_shape=jax.ShapeDtypeStruct((M, C), out_dtype),
        grid=(M // tm,),
        in_specs=[pl.BlockSpec((tm, C), lambda i: (i, 0)), pl.BlockSpec((1, C), lambda i: (0, 0))],
        out_specs=pl.BlockSpec((tm, C), lambda i: (i, 0)),
        name=name,
        compiler_params=_cp(("parallel",)),
    )(x, g.reshape(1, C))


def _rms_bwd(x, g, dy, dres, name):
    M, C = x.shape
    tm = _row_tile(M, 640)
    has_res = dres is not None

    def body(*refs):
        if has_res:
            x_ref, g_ref, dy_ref, dres_ref, dx_ref, dg_ref = refs
        else:
            x_ref, g_ref, dy_ref, dx_ref, dg_ref = refs
        xf = x_ref[...]
        r = lax.rsqrt(jnp.mean(xf * xf, axis=-1, keepdims=True) + EPS)
        xhat = xf * r
        dyf = dy_ref[...].astype(F32)

        @pl.when(pl.program_id(0) == 0)
        def _():
            dg_ref[...] = jnp.zeros_like(dg_ref)

        dg_ref[...] += jnp.sum(dyf * xhat, axis=0, keepdims=True)
        dxh = dyf * g_ref[...]
        dx = r * (dxh - xhat * jnp.mean(dxh * xhat, axis=-1, keepdims=True))
        if has_res:
            dx = dx + dres_ref[...]
        dx_ref[...] = dx

    row = pl.BlockSpec((tm, C), lambda i: (i, 0))
    vec = pl.BlockSpec((1, C), lambda i: (0, 0))
    ins = [x, g.reshape(1, C), dy] + ([dres] if has_res else [])
    return pl.pallas_call(
        body,
        out_shape=(jax.ShapeDtypeStruct((M, C), F32), jax.ShapeDtypeStruct((1, C), F32)),
        grid=(M // tm,),
        in_specs=[row, vec, row] + ([row] if has_res else []),
        out_specs=(row, vec),
        name=name,
        compiler_params=_cp(("arbitrary",)),
    )(*ins)


def _sigmoid(x):
    return 1.0 / (1.0 + jnp.exp(-x))


def _swiglu_fwd(g, u, name):
    M, F = g.shape
    tm = _row_tile(M, 320)

    def body(g_ref, u_ref, o_ref):
        gv = g_ref[...]
        o_ref[...] = ((gv * _sigmoid(gv)) * u_ref[...]).astype(o_ref.dtype)

    blk = pl.BlockSpec((tm, F), lambda i: (i, 0))
    return pl.pallas_call(
        body,
        out_shape=jax.ShapeDtypeStruct((M, F), BF16),
        grid=(M // tm,),
        in_specs=[blk, blk],
        out_specs=blk,
        name=name,
        compiler_params=_cp(("parallel",)),
    )(g, u)


def _swiglu_bwd(g, u, dh, name):
    M, F = g.shape
    tm = _row_tile(M, 320)

    def body(g_ref, u_ref, dh_ref, dg_ref, du_ref):
        gv = g_ref[...]
        sg = _sigmoid(gv)
        d = dh_ref[...]
        du_ref[...] = (d * (gv * sg)).astype(du_ref.dtype)
        dg_ref[...] = ((d * u_ref[...]) * (sg * (1.0 + gv * (1.0 - sg)))).astype(dg_ref.dtype)

    blk = pl.BlockSpec((tm, F), lambda i: (i, 0))
    return pl.pallas_call(
        body,
        out_shape=(jax.ShapeDtypeStruct((M, F), BF16), jax.ShapeDtypeStruct((M, F), BF16)),
        grid=(M // tm,),
        in_specs=[blk, blk, blk],
        out_specs=(blk, blk),
        name=name,
        compiler_params=_cp(("parallel",)),
    )(g, u, dh)


def _attn_tile(lp):
    return _col_tile(lp, 640)


def _head_specs(lp, t, offs):
    q_spec = pl.BlockSpec((t, LANES), lambda h, i: (i, offs[0] + h))
    k_spec = pl.BlockSpec((lp, LANES), lambda h, i: (0, offs[1] + h))
    v_spec = pl.BlockSpec((lp, LANES), lambda h, i: (0, offs[2] + h))
    return q_spec, k_spec, v_spec


def _attn_fwd(qa, ka, va, kb, fq, *, nh, offs, name, tile=640):
    lp = qa.shape[0]
    t = _col_tile(lp, tile)
    nq = lp // t
    has_fq = fq is not None

    def body(*refs):
        if has_fq:
            q_ref, k_ref, v_ref, kb_ref, fq_ref, o_ref, lse_ref = refs
        else:
            q_ref, k_ref, v_ref, kb_ref, o_ref, lse_ref = refs
        i = pl.program_id(1)
        q = q_ref[...]
        fqc = fq_ref[:, 0:1] if has_fq else None
        causal = lax.broadcasted_iota(jnp.int32, (t, t), 1) <= lax.broadcasted_iota(jnp.int32, (t, t), 0)

        def step(j, carry, masked):
            m, l, acc = carry
            st = pl.multiple_of(j * t, t)
            k = k_ref[pl.ds(st, t), :]
            v = v_ref[pl.ds(st, t), :]
            s = lax.dot_general(q, k, _NT, preferred_element_type=F32)
            bias = kb_ref[j]
            if has_fq:
                bias = fqc + bias
            s = s + bias
            if masked:
                s = jnp.where(causal, s, NEG)
            m_new = jnp.maximum(m, jnp.max(s, axis=1, keepdims=True))
            p = jnp.exp(s - m_new)
            alpha = jnp.exp(m - m_new)
            l = alpha * l + jnp.sum(p, axis=1, keepdims=True)
            acc = alpha * acc + jnp.dot(p.astype(BF16), v, preferred_element_type=F32)
            return m_new, l, acc

        init = (jnp.full((t, 1), NEG, F32), jnp.zeros((t, 1), F32), jnp.zeros((t, LANES), F32))
        carry = lax.fori_loop(0, i, lambda j, c: step(j, c, False), init)
        m, l, acc = step(i, carry, True)
        valid = (i * t + lax.broadcasted_iota(jnp.int32, (t, 1), 0)) >= PAD0
        o_ref[...] = jnp.where(valid, acc / l, 0.0).astype(o_ref.dtype)
        lse_ref[...] = jnp.broadcast_to(m + jnp.log(l), (t, LANES))

    q_spec, k_spec, v_spec = _head_specs(lp, t, offs)
    kb_spec = pl.BlockSpec((None, nq, 1, t), lambda h, i: (h, 0, 0, 0))
    row_spec = pl.BlockSpec((t, LANES), lambda h, i: (i, h))
    ins = [qa, ka, va, kb] + ([fq] if has_fq else [])
    return pl.pallas_call(
        body,
        out_shape=(jax.ShapeDtypeStruct((lp, nh * LANES), BF16), jax.ShapeDtypeStruct((lp, nh * LANES), F32)),
        grid=(nh, nq),
        in_specs=[q_spec, k_spec, v_spec, kb_spec] + ([row_spec] if has_fq else []),
        out_specs=(row_spec, row_spec),
        name=name,
        compiler_params=_cp(("parallel", "arbitrary")),
    )(*ins)


def _attn_bwd(qa, ka, va, kb, fq, o, do, lse, *, nh, offs, name, tile=640):
    lp = qa.shape[0]
    t = _col_tile(lp, tile)
    nq = lp // t
    has_fq = fq is not None

    def body(*refs):
        if has_fq:
            q_ref, k_ref, v_ref, kb_ref, fq_ref, o_ref, do_ref, lse_ref, dq_ref, dk_ref, dv_ref, dkb_ref, dqb_ref, dk_acc, dv_acc = refs
        else:
            q_ref, k_ref, v_ref, kb_ref, o_ref, do_ref, lse_ref, dq_ref, dk_ref, dv_ref, dk_acc, dv_acc = refs
        i = pl.program_id(1)

        @pl.when(i == 0)
        def _():
            dk_acc[...] = jnp.zeros_like(dk_acc)
            dv_acc[...] = jnp.zeros_like(dv_acc)
            if has_fq:
                dkb_ref[...] = jnp.zeros_like(dkb_ref)

        q = q_ref[...]
        dov = do_ref[...]
        delta = jnp.sum(o_ref[...].astype(F32) * dov.astype(F32), axis=1, keepdims=True)
        lse_c = lse_ref[:, 0:1]
        fqc = fq_ref[:, 0:1] if has_fq else None
        causal = lax.broadcasted_iota(jnp.int32, (t, t), 1) <= lax.broadcasted_iota(jnp.int32, (t, t), 0)

        def step(j, carry, masked):
            dq_acc, rs = carry
            st = pl.multiple_of(j * t, t)
            k = k_ref[pl.ds(st, t), :]
            v = v_ref[pl.ds(st, t), :]
            s = lax.dot_general(q, k, _NT, preferred_element_type=F32)
            bias = kb_ref[j]
            if has_fq:
                bias = fqc + bias
            s = s + bias
            if masked:
                s = jnp.where(causal, s, NEG)
            p = jnp.exp(s - lse_c)
            dp = lax.dot_general(dov, v, (((1,), (1,)), ((), ())), preferred_element_type=F32)
            ds = p * (dp - delta)
            dv_acc[pl.ds(st, t), :] += lax.dot_general(p.astype(BF16), dov, (((0,), (0,)), ((), ())), preferred_element_type=F32)
            dsb = ds.astype(BF16)
            dk_acc[pl.ds(st, t), :] += lax.dot_general(dsb, q, (((0,), (0,)), ((), ())), preferred_element_type=F32)
            if has_fq:
                dkb_ref[j] += jnp.sum(ds, axis=0, keepdims=True)
                rs = rs + jnp.sum(ds, axis=1, keepdims=True)
            return dq_acc + jnp.dot(dsb, k, preferred_element_type=F32), rs

        carry = (jnp.zeros((t, LANES), F32), jnp.zeros((t, 1), F32))
        carry = lax.fori_loop(0, i, lambda j, c: step(j, c, False), carry)
        dq_acc, rs = step(i, carry, True)
        dq_ref[...] = dq_acc.astype(dq_ref.dtype)
        if has_fq:
            dqb_ref[...] = jnp.broadcast_to(rs, (t, LANES))

        @pl.when(i == nq - 1)
        def _():
            dk_ref[...] = dk_acc[...].astype(dk_ref.dtype)
            dv_ref[...] = dv_acc[...].astype(dv_ref.dtype)

    q_spec, k_spec, v_spec = _head_specs(lp, t, offs)
    kb_spec = pl.BlockSpec((None, nq, 1, t), lambda h, i: (h, 0, 0, 0))
    row_spec = pl.BlockSpec((t, LANES), lambda h, i: (i, h))
    col_spec = pl.BlockSpec((lp, LANES), lambda h, i: (0, h))
    ins = [qa, ka, va, kb] + ([fq] if has_fq else []) + [o, do, lse]
    wide = jax.ShapeDtypeStruct((lp, nh * LANES), BF16)
    extra_shapes = (jax.ShapeDtypeStruct(kb.shape, F32), jax.ShapeDtypeStruct((lp, nh * LANES), F32)) if has_fq else ()
    extra_specs = (kb_spec, row_spec) if has_fq else ()
    return pl.pallas_call(
        body,
        out_shape=(wide, wide, wide) + extra_shapes,
        grid=(nh, nq),
        in_specs=[q_spec, k_spec, v_spec, kb_spec] + ([row_spec] if has_fq else []) + [row_spec, row_spec, row_spec],
        out_specs=(row_spec, col_spec, col_spec) + extra_specs,
        scratch_shapes=[pltpu.VMEM((lp, LANES), F32), pltpu.VMEM((lp, LANES), F32)],
        name=name,
        compiler_params=_cp(("parallel", "arbitrary")),
    )(*ins)


SB_TK = 128


def _split3(x):
    hi = x.astype(BF16)
    r1 = x - hi.astype(F32)
    mid = r1.astype(BF16)
    lo = (r1 - mid.astype(F32)).astype(BF16)
    return hi, mid, lo


SB_RC = 128


_NT = (((1,), (1,)), ((), ()))
_TN = (((0,), (0,)), ((), ()))


def _sb_logits(zraw, kbj, mask):
    z = zraw + kbj
    if mask is not None:
        z = jnp.where(mask, z, NEG)
    e = jnp.exp(-jnp.abs(z))
    g = jnp.minimum(z, 0.0) - jnp.log(1.0 + e)
    lk = g - z
    return z, e, g, lk


def _dot3_parts(parts, tri):
    d = functools.partial(jnp.dot, preferred_element_type=F32)
    return d(parts[0], tri) + d(parts[1], tri) + d(parts[2], tri)


def _split2(x):
    hi = x.astype(BF16)
    return hi, (x - hi.astype(F32)).astype(BF16)


def _dot_parts(parts, m):
    out = jnp.dot(parts[0], m, preferred_element_type=F32)
    for p in parts[1:]:
        out = out + jnp.dot(p, m, preferred_element_type=F32)
    return out


def _tri(n, pred):
    return pred(lax.broadcasted_iota(jnp.int32, (n, n), 0), lax.broadcasted_iota(jnp.int32, (n, n), 1)).astype(BF16)


def _sb_diag_chunks(jj, nrc, rc, tk):
    plan = []
    for r in range(nrc):
        lo_row, hi_row = r * rc, (r + 1) * rc - 1
        lo_col, hi_col = jj * tk, (jj + 1) * tk - 1
        if hi_row <= lo_col:
            plan.append(None)
        elif lo_row > hi_col:
            plan.append("all")
        else:
            plan.append(lo_col - lo_row)
    return plan


def _sb_fwd(qa, kb, *, nh, name, tq=640):
    lp = qa.shape[0]
    tq = _col_tile(lp, tq)
    tk = SB_TK
    rc = min(SB_RC, tq)
    nq, sub, nrc = lp // tq, tq // tk, tq // rc

    def body(q_ref, k_ref, v_ref, kb_ref, o_ref, c_scr, acc_scr):
        i = pl.program_id(1)
        c_scr[...] = jnp.zeros_like(c_scr)
        acc_scr[...] = jnp.zeros_like(acc_scr)
        tri = _tri(tk, lambda r, c: r > c)
        row_io = lax.broadcasted_iota(jnp.int32, (rc, tk), 0)
        col_io = lax.broadcasted_iota(jnp.int32, (rc, tk), 1)

        def scores(j, rows):
            k = k_ref[pl.ds(pl.multiple_of(j * tk, tk), tk), :]
            return [lax.dot_general(q_ref[rs, :], k, _NT, preferred_element_type=F32) for rs in rows]

        def weights(j, rows, masks, zs):
            kbj = kb_ref[j]
            gs, splits, firsts = [], [], []
            for mask, zraw in zip(masks, zs):
                _, _, g, lk = _sb_logits(zraw, kbj, mask)
                gs.append(g)
                firsts.append(lk[:, 0:1])
                splits.append(_split2(lk))
            sums = [_dot_parts(p, tri) for p in splits]
            avs = [jnp.exp(g + (sm + c_scr[rs, :])).astype(BF16) for g, sm, rs in zip(gs, sums, rows)]
            for rs, sm, first in zip(rows, sums, firsts):
                c_scr[rs, :] += jnp.broadcast_to(sm[:, 0:1] + first, (rc, tk))
            return avs

        def values(j, rows, avs):
            v = v_ref[pl.ds(pl.multiple_of(j * tk, tk), tk), :]
            pvs = [jnp.dot(a, v, preferred_element_type=F32) for a in avs]
            for rs, pv in zip(rows, pvs):
                acc_scr[rs, :] += pv

        for jj in reversed(range(sub)):
            plan = _sb_diag_chunks(jj, nrc, rc, tk)
            live = [r for r, what in enumerate(plan) if what is not None]
            rows = [pl.ds(r * rc, rc) for r in live]
            masks = [None if plan[r] == "all" else (col_io + plan[r]) < row_io for r in live]
            j = i * sub + jj
            values(j, rows, weights(j, rows, masks, scores(j, rows)))

        n = i * sub
        rows = [pl.ds(r * rc, rc) for r in range(nrc)]
        nomask = [None] * nrc
        blk = lambda t: jnp.maximum(n - 1 - t, 0)

        def left(m, carry):
            zs, avs = carry
            t0, t1 = 2 * m, 2 * m + 1
            zs_next = scores(blk(t1 + 1), rows)
            zs1 = scores(blk(t1), rows)
            values(blk(t0 - 1), rows, avs)
            av0 = weights(blk(t0), rows, nomask, zs)
            av1 = weights(blk(t1), rows, nomask, zs1)
            values(blk(t0), rows, av0)
            return zs_next, av1

        first = (scores(blk(0), rows), [jnp.zeros((rc, tk), BF16)] * nrc)
        zs, avs = lax.fori_loop(0, n // 2, left, first)
        values(blk(2 * (n // 2) - 1), rows, avs)

        @pl.when(n % 2 == 1)
        def _():
            values(0, rows, weights(0, rows, nomask, zs))

        o_ref[...] = acc_scr[...].astype(o_ref.dtype)

    q_spec, k_spec, v_spec = _head_specs(lp, tq, (0, nh, 2 * nh))
    kb_spec = pl.BlockSpec((None, lp // tk, 1, tk), lambda h, i: (h, 0, 0, 0))
    row_spec = pl.BlockSpec((tq, LANES), lambda h, i: (i, h))
    return pl.pallas_call(
        body,
        out_shape=jax.ShapeDtypeStruct((lp, nh * LANES), BF16),
        grid=(nh, nq),
        in_specs=[q_spec, k_spec, v_spec, kb_spec],
        out_specs=row_spec,
        scratch_shapes=[pltpu.VMEM((tq, tk), F32), pltpu.VMEM((tq, LANES), F32)],
        name=name,
        compiler_params=_cp(("parallel", "arbitrary")),
    )(qa, qa, qa, kb)


def _sb_bwd(qa, kb, do, *, nh, name, tq=640):
    lp = qa.shape[0]
    tq = _col_tile(lp, tq)
    tk = SB_TK
    rc = min(SB_RC, tq)
    nq, nk, sub, nrc = lp // tq, lp // tk, tq // tk, tq // rc

    def body(q_ref, k_ref, v_ref, kb_ref, do_ref, dq_ref, dk_ref, dv_ref, dkt_acc, dvt_acc, w_scr, b_scr, c_scr, u_scr, dq_scr):
        i = pl.program_id(1)

        @pl.when(i == 0)
        def _():
            dkt_acc[...] = jnp.zeros_like(dkt_acc)
            dvt_acc[...] = jnp.zeros_like(dvt_acc)

        c_scr[...] = jnp.zeros_like(c_scr)
        u_scr[...] = jnp.zeros_like(u_scr)
        dq_scr[...] = jnp.zeros_like(dq_scr)
        tri_gt = _tri(tk, lambda r, c: r > c)
        tri_lt = _tri(tk, lambda r, c: r < c)
        row_io = lax.broadcasted_iota(jnp.int32, (rc, tk), 0)
        col_io = lax.broadcasted_iota(jnp.int32, (rc, tk), 1)
        qt = q_ref[...].astype(F32).T.astype(BF16)
        dot_t = do_ref[...].astype(F32).T.astype(BF16)
        zero_blk = jnp.zeros((rc, tk), BF16)

        def full_rows(plan, parts):
            it = iter(parts)
            return jnp.concatenate([zero_blk if what is None else next(it) for what in plan], axis=0)

        def key_rows(j):
            return pl.ds(pl.multiple_of(j * tk, tk), tk)

        def scores(j, rows):
            k = k_ref[key_rows(j), :]
            v = v_ref[key_rows(j), :]
            zs = [lax.dot_general(q_ref[rs, :], k, _NT, preferred_element_type=F32) for rs in rows]
            das = [lax.dot_general(do_ref[rs, :], v, _NT, preferred_element_type=F32) for rs in rows]
            return zs, das

        def weights(j, rows, masks, zs, das):
            kbj = kb_ref[j]
            gs, splits, firsts = [], [], []
            for rs, mask, zraw in zip(rows, masks, zs):
                _, _, g, lk = _sb_logits(zraw, kbj, mask)
                b_scr[j, rs, :] = jnp.exp(g).astype(BF16)
                gs.append(g)
                firsts.append(lk[:, 0:1])
                splits.append(_split2(lk))
            sums = [_dot_parts(p, tri_gt) for p in splits]
            avs = []
            for rs, g, sm, da, first in zip(rows, gs, sums, das, firsts):
                a = jnp.exp(g + (sm + c_scr[rs, :]))
                w_scr[j, rs, :] = (a * da).astype(BF16)
                avs.append(a.astype(BF16))
                c_scr[rs, :] += jnp.broadcast_to(sm[:, 0:1] + first, (rc, tk))
            return avs

        def dv_update(j, plan, avs):
            dvt_acc[j] += jnp.dot(dot_t, full_rows(plan, avs), preferred_element_type=F32)

        for jj in reversed(range(sub)):
            plan = _sb_diag_chunks(jj, nrc, rc, tk)
            live = [r for r, what in enumerate(plan) if what is not None]
            rows = [pl.ds(r * rc, rc) for r in live]
            masks = [None if plan[r] == "all" else (col_io + plan[r]) < row_io for r in live]
            j = i * sub + jj
            dv_update(j, plan, weights(j, rows, masks, *scores(j, rows)))

        n = i * sub
        everything = ["all"] * nrc
        rows = [pl.ds(r * rc, rc) for r in range(nrc)]
        nomask = [None] * nrc
        def left1(m, carry):
            j0, j1 = n - 1 - 2 * m, n - 2 - 2 * m
            sc0, sc1 = scores(j0, rows), scores(j1, rows)
            av0 = weights(j0, rows, nomask, *sc0)
            av1 = weights(j1, rows, nomask, *sc1)
            dv_update(j0, everything, av0)
            dv_update(j1, everything, av1)
            return carry

        lax.fori_loop(0, n // 2, left1, 0)

        @pl.when(n % 2 == 1)
        def _():
            dv_update(0, everything, weights(0, rows, nomask, *scores(0, rows)))

        def prefix(j, rows):
            return [jnp.dot(w_scr[j, rs, :], tri_lt, preferred_element_type=F32) for rs in rows]

        def dlogits(j, rows, sums):
            dzs = []
            for rs, sm in zip(rows, sums):
                w = w_scr[j, rs, :].astype(F32)
                beta = b_scr[j, rs, :].astype(F32)
                dzs.append((w - beta * (w + (sm + u_scr[rs, :]))).astype(BF16))
                u_scr[rs, :] += jnp.broadcast_to(sm[:, tk - 1:tk] + w[:, tk - 1:tk], (rc, tk))
            return dzs

        def dqk_update(j, plan, rows, dzs):
            k = k_ref[key_rows(j), :]
            dqs = [jnp.dot(dz, k, preferred_element_type=F32) for dz in dzs]
            for rs, dq in zip(rows, dqs):
                dq_scr[rs, :] += dq
            dkt_acc[j] += jnp.dot(qt, full_rows(plan, dzs), preferred_element_type=F32)

        def left2(m, carry):
            j0, j1 = 2 * m, 2 * m + 1
            s0, s1 = prefix(j0, rows), prefix(j1, rows)
            dz0 = dlogits(j0, rows, s0)
            dz1 = dlogits(j1, rows, s1)
            dqk_update(j0, everything, rows, dz0)
            dqk_update(j1, everything, rows, dz1)
            return carry

        lax.fori_loop(0, n // 2, left2, 0)

        @pl.when(n % 2 == 1)
        def _():
            dqk_update(n - 1, everything, rows, dlogits(n - 1, rows, prefix(n - 1, rows)))
        for jj in range(sub):
            plan = _sb_diag_chunks(jj, nrc, rc, tk)
            live_rows = [pl.ds(r * rc, rc) for r, what in enumerate(plan) if what is not None]
            j = i * sub + jj
            dqk_update(j, plan, live_rows, dlogits(j, live_rows, prefix(j, live_rows)))
        dq_ref[...] = dq_scr[...].astype(dq_ref.dtype)

        @pl.when(i == nq - 1)
        def _():
            def flush(j, carry):
                dk_ref[key_rows(j), :] = dkt_acc[j].T.astype(dk_ref.dtype)
                dv_ref[key_rows(j), :] = dvt_acc[j].T.astype(dv_ref.dtype)
                return carry

            lax.fori_loop(0, nk, flush, 0)

    q_spec, k_spec, v_spec = _head_specs(lp, tq, (0, nh, 2 * nh))
    kb_spec = pl.BlockSpec((None, nk, 1, tk), lambda h, i: (h, 0, 0, 0))
    row_spec = pl.BlockSpec((tq, LANES), lambda h, i: (i, h))
    col_spec = pl.BlockSpec((lp, LANES), lambda h, i: (0, h))
    wide = jax.ShapeDtypeStruct((lp, nh * LANES), BF16)
    return pl.pallas_call(
        body,
        out_shape=(wide, wide, wide),
        grid=(nh, nq),
        in_specs=[q_spec, k_spec, v_spec, kb_spec, row_spec],
        out_specs=(row_spec, col_spec, col_spec),
        scratch_shapes=[
            pltpu.VMEM((nk, LANES, tk), F32),
            pltpu.VMEM((nk, LANES, tk), F32),
            pltpu.VMEM((nk, tq, tk), BF16),
            pltpu.VMEM((nk, tq, tk), BF16),
            pltpu.VMEM((tq, tk), F32),
            pltpu.VMEM((tq, tk), F32),
            pltpu.VMEM((tq, LANES), F32),
        ],
        name=name,
        compiler_params=_cp(("parallel", "arbitrary")),
    )(qa, qa, qa, kb, do)


def _pool_counts(pos, win):
    return jnp.clip(pos + 1, 1, win).astype(F32)


def _pool_fwd(a, name):
    lp, C = a.shape
    tm = _row_tile(lp, 640)
    hb = tm // POOL_HALO

    def body(prev_ref, cur_ref, o_ref, xs):
        i = pl.program_id(0)
        xs[pl.ds(0, POOL_HALO), :] = jnp.where(i > 0, prev_ref[...], 0.0)
        xs[pl.ds(POOL_HALO, tm), :] = cur_ref[...]
        pos = i * tm + lax.broadcasted_iota(jnp.int32, (tm, 1), 0) - PAD0
        for g, win in enumerate(POOL_WINDOWS):
            cols = pl.ds(g * POOL_GROUP, POOL_GROUP)
            s = xs[pl.ds(POOL_HALO, tm), cols]
            for k in range(1, win):
                s = s + xs[pl.ds(POOL_HALO - k, tm), cols]
            o_ref[:, cols] = (s / _pool_counts(pos, win) - xs[pl.ds(POOL_HALO, tm), cols]).astype(o_ref.dtype)

    return pl.pallas_call(
        body,
        out_shape=jax.ShapeDtypeStruct((lp, C), BF16),
        grid=(lp // tm,),
        in_specs=[
            pl.BlockSpec((POOL_HALO, C), lambda i: (jnp.maximum(i * hb - 1, 0), 0)),
            pl.BlockSpec((tm, C), lambda i: (i, 0)),
        ],
        out_specs=pl.BlockSpec((tm, C), lambda i: (i, 0)),
        scratch_shapes=[pltpu.VMEM((tm + POOL_HALO, C), F32)],
        name=name,
        compiler_params=_cp(("parallel",)),
    )(a, a)


def _pool_bwd(dp, name):
    lp, C = dp.shape
    tm = _row_tile(lp, 640)
    hb = tm // POOL_HALO
    nt = lp // tm
    last_halo = lp // POOL_HALO - 1

    def body(cur_ref, next_ref, o_ref, xs):
        i = pl.program_id(0)
        pos = i * tm + lax.broadcasted_iota(jnp.int32, (tm, 1), 0) - PAD0
        pos_h = (i + 1) * tm + lax.broadcasted_iota(jnp.int32, (POOL_HALO, 1), 0) - PAD0
        for g, win in enumerate(POOL_WINDOWS):
            cols = pl.ds(g * POOL_GROUP, POOL_GROUP)
            cur = cur_ref[:, cols]
            xs[pl.ds(0, tm), cols] = cur / _pool_counts(pos, win)
            xs[pl.ds(tm, POOL_HALO), cols] = jnp.where(i < nt - 1, next_ref[:, cols], 0.0) / _pool_counts(pos_h, win)
            s = xs[pl.ds(0, tm), cols]
            for k in range(1, win):
                s = s + xs[pl.ds(k, tm), cols]
            o_ref[:, cols] = jnp.where(pos >= 0, s - cur, 0.0)

    return pl.pallas_call(
        body,
        out_shape=jax.ShapeDtypeStruct((lp, C), F32),
        grid=(nt,),
        in_specs=[
            pl.BlockSpec((tm, C), lambda i: (i, 0)),
            pl.BlockSpec((POOL_HALO, C), lambda i: (jnp.minimum((i + 1) * hb, last_halo), 0)),
        ],
        out_specs=pl.BlockSpec((tm, C), lambda i: (i, 0)),
        scratch_shapes=[pltpu.VMEM((tm + POOL_HALO, C), F32)],
        name=name,
        compiler_params=_cp(("parallel",)),
    )(dp, dp)


def _scale_add(h, pre, scale, name):
    M, C = h.shape
    tm = _row_tile(M, 640)

    def body(h_ref, p_ref, s_ref, o_ref):
        o_ref[...] = h_ref[...] + p_ref[...] * s_ref[...]

    row = pl.BlockSpec((tm, C), lambda i: (i, 0))
    return pl.pallas_call(
        body,
        out_shape=jax.ShapeDtypeStruct((M, C), F32),
        grid=(M // tm,),
        in_specs=[row, row, pl.BlockSpec((1, C), lambda i: (0, 0))],
        out_specs=row,
        name=name,
        compiler_params=_cp(("parallel",)),
    )(h, pre, scale.reshape(1, C))


def _scale_bwd(dh, pre, scale, name):
    M, C = dh.shape
    tm = _row_tile(M, 640)

    def body(dh_ref, p_ref, s_ref, dp_ref, ds_ref):
        @pl.when(pl.program_id(0) == 0)
        def _():
            ds_ref[...] = jnp.zeros_like(ds_ref)

        d = dh_ref[...]
        ds_ref[...] += jnp.sum(d * p_ref[...], axis=0, keepdims=True)
        dp_ref[...] = (d * s_ref[...]).astype(dp_ref.dtype)

    row = pl.BlockSpec((tm, C), lambda i: (i, 0))
    vec = pl.BlockSpec((1, C), lambda i: (0, 0))
    return pl.pallas_call(
        body,
        out_shape=(jax.ShapeDtypeStruct((M, C), BF16), jax.ShapeDtypeStruct((1, C), F32)),
        grid=(M // tm,),
        in_specs=[row, row, vec],
        out_specs=(row, vec),
        name=name,
        compiler_params=_cp(("arbitrary",)),
    )(dh, pre, scale.reshape(1, C))


def _gate_parts(z):
    e = jnp.exp(-jnp.abs(z))
    return e, jnp.minimum(z, 0.0) - jnp.log(1.0 + e)


def _tri_dot3(tri, x):
    hi, mid, lo = _split3(x)
    d = functools.partial(jnp.dot, preferred_element_type=F32)
    return d(tri, hi) + d(tri, mid) + d(tri, lo)


def _gate_fwd(x, b, name):
    lp, C = x.shape
    tm = _row_tile(lp, 640)

    def body(x_ref, b_ref, o_ref, carry):
        i = pl.program_id(0)

        @pl.when(i == 0)
        def _():
            carry[...] = jnp.zeros_like(carry)

        _, ls = _gate_parts(x_ref[...] + b_ref[...])
        rows = i * tm + lax.broadcasted_iota(jnp.int32, (tm, 1), 0)
        ls = jnp.where(rows >= PAD0, ls, 0.0)
        tri = (lax.broadcasted_iota(jnp.int32, (tm, tm), 0) >= lax.broadcasted_iota(jnp.int32, (tm, tm), 1)).astype(BF16)
        f = _tri_dot3(tri, ls) + carry[...]
        o_ref[...] = f
        carry[...] = f[tm - 1:tm, :]

    return pl.pallas_call(
        body,
        out_shape=jax.ShapeDtypeStruct((lp, C), F32),
        grid=(lp // tm,),
        in_specs=[pl.BlockSpec((tm, C), lambda i: (i, 0)), pl.BlockSpec((1, C), lambda i: (0, 0))],
        out_specs=pl.BlockSpec((tm, C), lambda i: (i, 0)),
        scratch_shapes=[pltpu.VMEM((1, C), F32)],
        name=name,
        compiler_params=_cp(("arbitrary",)),
    )(x, b)


def _gate_bwd(x, b, df, name):
    lp, C = x.shape
    tm = _row_tile(lp, 640)
    nt = lp // tm

    def body(x_ref, b_ref, df_ref, dx_ref, db_ref, carry):
        i = pl.program_id(0)

        @pl.when(i == 0)
        def _():
            carry[...] = jnp.zeros_like(carry)
            db_ref[...] = jnp.zeros_like(db_ref)

        z = x_ref[...] + b_ref[...]
        e, _ = _gate_parts(z)
        tri = (lax.broadcasted_iota(jnp.int32, (tm, tm), 0) <= lax.broadcasted_iota(jnp.int32, (tm, tm), 1)).astype(BF16)
        r = _tri_dot3(tri, df_ref[...]) + carry[...]
        carry[...] = r[0:1, :]
        rows = (nt - 1 - i) * tm + lax.broadcasted_iota(jnp.int32, (tm, 1), 0)
        dx = jnp.where(rows >= PAD0, r * (jnp.where(z >= 0.0, e, 1.0) / (1.0 + e)), 0.0)
        dx_ref[...] = dx
        db_ref[...] += jnp.sum(dx, axis=0, keepdims=True)

    rev = pl.BlockSpec((tm, C), lambda i: (nt - 1 - i, 0))
    vec = pl.BlockSpec((1, C), lambda i: (0, 0))
    return pl.pallas_call(
        body,
        out_shape=(jax.ShapeDtypeStruct((lp, C), F32), jax.ShapeDtypeStruct((1, C), F32)),
        grid=(nt,),
        in_specs=[rev, vec, rev],
        out_specs=(rev, vec),
        scratch_shapes=[pltpu.VMEM((1, C), F32)],
        name=name,
        compiler_params=_cp(("arbitrary",)),
    )(x, b, df)


MLA_SCALE = (MLA_NOPE + MLA_ROPE) ** -0.5


def _rope_apply(x, c, a, b):
    return x * c + pltpu.roll(x, LANES - 16, 1) * a + pltpu.roll(x, 16, 1) * b


def _rope_transpose(dy, c, a, b):
    return dy * c + pltpu.roll(dy * a, 16, 1) + pltpu.roll(dy * b, LANES - 16, 1)


def _mla_prep_fwd(q, kmat, kr, c, a, b, name):
    lp, W = q.shape
    nh = W // LANES
    tm = _row_tile(lp, 640)

    def body(q_ref, k_ref, kr_ref, c_ref, a_ref, b_ref, qo_ref, ko_ref):
        cv, av, bv = c_ref[...], a_ref[...], b_ref[...]
        qo_ref[...] = (_rope_apply(q_ref[...], cv, av, bv) * MLA_SCALE).astype(qo_ref.dtype)
        ko_ref[...] = (k_ref[...] + _rope_apply(kr_ref[...], cv, av, bv)).astype(ko_ref.dtype)

    head = pl.BlockSpec((tm, LANES), lambda i, h: (i, h))
    tab = pl.BlockSpec((tm, LANES), lambda i, h: (i, 0))
    wide = jax.ShapeDtypeStruct((lp, W), BF16)
    return pl.pallas_call(
        body,
        out_shape=(wide, wide),
        grid=(lp // tm, nh),
        in_specs=[head, head, tab, tab, tab, tab],
        out_specs=(head, head),
        name=name,
        compiler_params=_cp(("parallel", "parallel")),
    )(q, kmat, kr, c, a, b)


def _mla_prep_bwd(dq, dk, c, a, b, name):
    lp, W = dq.shape
    nh = W // LANES
    tm = _row_tile(lp, 640)

    def body(dq_ref, dk_ref, c_ref, a_ref, b_ref, dqo_ref, dkr_ref):
        cv, av, bv = c_ref[...], a_ref[...], b_ref[...]
        ksum = jnp.zeros((tm, LANES), F32)
        for h in range(nh):
            cols = pl.ds(h * LANES, LANES)
            dqo_ref[:, cols] = _rope_transpose(dq_ref[:, cols].astype(F32) * MLA_SCALE, cv, av, bv).astype(dqo_ref.dtype)
            ksum = ksum + dk_ref[:, cols].astype(F32)
        dkr_ref[...] = _rope_transpose(ksum, cv, av, bv)

    wide = pl.BlockSpec((tm, W), lambda i: (i, 0))
    tab = pl.BlockSpec((tm, LANES), lambda i: (i, 0))
    return pl.pallas_call(
        body,
        out_shape=(jax.ShapeDtypeStruct((lp, W), BF16), jax.ShapeDtypeStruct((lp, LANES), F32)),
        grid=(lp // tm,),
        in_specs=[wide, wide, tab, tab, tab],
        out_specs=(wide, tab),
        name=name,
        compiler_params=_cp(("parallel",)),
    )(dq, dk, c, a, b)


def _loss_head(h, g, target, name):
    lp, C = h.shape
    tm = _row_tile(lp, 640)
    nt = lp // tm

    def body(h_ref, g_ref, t_ref, loss_ref, dh_ref, dg_ref, sq):
        i = pl.program_id(0)

        @pl.when(i == 0)
        def _():
            dg_ref[...] = jnp.zeros_like(dg_ref)
            sq[...] = jnp.zeros_like(sq)

        xf = h_ref[...]
        gv = g_ref[...]
        r = lax.rsqrt(jnp.mean(xf * xf, axis=-1, keepdims=True) + EPS)
        xhat = xf * r
        rows = i * tm + lax.broadcasted_iota(jnp.int32, (tm, 1), 0)
        err = jnp.where(rows >= PAD0 + N_META, xhat * gv - t_ref[...], 0.0)
        sq[...] += jnp.sum(err * err, axis=0, keepdims=True)
        dy = err * (1.0 / C)
        dg_ref[...] += jnp.sum(dy * xhat, axis=0, keepdims=True)
        dxh = dy * gv
        dh_ref[...] = r * (dxh - xhat * jnp.mean(dxh * xhat, axis=-1, keepdims=True))

        @pl.when(i == nt - 1)
        def _():
            loss_ref[...] = jnp.broadcast_to(jnp.sum(sq[...], axis=1, keepdims=True) * (0.5 / C), (1, LANES))

    row = pl.BlockSpec((tm, C), lambda i: (i, 0))
    vec = pl.BlockSpec((1, C), lambda i: (0, 0))
    return pl.pallas_call(
        body,
        out_shape=(jax.ShapeDtypeStruct((1, LANES), F32), jax.ShapeDtypeStruct((lp, C), F32), jax.ShapeDtypeStruct((1, C), F32)),
        grid=(nt,),
        in_specs=[row, vec, row],
        out_specs=(pl.BlockSpec((1, LANES), lambda i: (0, 0)), row, vec),
        scratch_shapes=[pltpu.VMEM((1, C), F32)],
        name=name,
        compiler_params=_cp(("arbitrary",)),
    )(h, g.reshape(1, C), target)


def _adamw(w, g, m, v, name):
    shape = w.shape
    C = shape[-1]
    R = w.size // C
    tr = R
    if R % 8 == 0:
        for cand in range(8, R + 1, 8):
            if R % cand == 0 and cand * C * 4 <= (1 << 20):
                tr = cand
    c1 = 1.0 - ADAM_B1 ** ADAM_STEP
    c2 = 1.0 - ADAM_B2 ** ADAM_STEP

    def body(w_ref, g_ref, m_ref, v_ref, d_ref, nm_ref, nv_ref):
        gv = g_ref[...]
        nm = ADAM_B1 * m_ref[...] + (1.0 - ADAM_B1) * gv
        nv = ADAM_B2 * v_ref[...] + (1.0 - ADAM_B2) * (gv * gv)
        nm_ref[...] = nm
        nv_ref[...] = nv
        d_ref[...] = -ADAM_LR * ((nm / c1) / (jnp.sqrt(nv / c2) + ADAM_EPS) + ADAM_WD * w_ref[...])

    blk = pl.BlockSpec((tr, C), lambda i: (i, 0))
    out = jax.ShapeDtypeStruct((R, C), F32)
    outs = pl.pallas_call(
        body,
        out_shape=(out, out, out),
        grid=(R // tr,),
        in_specs=[blk] * 4,
        out_specs=(blk, blk, blk),
        name=name,
        compiler_params=_cp(("parallel",)),
    )(*(t.reshape(R, C) for t in (w, g, m, v)))
    return tuple(t.reshape(shape) for t in outs)


def _exchange(send, axes, same, name):
    na = len(axes)
    n = 1 << na
    _, R, C = send.shape
    parts = max(p for p in (8, 4, 2, 1) if R % (16 * p) == 0 or p == 1)
    pr = R // parts

    def body(send_ref, recv_ref, send_sems, recv_sems, local_sem):
        coords = {ax: lax.axis_index(ax) for ax in MESH_AXES}
        me = 0
        for ax in axes:
            me = me * 2 + coords[ax]

        def member(r):
            dev = dict(coords)
            for b, ax in enumerate(axes):
                if (r >> (na - 1 - b)) & 1:
                    dev[ax] = 1 - dev[ax]
            return tuple(dev[ax] for ax in MESH_AXES)

        def chunk(j, p):
            return (send_ref.at[0] if same else send_ref.at[j]).at[pl.ds(p * pr, pr)]

        def slot(j, p):
            return recv_ref.at[j].at[pl.ds(p * pr, pr)]

        own = pltpu.make_async_copy(send_ref.at[0] if same else send_ref.at[me], recv_ref.at[me], local_sem)
        own.start()
        copies = []
        for r in range(1, n):
            peer = me ^ r
            for p in range(parts):
                cp = pltpu.make_async_remote_copy(
                    src_ref=chunk(peer, p), dst_ref=slot(me, p), send_sem=send_sems.at[r, p], recv_sem=recv_sems.at[r, p],
                    device_id=member(r), device_id_type=pl.DeviceIdType.MESH)
                cp.start()
                copies.append(cp)
        for r in range(1, n):
            for p in range(parts):
                arrival = pltpu.make_async_remote_copy(
                    src_ref=chunk(me, p), dst_ref=slot(me ^ r, p), send_sem=send_sems.at[r, p], recv_sem=recv_sems.at[r, p],
                    device_id=member(r), device_id_type=pl.DeviceIdType.MESH)
                arrival.wait_recv()
        for cp in copies:
            cp.wait_send()
        own.wait()

    any_spec = pl.BlockSpec(memory_space=pl.ANY)
    return pl.pallas_call(
        body,
        out_shape=jax.ShapeDtypeStruct((n, R, C), send.dtype),
        in_specs=[any_spec],
        out_specs=any_spec,
        scratch_shapes=[pltpu.SemaphoreType.DMA((n, parts)), pltpu.SemaphoreType.DMA((n, parts)), pltpu.SemaphoreType.DMA],
        name=name,
        compiler_params=pltpu.CompilerParams(has_side_effects=True),
    )(send)


def _sum_chunks(x, name, out_dtype=F32):
    n, R, C = x.shape
    tr = _row_tile(R, 512)

    def body(x_ref, o_ref):
        acc = x_ref[0].astype(F32)
        for j in range(1, n):
            acc = acc + x_ref[j].astype(F32)
        o_ref[...] = acc.astype(o_ref.dtype)

    return pl.pallas_call(
        body,
        out_shape=jax.ShapeDtypeStruct((R, C), out_dtype),
        grid=(R // tr,),
        in_specs=[pl.BlockSpec((n, tr, C), lambda i: (0, i, 0))],
        out_specs=pl.BlockSpec((tr, C), lambda i: (i, 0)),
        name=name,
        compiler_params=_cp(("parallel",)),
    )(x)


def _pad_heads_cols(w, groups, d):
    k = w.shape[0]
    w = w.reshape(k, groups * N_HEADS, d)
    return jnp.pad(w, ((0, 0), (0, 0), (0, LANES - d))).reshape(k, groups * N_HEADS * LANES)


def _unpad_heads_cols(w, groups, d):
    k = w.shape[0]
    return w.reshape(k, groups * N_HEADS, LANES)[:, :, :d].reshape(k, groups * N_HEADS * d)


def _pad_heads_rows(w, d):
    n = w.shape[1]
    return jnp.pad(w.reshape(N_HEADS, d, n), ((0, 0), (0, LANES - d), (0, 0))).reshape(N_HEADS * LANES, n)


def _unpad_heads_rows(w, d):
    n = w.shape[1]
    return w.reshape(N_HEADS, LANES, n)[:, :d].reshape(N_HEADS * d, n)


Q_SCALE = HEAD_DIM ** -0.5


def _scale_q_cols(w):
    nq = N_HEADS * LANES
    return jnp.concatenate([w[:, :nq] * Q_SCALE, w[:, nq:]], axis=1)


def _kernel_weights(W):
    P = dict(W)
    pw = W["pool_w"][0]
    bd = jnp.zeros((D_MODEL, D_MODEL), pw.dtype)
    for g in range(len(POOL_WINDOWS)):
        bd = lax.dynamic_update_slice(bd, pw[g], (g * POOL_GROUP, g * POOL_GROUP))
    P["pool_bd"] = bd
    P["sb_qkv"] = _scale_q_cols(_pad_heads_cols(W["sb_w_qkv"][0], 3, HEAD_DIM))
    P["sb_o"] = _pad_heads_rows(W["sb_w_o"][0], HEAD_DIM)
    nq = 3 * N_HEADS * HEAD_DIM
    P["fox_qkv"] = _scale_q_cols(_pad_heads_cols(W["fox_w_qkvf"][0][:, :nq], 3, HEAD_DIM))
    P["fox_f"] = jnp.pad(W["fox_w_qkvf"][0][:, nq:], ((0, 0), (0, LANES - N_HEADS)))
    P["fox_o"] = _pad_heads_rows(W["fox_w_o"][0], HEAD_DIM)
    P["fox_b"] = jnp.pad(W["fox_b_f"], ((0, 0), (0, LANES - N_HEADS)))
    P["mla_down"] = jnp.pad(W["mla_w_down"][0], ((0, 0), (0, MLA_DOWN_PAD - W["mla_w_down"].shape[2])))
    P["mla_uq"] = _pad_heads_cols(W["mla_w_uq"][0], 1, MLA_NOPE + MLA_ROPE)
    ukv = W["mla_w_ukv"][0].reshape(MLA_KV_RANK, N_HEADS, 2 * HEAD_DIM)
    padk = ((0, 0), (0, 0), (0, LANES - HEAD_DIM))
    P["mla_ukv"] = jnp.concatenate(
        [jnp.pad(ukv[:, :, :MLA_NOPE], padk).reshape(MLA_KV_RANK, -1), jnp.pad(ukv[:, :, MLA_NOPE:], padk).reshape(MLA_KV_RANK, -1)], axis=1)
    P["mla_o"] = _pad_heads_rows(W["mla_w_o"][0], HEAD_DIM)
    return P


def _rope_tables(lp):
    pos = (jnp.arange(lp) - PAD0).astype(F32)
    inv = ROPE_THETA ** (-jnp.arange(0, MLA_ROPE, 2, dtype=F32) / MLA_ROPE)
    ang = pos[:, None] * inv[None, :]
    cos, sin = jnp.cos(ang), jnp.sin(ang)
    half = MLA_ROPE // 2
    z = lambda n: jnp.zeros((lp, n), F32)
    c = jnp.concatenate([jnp.ones((lp, MLA_NOPE), F32), cos, cos, z(LANES - MLA_NOPE - MLA_ROPE)], axis=1)
    a = jnp.concatenate([z(MLA_NOPE), -sin, z(LANES - MLA_NOPE - half)], axis=1)
    b = jnp.concatenate([z(MLA_NOPE + half), sin, z(LANES - MLA_NOPE - MLA_ROPE)], axis=1)
    return c, a, b


def _key_bias(lp, t, per_head=None):
    pad = jnp.arange(lp)[None, :] < PAD0
    body = jnp.zeros((N_HEADS, lp), F32) if per_head is None else per_head
    return jnp.where(pad, NEG, body).reshape(N_HEADS, lp // t, 1, t)


def _ffn_fwd(h, i, P):
    b = _rms_fwd(h, P["norm_ffn"][i], BF16, "ffn_norm")
    g = _mm(b, P["ffn_w_gate"][i], "nn", "ffn_gate")
    u = _mm(b, P["ffn_w_up"][i], "nn", "ffn_up")
    hd = _swiglu_fwd(g, u, "ffn_act")
    return _mm(hd, P["ffn_w_down"][i], "nn", "ffn_down", add=h), (h, b, g, u, hd)


def _ffn_bwd(dh, i, P, saved):
    h, b, g, u, hd = saved
    dwd = _mm(hd, dh, "tn", "ffn_down_dw")
    dhd = _mm(dh, P["ffn_w_down"][i], "nt", "ffn_down_dx")
    dg, du = _swiglu_bwd(g, u, dhd, "ffn_act_bwd")
    dwg = _mm(b, dg, "tn", "ffn_gate_dw")
    dwu = _mm(b, du, "tn", "ffn_up_dw")
    db = _mm(dg, P["ffn_w_gate"][i], "nt", "ffn_gate_dx")
    db = _mm(du, P["ffn_w_up"][i], "nt", "ffn_up_dx", add=db)
    dh_in, dgain = _rms_bwd(h, P["norm_ffn"][i], db, dh, "ffn_norm_bwd")
    return dh_in, dgain, dwg, dwu, dwd


def _pool_layer_fwd(h, P):
    a = _rms_fwd(h, P["norm_mix"][0], F32, "pool_norm")
    pooled = _pool_fwd(a, "pool_window")
    pre = _mm(pooled, P["pool_bd"], "nn", "pool_mix")
    return _scale_add(h, pre, P["pool_scale"][0], "pool_scale_add"), (h, pooled, pre)


def _pool_layer_bwd(dh, P, saved):
    h, pooled, pre = saved
    dpre, dscale = _scale_bwd(dh, pre, P["pool_scale"][0], "pool_scale_bwd")
    dbd = _mm(pooled, dpre, "tn", "pool_mix_dw")
    dpooled = _mm(dpre, P["pool_bd"], "nt", "pool_mix_dx")
    da = _pool_bwd(dpooled, "pool_window_bwd")
    dh_in, dgain = _rms_bwd(h, P["norm_mix"][0], da, dh, "mix_norm_bwd")
    dw = jnp.stack([dbd[g * POOL_GROUP:(g + 1) * POOL_GROUP, g * POOL_GROUP:(g + 1) * POOL_GROUP] for g in range(len(POOL_WINDOWS))])
    return dh_in, {"norm_mix0": dgain, "pool_w": dw[None], "pool_scale": dscale}


def _out_proj_bwd(o, dh, wo, tag):
    return _mm(o, dh, "tn", tag + "_o_dw"), _mm(dh, wo, "nt", tag + "_o_dx", out_dtype=BF16)


def _sb_layer_fwd(h, P):
    lp = h.shape[0]
    a = _rms_fwd(h, P["norm_mix"][1], BF16, "mix_norm")
    qkv = _mm(a, P["sb_qkv"], "nn", "sb_qkv", out_dtype=BF16)
    kb = _key_bias(lp, SB_TK)
    o = _sb_fwd(qkv, kb, nh=N_HEADS, name="sb_attn")
    return _mm(o, P["sb_o"], "nn", "attn_out", add=h), (h, a, qkv, kb, o)


def _sb_layer_bwd(dh, P, saved):
    h, a, qkv, kb, o = saved
    dwo, do = _out_proj_bwd(o, dh, P["sb_o"], "attn")
    dq, dk, dv = _sb_bwd(qkv, kb, do, nh=N_HEADS, name="sb_attn_bwd")
    dqkv = jnp.concatenate([dq, dk, dv], axis=1)
    dw = _scale_q_cols(_mm(a, dqkv, "tn", "qkv_dw"))
    da = _mm(dqkv, P["sb_qkv"], "nt", "qkv_dx")
    dh_in, dgain = _rms_bwd(h, P["norm_mix"][1], da, dh, "mix_norm_bwd")
    return dh_in, {"norm_mix1": dgain, "sb_w_qkv": _unpad_heads_cols(dw, 3, HEAD_DIM)[None], "sb_w_o": _unpad_heads_rows(dwo, HEAD_DIM)[None]}


def _fox_layer_fwd(h, P):
    lp = h.shape[0]
    t = _attn_tile(lp)
    a = _rms_fwd(h, P["norm_mix"][3], BF16, "mix_norm")
    qkv = _mm(a, P["fox_qkv"], "nn", "sb_qkv", out_dtype=BF16)
    f = _mm(a, P["fox_f"], "nn", "fox_gate_proj")
    fc = _gate_fwd(f, P["fox_b"], "fox_gate")[:, :N_HEADS]
    kb = _key_bias(lp, t, -fc.T)
    fq = jnp.broadcast_to(fc[:, :, None], (lp, N_HEADS, LANES)).reshape(lp, N_HEADS * LANES)
    o, lse = _attn_fwd(qkv, qkv, qkv, kb, fq, nh=N_HEADS, offs=(0, N_HEADS, 2 * N_HEADS), name="fox_attn")
    return _mm(o, P["fox_o"], "nn", "attn_out", add=h), (h, a, qkv, f, kb, fq, o, lse)


def _fox_layer_bwd(dh, P, saved):
    h, a, qkv, f, kb, fq, o, lse = saved
    lp = h.shape[0]
    dwo, do = _out_proj_bwd(o, dh, P["fox_o"], "attn")
    dq, dk, dv, dkb, dqb = _attn_bwd(qkv, qkv, qkv, kb, fq, o, do, lse, nh=N_HEADS, offs=(0, N_HEADS, 2 * N_HEADS),
                                     name="fox_attn_bwd")
    dfc = jnp.pad(dqb.reshape(lp, N_HEADS, LANES)[:, :, 0] - dkb.reshape(N_HEADS, lp).T, ((0, 0), (0, LANES - N_HEADS)))
    df, dbf = _gate_bwd(f, P["fox_b"], dfc, "fox_gate_bwd")
    dqkv = jnp.concatenate([dq, dk, dv], axis=1)
    dw = _scale_q_cols(_mm(a, dqkv, "tn", "qkv_dw"))
    dwf = _mm(a, df, "tn", "fox_gate_dw")
    da = _mm(dqkv, P["fox_qkv"], "nt", "qkv_dx")
    da = _mm(df, P["fox_f"], "nt", "fox_gate_dx", add=da)
    dh_in, dgain = _rms_bwd(h, P["norm_mix"][3], da, dh, "mix_norm_bwd")
    dwqkvf = jnp.concatenate([_unpad_heads_cols(dw, 3, HEAD_DIM), dwf[:, :N_HEADS]], axis=1)
    return dh_in, {"norm_mix3": dgain, "fox_w_qkvf": dwqkvf[None], "fox_b_f": dbf[:, :N_HEADS], "fox_w_o": _unpad_heads_rows(dwo, HEAD_DIM)[None]}


def _mla_layer_fwd(h, P):
    lp = h.shape[0]
    a = _rms_fwd(h, P["norm_mix"][2], BF16, "mix_norm")
    down = _mm(a, P["mla_down"], "nn", "mla_down")
    cq_pre = down[:, :MLA_Q_RANK]
    ckv_pre = down[:, MLA_Q_RANK:MLA_Q_RANK + MLA_KV_RANK]
    kr = jnp.pad(down[:, MLA_Q_RANK + MLA_KV_RANK:MLA_Q_RANK + MLA_KV_RANK + MLA_ROPE], ((0, 0), (MLA_NOPE, LANES - MLA_NOPE - MLA_ROPE)))
    cq = _rms_fwd(cq_pre, P["mla_q_norm"][0], BF16, "mla_q_norm")
    ckv = _rms_fwd(ckv_pre, P["mla_kv_norm"][0], BF16, "mla_kv_norm")
    q = _mm(cq, P["mla_uq"], "nn", "mla_uq")
    kv = _mm(ckv, P["mla_ukv"], "nn", "mla_ukv", out_dtype=BF16)
    tabs = _rope_tables(lp)
    qr, kc = _mla_prep_fwd(q, kv, kr, *tabs, "mla_rope")
    kb = _key_bias(lp, _attn_tile(lp))
    o, lse = _attn_fwd(qr, kc, kv, kb, None, nh=N_HEADS, offs=(0, 0, N_HEADS), name="mla_attn")
    return _mm(o, P["mla_o"], "nn", "attn_out", add=h), (h, a, cq_pre, ckv_pre, cq, ckv, qr, kc, kv, tabs, kb, o, lse)


def _mla_layer_bwd(dh, P, saved):
    h, a, cq_pre, ckv_pre, cq, ckv, qr, kc, kv, tabs, kb, o, lse = saved
    lp = h.shape[0]
    dwo, do = _out_proj_bwd(o, dh, P["mla_o"], "attn")
    dqr, dkc, dv = _attn_bwd(qr, kc, kv, kb, None, o, do, lse, nh=N_HEADS, offs=(0, 0, N_HEADS),
                             name="mla_attn_bwd")
    dq, dkr = _mla_prep_bwd(dqr, dkc, *tabs, "mla_rope_bwd")
    dkv = jnp.concatenate([dkc, dv], axis=1)
    dwuq = _mm(cq, dq, "tn", "mla_uq_dw")
    dcq = _mm(dq, P["mla_uq"], "nt", "mla_uq_dx")
    dwukv = _mm(ckv, dkv, "tn", "mla_ukv_dw")
    dckv = _mm(dkv, P["mla_ukv"], "nt", "mla_ukv_dx")
    dcq_pre, dqn = _rms_bwd(cq_pre, P["mla_q_norm"][0], dcq, None, "mla_q_norm_bwd")
    dckv_pre, dkvn = _rms_bwd(ckv_pre, P["mla_kv_norm"][0], dckv, None, "mla_kv_norm_bwd")
    used = MLA_Q_RANK + MLA_KV_RANK + MLA_ROPE
    ddown = jnp.concatenate([dcq_pre, dckv_pre, dkr[:, MLA_NOPE:MLA_NOPE + MLA_ROPE], jnp.zeros((lp, MLA_DOWN_PAD - used), F32)], axis=1)
    dwdown = _mm(a, ddown, "tn", "mla_down_dw")
    da = _mm(ddown, P["mla_down"], "nt", "mla_down_dx")
    dh_in, dgain = _rms_bwd(h, P["norm_mix"][2], da, dh, "mix_norm_bwd")
    dukv = dwukv.reshape(MLA_KV_RANK, 2, N_HEADS, LANES)[:, :, :, :HEAD_DIM]
    dukv = jnp.concatenate([dukv[:, 0], dukv[:, 1]], axis=-1).reshape(MLA_KV_RANK, N_HEADS * 2 * HEAD_DIM)
    return dh_in, {
        "norm_mix2": dgain, "mla_w_down": dwdown[:, :used][None], "mla_q_norm": dqn, "mla_kv_norm": dkvn,
        "mla_w_uq": _unpad_heads_cols(dwuq, 1, MLA_NOPE + MLA_ROPE)[None], "mla_w_ukv": dukv[None],
        "mla_w_o": _unpad_heads_rows(dwo, HEAD_DIM)[None]}


_MIXERS = ((_pool_layer_fwd, _pool_layer_bwd), (_sb_layer_fwd, _sb_layer_bwd), (_mla_layer_fwd, _mla_layer_bwd), (_fox_layer_fwd, _fox_layer_bwd))


def _step_local(x, target, W):
    seq = x.shape[0]
    P = _kernel_weights(W)
    h = jnp.concatenate([jnp.zeros((PAD0, D_MODEL), F32), W["meta"], x], axis=0)
    tpad = jnp.pad(target, ((PAD0 + N_META, 0), (0, 0)))
    saved = []
    for i in range(4):
        h, s_mix = _MIXERS[i][0](h, P)
        h, s_ffn = _ffn_fwd(h, i, P)
        saved.append((s_mix, s_ffn))
    loss, dh, dfinal = _loss_head(h, W["final_norm"], tpad, "loss_head")
    grads = {"final_norm": dfinal.reshape(-1)}
    gains_mix, gains_ffn, dwg, dwu, dwd = [None] * 4, [None] * 4, [None] * 4, [None] * 4, [None] * 4
    for i in reversed(range(4)):
        s_mix, s_ffn = saved[i]
        dh, gains_ffn[i], dwg[i], dwu[i], dwd[i] = _ffn_bwd(dh, i, P, s_ffn)
        dh, g = _MIXERS[i][1](dh, P, s_mix)
        gains_mix[i] = g.pop("norm_mix%d" % i)
        grads.update(g)
    grads["norm_mix"] = jnp.concatenate(gains_mix, axis=0)
    grads["norm_ffn"] = jnp.concatenate(gains_ffn, axis=0)
    grads["ffn_w_gate"] = jnp.stack(dwg)
    grads["ffn_w_up"] = jnp.stack(dwu)
    grads["ffn_w_down"] = jnp.stack(dwd)
    grads["meta"] = dh[PAD0:PAD0 + N_META]
    return loss, dh[PAD0 + N_META:], grads


_WEIGHTS = ("meta", "norm_mix", "norm_ffn", "pool_w", "pool_scale", "sb_w_qkv", "sb_w_o", "mla_w_down", "mla_q_norm",
            "mla_kv_norm", "mla_w_uq", "mla_w_ukv", "mla_w_o", "fox_w_qkvf", "fox_b_f", "fox_w_o", "ffn_w_gate",
            "ffn_w_up", "ffn_w_down", "final_norm")
_SHARD_AXIS = {"meta": 1, "pool_w": 2, "sb_w_qkv": 2, "sb_w_o": 1, "mla_w_down": 1, "mla_q_norm": 1, "mla_kv_norm": 1,
               "mla_w_uq": 2, "mla_w_ukv": 2, "mla_w_o": 1, "fox_w_qkvf": 2, "fox_b_f": None, "fox_w_o": 1,
               "ffn_w_gate": 2, "ffn_w_up": 2, "ffn_w_down": 1}
_SHARDED = tuple(n for n in _WEIGHTS if _SHARD_AXIS.get(n) is not None)
_REPLICATED = tuple(n for n in _WEIGHTS if _SHARD_AXIS.get(n) is None)
_EXACT = ("meta", "mla_q_norm", "mla_kv_norm")
N_CHIPS = 4
GRAD_ROW_TILE = 512


PACK_ROWS = 16


def _piece_rows(t):
    return -(-t.size // (LANES * PACK_ROWS)) * PACK_ROWS


def _flat_rows(parts, dtype, row_multiple):
    pieces = []
    for p in parts:
        flat = p.astype(dtype).reshape(-1)
        pieces.append(jnp.pad(flat, (0, _piece_rows(p) * LANES - flat.shape[0])).reshape(-1, LANES))
    rows = sum(q.shape[0] for q in pieces)
    pad = -(-rows // row_multiple) * row_multiple - rows
    if pad:
        pieces.append(jnp.zeros((pad, LANES), dtype))
    return jnp.concatenate(pieces, axis=0)


def _split_flat(flat, like):
    out, off = [], 0
    for t in like:
        out.append(flat[off:off + _piece_rows(t)].reshape(-1)[:t.size].reshape(t.shape))
        off += _piece_rows(t)
    return out


def _gather_shards(local, names, dtype, name):
    blocks = [local[n] for n in names]
    recv = _exchange(_flat_rows(blocks, dtype, PACK_ROWS)[None], ("x", "y"), True, name)
    per_chip = [_split_flat(recv[s], blocks) for s in range(N_CHIPS)]
    return {n: jnp.concatenate([per_chip[s][k] for s in range(N_CHIPS)], axis=_SHARD_AXIS[n]) for k, n in enumerate(names)}


def _shard_of(g, n, s):
    w = g.shape[_SHARD_AXIS[n]] // N_CHIPS
    return lax.slice_in_dim(g, s * w, (s + 1) * w, axis=_SHARD_AXIS[n])


def _train_step(a):
    local = {n: a[n] for n in _WEIGHTS}
    full = {n: local[n] for n in _REPLICATED}
    full.update(_gather_shards(local, [n for n in _SHARDED if n not in _EXACT], BF16, "gather_weights"))
    full.update(_gather_shards(local, list(_EXACT), F32, "gather_exact"))

    loss, grad_x, grads = _step_local(a["x"][0], a["loss_target"][0], full)

    send = jnp.stack([
        _flat_rows([_shard_of(grads[n], n, s) for n in _SHARDED], BF16, 2 * GRAD_ROW_TILE).reshape(2, -1, LANES)
        for s in range(N_CHIPS)]).reshape(2 * N_CHIPS, -1, LANES)
    mine = _sum_chunks(_exchange(send, MESH_AXES, False, "scatter_grads"), "sum_grads", out_dtype=BF16)
    both = _exchange(mine[None], ("c",), True, "pair_grads").reshape(-1, LANES).astype(F32)
    reduced = dict(zip(_SHARDED, _split_flat(both, [local[n] for n in _SHARDED])))
    small = _flat_rows([grads[n] for n in _REPLICATED], F32, 8)
    small = _sum_chunks(_exchange(small[None], MESH_AXES, True, "gather_small_grads"), "sum_small_grads")
    reduced.update(zip(_REPLICATED, _split_flat(small, [local[n] for n in _REPLICATED])))

    deltas, new_m, new_v = {}, {}, {}
    for n in _WEIGHTS:
        deltas[n], new_m[n], new_v[n] = _adamw(local[n], reduced[n], a["m_" + n], a["v_" + n], "adamw")
    total = lax.psum(loss[0, 0], MESH_AXES)
    return (total, grad_x[None], *[reduced[n] for n in _WEIGHTS], *[deltas[n] for n in _WEIGHTS],
            *[new_m[n] for n in _WEIGHTS], *[new_v[n] for n in _WEIGHTS])


def kernel(x, meta, norm_mix, norm_ffn, pool_w, pool_scale, sb_w_qkv, sb_w_o, mla_w_down, mla_q_norm, mla_kv_norm, mla_w_uq, mla_w_ukv, mla_w_o, fox_w_qkvf, fox_b_f, fox_w_o, ffn_w_gate, ffn_w_up, ffn_w_down, final_norm, loss_target, m_meta, m_norm_mix, m_norm_ffn, m_pool_w, m_pool_scale, m_sb_w_qkv, m_sb_w_o, m_mla_w_down, m_mla_q_norm, m_mla_kv_norm, m_mla_w_uq, m_mla_w_ukv, m_mla_w_o, m_fox_w_qkvf, m_fox_b_f, m_fox_w_o, m_ffn_w_gate, m_ffn_w_up, m_ffn_w_down, m_final_norm, v_meta, v_norm_mix, v_norm_ffn, v_pool_w, v_pool_scale, v_sb_w_qkv, v_sb_w_o, v_mla_w_down, v_mla_q_norm, v_mla_kv_norm, v_mla_w_uq, v_mla_w_ukv, v_mla_w_o, v_fox_w_qkvf, v_fox_b_f, v_fox_w_o, v_ffn_w_gate, v_ffn_w_up, v_ffn_w_down, v_final_norm):
    return _train_step(dict(locals()))
```

```python
import functools

import jax
import jax.numpy as jnp
from jax import lax
from jax.experimental import pallas as pl
from jax.experimental.pallas import tpu as pltpu

F32 = jnp.float32
BF16 = jnp.bfloat16

D_MODEL = 1024
N_META = 16
PAD0 = 112
LANES = 128
N_HEADS = 16
HEAD_DIM = 64
POOL_WINDOWS = (2, 4, 8, 16)
POOL_GROUP = 256
POOL_HALO = 16
MLA_Q_RANK = 384
MLA_KV_RANK = 256
MLA_NOPE = 64
MLA_ROPE = 32
MLA_DOWN_PAD = 768
ROPE_THETA = 10000.0
D_FF = 2816
EPS = 1e-6
NEG = -1e30
ADAM_LR = 0.001
ADAM_B1 = 0.9
ADAM_B2 = 0.999
ADAM_EPS = 1e-08
ADAM_WD = 0.01
ADAM_STEP = 10
VMEM_LIMIT = 56 * 1024 * 1024
MESH_AXES = ("x", "y", "c")


def _cp(sem, **kw):
    return pltpu.CompilerParams(dimension_semantics=sem, vmem_limit_bytes=VMEM_LIMIT, **kw)


def _row_tile(m, target):
    best = None
    for t in range(16, min(m, target) + 1, 16):
        if m % t == 0:
            best = t
    return best or m


def _col_tile(n, target):
    best = None
    for t in range(LANES, min(n, target) + 1, LANES):
        if n % t == 0:
            best = t
    return best or n


def _mm(a, b, mode, name, out_dtype=F32, add=None, tm=640, tn=1536, tk=2048):
    if mode == "nn":
        (M, K), (K2, N) = a.shape, b.shape
    elif mode == "nt":
        (M, K), (N, K2) = a.shape, b.shape
    else:
        (K, M), (K2, N) = a.shape, b.shape
    assert K == K2, (mode, a.shape, b.shape)
    if mode == "tn":
        tm_ = _col_tile(M, 1408)
        tk_ = _row_tile(K, 640)
    else:
        tm_ = _row_tile(M, tm)
        tk_ = _col_tile(K, tk) if K > tk else K
    tn_ = _col_tile(N, tn)
    nk = K // tk_
    if mode == "nn":
        a_spec = pl.BlockSpec((tm_, tk_), lambda i, j, k: (i, k))
        b_spec = pl.BlockSpec((tk_, tn_), lambda i, j, k: (k, j))
        dims = (((1,), (0,)), ((), ()))
    elif mode == "nt":
        a_spec = pl.BlockSpec((tm_, tk_), lambda i, j, k: (i, k))
        b_spec = pl.BlockSpec((tn_, tk_), lambda i, j, k: (j, k))
        dims = (((1,), (1,)), ((), ()))
    else:
        a_spec = pl.BlockSpec((tk_, tm_), lambda i, j, k: (k, i))
        b_spec = pl.BlockSpec((tk_, tn_), lambda i, j, k: (k, j))
        dims = (((0,), (0,)), ((), ()))
    o_spec = pl.BlockSpec((tm_, tn_), lambda i, j, k: (i, j))
    has_add = add is not None

    def body(*refs):
        if has_add:
            a_ref, b_ref, add_ref, o_ref, acc_ref = refs
        else:
            a_ref, b_ref, o_ref, acc_ref = refs
        k = pl.program_id(2)
        part = lax.dot_general(a_ref[...].astype(BF16), b_ref[...].astype(BF16), dims, preferred_element_type=F32)

        @pl.when(k == 0)
        def _():
            acc_ref[...] = part

        @pl.when(k > 0)
        def _():
            acc_ref[...] += part

        @pl.when(k == nk - 1)
        def _():
            r = acc_ref[...]
            if has_add:
                r = r + add_ref[...]
            o_ref[...] = r.astype(o_ref.dtype)

    ins = [a, b] + ([add] if has_add else [])
    in_specs = [a_spec, b_spec] + ([o_spec] if has_add else [])
    return pl.pallas_call(
        body,
        out_shape=jax.ShapeDtypeStruct((M, N), out_dtype),
        grid=(M // tm_, N // tn_, nk),
        in_specs=in_specs,
        out_specs=o_spec,
        scratch_shapes=[pltpu.VMEM((tm_, tn_), F32)],
        name=name,
        compiler_params=_cp(("parallel", "parallel", "arbitrary")),
    )(*ins)


def _rms_fwd(x, g, out_dtype, name):
    M, C = x.shape
    tm = _row_tile(M, 640)

    def body(x_ref, g_ref, o_ref):
        xf = x_ref[...]
        r = lax.rsqrt(jnp.mean(xf * xf, axis=-1, keepdims=True) + EPS)
        o_ref[...] = ((xf * r) * g_ref[...]).astype(o_ref.dtype)

    return pl.pallas_call(
        body,
        out_shape=jax.ShapeDtypeStruct((M, C), out_dtype),
        grid=(M // tm,),
        in_specs=[pl.BlockSpec((tm, C), lambda i: (i, 0)), pl.BlockSpec((1, C), lambda i: (0, 0))],
        out_specs=pl.BlockSpec((tm, C), lambda i: (i, 0)),
        name=name,
        compiler_params=_cp(("parallel",)),
    )(x, g.reshape(1, C))


def _rms_bwd(x, g, dy, dres, name):
    M, C = x.shape
    tm = _row_tile(M, 640)
    has_res = dres is not None

    def body(*refs):
        if has_res:
            x_ref, g_ref, dy_ref, dres_ref, dx_ref, dg_ref = refs
        else:
            x_ref, g_ref, dy_ref, dx_ref, dg_ref = refs
        xf = x_ref[...]
        r = lax.rsqrt(jnp.mean(xf * xf, axis=-1, keepdims=True) + EPS)
        xhat = xf * r
        dyf = dy_ref[...].astype(F32)

        @pl.when(pl.program_id(0) == 0)
        def _():
            dg_ref[...] = jnp.zeros_like(dg_ref)

        dg_ref[...] += jnp.sum(dyf * xhat, axis=0, keepdims=True)
        dxh = dyf * g_ref[...]
        dx = r * (dxh - xhat * jnp.mean(dxh * xhat, axis=-1, keepdims=True))
        if has_res:
            dx = dx + dres_ref[...]
        dx_ref[...] = dx

    row = pl.BlockSpec((tm, C), lambda i: (i, 0))
    vec = pl.BlockSpec((1, C), lambda i: (0, 0))
    ins = [x, g.reshape(1, C), dy] + ([dres] if has_res else [])
    return pl.pallas_call(
        body,
        out_shape=(jax.ShapeDtypeStruct((M, C), F32), jax.ShapeDtypeStruct((1, C), F32)),
        grid=(M // tm,),
        in_specs=[row, vec, row] + ([row] if has_res else []),
        out_specs=(row, vec),
        name=name,
        compiler_params=_cp(("arbitrary",)),
    )(*ins)


def _sigmoid(x):
    return 1.0 / (1.0 + jnp.exp(-x))


def _swiglu_fwd(g, u, name):
    M, F = g.shape
    tm = _row_tile(M, 320)

    def body(g_ref, u_ref, o_ref):
        gv = g_ref[...]
        o_ref[...] = ((gv * _sigmoid(gv)) * u_ref[...]).astype(o_ref.dtype)

    blk = pl.BlockSpec((tm, F), lambda i: (i, 0))
    return pl.pallas_call(
        body,
        out_shape=jax.ShapeDtypeStruct((M, F), BF16),
        grid=(M // tm,),
        in_specs=[blk, blk],
        out_specs=blk,
        name=name,
        compiler_params=_cp(("parallel",)),
    )(g, u)


def _swiglu_bwd(g, u, dh, name):
    M, F = g.shape
    tm = _row_tile(M, 320)

    def body(g_ref, u_ref, dh_ref, dg_ref, du_ref):
        gv = g_ref[...]
        sg = _sigmoid(gv)
        d = dh_ref[...]
        du_ref[...] = (d * (gv * sg)).astype(du_ref.dtype)
        dg_ref[...] = ((d * u_ref[...]) * (sg * (1.0 + gv * (1.0 - sg)))).astype(dg_ref.dtype)

    blk = pl.BlockSpec((tm, F), lambda i: (i, 0))
    return pl.pallas_call(
        body,
        out_shape=(jax.ShapeDtypeStruct((M, F), BF16), jax.ShapeDtypeStruct((M, F), BF16)),
        grid=(M // tm,),
        in_specs=[blk, blk, blk],
        out_specs=(blk, blk),
        name=name,
        compiler_params=_cp(("parallel",)),
    )(g, u, dh)


ATTN_RC = 64


def _attn_tile(lp):
    return _col_tile(lp, 640)


def _head_specs(lp, t, offs):
    q_spec = pl.BlockSpec((t, LANES), lambda h, i: (i, offs[0] + h))
    k_spec = pl.BlockSpec((lp, LANES), lambda h, i: (0, offs[1] + h))
    v_spec = pl.BlockSpec((lp, LANES), lambda h, i: (0, offs[2] + h))
    return q_spec, k_spec, v_spec


def _attn_fwd(qa, ka, va, kb, fq, *, nh, offs, name, tile=640):
    lp = qa.shape[0]
    t = _col_tile(lp, tile)
    nq = lp // t
    rc = ATTN_RC
    has_fq = fq is not None

    def body(*refs):
        if has_fq:
            q_ref, k_ref, v_ref, kb_ref, fq_ref, o_ref, lse_ref, s_scr, p_scr, m_scr, l_scr, a_scr, acc_scr = refs
        else:
            q_ref, k_ref, v_ref, kb_ref, o_ref, lse_ref, s_scr, p_scr, m_scr, l_scr, a_scr, acc_scr = refs
        i = pl.program_id(1)
        m_scr[...] = jnp.full_like(m_scr, NEG)
        l_scr[...] = jnp.zeros_like(l_scr)
        acc_scr[...] = jnp.zeros_like(acc_scr)
        row_io = lax.broadcasted_iota(jnp.int32, (rc, t), 0)
        col_io = lax.broadcasted_iota(jnp.int32, (rc, t), 1)

        def step(j, masked):
            st = pl.multiple_of(j * t, t)
            s_scr[...] = lax.dot_general(q_ref[...], k_ref[pl.ds(st, t), :], _NT, preferred_element_type=F32)
            kbj = kb_ref[j]
            for r in range(t // rc):
                rs = pl.ds(r * rc, rc)
                bias = fq_ref[rs, 0:1] + kbj if has_fq else kbj
                s = s_scr[rs, :] + bias
                if masked:
                    s = jnp.where(col_io <= row_io + r * rc, s, NEG)
                m_old = m_scr[rs, :]
                m_new = jnp.maximum(m_old, jnp.max(s, axis=1, keepdims=True))
                p = jnp.exp(s - m_new)
                alpha = jnp.exp(m_old - m_new)
                l_scr[rs, :] = alpha * l_scr[rs, :] + jnp.sum(p, axis=1, keepdims=True)
                m_scr[rs, :] = m_new
                a_scr[rs, :] = alpha
                p_scr[rs, :] = p.astype(BF16)
            acc_scr[...] = a_scr[...] * acc_scr[...] + jnp.dot(p_scr[...], v_ref[pl.ds(st, t), :], preferred_element_type=F32)

        def left(j, carry):
            step(j, False)
            return carry

        lax.fori_loop(0, i, left, 0)
        step(i, True)
        valid = (i * t + lax.broadcasted_iota(jnp.int32, (t, 1), 0)) >= PAD0
        o_ref[...] = jnp.where(valid, acc_scr[...] / l_scr[...], 0.0).astype(o_ref.dtype)
        lse_ref[...] = jnp.broadcast_to(m_scr[...] + jnp.log(l_scr[...]), (t, LANES))

    q_spec, k_spec, v_spec = _head_specs(lp, t, offs)
    kb_spec = pl.BlockSpec((None, nq, 1, t), lambda h, i: (h, 0, 0, 0))
    row_spec = pl.BlockSpec((t, LANES), lambda h, i: (i, h))
    ins = [qa, ka, va, kb] + ([fq] if has_fq else [])
    col = pltpu.VMEM((t, 1), F32)
    return pl.pallas_call(
        body,
        out_shape=(jax.ShapeDtypeStruct((lp, nh * LANES), BF16), jax.ShapeDtypeStruct((lp, nh * LANES), F32)),
        grid=(nh, nq),
        in_specs=[q_spec, k_spec, v_spec, kb_spec] + ([row_spec] if has_fq else []),
        out_specs=(row_spec, row_spec),
        scratch_shapes=[pltpu.VMEM((t, t), F32), pltpu.VMEM((t, t), BF16), col, col, col, pltpu.VMEM((t, LANES), F32)],
        name=name,
        compiler_params=_cp(("parallel", "arbitrary")),
    )(*ins)


def _attn_bwd(qa, ka, va, kb, fq, o, do, lse, *, nh, offs, name, tile=640):
    lp = qa.shape[0]
    t = _col_tile(lp, tile)
    nq = lp // t
    has_fq = fq is not None

    def body(*refs):
        if has_fq:
            q_ref, k_ref, v_ref, kb_ref, fq_ref, o_ref, do_ref, lse_ref, dq_ref, dk_ref, dv_ref, dkb_ref, dqb_ref, dk_acc, dv_acc = refs
        else:
            q_ref, k_ref, v_ref, kb_ref, o_ref, do_ref, lse_ref, dq_ref, dk_ref, dv_ref, dk_acc, dv_acc = refs
        i = pl.program_id(1)

        @pl.when(i == 0)
        def _():
            dk_acc[...] = jnp.zeros_like(dk_acc)
            dv_acc[...] = jnp.zeros_like(dv_acc)
            if has_fq:
                dkb_ref[...] = jnp.zeros_like(dkb_ref)

        q = q_ref[...]
        dov = do_ref[...]
        delta = jnp.sum(o_ref[...].astype(F32) * dov.astype(F32), axis=1, keepdims=True)
        lse_c = lse_ref[:, 0:1]
        fqc = fq_ref[:, 0:1] if has_fq else None
        causal = lax.broadcasted_iota(jnp.int32, (t, t), 1) <= lax.broadcasted_iota(jnp.int32, (t, t), 0)

        def step(j, carry, masked):
            dq_acc, rs = carry
            st = pl.multiple_of(j * t, t)
            k = k_ref[pl.ds(st, t), :]
            v = v_ref[pl.ds(st, t), :]
            s = lax.dot_general(q, k, _NT, preferred_element_type=F32)
            bias = kb_ref[j]
            if has_fq:
                bias = fqc + bias
            s = s + bias
            if masked:
                s = jnp.where(causal, s, NEG)
            p = jnp.exp(s - lse_c)
            dp = lax.dot_general(dov, v, (((1,), (1,)), ((), ())), preferred_element_type=F32)
            ds = p * (dp - delta)
            dv_acc[pl.ds(st, t), :] += lax.dot_general(p.astype(BF16), dov, (((0,), (0,)), ((), ())), preferred_element_type=F32)
            dsb = ds.astype(BF16)
            dk_acc[pl.ds(st, t), :] += lax.dot_general(dsb, q, (((0,), (0,)), ((), ())), preferred_element_type=F32)
            if has_fq:
                dkb_ref[j] += jnp.sum(ds, axis=0, keepdims=True)
                rs = rs + jnp.sum(ds, axis=1, keepdims=True)
            return dq_acc + jnp.dot(dsb, k, preferred_element_type=F32), rs

        carry = (jnp.zeros((t, LANES), F32), jnp.zeros((t, 1), F32))
        carry = lax.fori_loop(0, i, lambda j, c: step(j, c, False), carry)
        dq_acc, rs = step(i, carry, True)
        dq_ref[...] = dq_acc.astype(dq_ref.dtype)
        if has_fq:
            dqb_ref[...] = jnp.broadcast_to(rs, (t, LANES))

        @pl.when(i == nq - 1)
        def _():
            dk_ref[...] = dk_acc[...].astype(dk_ref.dtype)
            dv_ref[...] = dv_acc[...].astype(dv_ref.dtype)

    q_spec, k_spec, v_spec = _head_specs(lp, t, offs)
    kb_spec = pl.BlockSpec((None, nq, 1, t), lambda h, i: (h, 0, 0, 0))
    row_spec = pl.BlockSpec((t, LANES), lambda h, i: (i, h))
    col_spec = pl.BlockSpec((lp, LANES), lambda h, i: (0, h))
    ins = [qa, ka, va, kb] + ([fq] if has_fq else []) + [o, do, lse]
    wide = jax.ShapeDtypeStruct((lp, nh * LANES), BF16)
    extra_shapes = (jax.ShapeDtypeStruct(kb.shape, F32), jax.ShapeDtypeStruct((lp, nh * LANES), F32)) if has_fq else ()
    extra_specs = (kb_spec, row_spec) if has_fq else ()
    return pl.pallas_call(
        body,
        out_shape=(wide, wide, wide) + extra_shapes,
        grid=(nh, nq),
        in_specs=[q_spec, k_spec, v_spec, kb_spec] + ([row_spec] if has_fq else []) + [row_spec, row_spec, row_spec],
        out_specs=(row_spec, col_spec, col_spec) + extra_specs,
        scratch_shapes=[pltpu.VMEM((lp, LANES), F32), pltpu.VMEM((lp, LANES), F32)],
        name=name,
        compiler_params=_cp(("parallel", "arbitrary")),
    )(*ins)


SB_TK = 128


def _split3(x):
    hi = x.astype(BF16)
    r1 = x - hi.astype(F32)
    mid = r1.astype(BF16)
    lo = (r1 - mid.astype(F32)).astype(BF16)
    return hi, mid, lo


SB_RC = 128


_NT = (((1,), (1,)), ((), ()))
_TN = (((0,), (0,)), ((), ()))


def _sb_logits(zraw, kbj, mask):
    z = zraw + kbj
    if mask is not None:
        z = jnp.where(mask, z, NEG)
    e = jnp.exp(-jnp.abs(z))
    g = jnp.minimum(z, 0.0) - jnp.log(1.0 + e)
    lk = g - z
    return z, e, g, lk


def _dot3_parts(parts, tri):
    d = functools.partial(jnp.dot, preferred_element_type=F32)
    return d(parts[0], tri) + d(parts[1], tri) + d(parts[2], tri)


def _split2(x):
    hi = x.astype(BF16)
    return hi, (x - hi.astype(F32)).astype(BF16)


def _dot_parts(parts, m):
    out = jnp.dot(parts[0], m, preferred_element_type=F32)
    for p in parts[1:]:
        out = out + jnp.dot(p, m, preferred_element_type=F32)
    return out


def _tri(n, pred):
    return pred(lax.broadcasted_iota(jnp.int32, (n, n), 0), lax.broadcasted_iota(jnp.int32, (n, n), 1)).astype(BF16)


def _sb_diag_chunks(jj, nrc, rc, tk):
    plan = []
    for r in range(nrc):
        lo_row, hi_row = r * rc, (r + 1) * rc - 1
        lo_col, hi_col = jj * tk, (jj + 1) * tk - 1
        if hi_row <= lo_col:
            plan.append(None)
        elif lo_row > hi_col:
            plan.append("all")
        else:
            plan.append(lo_col - lo_row)
    return plan


def _sb_fwd(qa, kb, *, nh, name, tq=640):
    lp = qa.shape[0]
    tq = _col_tile(lp, tq)
    tk = SB_TK
    rc = min(SB_RC, tq)
    nq, sub, nrc = lp // tq, tq // tk, tq // rc

    def body(q_ref, k_ref, v_ref, kb_ref, o_ref, c_scr, acc_scr):
        i = pl.program_id(1)
        c_scr[...] = jnp.zeros_like(c_scr)
        acc_scr[...] = jnp.zeros_like(acc_scr)
        tri = _tri(tk, lambda r, c: r > c)
        row_io = lax.broadcasted_iota(jnp.int32, (rc, tk), 0)
        col_io = lax.broadcasted_iota(jnp.int32, (rc, tk), 1)

        def scores(j, rows):
            k = k_ref[pl.ds(pl.multiple_of(j * tk, tk), tk), :]
            return [lax.dot_general(q_ref[rs, :], k, _NT, preferred_element_type=F32) for rs in rows]

        def weights(j, rows, masks, zs):
            kbj = kb_ref[j]
            gs, splits, firsts = [], [], []
            for mask, zraw in zip(masks, zs):
                _, _, g, lk = _sb_logits(zraw, kbj, mask)
                gs.append(g)
                firsts.append(lk[:, 0:1])
                splits.append(_split2(lk))
            sums = [_dot_parts(p, tri) for p in splits]
            avs = [jnp.exp(g + (sm + c_scr[rs, :])).astype(BF16) for g, sm, rs in zip(gs, sums, rows)]
            for rs, sm, first in zip(rows, sums, firsts):
                c_scr[rs, :] += jnp.broadcast_to(sm[:, 0:1] + first, (rc, tk))
            return avs

        def values(j, rows, avs):
            v = v_ref[pl.ds(pl.multiple_of(j * tk, tk), tk), :]
            pvs = [jnp.dot(a, v, preferred_element_type=F32) for a in avs]
            for rs, pv in zip(rows, pvs):
                acc_scr[rs, :] += pv

        for jj in reversed(range(sub)):
            plan = _sb_diag_chunks(jj, nrc, rc, tk)
            live = [r for r, what in enumerate(plan) if what is not None]
            rows = [pl.ds(r * rc, rc) for r in live]
            masks = [None if plan[r] == "all" else (col_io + plan[r]) < row_io for r in live]
            j = i * sub + jj
            values(j, rows, weights(j, rows, masks, scores(j, rows)))

        n = i * sub
        rows = [pl.ds(r * rc, rc) for r in range(nrc)]
        nomask = [None] * nrc
        blk = lambda t: jnp.maximum(n - 1 - t, 0)

        def left(m, carry):
            zs, avs = carry
            t0, t1 = 2 * m, 2 * m + 1
            zs_next = scores(blk(t1 + 1), rows)
            zs1 = scores(blk(t1), rows)
            values(blk(t0 - 1), rows, avs)
            av0 = weights(blk(t0), rows, nomask, zs)
            av1 = weights(blk(t1), rows, nomask, zs1)
            values(blk(t0), rows, av0)
            return zs_next, av1

        first = (scores(blk(0), rows), [jnp.zeros((rc, tk), BF16)] * nrc)
        zs, avs = lax.fori_loop(0, n // 2, left, first)
        values(blk(2 * (n // 2) - 1), rows, avs)

        @pl.when(n % 2 == 1)
        def _():
            values(0, rows, weights(0, rows, nomask, zs))

        o_ref[...] = acc_scr[...].astype(o_ref.dtype)

    q_spec, k_spec, v_spec = _head_specs(lp, tq, (0, nh, 2 * nh))
    kb_spec = pl.BlockSpec((None, lp // tk, 1, tk), lambda h, i: (h, 0, 0, 0))
    row_spec = pl.BlockSpec((tq, LANES), lambda h, i: (i, h))
    return pl.pallas_call(
        body,
        out_shape=jax.ShapeDtypeStruct((lp, nh * LANES), BF16),
        grid=(nh, nq),
        in_specs=[q_spec, k_spec, v_spec, kb_spec],
        out_specs=row_spec,
        scratch_shapes=[pltpu.VMEM((tq, tk), F32), pltpu.VMEM((tq, LANES), F32)],
        name=name,
        compiler_params=_cp(("parallel", "arbitrary")),
    )(qa, qa, qa, kb)


def _sb_bwd(qa, kb, do, *, nh, name, tq=640):
    lp = qa.shape[0]
    tq = _col_tile(lp, tq)
    tk = SB_TK
    rc = min(SB_RC, tq)
    nq, nk, sub, nrc = lp // tq, lp // tk, tq // tk, tq // rc

    def body(q_ref, k_ref, v_ref, kb_ref, do_ref, dq_ref, dk_ref, dv_ref, dkt_acc, dvt_acc, w_scr, b_scr, c_scr, u_scr, dq_scr):
        i = pl.program_id(1)

        @pl.when(i == 0)
        def _():
            dkt_acc[...] = jnp.zeros_like(dkt_acc)
            dvt_acc[...] = jnp.zeros_like(dvt_acc)

        c_scr[...] = jnp.zeros_like(c_scr)
        u_scr[...] = jnp.zeros_like(u_scr)
        dq_scr[...] = jnp.zeros_like(dq_scr)
        tri_gt = _tri(tk, lambda r, c: r > c)
        tri_lt = _tri(tk, lambda r, c: r < c)
        row_io = lax.broadcasted_iota(jnp.int32, (rc, tk), 0)
        col_io = lax.broadcasted_iota(jnp.int32, (rc, tk), 1)
        qt = q_ref[...].astype(F32).T.astype(BF16)
        dot_t = do_ref[...].astype(F32).T.astype(BF16)
        zero_blk = jnp.zeros((rc, tk), BF16)

        def full_rows(plan, parts):
            it = iter(parts)
            return jnp.concatenate([zero_blk if what is None else next(it) for what in plan], axis=0)

        def key_rows(j):
            return pl.ds(pl.multiple_of(j * tk, tk), tk)

        def scores(j, rows):
            k = k_ref[key_rows(j), :]
            v = v_ref[key_rows(j), :]
            zs = [lax.dot_general(q_ref[rs, :], k, _NT, preferred_element_type=F32) for rs in rows]
            das = [lax.dot_general(do_ref[rs, :], v, _NT, preferred_element_type=F32) for rs in rows]
            return zs, das

        def weights(j, rows, masks, zs, das):
            kbj = kb_ref[j]
            gs, splits, firsts = [], [], []
            for rs, mask, zraw in zip(rows, masks, zs):
                z, e, g, lk = _sb_logits(zraw, kbj, mask)
                b_scr[j, rs, :] = (jnp.where(z >= 0.0, 1.0, e) / (1.0 + e)).astype(BF16)
                gs.append(g)
                firsts.append(lk[:, 0:1])
                splits.append(_split2(lk))
            sums = [_dot_parts(p, tri_gt) for p in splits]
            avs = []
            for rs, g, sm, da, first in zip(rows, gs, sums, das, firsts):
                a = jnp.exp(g + (sm + c_scr[rs, :]))
                w_scr[j, rs, :] = (a * da).astype(BF16)
                avs.append(a.astype(BF16))
                c_scr[rs, :] += jnp.broadcast_to(sm[:, 0:1] + first, (rc, tk))
            return avs

        def dv_update(j, plan, avs):
            dvt_acc[j] += jnp.dot(dot_t, full_rows(plan, avs), preferred_element_type=F32)

        for jj in reversed(range(sub)):
            plan = _sb_diag_chunks(jj, nrc, rc, tk)
            live = [r for r, what in enumerate(plan) if what is not None]
            rows = [pl.ds(r * rc, rc) for r in live]
            masks = [None if plan[r] == "all" else (col_io + plan[r]) < row_io for r in live]
            j = i * sub + jj
            dv_update(j, plan, weights(j, rows, masks, *scores(j, rows)))

        n = i * sub
        everything = ["all"] * nrc
        rows = [pl.ds(r * rc, rc) for r in range(nrc)]
        nomask = [None] * nrc
        def left1(m, carry):
            j0, j1 = n - 1 - 2 * m, n - 2 - 2 * m
            sc0, sc1 = scores(j0, rows), scores(j1, rows)
            av0 = weights(j0, rows, nomask, *sc0)
            av1 = weights(j1, rows, nomask, *sc1)
            dv_update(j0, everything, av0)
            dv_update(j1, everything, av1)
            return carry

        lax.fori_loop(0, n // 2, left1, 0)

        @pl.when(n % 2 == 1)
        def _():
            dv_update(0, everything, weights(0, rows, nomask, *scores(0, rows)))

        def prefix(j, rows):
            return [jnp.dot(w_scr[j, rs, :], tri_lt, preferred_element_type=F32) for rs in rows]

        def dlogits(j, rows, sums):
            dzs = []
            for rs, sm in zip(rows, sums):
                w = w_scr[j, rs, :].astype(F32)
                beta = b_scr[j, rs, :].astype(F32)
                dzs.append((w - beta * (w + (sm + u_scr[rs, :]))).astype(BF16))
                u_scr[rs, :] += jnp.broadcast_to(sm[:, tk - 1:tk] + w[:, tk - 1:tk], (rc, tk))
            return dzs

        def dqk_update(j, plan, rows, dzs):
            k = k_ref[key_rows(j), :]
            dqs = [jnp.dot(dz, k, preferred_element_type=F32) for dz in dzs]
            for rs, dq in zip(rows, dqs):
                dq_scr[rs, :] += dq
            dkt_acc[j] += jnp.dot(qt, full_rows(plan, dzs), preferred_element_type=F32)

        def left2(m, carry):
            j0, j1 = 2 * m, 2 * m + 1
            s0, s1 = prefix(j0, rows), prefix(j1, rows)
            dz0 = dlogits(j0, rows, s0)
            dz1 = dlogits(j1, rows, s1)
            dqk_update(j0, everything, rows, dz0)
            dqk_update(j1, everything, rows, dz1)
            return carry

        lax.fori_loop(0, n // 2, left2, 0)

        @pl.when(n % 2 == 1)
        def _():
            dqk_update(n - 1, everything, rows, dlogits(n - 1, rows, prefix(n - 1, rows)))
        for jj in range(sub):
            plan = _sb_diag_chunks(jj, nrc, rc, tk)
            live_rows = [pl.ds(r * rc, rc) for r, what in enumerate(plan) if what is not None]
            j = i * sub + jj
            dqk_update(j, plan, live_rows, dlogits(j, live_rows, prefix(j, live_rows)))
        dq_ref[...] = dq_scr[...].astype(dq_ref.dtype)

        @pl.when(i == nq - 1)
        def _():
            def flush(j, carry):
                dk_ref[key_rows(j), :] = dkt_acc[j].T.astype(dk_ref.dtype)
                dv_ref[key_rows(j), :] = dvt_acc[j].T.astype(dv_ref.dtype)
                return carry

            lax.fori_loop(0, nk, flush, 0)

    q_spec, k_spec, v_spec = _head_specs(lp, tq, (0, nh, 2 * nh))
    kb_spec = pl.BlockSpec((None, nk, 1, tk), lambda h, i: (h, 0, 0, 0))
    row_spec = pl.BlockSpec((tq, LANES), lambda h, i: (i, h))
    col_spec = pl.BlockSpec((lp, LANES), lambda h, i: (0, h))
    wide = jax.ShapeDtypeStruct((lp, nh * LANES), BF16)
    return pl.pallas_call(
        body,
        out_shape=(wide, wide, wide),
        grid=(nh, nq),
        in_specs=[q_spec, k_spec, v_spec, kb_spec, row_spec],
        out_specs=(row_spec, col_spec, col_spec),
        scratch_shapes=[
            pltpu.VMEM((nk, LANES, tk), F32),
            pltpu.VMEM((nk, LANES, tk), F32),
            pltpu.VMEM((nk, tq, tk), BF16),
            pltpu.VMEM((nk, tq, tk), BF16),
            pltpu.VMEM((tq, tk), F32),
            pltpu.VMEM((tq, tk), F32),
            pltpu.VMEM((tq, LANES), F32),
        ],
        name=name,
        compiler_params=_cp(("parallel", "arbitrary")),
    )(qa, qa, qa, kb, do)


def _pool_counts(pos, win):
    return jnp.clip(pos + 1, 1, win).astype(F32)


def _pool_fwd(a, name):
    lp, C = a.shape
    tm = _row_tile(lp, 640)
    hb = tm // POOL_HALO

    def body(prev_ref, cur_ref, o_ref, xs):
        i = pl.program_id(0)
        xs[pl.ds(0, POOL_HALO), :] = jnp.where(i > 0, prev_ref[...], 0.0)
        xs[pl.ds(POOL_HALO, tm), :] = cur_ref[...]
        pos = i * tm + lax.broadcasted_iota(jnp.int32, (tm, 1), 0) - PAD0
        for g, win in enumerate(POOL_WINDOWS):
            cols = pl.ds(g * POOL_GROUP, POOL_GROUP)
            s = xs[pl.ds(POOL_HALO, tm), cols]
            for k in range(1, win):
                s = s + xs[pl.ds(POOL_HALO - k, tm), cols]
            o_ref[:, cols] = (s / _pool_counts(pos, win) - xs[pl.ds(POOL_HALO, tm), cols]).astype(o_ref.dtype)

    return pl.pallas_call(
        body,
        out_shape=jax.ShapeDtypeStruct((lp, C), BF16),
        grid=(lp // tm,),
        in_specs=[
            pl.BlockSpec((POOL_HALO, C), lambda i: (jnp.maximum(i * hb - 1, 0), 0)),
            pl.BlockSpec((tm, C), lambda i: (i, 0)),
        ],
        out_specs=pl.BlockSpec((tm, C), lambda i: (i, 0)),
        scratch_shapes=[pltpu.VMEM((tm + POOL_HALO, C), F32)],
        name=name,
        compiler_params=_cp(("parallel",)),
    )(a, a)


def _pool_bwd(dp, name):
    lp, C = dp.shape
    tm = _row_tile(lp, 640)
    hb = tm // POOL_HALO
    nt = lp // tm
    last_halo = lp // POOL_HALO - 1

    def body(cur_ref, next_ref, o_ref, xs):
        i = pl.program_id(0)
        pos = i * tm + lax.broadcasted_iota(jnp.int32, (tm, 1), 0) - PAD0
        pos_h = (i + 1) * tm + lax.broadcasted_iota(jnp.int32, (POOL_HALO, 1), 0) - PAD0
        for g, win in enumerate(POOL_WINDOWS):
            cols = pl.ds(g * POOL_GROUP, POOL_GROUP)
            cur = cur_ref[:, cols]
            xs[pl.ds(0, tm), cols] = cur / _pool_counts(pos, win)
            xs[pl.ds(tm, POOL_HALO), cols] = jnp.where(i < nt - 1, next_ref[:, cols], 0.0) / _pool_counts(pos_h, win)
            s = xs[pl.ds(0, tm), cols]
            for k in range(1, win):
                s = s + xs[pl.ds(k, tm), cols]
            o_ref[:, cols] = jnp.where(pos >= 0, s - cur, 0.0)

    return pl.pallas_call(
        body,
        out_shape=jax.ShapeDtypeStruct((lp, C), F32),
        grid=(nt,),
        in_specs=[
            pl.BlockSpec((tm, C), lambda i: (i, 0)),
            pl.BlockSpec((POOL_HALO, C), lambda i: (jnp.minimum((i + 1) * hb, last_halo), 0)),
        ],
        out_specs=pl.BlockSpec((tm, C), lambda i: (i, 0)),
        scratch_shapes=[pltpu.VMEM((tm + POOL_HALO, C), F32)],
        name=name,
        compiler_params=_cp(("parallel",)),
    )(dp, dp)


def _scale_add(h, pre, scale, name):
    M, C = h.shape
    tm = _row_tile(M, 640)

    def body(h_ref, p_ref, s_ref, o_ref):
        o_ref[...] = h_ref[...] + p_ref[...] * s_ref[...]

    row = pl.BlockSpec((tm, C), lambda i: (i, 0))
    return pl.pallas_call(
        body,
        out_shape=jax.ShapeDtypeStruct((M, C), F32),
        grid=(M // tm,),
        in_specs=[row, row, pl.BlockSpec((1, C), lambda i: (0, 0))],
        out_specs=row,
        name=name,
        compiler_params=_cp(("parallel",)),
    )(h, pre, scale.reshape(1, C))


def _scale_bwd(dh, pre, scale, name):
    M, C = dh.shape
    tm = _row_tile(M, 640)

    def body(dh_ref, p_ref, s_ref, dp_ref, ds_ref):
        @pl.when(pl.program_id(0) == 0)
        def _():
            ds_ref[...] = jnp.zeros_like(ds_ref)

        d = dh_ref[...]
        ds_ref[...] += jnp.sum(d * p_ref[...], axis=0, keepdims=True)
        dp_ref[...] = (d * s_ref[...]).astype(dp_ref.dtype)

    row = pl.BlockSpec((tm, C), lambda i: (i, 0))
    vec = pl.BlockSpec((1, C), lambda i: (0, 0))
    return pl.pallas_call(
        body,
        out_shape=(jax.ShapeDtypeStruct((M, C), BF16), jax.ShapeDtypeStruct((1, C), F32)),
        grid=(M // tm,),
        in_specs=[row, row, vec],
        out_specs=(row, vec),
        name=name,
        compiler_params=_cp(("arbitrary",)),
    )(dh, pre, scale.reshape(1, C))


def _gate_parts(z):
    e = jnp.exp(-jnp.abs(z))
    return e, jnp.minimum(z, 0.0) - jnp.log(1.0 + e)


def _tri_dot3(tri, x):
    hi, mid, lo = _split3(x)
    d = functools.partial(jnp.dot, preferred_element_type=F32)
    return d(tri, hi) + d(tri, mid) + d(tri, lo)


def _gate_fwd(x, b, name):
    lp, C = x.shape
    tm = _row_tile(lp, 640)

    def body(x_ref, b_ref, o_ref, carry):
        i = pl.program_id(0)

        @pl.when(i == 0)
        def _():
            carry[...] = jnp.zeros_like(carry)

        _, ls = _gate_parts(x_ref[...] + b_ref[...])
        rows = i * tm + lax.broadcasted_iota(jnp.int32, (tm, 1), 0)
        ls = jnp.where(rows >= PAD0, ls, 0.0)
        tri = (lax.broadcasted_iota(jnp.int32, (tm, tm), 0) >= lax.broadcasted_iota(jnp.int32, (tm, tm), 1)).astype(BF16)
        f = _tri_dot3(tri, ls) + carry[...]
        o_ref[...] = f
        carry[...] = f[tm - 1:tm, :]

    return pl.pallas_call(
        body,
        out_shape=jax.ShapeDtypeStruct((lp, C), F32),
        grid=(lp // tm,),
        in_specs=[pl.BlockSpec((tm, C), lambda i: (i, 0)), pl.BlockSpec((1, C), lambda i: (0, 0))],
        out_specs=pl.BlockSpec((tm, C), lambda i: (i, 0)),
        scratch_shapes=[pltpu.VMEM((1, C), F32)],
        name=name,
        compiler_params=_cp(("arbitrary",)),
    )(x, b)


def _gate_bwd(x, b, df, name):
    lp, C = x.shape
    tm = _row_tile(lp, 640)
    nt = lp // tm

    def body(x_ref, b_ref, df_ref, dx_ref, db_ref, carry):
        i = pl.program_id(0)

        @pl.when(i == 0)
        def _():
            carry[...] = jnp.zeros_like(carry)
            db_ref[...] = jnp.zeros_like(db_ref)

        z = x_ref[...] + b_ref[...]
        e, _ = _gate_parts(z)
        tri = (lax.broadcasted_iota(jnp.int32, (tm, tm), 0) <= lax.broadcasted_iota(jnp.int32, (tm, tm), 1)).astype(BF16)
        r = _tri_dot3(tri, df_ref[...]) + carry[...]
        carry[...] = r[0:1, :]
        rows = (nt - 1 - i) * tm + lax.broadcasted_iota(jnp.int32, (tm, 1), 0)
        dx = jnp.where(rows >= PAD0, r * (jnp.where(z >= 0.0, e, 1.0) / (1.0 + e)), 0.0)
        dx_ref[...] = dx
        db_ref[...] += jnp.sum(dx, axis=0, keepdims=True)

    rev = pl.BlockSpec((tm, C), lambda i: (nt - 1 - i, 0))
    vec = pl.BlockSpec((1, C), lambda i: (0, 0))
    return pl.pallas_call(
        body,
        out_shape=(jax.ShapeDtypeStruct((lp, C), F32), jax.ShapeDtypeStruct((1, C), F32)),
        grid=(nt,),
        in_specs=[rev, vec, rev],
        out_specs=(rev, vec),
        scratch_shapes=[pltpu.VMEM((1, C), F32)],
        name=name,
        compiler_params=_cp(("arbitrary",)),
    )(x, b, df)


MLA_SCALE = (MLA_NOPE + MLA_ROPE) ** -0.5


def _rope_apply(x, c, a, b):
    return x * c + pltpu.roll(x, LANES - 16, 1) * a + pltpu.roll(x, 16, 1) * b


def _rope_transpose(dy, c, a, b):
    return dy * c + pltpu.roll(dy * a, 16, 1) + pltpu.roll(dy * b, LANES - 16, 1)


def _mla_prep_fwd(q, kmat, kr, c, a, b, name):
    lp, W = q.shape
    nh = W // LANES
    tm = _row_tile(lp, 640)

    def body(q_ref, k_ref, kr_ref, c_ref, a_ref, b_ref, qo_ref, ko_ref):
        cv, av, bv = c_ref[...], a_ref[...], b_ref[...]
        qo_ref[...] = (_rope_apply(q_ref[...], cv, av, bv) * MLA_SCALE).astype(qo_ref.dtype)
        ko_ref[...] = (k_ref[...] + _rope_apply(kr_ref[...], cv, av, bv)).astype(ko_ref.dtype)

    head = pl.BlockSpec((tm, LANES), lambda i, h: (i, h))
    tab = pl.BlockSpec((tm, LANES), lambda i, h: (i, 0))
    wide = jax.ShapeDtypeStruct((lp, W), BF16)
    return pl.pallas_call(
        body,
        out_shape=(wide, wide),
        grid=(lp // tm, nh),
        in_specs=[head, head, tab, tab, tab, tab],
        out_specs=(head, head),
        name=name,
        compiler_params=_cp(("parallel", "parallel")),
    )(q, kmat, kr, c, a, b)


def _mla_prep_bwd(dq, dk, c, a, b, name):
    lp, W = dq.shape
    nh = W // LANES
    tm = _row_tile(lp, 640)

    def body(dq_ref, dk_ref, c_ref, a_ref, b_ref, dqo_ref, dkr_ref):
        cv, av, bv = c_ref[...], a_ref[...], b_ref[...]
        ksum = jnp.zeros((tm, LANES), F32)
        for h in range(nh):
            cols = pl.ds(h * LANES, LANES)
            dqo_ref[:, cols] = _rope_transpose(dq_ref[:, cols].astype(F32) * MLA_SCALE, cv, av, bv).astype(dqo_ref.dtype)
            ksum = ksum + dk_ref[:, cols].astype(F32)
        dkr_ref[...] = _rope_transpose(ksum, cv, av, bv)

    wide = pl.BlockSpec((tm, W), lambda i: (i, 0))
    tab = pl.BlockSpec((tm, LANES), lambda i: (i, 0))
    return pl.pallas_call(
        body,
        out_shape=(jax.ShapeDtypeStruct((lp, W), BF16), jax.ShapeDtypeStruct((lp, LANES), F32)),
        grid=(lp // tm,),
        in_specs=[wide, wide, tab, tab, tab],
        out_specs=(wide, tab),
        name=name,
        compiler_params=_cp(("parallel",)),
    )(dq, dk, c, a, b)


def _loss_head(h, g, target, name):
    lp, C = h.shape
    tm = _row_tile(lp, 640)
    nt = lp // tm

    def body(h_ref, g_ref, t_ref, loss_ref, dh_ref, dg_ref, sq):
        i = pl.program_id(0)

        @pl.when(i == 0)
        def _():
            dg_ref[...] = jnp.zeros_like(dg_ref)
            sq[...] = jnp.zeros_like(sq)

        xf = h_ref[...]
        gv = g_ref[...]
        r = lax.rsqrt(jnp.mean(xf * xf, axis=-1, keepdims=True) + EPS)
        xhat = xf * r
        rows = i * tm + lax.broadcasted_iota(jnp.int32, (tm, 1), 0)
        err = jnp.where(rows >= PAD0 + N_META, xhat * gv - t_ref[...], 0.0)
        sq[...] += jnp.sum(err * err, axis=0, keepdims=True)
        dy = err * (1.0 / C)
        dg_ref[...] += jnp.sum(dy * xhat, axis=0, keepdims=True)
        dxh = dy * gv
        dh_ref[...] = r * (dxh - xhat * jnp.mean(dxh * xhat, axis=-1, keepdims=True))

        @pl.when(i == nt - 1)
        def _():
            loss_ref[...] = jnp.broadcast_to(jnp.sum(sq[...], axis=1, keepdims=True) * (0.5 / C), (1, LANES))

    row = pl.BlockSpec((tm, C), lambda i: (i, 0))
    vec = pl.BlockSpec((1, C), lambda i: (0, 0))
    return pl.pallas_call(
        body,
        out_shape=(jax.ShapeDtypeStruct((1, LANES), F32), jax.ShapeDtypeStruct((lp, C), F32), jax.ShapeDtypeStruct((1, C), F32)),
        grid=(nt,),
        in_specs=[row, vec, row],
        out_specs=(pl.BlockSpec((1, LANES), lambda i: (0, 0)), row, vec),
        scratch_shapes=[pltpu.VMEM((1, C), F32)],
        name=name,
        compiler_params=_cp(("arbitrary",)),
    )(h, g.reshape(1, C), target)


def _adamw(w, g, m, v, name):
    shape = w.shape
    C = shape[-1]
    R = w.size // C
    tr = R
    if R % 8 == 0:
        for cand in range(8, R + 1, 8):
            if R % cand == 0 and cand * C * 4 <= (1 << 20):
                tr = cand
    c1 = 1.0 - ADAM_B1 ** ADAM_STEP
    c2 = 1.0 - ADAM_B2 ** ADAM_STEP

    def body(w_ref, g_ref, m_ref, v_ref, d_ref, nm_ref, nv_ref):
        gv = g_ref[...]
        nm = ADAM_B1 * m_ref[...] + (1.0 - ADAM_B1) * gv
        nv = ADAM_B2 * v_ref[...] + (1.0 - ADAM_B2) * (gv * gv)
        nm_ref[...] = nm
        nv_ref[...] = nv
        d_ref[...] = -ADAM_LR * ((nm / c1) / (jnp.sqrt(nv / c2) + ADAM_EPS) + ADAM_WD * w_ref[...])

    blk = pl.BlockSpec((tr, C), lambda i: (i, 0))
    out = jax.ShapeDtypeStruct((R, C), F32)
    outs = pl.pallas_call(
        body,
        out_shape=(out, out, out),
        grid=(R // tr,),
        in_specs=[blk] * 4,
        out_specs=(blk, blk, blk),
        name=name,
        compiler_params=_cp(("parallel",)),
    )(*(t.reshape(R, C) for t in (w, g, m, v)))
    return tuple(t.reshape(shape) for t in outs)


def _exchange(send, axes, same, name):
    na = len(axes)
    n = 1 << na
    _, R, C = send.shape
    parts = max(p for p in (8, 4, 2, 1) if R % (16 * p) == 0 or p == 1)
    pr = R // parts

    def body(send_ref, recv_ref, send_sems, recv_sems, local_sem):
        coords = {ax: lax.axis_index(ax) for ax in MESH_AXES}
        me = 0
        for ax in axes:
            me = me * 2 + coords[ax]

        def member(r):
            dev = dict(coords)
            for b, ax in enumerate(axes):
                if (r >> (na - 1 - b)) & 1:
                    dev[ax] = 1 - dev[ax]
            return tuple(dev[ax] for ax in MESH_AXES)

        def chunk(j, p):
            return (send_ref.at[0] if same else send_ref.at[j]).at[pl.ds(p * pr, pr)]

        def slot(j, p):
            return recv_ref.at[j].at[pl.ds(p * pr, pr)]

        own = pltpu.make_async_copy(send_ref.at[0] if same else send_ref.at[me], recv_ref.at[me], local_sem)
        own.start()
        copies = []
        for r in range(1, n):
            peer = me ^ r
            for p in range(parts):
                cp = pltpu.make_async_remote_copy(
                    src_ref=chunk(peer, p), dst_ref=slot(me, p), send_sem=send_sems.at[r, p], recv_sem=recv_sems.at[r, p],
                    device_id=member(r), device_id_type=pl.DeviceIdType.MESH)
                cp.start()
                copies.append(cp)
        for r in range(1, n):
            for p in range(parts):
                arrival = pltpu.make_async_remote_copy(
                    src_ref=chunk(me, p), dst_ref=slot(me ^ r, p), send_sem=send_sems.at[r, p], recv_sem=recv_sems.at[r, p],
                    device_id=member(r), device_id_type=pl.DeviceIdType.MESH)
                arrival.wait_recv()
        for cp in copies:
            cp.wait_send()
        own.wait()

    any_spec = pl.BlockSpec(memory_space=pl.ANY)
    return pl.pallas_call(
        body,
        out_shape=jax.ShapeDtypeStruct((n, R, C), send.dtype),
        in_specs=[any_spec],
        out_specs=any_spec,
        scratch_shapes=[pltpu.SemaphoreType.DMA((n, parts)), pltpu.SemaphoreType.DMA((n, parts)), pltpu.SemaphoreType.DMA],
        name=name,
        compiler_params=pltpu.CompilerParams(has_side_effects=True),
    )(send)


def _sum_chunks(x, name, out_dtype=F32):
    n, R, C = x.shape
    tr = _row_tile(R, 512)

    def body(x_ref, o_ref):
        acc = x_ref[0].astype(F32)
        for j in range(1, n):
            acc = acc + x_ref[j].astype(F32)
        o_ref[...] = acc.astype(o_ref.dtype)

    return pl.pallas_call(
        body,
        out_shape=jax.ShapeDtypeStruct((R, C), out_dtype),
        grid=(R // tr,),
        in_specs=[pl.BlockSpec((n, tr, C), lambda i: (0, i, 0))],
        out_specs=pl.BlockSpec((tr, C), lambda i: (i, 0)),
        name=name,
        compiler_params=_cp(("parallel",)),
    )(x)


def _pad_heads_cols(w, groups, d):
    k = w.shape[0]
    w = w.reshape(k, groups * N_HEADS, d)
    return jnp.pad(w, ((0, 0), (0, 0), (0, LANES - d))).reshape(k, groups * N_HEADS * LANES)


def _unpad_heads_cols(w, groups, d):
    k = w.shape[0]
    return w.reshape(k, groups * N_HEADS, LANES)[:, :, :d].reshape(k, groups * N_HEADS * d)


def _pad_heads_rows(w, d):
    n = w.shape[1]
    return jnp.pad(w.reshape(N_HEADS, d, n), ((0, 0), (0, LANES - d), (0, 0))).reshape(N_HEADS * LANES, n)


def _unpad_heads_rows(w, d):
    n = w.shape[1]
    return w.reshape(N_HEADS, LANES, n)[:, :d].reshape(N_HEADS * d, n)


Q_SCALE = HEAD_DIM ** -0.5


def _scale_q_cols(w):
    nq = N_HEADS * LANES
    return jnp.concatenate([w[:, :nq] * Q_SCALE, w[:, nq:]], axis=1)


def _kernel_weights(W):
    P = dict(W)
    pw = W["pool_w"][0]
    bd = jnp.zeros((D_MODEL, D_MODEL), pw.dtype)
    for g in range(len(POOL_WINDOWS)):
        bd = lax.dynamic_update_slice(bd, pw[g], (g * POOL_GROUP, g * POOL_GROUP))
    P["pool_bd"] = bd
    P["sb_qkv"] = _scale_q_cols(_pad_heads_cols(W["sb_w_qkv"][0], 3, HEAD_DIM))
    P["sb_o"] = _pad_heads_rows(W["sb_w_o"][0], HEAD_DIM)
    nq = 3 * N_HEADS * HEAD_DIM
    P["fox_qkv"] = _scale_q_cols(_pad_heads_cols(W["fox_w_qkvf"][0][:, :nq], 3, HEAD_DIM))
    P["fox_f"] = jnp.pad(W["fox_w_qkvf"][0][:, nq:], ((0, 0), (0, LANES - N_HEADS)))
    P["fox_o"] = _pad_heads_rows(W["fox_w_o"][0], HEAD_DIM)
    P["fox_b"] = jnp.pad(W["fox_b_f"], ((0, 0), (0, LANES - N_HEADS)))
    P["mla_down"] = jnp.pad(W["mla_w_down"][0], ((0, 0), (0, MLA_DOWN_PAD - W["mla_w_down"].shape[2])))
    P["mla_uq"] = _pad_heads_cols(W["mla_w_uq"][0], 1, MLA_NOPE + MLA_ROPE)
    ukv = W["mla_w_ukv"][0].reshape(MLA_KV_RANK, N_HEADS, 2 * HEAD_DIM)
    padk = ((0, 0), (0, 0), (0, LANES - HEAD_DIM))
    P["mla_ukv"] = jnp.concatenate(
        [jnp.pad(ukv[:, :, :MLA_NOPE], padk).reshape(MLA_KV_RANK, -1), jnp.pad(ukv[:, :, MLA_NOPE:], padk).reshape(MLA_KV_RANK, -1)], axis=1)
    P["mla_o"] = _pad_heads_rows(W["mla_w_o"][0], HEAD_DIM)
    return P


def _rope_tables(lp):
    pos = (jnp.arange(lp) - PAD0).astype(F32)
    inv = ROPE_THETA ** (-jnp.arange(0, MLA_ROPE, 2, dtype=F32) / MLA_ROPE)
    ang = pos[:, None] * inv[None, :]
    cos, sin = jnp.cos(ang), jnp.sin(ang)
    half = MLA_ROPE // 2
    z = lambda n: jnp.zeros((lp, n), F32)
    c = jnp.concatenate([jnp.ones((lp, MLA_NOPE), F32), cos, cos, z(LANES - MLA_NOPE - MLA_ROPE)], axis=1)
    a = jnp.concatenate([z(MLA_NOPE), -sin, z(LANES - MLA_NOPE - half)], axis=1)
    b = jnp.concatenate([z(MLA_NOPE + half), sin, z(LANES - MLA_NOPE - MLA_ROPE)], axis=1)
    return c, a, b


def _key_bias(lp, t, per_head=None):
    pad = jnp.arange(lp)[None, :] < PAD0
    body = jnp.zeros((N_HEADS, lp), F32) if per_head is None else per_head
    return jnp.where(pad, NEG, body).reshape(N_HEADS, lp // t, 1, t)


def _ffn_fwd(h, i, P):
    b = _rms_fwd(h, P["norm_ffn"][i], BF16, "ffn_norm")
    g = _mm(b, P["ffn_w_gate"][i], "nn", "ffn_gate")
    u = _mm(b, P["ffn_w_up"][i], "nn", "ffn_up")
    hd = _swiglu_fwd(g, u, "ffn_act")
    return _mm(hd, P["ffn_w_down"][i], "nn", "ffn_down", add=h), (h, b, g, u, hd)


def _ffn_bwd(dh, i, P, saved):
    h, b, g, u, hd = saved
    dwd = _mm(hd, dh, "tn", "ffn_down_dw")
    dhd = _mm(dh, P["ffn_w_down"][i], "nt", "ffn_down_dx")
    dg, du = _swiglu_bwd(g, u, dhd, "ffn_act_bwd")
    dwg = _mm(b, dg, "tn", "ffn_gate_dw")
    dwu = _mm(b, du, "tn", "ffn_up_dw")
    db = _mm(dg, P["ffn_w_gate"][i], "nt", "ffn_gate_dx")
    db = _mm(du, P["ffn_w_up"][i], "nt", "ffn_up_dx", add=db)
    dh_in, dgain = _rms_bwd(h, P["norm_ffn"][i], db, dh, "ffn_norm_bwd")
    return dh_in, dgain, dwg, dwu, dwd


def _pool_layer_fwd(h, P):
    a = _rms_fwd(h, P["norm_mix"][0], F32, "pool_norm")
    pooled = _pool_fwd(a, "pool_window")
    pre = _mm(pooled, P["pool_bd"], "nn", "pool_mix")
    return _scale_add(h, pre, P["pool_scale"][0], "pool_scale_add"), (h, pooled, pre)


def _pool_layer_bwd(dh, P, saved):
    h, pooled, pre = saved
    dpre, dscale = _scale_bwd(dh, pre, P["pool_scale"][0], "pool_scale_bwd")
    dbd = _mm(pooled, dpre, "tn", "pool_mix_dw")
    dpooled = _mm(dpre, P["pool_bd"], "nt", "pool_mix_dx")
    da = _pool_bwd(dpooled, "pool_window_bwd")
    dh_in, dgain = _rms_bwd(h, P["norm_mix"][0], da, dh, "mix_norm_bwd")
    dw = jnp.stack([dbd[g * POOL_GROUP:(g + 1) * POOL_GROUP, g * POOL_GROUP:(g + 1) * POOL_GROUP] for g in range(len(POOL_WINDOWS))])
    return dh_in, {"norm_mix0": dgain, "pool_w": dw[None], "pool_scale": dscale}


def _out_proj_bwd(o, dh, wo, tag):
    return _mm(o, dh, "tn", tag + "_o_dw"), _mm(dh, wo, "nt", tag + "_o_dx", out_dtype=BF16)


def _sb_layer_fwd(h, P):
    lp = h.shape[0]
    a = _rms_fwd(h, P["norm_mix"][1], BF16, "mix_norm")
    qkv = _mm(a, P["sb_qkv"], "nn", "sb_qkv", out_dtype=BF16)
    kb = _key_bias(lp, SB_TK)
    o = _sb_fwd(qkv, kb, nh=N_HEADS, name="sb_attn")
    return _mm(o, P["sb_o"], "nn", "attn_out", add=h), (h, a, qkv, kb, o)


def _sb_layer_bwd(dh, P, saved):
    h, a, qkv, kb, o = saved
    dwo, do = _out_proj_bwd(o, dh, P["sb_o"], "attn")
    dq, dk, dv = _sb_bwd(qkv, kb, do, nh=N_HEADS, name="sb_attn_bwd")
    dqkv = jnp.concatenate([dq, dk, dv], axis=1)
    dw = _scale_q_cols(_mm(a, dqkv, "tn", "qkv_dw"))
    da = _mm(dqkv, P["sb_qkv"], "nt", "qkv_dx")
    dh_in, dgain = _rms_bwd(h, P["norm_mix"][1], da, dh, "mix_norm_bwd")
    return dh_in, {"norm_mix1": dgain, "sb_w_qkv": _unpad_heads_cols(dw, 3, HEAD_DIM)[None], "sb_w_o": _unpad_heads_rows(dwo, HEAD_DIM)[None]}


def _fox_layer_fwd(h, P):
    lp = h.shape[0]
    t = _attn_tile(lp)
    a = _rms_fwd(h, P["norm_mix"][3], BF16, "mix_norm")
    qkv = _mm(a, P["fox_qkv"], "nn", "sb_qkv", out_dtype=BF16)
    f = _mm(a, P["fox_f"], "nn", "fox_gate_proj")
    fc = _gate_fwd(f, P["fox_b"], "fox_gate")[:, :N_HEADS]
    kb = _key_bias(lp, t, -fc.T)
    fq = jnp.broadcast_to(fc[:, :, None], (lp, N_HEADS, LANES)).reshape(lp, N_HEADS * LANES)
    o, lse = _attn_fwd(qkv, qkv, qkv, kb, fq, nh=N_HEADS, offs=(0, N_HEADS, 2 * N_HEADS), name="fox_attn")
    return _mm(o, P["fox_o"], "nn", "attn_out", add=h), (h, a, qkv, f, kb, fq, o, lse)


def _fox_layer_bwd(dh, P, saved):
    h, a, qkv, f, kb, fq, o, lse = saved
    lp = h.shape[0]
    dwo, do = _out_proj_bwd(o, dh, P["fox_o"], "attn")
    dq, dk, dv, dkb, dqb = _attn_bwd(qkv, qkv, qkv, kb, fq, o, do, lse, nh=N_HEADS, offs=(0, N_HEADS, 2 * N_HEADS),
                                     name="fox_attn_bwd")
    dfc = jnp.pad(dqb.reshape(lp, N_HEADS, LANES)[:, :, 0] - dkb.reshape(N_HEADS, lp).T, ((0, 0), (0, LANES - N_HEADS)))
    df, dbf = _gate_bwd(f, P["fox_b"], dfc, "fox_gate_bwd")
    dqkv = jnp.concatenate([dq, dk, dv], axis=1)
    dw = _scale_q_cols(_mm(a, dqkv, "tn", "qkv_dw"))
    dwf = _mm(a, df, "tn", "fox_gate_dw")
    da = _mm(dqkv, P["fox_qkv"], "nt", "qkv_dx")
    da = _mm(df, P["fox_f"], "nt", "fox_gate_dx", add=da)
    dh_in, dgain = _rms_bwd(h, P["norm_mix"][3], da, dh, "mix_norm_bwd")
    dwqkvf = jnp.concatenate([_unpad_heads_cols(dw, 3, HEAD_DIM), dwf[:, :N_HEADS]], axis=1)
    return dh_in, {"norm_mix3": dgain, "fox_w_qkvf": dwqkvf[None], "fox_b_f": dbf[:, :N_HEADS], "fox_w_o": _unpad_heads_rows(dwo, HEAD_DIM)[None]}


def _mla_layer_fwd(h, P):
    lp = h.shape[0]
    a = _rms_fwd(h, P["norm_mix"][2], BF16, "mix_norm")
    down = _mm(a, P["mla_down"], "nn", "mla_down")
    cq_pre = down[:, :MLA_Q_RANK]
    ckv_pre = down[:, MLA_Q_RANK:MLA_Q_RANK + MLA_KV_RANK]
    kr = jnp.pad(down[:, MLA_Q_RANK + MLA_KV_RANK:MLA_Q_RANK + MLA_KV_RANK + MLA_ROPE], ((0, 0), (MLA_NOPE, LANES - MLA_NOPE - MLA_ROPE)))
    cq = _rms_fwd(cq_pre, P["mla_q_norm"][0], BF16, "mla_q_norm")
    ckv = _rms_fwd(ckv_pre, P["mla_kv_norm"][0], BF16, "mla_kv_norm")
    q = _mm(cq, P["mla_uq"], "nn", "mla_uq")
    kv = _mm(ckv, P["mla_ukv"], "nn", "mla_ukv", out_dtype=BF16)
    tabs = _rope_tables(lp)
    qr, kc = _mla_prep_fwd(q, kv, kr, *tabs, "mla_rope")
    kb = _key_bias(lp, _attn_tile(lp))
    o, lse = _attn_fwd(qr, kc, kv, kb, None, nh=N_HEADS, offs=(0, 0, N_HEADS), name="mla_attn")
    return _mm(o, P["mla_o"], "nn", "attn_out", add=h), (h, a, cq_pre, ckv_pre, cq, ckv, qr, kc, kv, tabs, kb, o, lse)


def _mla_layer_bwd(dh, P, saved):
    h, a, cq_pre, ckv_pre, cq, ckv, qr, kc, kv, tabs, kb, o, lse = saved
    lp = h.shape[0]
    dwo, do = _out_proj_bwd(o, dh, P["mla_o"], "attn")
    dqr, dkc, dv = _attn_bwd(qr, kc, kv, kb, None, o, do, lse, nh=N_HEADS, offs=(0, 0, N_HEADS),
                             name="mla_attn_bwd")
    dq, dkr = _mla_prep_bwd(dqr, dkc, *tabs, "mla_rope_bwd")
    dkv = jnp.concatenate([dkc, dv], axis=1)
    dwuq = _mm(cq, dq, "tn", "mla_uq_dw")
    dcq = _mm(dq, P["mla_uq"], "nt", "mla_uq_dx")
    dwukv = _mm(ckv, dkv, "tn", "mla_ukv_dw")
    dckv = _mm(dkv, P["mla_ukv"], "nt", "mla_ukv_dx")
    dcq_pre, dqn = _rms_bwd(cq_pre, P["mla_q_norm"][0], dcq, None, "mla_q_norm_bwd")
    dckv_pre, dkvn = _rms_bwd(ckv_pre, P["mla_kv_norm"][0], dckv, None, "mla_kv_norm_bwd")
    used = MLA_Q_RANK + MLA_KV_RANK + MLA_ROPE
    ddown = jnp.concatenate([dcq_pre, dckv_pre, dkr[:, MLA_NOPE:MLA_NOPE + MLA_ROPE], jnp.zeros((lp, MLA_DOWN_PAD - used), F32)], axis=1)
    dwdown = _mm(a, ddown, "tn", "mla_down_dw")
    da = _mm(ddown, P["mla_down"], "nt", "mla_down_dx")
    dh_in, dgain = _rms_bwd(h, P["norm_mix"][2], da, dh, "mix_norm_bwd")
    dukv = dwukv.reshape(MLA_KV_RANK, 2, N_HEADS, LANES)[:, :, :, :HEAD_DIM]
    dukv = jnp.concatenate([dukv[:, 0], dukv[:, 1]], axis=-1).reshape(MLA_KV_RANK, N_HEADS * 2 * HEAD_DIM)
    return dh_in, {
        "norm_mix2": dgain, "mla_w_down": dwdown[:, :used][None], "mla_q_norm": dqn, "mla_kv_norm": dkvn,
        "mla_w_uq": _unpad_heads_cols(dwuq, 1, MLA_NOPE + MLA_ROPE)[None], "mla_w_ukv": dukv[None],
        "mla_w_o": _unpad_heads_rows(dwo, HEAD_DIM)[None]}


_MIXERS = ((_pool_layer_fwd, _pool_layer_bwd), (_sb_layer_fwd, _sb_layer_bwd), (_mla_layer_fwd, _mla_layer_bwd), (_fox_layer_fwd, _fox_layer_bwd))


def _step_local(x, target, W):
    seq = x.shape[0]
    P = _kernel_weights(W)
    h = jnp.concatenate([jnp.zeros((PAD0, D_MODEL), F32), W["meta"], x], axis=0)
    tpad = jnp.pad(target, ((PAD0 + N_META, 0), (0, 0)))
    saved = []
    for i in range(4):
        h, s_mix = _MIXERS[i][0](h, P)
        h, s_ffn = _ffn_fwd(h, i, P)
        saved.append((s_mix, s_ffn))
    loss, dh, dfinal = _loss_head(h, W["final_norm"], tpad, "loss_head")
    grads = {"final_norm": dfinal.reshape(-1)}
    gains_mix, gains_ffn, dwg, dwu, dwd = [None] * 4, [None] * 4, [None] * 4, [None] * 4, [None] * 4
    for i in reversed(range(4)):
        s_mix, s_ffn = saved[i]
        dh, gains_ffn[i], dwg[i], dwu[i], dwd[i] = _ffn_bwd(dh, i, P, s_ffn)
        dh, g = _MIXERS[i][1](dh, P, s_mix)
        gains_mix[i] = g.pop("norm_mix%d" % i)
        grads.update(g)
    grads["norm_mix"] = jnp.concatenate(gains_mix, axis=0)
    grads["norm_ffn"] = jnp.concatenate(gains_ffn, axis=0)
    grads["ffn_w_gate"] = jnp.stack(dwg)
    grads["ffn_w_up"] = jnp.stack(dwu)
    grads["ffn_w_down"] = jnp.stack(dwd)
    grads["meta"] = dh[PAD0:PAD0 + N_META]
    return loss, dh[PAD0 + N_META:], grads


_WEIGHTS = ("meta", "norm_mix", "norm_ffn", "pool_w", "pool_scale", "sb_w_qkv", "sb_w_o", "mla_w_down", "mla_q_norm",
            "mla_kv_norm", "mla_w_uq", "mla_w_ukv", "mla_w_o", "fox_w_qkvf", "fox_b_f", "fox_w_o", "ffn_w_gate",
            "ffn_w_up", "ffn_w_down", "final_norm")
_SHARD_AXIS = {"meta": 1, "pool_w": 2, "sb_w_qkv": 2, "sb_w_o": 1, "mla_w_down": 1, "mla_q_norm": 1, "mla_kv_norm": 1,
               "mla_w_uq": 2, "mla_w_ukv": 2, "mla_w_o": 1, "fox_w_qkvf": 2, "fox_b_f": None, "fox_w_o": 1,
               "ffn_w_gate": 2, "ffn_w_up": 2, "ffn_w_down": 1}
_SHARDED = tuple(n for n in _WEIGHTS if _SHARD_AXIS.get(n) is not None)
_REPLICATED = tuple(n for n in _WEIGHTS if _SHARD_AXIS.get(n) is None)
_EXACT = ("meta", "mla_q_norm", "mla_kv_norm")
N_CHIPS = 4
GRAD_ROW_TILE = 512


PACK_ROWS = 16


def _piece_rows(t):
    return -(-t.size // (LANES * PACK_ROWS)) * PACK_ROWS


def _flat_rows(parts, dtype, row_multiple):
    pieces = []
    for p in parts:
        flat = p.astype(dtype).reshape(-1)
        pieces.append(jnp.pad(flat, (0, _piece_rows(p) * LANES - flat.shape[0])).reshape(-1, LANES))
    rows = sum(q.shape[0] for q in pieces)
    pad = -(-rows // row_multiple) * row_multiple - rows
    if pad:
        pieces.append(jnp.zeros((pad, LANES), dtype))
    return jnp.concatenate(pieces, axis=0)


def _split_flat(flat, like):
    out, off = [], 0
    for t in like:
        out.append(flat[off:off + _piece_rows(t)].reshape(-1)[:t.size].reshape(t.shape))
        off += _piece_rows(t)
    return out


def _gather_shards(local, names, dtype, name):
    blocks = [local[n] for n in names]
    recv = _exchange(_flat_rows(blocks, dtype, PACK_ROWS)[None], ("x", "y"), True, name)
    per_chip = [_split_flat(recv[s], blocks) for s in range(N_CHIPS)]
    return {n: jnp.concatenate([per_chip[s][k] for s in range(N_CHIPS)], axis=_SHARD_AXIS[n]) for k, n in enumerate(names)}


def _shard_of(g, n, s):
    w = g.shape[_SHARD_AXIS[n]] // N_CHIPS
    return lax.slice_in_dim(g, s * w, (s + 1) * w, axis=_SHARD_AXIS[n])


def _train_step(a):
    local = {n: a[n] for n in _WEIGHTS}
    full = {n: local[n] for n in _REPLICATED}
    full.update(_gather_shards(local, [n for n in _SHARDED if n not in _EXACT], BF16, "gather_weights"))
    full.update(_gather_shards(local, list(_EXACT), F32, "gather_exact"))

    loss, grad_x, grads = _step_local(a["x"][0], a["loss_target"][0], full)

    send = jnp.stack([
        _flat_rows([_shard_of(grads[n], n, s) for n in _SHARDED], BF16, 2 * GRAD_ROW_TILE).reshape(2, -1, LANES)
        for s in range(N_CHIPS)]).reshape(2 * N_CHIPS, -1, LANES)
    mine = _sum_chunks(_exchange(send, MESH_AXES, False, "scatter_grads"), "sum_grads", out_dtype=BF16)
    both = _exchange(mine[None], ("c",), True, "pair_grads").reshape(-1, LANES).astype(F32)
    reduced = dict(zip(_SHARDED, _split_flat(both, [local[n] for n in _SHARDED])))
    small = _flat_rows([grads[n] for n in _REPLICATED], F32, 8)
    small = _sum_chunks(_exchange(small[None], MESH_AXES, True, "gather_small_grads"), "sum_small_grads")
    reduced.update(zip(_REPLICATED, _split_flat(small, [local[n] for n in _REPLICATED])))

    deltas, new_m, new_v = {}, {}, {}
    for n in _WEIGHTS:
        deltas[n], new_m[n], new_v[n] = _adamw(local[n], reduced[n], a["m_" + n], a["v_" + n], "adamw")
    total = lax.psum(loss[0, 0], MESH_AXES)
    return (total, grad_x[None], *[reduced[n] for n in _WEIGHTS], *[deltas[n] for n in _WEIGHTS],
            *[new_m[n] for n in _WEIGHTS], *[new_v[n] for n in _WEIGHTS])


def kernel(x, meta, norm_mix, norm_ffn, pool_w, pool_scale, sb_w_qkv, sb_w_o, mla_w_down, mla_q_norm, mla_kv_norm, mla_w_uq, mla_w_ukv, mla_w_o, fox_w_qkvf, fox_b_f, fox_w_o, ffn_w_gate, ffn_w_up, ffn_w_down, final_norm, loss_target, m_meta, m_norm_mix, m_norm_ffn, m_pool_w, m_pool_scale, m_sb_w_qkv, m_sb_w_o, m_mla_w_down, m_mla_q_norm, m_mla_kv_norm, m_mla_w_uq, m_mla_w_ukv, m_mla_w_o, m_fox_w_qkvf, m_fox_b_f, m_fox_w_o, m_ffn_w_gate, m_ffn_w_up, m_ffn_w_down, m_final_norm, v_meta, v_norm_mix, v_norm_ffn, v_pool_w, v_pool_scale, v_sb_w_qkv, v_sb_w_o, v_mla_w_down, v_mla_q_norm, v_mla_kv_norm, v_mla_w_uq, v_mla_w_ukv, v_mla_w_o, v_fox_w_qkvf, v_fox_b_f, v_fox_w_o, v_ffn_w_gate, v_ffn_w_up, v_ffn_w_down, v_final_norm):
    return _train_step(dict(locals()))
```

```python
import functools

import jax
import jax.numpy as jnp
from jax import lax
from jax.experimental import pallas as pl
from jax.experimental.pallas import tpu as pltpu

F32 = jnp.float32
BF16 = jnp.bfloat16

D_MODEL = 1024
N_META = 16
PAD0 = 112
LANES = 128
N_HEADS = 16
HEAD_DIM = 64
POOL_WINDOWS = (2, 4, 8, 16)
POOL_GROUP = 256
POOL_HALO = 16
MLA_Q_RANK = 384
MLA_KV_RANK = 256
MLA_NOPE = 64
MLA_ROPE = 32
MLA_DOWN_PAD = 768
ROPE_THETA = 10000.0
D_FF = 2816
EPS = 1e-6
NEG = -1e30
ADAM_LR = 0.001
ADAM_B1 = 0.9
ADAM_B2 = 0.999
ADAM_EPS = 1e-08
ADAM_WD = 0.01
ADAM_STEP = 10
VMEM_LIMIT = 56 * 1024 * 1024
MESH_AXES = ("x", "y", "c")


def _cp(sem, **kw):
    return pltpu.CompilerParams(dimension_semantics=sem, vmem_limit_bytes=VMEM_LIMIT, **kw)


def _row_tile(m, target):
    best = None
    for t in range(16, min(m, target) + 1, 16):
        if m % t == 0:
            best = t
    return best or m


def _col_tile(n, target):
    best = None
    for t in range(LANES, min(n, target) + 1, LANES):
        if n % t == 0:
            best = t
    return best or n


def _mm(a, b, mode, name, out_dtype=F32, add=None, tm=640, tn=1536, tk=2048):
    if mode == "nn":
        (M, K), (K2, N) = a.shape, b.shape
    elif mode == "nt":
        (M, K), (N, K2) = a.shape, b.shape
    else:
        (K, M), (K2, N) = a.shape, b.shape
    assert K == K2, (mode, a.shape, b.shape)
    if mode == "tn":
        tm_ = _col_tile(M, 1408)
        tk_ = _row_tile(K, 1664)
    else:
        tm_ = _row_tile(M, tm)
        tk_ = _col_tile(K, tk) if K > tk else K
    tn_ = _col_tile(N, tn)
    nk = K // tk_
    if mode == "nn":
        a_spec = pl.BlockSpec((tm_, tk_), lambda i, j, k: (i, k))
        b_spec = pl.BlockSpec((tk_, tn_), lambda i, j, k: (k, j))
        dims = (((1,), (0,)), ((), ()))
    elif mode == "nt":
        a_spec = pl.BlockSpec((tm_, tk_), lambda i, j, k: (i, k))
        b_spec = pl.BlockSpec((tn_, tk_), lambda i, j, k: (j, k))
        dims = (((1,), (1,)), ((), ()))
    else:
        a_spec = pl.BlockSpec((tk_, tm_), lambda i, j, k: (k, i))
        b_spec = pl.BlockSpec((tk_, tn_), lambda i, j, k: (k, j))
        dims = (((0,), (0,)), ((), ()))
    o_spec = pl.BlockSpec((tm_, tn_), lambda i, j, k: (i, j))
    has_add = add is not None

    def body(*refs):
        if has_add:
            a_ref, b_ref, add_ref, o_ref, acc_ref = refs
        else:
            a_ref, b_ref, o_ref, acc_ref = refs
        k = pl.program_id(2)
        part = lax.dot_general(a_ref[...].astype(BF16), b_ref[...].astype(BF16), dims, preferred_element_type=F32)

        @pl.when(k == 0)
        def _():
            acc_ref[...] = part

        @pl.when(k > 0)
        def _():
            acc_ref[...] += part

        @pl.when(k == nk - 1)
        def _():
            r = acc_ref[...]
            if has_add:
                r = r + add_ref[...]
            o_ref[...] = r.astype(o_ref.dtype)

    ins = [a, b] + ([add] if has_add else [])
    in_specs = [a_spec, b_spec] + ([o_spec] if has_add else [])
    return pl.pallas_call(
        body,
        out_shape=jax.ShapeDtypeStruct((M, N), out_dtype),
        grid=(M // tm_, N // tn_, nk),
        in_specs=in_specs,
        out_specs=o_spec,
        scratch_shapes=[pltpu.VMEM((tm_, tn_), F32)],
        name=name,
        compiler_params=_cp(("parallel", "parallel", "arbitrary")),
    )(*ins)


def _rms_fwd(x, g, out_dtype, name):
    M, C = x.shape
    tm = _row_tile(M, 640)

    def body(x_ref, g_ref, o_ref):
        xf = x_ref[...]
        r = lax.rsqrt(jnp.mean(xf * xf, axis=-1, keepdims=True) + EPS)
        o_ref[...] = ((xf * r) * g_ref[...]).astype(o_ref.dtype)

    return pl.pallas_call(
        body,
        out_shape=jax.ShapeDtypeStruct((M, C), out_dtype),
        grid=(M // tm,),
        in_specs=[pl.BlockSpec((tm, C), lambda i: (i, 0)), pl.BlockSpec((1, C), lambda i: (0, 0))],
        out_specs=pl.BlockSpec((tm, C), lambda i: (i, 0)),
        name=name,
        compiler_params=_cp(("parallel",)),
    )(x, g.reshape(1, C))


def _rms_bwd(x, g, dy, dres, name):
    M, C = x.shape
    tm = _row_tile(M, 640)
    has_res = dres is not None

    def body(*refs):
        if has_res:
            x_ref, g_ref, dy_ref, dres_ref, dx_ref, dg_ref = refs
        else:
            x_ref, g_ref, dy_ref, dx_ref, dg_ref = refs
        xf = x_ref[...]
        r = lax.rsqrt(jnp.mean(xf * xf, axis=-1, keepdims=True) + EPS)
        xhat = xf * r
        dyf = dy_ref[...].astype(F32)

        @pl.when(pl.program_id(0) == 0)
        def _():
            dg_ref[...] = jnp.zeros_like(dg_ref)

        dg_ref[...] += jnp.sum(dyf * xhat, axis=0, keepdims=True)
        dxh = dyf * g_ref[...]
        dx = r * (dxh - xhat * jnp.mean(dxh * xhat, axis=-1, keepdims=True))
        if has_res:
            dx = dx + dres_ref[...]
        dx_ref[...] = dx

    row = pl.BlockSpec((tm, C), lambda i: (i, 0))
    vec = pl.BlockSpec((1, C), lambda i: (0, 0))
    ins = [x, g.reshape(1, C), dy] + ([dres] if has_res else [])
    return pl.pallas_call(
        body,
        out_shape=(jax.ShapeDtypeStruct((M, C), F32), jax.ShapeDtypeStruct((1, C), F32)),
        grid=(M // tm,),
        in_specs=[row, vec, row] + ([row] if has_res else []),
        out_specs=(row, vec),
        name=name,
        compiler_params=_cp(("arbitrary",)),
    )(*ins)


def _sigmoid(x):
    return 1.0 / (1.0 + jnp.exp(-x))


def _ffn_up(b, wg, wu, name):
    M, K = b.shape
    N = wg.shape[1]
    tm, tn = _row_tile(M, 640), _col_tile(N, 1536)

    def body(b_ref, wg_ref, wu_ref, g_ref, u_ref, h_ref):
        bv = b_ref[...]
        g = jnp.dot(bv, wg_ref[...], preferred_element_type=F32)
        u = jnp.dot(bv, wu_ref[...], preferred_element_type=F32)
        g_ref[...] = g
        u_ref[...] = u
        h_ref[...] = ((g * _sigmoid(g)) * u).astype(h_ref.dtype)

    w_spec = pl.BlockSpec((K, tn), lambda i, j: (0, j))
    o_spec = pl.BlockSpec((tm, tn), lambda i, j: (i, j))
    f32 = jax.ShapeDtypeStruct((M, N), F32)
    return pl.pallas_call(
        body,
        out_shape=(f32, f32, jax.ShapeDtypeStruct((M, N), BF16)),
        grid=(M // tm, N // tn),
        in_specs=[pl.BlockSpec((tm, K), lambda i, j: (i, 0)), w_spec, w_spec],
        out_specs=(o_spec, o_spec, o_spec),
        name=name,
        compiler_params=_cp(("parallel", "parallel")),
    )(b, wg, wu)


def _ffn_down_bwd(dh, wd, g, u, name):
    M, K = dh.shape
    N = wd.shape[0]
    tm, tn = _row_tile(M, 640), _col_tile(N, 1536)

    def body(dh_ref, wd_ref, g_ref, u_ref, dg_ref, du_ref):
        d = lax.dot_general(dh_ref[...].astype(BF16), wd_ref[...], _NT, preferred_element_type=F32)
        gv = g_ref[...]
        sg = _sigmoid(gv)
        du_ref[...] = (d * (gv * sg)).astype(du_ref.dtype)
        dg_ref[...] = ((d * u_ref[...]) * (sg * (1.0 + gv * (1.0 - sg)))).astype(dg_ref.dtype)

    o_spec = pl.BlockSpec((tm, tn), lambda i, j: (i, j))
    bf = jax.ShapeDtypeStruct((M, N), BF16)
    return pl.pallas_call(
        body,
        out_shape=(bf, bf),
        grid=(M // tm, N // tn),
        in_specs=[pl.BlockSpec((tm, K), lambda i, j: (i, 0)), pl.BlockSpec((tn, K), lambda i, j: (j, 0)), o_spec, o_spec],
        out_specs=(o_spec, o_spec),
        name=name,
        compiler_params=_cp(("parallel", "parallel")),
    )(dh, wd, g, u)


def _attn_tile(lp):
    return _col_tile(lp, 640)


def _head_specs(lp, t, offs):
    q_spec = pl.BlockSpec((t, LANES), lambda h, i: (i, offs[0] + h))
    k_spec = pl.BlockSpec((lp, LANES), lambda h, i: (0, offs[1] + h))
    v_spec = pl.BlockSpec((lp, LANES), lambda h, i: (0, offs[2] + h))
    return q_spec, k_spec, v_spec


def _attn_fwd(qa, ka, va, kb, fq, *, nh, offs, name, tile=640):
    lp = qa.shape[0]
    t = _col_tile(lp, tile)
    nq = lp // t
    has_fq = fq is not None

    def body(*refs):
        if has_fq:
            q_ref, k_ref, v_ref, kb_ref, fq_ref, o_ref, lse_ref = refs
        else:
            q_ref, k_ref, v_ref, kb_ref, o_ref, lse_ref = refs
        i = pl.program_id(1)
        q = q_ref[...]
        fqc = fq_ref[:, 0:1] if has_fq else None
        causal = lax.broadcasted_iota(jnp.int32, (t, t), 1) <= lax.broadcasted_iota(jnp.int32, (t, t), 0)

        def step(j, carry, masked):
            m, l, acc = carry
            st = pl.multiple_of(j * t, t)
            k = k_ref[pl.ds(st, t), :]
            v = v_ref[pl.ds(st, t), :]
            s = lax.dot_general(q, k, _NT, preferred_element_type=F32)
            bias = kb_ref[j]
            if has_fq:
                bias = fqc + bias
            s = s + bias
            if masked:
                s = jnp.where(causal, s, NEG)
            m_new = jnp.maximum(m, jnp.max(s, axis=1, keepdims=True))
            p = jnp.exp(s - m_new)
            alpha = jnp.exp(m - m_new)
            l = alpha * l + jnp.sum(p, axis=1, keepdims=True)
            acc = alpha * acc + jnp.dot(p.astype(BF16), v, preferred_element_type=F32)
            return m_new, l, acc

        init = (jnp.full((t, 1), NEG, F32), jnp.zeros((t, 1), F32), jnp.zeros((t, LANES), F32))
        carry = lax.fori_loop(0, i, lambda j, c: step(j, c, False), init)
        m, l, acc = step(i, carry, True)
        valid = (i * t + lax.broadcasted_iota(jnp.int32, (t, 1), 0)) >= PAD0
        o_ref[...] = jnp.where(valid, acc / l, 0.0).astype(o_ref.dtype)
        lse_ref[...] = jnp.broadcast_to(m + jnp.log(l), (t, LANES))

    q_spec, k_spec, v_spec = _head_specs(lp, t, offs)
    kb_spec = pl.BlockSpec((None, nq, 1, t), lambda h, i: (h, 0, 0, 0))
    row_spec = pl.BlockSpec((t, LANES), lambda h, i: (i, h))
    ins = [qa, ka, va, kb] + ([fq] if has_fq else [])
    return pl.pallas_call(
        body,
        out_shape=(jax.ShapeDtypeStruct((lp, nh * LANES), BF16), jax.ShapeDtypeStruct((lp, nh * LANES), F32)),
        grid=(nh, nq),
        in_specs=[q_spec, k_spec, v_spec, kb_spec] + ([row_spec] if has_fq else []),
        out_specs=(row_spec, row_spec),
        name=name,
        compiler_params=_cp(("parallel", "arbitrary")),
    )(*ins)


def _attn_bwd(qa, ka, va, kb, fq, o, do, lse, *, nh, offs, name, tile=640):
    lp = qa.shape[0]
    t = _col_tile(lp, tile)
    nq = lp // t
    has_fq = fq is not None

    def body(*refs):
        if has_fq:
            q_ref, k_ref, v_ref, kb_ref, fq_ref, o_ref, do_ref, lse_ref, dq_ref, dk_ref, dv_ref, dkb_ref, dqb_ref, dk_acc, dv_acc = refs
        else:
            q_ref, k_ref, v_ref, kb_ref, o_ref, do_ref, lse_ref, dq_ref, dk_ref, dv_ref, dk_acc, dv_acc = refs
        i = pl.program_id(1)

        @pl.when(i == 0)
        def _():
            dk_acc[...] = jnp.zeros_like(dk_acc)
            dv_acc[...] = jnp.zeros_like(dv_acc)
            if has_fq:
                dkb_ref[...] = jnp.zeros_like(dkb_ref)

        q = q_ref[...]
        dov = do_ref[...]
        delta = jnp.sum(o_ref[...].astype(F32) * dov.astype(F32), axis=1, keepdims=True)
        lse_c = lse_ref[:, 0:1]
        fqc = fq_ref[:, 0:1] if has_fq else None
        causal = lax.broadcasted_iota(jnp.int32, (t, t), 1) <= lax.broadcasted_iota(jnp.int32, (t, t), 0)

        def step(j, carry, masked):
            dq_acc, rs = carry
            st = pl.multiple_of(j * t, t)
            k = k_ref[pl.ds(st, t), :]
            v = v_ref[pl.ds(st, t), :]
            s = lax.dot_general(q, k, _NT, preferred_element_type=F32)
            bias = kb_ref[j]
            if has_fq:
                bias = fqc + bias
            s = s + bias
            if masked:
                s = jnp.where(causal, s, NEG)
            p = jnp.exp(s - lse_c)
            dp = lax.dot_general(dov, v, (((1,), (1,)), ((), ())), preferred_element_type=F32)
            ds = p * (dp - delta)
            dv_acc[pl.ds(st, t), :] += lax.dot_general(p.astype(BF16), dov, (((0,), (0,)), ((), ())), preferred_element_type=F32)
            dsb = ds.astype(BF16)
            dk_acc[pl.ds(st, t), :] += lax.dot_general(dsb, q, (((0,), (0,)), ((), ())), preferred_element_type=F32)
            if has_fq:
                dkb_ref[j] += jnp.sum(ds, axis=0, keepdims=True)
                rs = rs + jnp.sum(ds, axis=1, keepdims=True)
            return dq_acc + jnp.dot(dsb, k, preferred_element_type=F32), rs

        carry = (jnp.zeros((t, LANES), F32), jnp.zeros((t, 1), F32))
        carry = lax.fori_loop(0, i, lambda j, c: step(j, c, False), carry)
        dq_acc, rs = step(i, carry, True)
        dq_ref[...] = dq_acc.astype(dq_ref.dtype)
        if has_fq:
            dqb_ref[...] = jnp.broadcast_to(rs, (t, LANES))

        @pl.when(i == nq - 1)
        def _():
            dk_ref[...] = dk_acc[...].astype(dk_ref.dtype)
            dv_ref[...] = dv_acc[...].astype(dv_ref.dtype)

    q_spec, k_spec, v_spec = _head_specs(lp, t, offs)
    kb_spec = pl.BlockSpec((None, nq, 1, t), lambda h, i: (h, 0, 0, 0))
    row_spec = pl.BlockSpec((t, LANES), lambda h, i: (i, h))
    col_spec = pl.BlockSpec((lp, LANES), lambda h, i: (0, h))
    ins = [qa, ka, va, kb] + ([fq] if has_fq else []) + [o, do, lse]
    wide = jax.ShapeDtypeStruct((lp, nh * LANES), BF16)
    extra_shapes = (jax.ShapeDtypeStruct(kb.shape, F32), jax.ShapeDtypeStruct((lp, nh * LANES), F32)) if has_fq else ()
    extra_specs = (kb_spec, row_spec) if has_fq else ()
    return pl.pallas_call(
        body,
        out_shape=(wide, wide, wide) + extra_shapes,
        grid=(nh, nq),
        in_specs=[q_spec, k_spec, v_spec, kb_spec] + ([row_spec] if has_fq else []) + [row_spec, row_spec, row_spec],
        out_specs=(row_spec, col_spec, col_spec) + extra_specs,
        scratch_shapes=[pltpu.VMEM((lp, LANES), F32), pltpu.VMEM((lp, LANES), F32)],
        name=name,
        compiler_params=_cp(("parallel", "arbitrary")),
    )(*ins)


SB_TK = 128


def _split3(x):
    hi = x.astype(BF16)
    r1 = x - hi.astype(F32)
    mid = r1.astype(BF16)
    lo = (r1 - mid.astype(F32)).astype(BF16)
    return hi, mid, lo


SB_RC = 128


_NT = (((1,), (1,)), ((), ()))
_TN = (((0,), (0,)), ((), ()))


def _sb_logits(zraw, kbj, mask):
    z = kbj + zraw
    if mask is not None:
        z = jnp.where(mask, z, NEG)
    e = jnp.exp(-jnp.abs(z))
    g = jnp.minimum(z, 0.0) - jnp.log(1.0 + e)
    lk = g - z
    return z, e, g, lk


def _dot3_parts(parts, tri):
    d = functools.partial(jnp.dot, preferred_element_type=F32)
    return d(parts[0], tri) + d(parts[1], tri) + d(parts[2], tri)


def _split2(x):
    hi = x.astype(BF16)
    return hi, (x - hi.astype(F32)).astype(BF16)


def _dot_parts(parts, m):
    out = jnp.dot(parts[0], m, preferred_element_type=F32)
    for p in parts[1:]:
        out = out + jnp.dot(p, m, preferred_element_type=F32)
    return out


def _tri(n, pred):
    return pred(lax.broadcasted_iota(jnp.int32, (n, n), 0), lax.broadcasted_iota(jnp.int32, (n, n), 1)).astype(BF16)


def _sb_diag_chunks(jj, nrc, rc, tk):
    plan = []
    for r in range(nrc):
        lo_row, hi_row = r * rc, (r + 1) * rc - 1
        lo_col, hi_col = jj * tk, (jj + 1) * tk - 1
        if hi_row <= lo_col:
            plan.append(None)
        elif lo_row > hi_col:
            plan.append("all")
        else:
            plan.append(lo_col - lo_row)
    return plan


def _sb_fwd(qa, kb, *, nh, name, tq=640):
    lp = qa.shape[0]
    tq = _col_tile(lp, tq)
    tk = SB_TK
    rc = min(SB_RC, tq)
    nq, sub, nrc = lp // tq, tq // tk, tq // rc

    def body(q_ref, k_ref, v_ref, kb_ref, o_ref, c_scr, acc_scr):
        i = pl.program_id(1)
        c_scr[...] = jnp.zeros_like(c_scr)
        acc_scr[...] = jnp.zeros_like(acc_scr)
        tri = _tri(tk, lambda r, c: r > c)
        row_io = lax.broadcasted_iota(jnp.int32, (rc, tk), 0)
        col_io = lax.broadcasted_iota(jnp.int32, (rc, tk), 1)

        def scores(j, rows):
            k = k_ref[pl.ds(pl.multiple_of(j * tk, tk), tk), :]
            return [lax.dot_general(q_ref[rs, :], k, _NT, preferred_element_type=F32) for rs in rows]

        def weights(j, rows, masks, zs):
            kbj = kb_ref[j]
            gs, splits, firsts = [], [], []
            for mask, zraw in zip(masks, zs):
                _, _, g, lk = _sb_logits(zraw, kbj, mask)
                gs.append(g)
                firsts.append(lk[:, 0:1])
                splits.append(_split2(lk))
            sums = [_dot_parts(p, tri) for p in splits]
            avs = [jnp.exp(g + (sm + c_scr[rs, :])).astype(BF16) for g, sm, rs in zip(gs, sums, rows)]
            for rs, sm, first in zip(rows, sums, firsts):
                c_scr[rs, :] += jnp.broadcast_to(sm[:, 0:1] + first, (rc, tk))
            return avs

        def values(j, rows, avs):
            v = v_ref[pl.ds(pl.multiple_of(j * tk, tk), tk), :]
            pvs = [jnp.dot(a, v, preferred_element_type=F32) for a in avs]
            for rs, pv in zip(rows, pvs):
                acc_scr[rs, :] += pv

        for jj in reversed(range(sub)):
            plan = _sb_diag_chunks(jj, nrc, rc, tk)
            live = [r for r, what in enumerate(plan) if what is not None]
            rows = [pl.ds(r * rc, rc) for r in live]
            masks = [None if plan[r] == "all" else (col_io + plan[r]) < row_io for r in live]
            j = i * sub + jj
            values(j, rows, weights(j, rows, masks, scores(j, rows)))

        n = i * sub
        rows = [pl.ds(r * rc, rc) for r in range(nrc)]
        nomask = [None] * nrc
        blk = lambda t: jnp.maximum(n - 1 - t, 0)

        def left(m, carry):
            zs, avs = carry
            t0, t1 = 2 * m, 2 * m + 1
            zs_next = scores(blk(t1 + 1), rows)
            zs1 = scores(blk(t1), rows)
            values(blk(t0 - 1), rows, avs)
            av0 = weights(blk(t0), rows, nomask, zs)
            av1 = weights(blk(t1), rows, nomask, zs1)
            values(blk(t0), rows, av0)
            return zs_next, av1

        first = (scores(blk(0), rows), [jnp.zeros((rc, tk), BF16)] * nrc)
        zs, avs = lax.fori_loop(0, n // 2, left, first)
        values(blk(2 * (n // 2) - 1), rows, avs)

        @pl.when(n % 2 == 1)
        def _():
            values(0, rows, weights(0, rows, nomask, zs))

        o_ref[...] = acc_scr[...].astype(o_ref.dtype)

    q_spec, k_spec, v_spec = _head_specs(lp, tq, (0, nh, 2 * nh))
    kb_spec = pl.BlockSpec((None, lp // tk, 1, tk), lambda h, i: (h, 0, 0, 0))
    row_spec = pl.BlockSpec((tq, LANES), lambda h, i: (i, h))
    return pl.pallas_call(
        body,
        out_shape=jax.ShapeDtypeStruct((lp, nh * LANES), BF16),
        grid=(nh, nq),
        in_specs=[q_spec, k_spec, v_spec, kb_spec],
        out_specs=row_spec,
        scratch_shapes=[pltpu.VMEM((tq, tk), F32), pltpu.VMEM((tq, LANES), F32)],
        name=name,
        compiler_params=_cp(("parallel", "arbitrary")),
    )(qa, qa, qa, kb)


def _sb_bwd(qa, kb, do, *, nh, name, tq=640):
    lp = qa.shape[0]
    tq = _col_tile(lp, tq)
    tk = SB_TK
    rc = min(SB_RC, tq)
    nq, nk, sub, nrc = lp // tq, lp // tk, tq // tk, tq // rc

    def body(q_ref, k_ref, v_ref, kb_ref, do_ref, dq_ref, dk_ref, dv_ref, dkt_acc, dvt_acc, w_scr, b_scr, c_scr, u_scr, dq_scr):
        i = pl.program_id(1)

        @pl.when(i == 0)
        def _():
            dkt_acc[...] = jnp.zeros_like(dkt_acc)
            dvt_acc[...] = jnp.zeros_like(dvt_acc)

        c_scr[...] = jnp.zeros_like(c_scr)
        u_scr[...] = jnp.zeros_like(u_scr)
        dq_scr[...] = jnp.zeros_like(dq_scr)
        tri_gt = _tri(tk, lambda r, c: r > c)
        tri_lt = _tri(tk, lambda r, c: r < c)
        row_io = lax.broadcasted_iota(jnp.int32, (rc, tk), 0)
        col_io = lax.broadcasted_iota(jnp.int32, (rc, tk), 1)
        qt = q_ref[...].astype(F32).T.astype(BF16)
        dot_t = do_ref[...].astype(F32).T.astype(BF16)
        zero_blk = jnp.zeros((rc, tk), BF16)

        def full_rows(plan, parts):
            it = iter(parts)
            return jnp.concatenate([zero_blk if what is None else next(it) for what in plan], axis=0)

        def key_rows(j):
            return pl.ds(pl.multiple_of(j * tk, tk), tk)

        def scores(j, rows):
            k = k_ref[key_rows(j), :]
            v = v_ref[key_rows(j), :]
            zs = [lax.dot_general(q_ref[rs, :], k, _NT, preferred_element_type=F32) for rs in rows]
            das = [lax.dot_general(do_ref[rs, :], v, _NT, preferred_element_type=F32) for rs in rows]
            return zs, das

        def weights(j, rows, masks, zs, das):
            kbj = kb_ref[j]
            gs, splits, firsts = [], [], []
            for rs, mask, zraw in zip(rows, masks, zs):
                z, e, g, lk = _sb_logits(zraw, kbj, mask)
                b_scr[j, rs, :] = (jnp.where(z >= 0.0, 1.0, e) / (1.0 + e)).astype(BF16)
                gs.append(g)
                firsts.append(lk[:, 0:1])
                splits.append(_split2(lk))
            sums = [_dot_parts(p, tri_gt) for p in splits]
            avs = []
            for rs, g, sm, da, first in zip(rows, gs, sums, das, firsts):
                a = jnp.exp(g + (sm + c_scr[rs, :]))
                w_scr[j, rs, :] = (a * da).astype(BF16)
                avs.append(a.astype(BF16))
                c_scr[rs, :] += jnp.broadcast_to(sm[:, 0:1] + first, (rc, tk))
            return avs

        def dv_update(j, plan, avs):
            dvt_acc[j] += jnp.dot(dot_t, full_rows(plan, avs), preferred_element_type=F32)

        for jj in reversed(range(sub)):
            plan = _sb_diag_chunks(jj, nrc, rc, tk)
            live = [r for r, what in enumerate(plan) if what is not None]
            rows = [pl.ds(r * rc, rc) for r in live]
            masks = [None if plan[r] == "all" else (col_io + plan[r]) < row_io for r in live]
            j = i * sub + jj
            dv_update(j, plan, weights(j, rows, masks, *scores(j, rows)))

        n = i * sub
        everything = ["all"] * nrc
        rows = [pl.ds(r * rc, rc) for r in range(nrc)]
        nomask = [None] * nrc
        def left1(m, carry):
            j0, j1 = n - 1 - 2 * m, n - 2 - 2 * m
            sc0, sc1 = scores(j0, rows), scores(j1, rows)
            av0 = weights(j0, rows, nomask, *sc0)
            av1 = weights(j1, rows, nomask, *sc1)
            dv_update(j0, everything, av0)
            dv_update(j1, everything, av1)
            return carry

        lax.fori_loop(0, n // 2, left1, 0)

        @pl.when(n % 2 == 1)
        def _():
            dv_update(0, everything, weights(0, rows, nomask, *scores(0, rows)))

        def prefix(j, rows):
            return [jnp.dot(w_scr[j, rs, :], tri_lt, preferred_element_type=F32) for rs in rows]

        def dlogits(j, rows, sums):
            dzs = []
            for rs, sm in zip(rows, sums):
                w = w_scr[j, rs, :].astype(F32)
                beta = b_scr[j, rs, :].astype(F32)
                dzs.append((w - beta * ((w + sm) + u_scr[rs, :])).astype(BF16))
                u_scr[rs, :] += jnp.broadcast_to(sm[:, tk - 1:tk] + w[:, tk - 1:tk], (rc, tk))
            return dzs

        def dqk_update(j, plan, rows, dzs):
            k = k_ref[key_rows(j), :]
            dqs = [jnp.dot(dz, k, preferred_element_type=F32) for dz in dzs]
            for rs, dq in zip(rows, dqs):
                dq_scr[rs, :] += dq
            dkt_acc[j] += jnp.dot(qt, full_rows(plan, dzs), preferred_element_type=F32)

        def left2(m, carry):
            j0, j1 = 2 * m, 2 * m + 1
            s0, s1 = prefix(j0, rows), prefix(j1, rows)
            dz0 = dlogits(j0, rows, s0)
            dz1 = dlogits(j1, rows, s1)
            dqk_update(j0, everything, rows, dz0)
            dqk_update(j1, everything, rows, dz1)
            return carry

        lax.fori_loop(0, n // 2, left2, 0)

        @pl.when(n % 2 == 1)
        def _():
            dqk_update(n - 1, everything, rows, dlogits(n - 1, rows, prefix(n - 1, rows)))
        for jj in range(sub):
            plan = _sb_diag_chunks(jj, nrc, rc, tk)
            live_rows = [pl.ds(r * rc, rc) for r, what in enumerate(plan) if what is not None]
            j = i * sub + jj
            dqk_update(j, plan, live_rows, dlogits(j, live_rows, prefix(j, live_rows)))
        dq_ref[...] = dq_scr[...].astype(dq_ref.dtype)

        @pl.when(i == nq - 1)
        def _():
            def flush(j, carry):
                dk_ref[key_rows(j), :] = dkt_acc[j].T.astype(dk_ref.dtype)
                dv_ref[key_rows(j), :] = dvt_acc[j].T.astype(dv_ref.dtype)
                return carry

            lax.fori_loop(0, nk, flush, 0)

    q_spec, k_spec, v_spec = _head_specs(lp, tq, (0, nh, 2 * nh))
    kb_spec = pl.BlockSpec((None, nk, 1, tk), lambda h, i: (h, 0, 0, 0))
    row_spec = pl.BlockSpec((tq, LANES), lambda h, i: (i, h))
    col_spec = pl.BlockSpec((lp, LANES), lambda h, i: (0, h))
    wide = jax.ShapeDtypeStruct((lp, nh * LANES), BF16)
    return pl.pallas_call(
        body,
        out_shape=(wide, wide, wide),
        grid=(nh, nq),
        in_specs=[q_spec, k_spec, v_spec, kb_spec, row_spec],
        out_specs=(row_spec, col_spec, col_spec),
        scratch_shapes=[
            pltpu.VMEM((nk, LANES, tk), F32),
            pltpu.VMEM((nk, LANES, tk), F32),
            pltpu.VMEM((nk, tq, tk), BF16),
            pltpu.VMEM((nk, tq, tk), BF16),
            pltpu.VMEM((tq, tk), F32),
            pltpu.VMEM((tq, tk), F32),
            pltpu.VMEM((tq, LANES), F32),
        ],
        name=name,
        compiler_params=_cp(("parallel", "arbitrary")),
    )(qa, qa, qa, kb, do)


def _pool_counts(pos, win):
    return jnp.clip(pos + 1, 1, win).astype(F32)


def _pool_fwd(a, name):
    lp, C = a.shape
    tm = _row_tile(lp, 640)
    hb = tm // POOL_HALO

    def body(prev_ref, cur_ref, o_ref, xs):
        i = pl.program_id(0)
        xs[pl.ds(0, POOL_HALO), :] = jnp.where(i > 0, prev_ref[...], 0.0)
        xs[pl.ds(POOL_HALO, tm), :] = cur_ref[...]
        pos = i * tm + lax.broadcasted_iota(jnp.int32, (tm, 1), 0) - PAD0
        for g, win in enumerate(POOL_WINDOWS):
            cols = pl.ds(g * POOL_GROUP, POOL_GROUP)
            s = xs[pl.ds(POOL_HALO, tm), cols]
            for k in range(1, win):
                s = s + xs[pl.ds(POOL_HALO - k, tm), cols]
            o_ref[:, cols] = (s / _pool_counts(pos, win) - xs[pl.ds(POOL_HALO, tm), cols]).astype(o_ref.dtype)

    return pl.pallas_call(
        body,
        out_shape=jax.ShapeDtypeStruct((lp, C), BF16),
        grid=(lp // tm,),
        in_specs=[
            pl.BlockSpec((POOL_HALO, C), lambda i: (jnp.maximum(i * hb - 1, 0), 0)),
            pl.BlockSpec((tm, C), lambda i: (i, 0)),
        ],
        out_specs=pl.BlockSpec((tm, C), lambda i: (i, 0)),
        scratch_shapes=[pltpu.VMEM((tm + POOL_HALO, C), F32)],
        name=name,
        compiler_params=_cp(("parallel",)),
    )(a, a)


def _pool_bwd(dp, name):
    lp, C = dp.shape
    tm = _row_tile(lp, 640)
    hb = tm // POOL_HALO
    nt = lp // tm
    last_halo = lp // POOL_HALO - 1

    def body(cur_ref, next_ref, o_ref, xs):
        i = pl.program_id(0)
        pos = i * tm + lax.broadcasted_iota(jnp.int32, (tm, 1), 0) - PAD0
        pos_h = (i + 1) * tm + lax.broadcasted_iota(jnp.int32, (POOL_HALO, 1), 0) - PAD0
        for g, win in enumerate(POOL_WINDOWS):
            cols = pl.ds(g * POOL_GROUP, POOL_GROUP)
            cur = cur_ref[:, cols]
            xs[pl.ds(0, tm), cols] = cur / _pool_counts(pos, win)
            xs[pl.ds(tm, POOL_HALO), cols] = jnp.where(i < nt - 1, next_ref[:, cols], 0.0) / _pool_counts(pos_h, win)
            s = xs[pl.ds(0, tm), cols]
            for k in range(1, win):
                s = s + xs[pl.ds(k, tm), cols]
            o_ref[:, cols] = jnp.where(pos >= 0, s - cur, 0.0)

    return pl.pallas_call(
        body,
        out_shape=jax.ShapeDtypeStruct((lp, C), F32),
        grid=(nt,),
        in_specs=[
            pl.BlockSpec((tm, C), lambda i: (i, 0)),
            pl.BlockSpec((POOL_HALO, C), lambda i: (jnp.minimum((i + 1) * hb, last_halo), 0)),
        ],
        out_specs=pl.BlockSpec((tm, C), lambda i: (i, 0)),
        scratch_shapes=[pltpu.VMEM((tm + POOL_HALO, C), F32)],
        name=name,
        compiler_params=_cp(("parallel",)),
    )(dp, dp)


def _scale_add(h, pre, scale, name):
    M, C = h.shape
    tm = _row_tile(M, 640)

    def body(h_ref, p_ref, s_ref, o_ref):
        o_ref[...] = h_ref[...] + p_ref[...] * s_ref[...]

    row = pl.BlockSpec((tm, C), lambda i: (i, 0))
    return pl.pallas_call(
        body,
        out_shape=jax.ShapeDtypeStruct((M, C), F32),
        grid=(M // tm,),
        in_specs=[row, row, pl.BlockSpec((1, C), lambda i: (0, 0))],
        out_specs=row,
        name=name,
        compiler_params=_cp(("parallel",)),
    )(h, pre, scale.reshape(1, C))


def _scale_bwd(dh, pre, scale, name):
    M, C = dh.shape
    tm = _row_tile(M, 640)

    def body(dh_ref, p_ref, s_ref, dp_ref, ds_ref):
        @pl.when(pl.program_id(0) == 0)
        def _():
            ds_ref[...] = jnp.zeros_like(ds_ref)

        d = dh_ref[...]
        ds_ref[...] += jnp.sum(d * p_ref[...], axis=0, keepdims=True)
        dp_ref[...] = (d * s_ref[...]).astype(dp_ref.dtype)

    row = pl.BlockSpec((tm, C), lambda i: (i, 0))
    vec = pl.BlockSpec((1, C), lambda i: (0, 0))
    return pl.pallas_call(
        body,
        out_shape=(jax.ShapeDtypeStruct((M, C), BF16), jax.ShapeDtypeStruct((1, C), F32)),
        grid=(M // tm,),
        in_specs=[row, row, vec],
        out_specs=(row, vec),
        name=name,
        compiler_params=_cp(("arbitrary",)),
    )(dh, pre, scale.reshape(1, C))


def _gate_parts(z):
    e = jnp.exp(-jnp.abs(z))
    return e, jnp.minimum(z, 0.0) - jnp.log(1.0 + e)


def _tri_dot3(tri, x):
    hi, mid, lo = _split3(x)
    d = functools.partial(jnp.dot, preferred_element_type=F32)
    return d(tri, hi) + d(tri, mid) + d(tri, lo)


def _gate_fwd(x, b, name):
    lp, C = x.shape
    tm = _row_tile(lp, 640)

    def body(x_ref, b_ref, o_ref, carry):
        i = pl.program_id(0)

        @pl.when(i == 0)
        def _():
            carry[...] = jnp.zeros_like(carry)

        _, ls = _gate_parts(x_ref[...] + b_ref[...])
        rows = i * tm + lax.broadcasted_iota(jnp.int32, (tm, 1), 0)
        ls = jnp.where(rows >= PAD0, ls, 0.0)
        tri = (lax.broadcasted_iota(jnp.int32, (tm, tm), 0) >= lax.broadcasted_iota(jnp.int32, (tm, tm), 1)).astype(BF16)
        f = _tri_dot3(tri, ls) + carry[...]
        o_ref[...] = f
        carry[...] = f[tm - 1:tm, :]

    return pl.pallas_call(
        body,
        out_shape=jax.ShapeDtypeStruct((lp, C), F32),
        grid=(lp // tm,),
        in_specs=[pl.BlockSpec((tm, C), lambda i: (i, 0)), pl.BlockSpec((1, C), lambda i: (0, 0))],
        out_specs=pl.BlockSpec((tm, C), lambda i: (i, 0)),
        scratch_shapes=[pltpu.VMEM((1, C), F32)],
        name=name,
        compiler_params=_cp(("arbitrary",)),
    )(x, b)


def _gate_bwd(x, b, df, name):
    lp, C = x.shape
    tm = _row_tile(lp, 640)
    nt = lp // tm

    def body(x_ref, b_ref, df_ref, dx_ref, db_ref, carry):
        i = pl.program_id(0)

        @pl.when(i == 0)
        def _():
            carry[...] = jnp.zeros_like(carry)
            db_ref[...] = jnp.zeros_like(db_ref)

        z = x_ref[...] + b_ref[...]
        e, _ = _gate_parts(z)
        tri = (lax.broadcasted_iota(jnp.int32, (tm, tm), 0) <= lax.broadcasted_iota(jnp.int32, (tm, tm), 1)).astype(BF16)
        r = _tri_dot3(tri, df_ref[...]) + carry[...]
        carry[...] = r[0:1, :]
        rows = (nt - 1 - i) * tm + lax.broadcasted_iota(jnp.int32, (tm, 1), 0)
        dx = jnp.where(rows >= PAD0, r * (jnp.where(z >= 0.0, e, 1.0) / (1.0 + e)), 0.0)
        dx_ref[...] = dx
        db_ref[...] += jnp.sum(dx, axis=0, keepdims=True)

    rev = pl.BlockSpec((tm, C), lambda i: (nt - 1 - i, 0))
    vec = pl.BlockSpec((1, C), lambda i: (0, 0))
    return pl.pallas_call(
        body,
        out_shape=(jax.ShapeDtypeStruct((lp, C), F32), jax.ShapeDtypeStruct((1, C), F32)),
        grid=(nt,),
        in_specs=[rev, vec, rev],
        out_specs=(rev, vec),
        scratch_shapes=[pltpu.VMEM((1, C), F32)],
        name=name,
        compiler_params=_cp(("arbitrary",)),
    )(x, b, df)


MLA_SCALE = (MLA_NOPE + MLA_ROPE) ** -0.5


def _rope_apply(x, c, a, b):
    return x * c + pltpu.roll(x, LANES - 16, 1) * a + pltpu.roll(x, 16, 1) * b


def _rope_transpose(dy, c, a, b):
    return dy * c + pltpu.roll(dy * a, 16, 1) + pltpu.roll(dy * b, LANES - 16, 1)


def _mla_prep_fwd(q, kmat, kr, c, a, b, name):
    lp, W = q.shape
    nh = W // LANES
    tm = _row_tile(lp, 640)

    def body(q_ref, k_ref, kr_ref, c_ref, a_ref, b_ref, qo_ref, ko_ref):
        cv, av, bv = c_ref[...], a_ref[...], b_ref[...]
        qo_ref[...] = (_rope_apply(q_ref[...], cv, av, bv) * MLA_SCALE).astype(qo_ref.dtype)
        ko_ref[...] = (k_ref[...] + _rope_apply(kr_ref[...], cv, av, bv)).astype(ko_ref.dtype)

    head = pl.BlockSpec((tm, LANES), lambda i, h: (i, h))
    tab = pl.BlockSpec((tm, LANES), lambda i, h: (i, 0))
    wide = jax.ShapeDtypeStruct((lp, W), BF16)
    return pl.pallas_call(
        body,
        out_shape=(wide, wide),
        grid=(lp // tm, nh),
        in_specs=[head, head, tab, tab, tab, tab],
        out_specs=(head, head),
        name=name,
        compiler_params=_cp(("parallel", "parallel")),
    )(q, kmat, kr, c, a, b)


def _mla_prep_bwd(dq, dk, c, a, b, name):
    lp, W = dq.shape
    nh = W // LANES
    tm = _row_tile(lp, 640)

    def body(dq_ref, dk_ref, c_ref, a_ref, b_ref, dqo_ref, dkr_ref):
        cv, av, bv = c_ref[...], a_ref[...], b_ref[...]
        ksum = jnp.zeros((tm, LANES), F32)
        for h in range(nh):
            cols = pl.ds(h * LANES, LANES)
            dqo_ref[:, cols] = _rope_transpose(dq_ref[:, cols].astype(F32) * MLA_SCALE, cv, av, bv).astype(dqo_ref.dtype)
            ksum = ksum + dk_ref[:, cols].astype(F32)
        dkr_ref[...] = _rope_transpose(ksum, cv, av, bv)

    wide = pl.BlockSpec((tm, W), lambda i: (i, 0))
    tab = pl.BlockSpec((tm, LANES), lambda i: (i, 0))
    return pl.pallas_call(
        body,
        out_shape=(jax.ShapeDtypeStruct((lp, W), BF16), jax.ShapeDtypeStruct((lp, LANES), F32)),
        grid=(lp // tm,),
        in_specs=[wide, wide, tab, tab, tab],
        out_specs=(wide, tab),
        name=name,
        compiler_params=_cp(("parallel",)),
    )(dq, dk, c, a, b)


def _loss_head(h, g, target, name):
    lp, C = h.shape
    tm = _row_tile(lp, 640)
    nt = lp // tm

    def body(h_ref, g_ref, t_ref, loss_ref, dh_ref, dg_ref, sq):
        i = pl.program_id(0)

        @pl.when(i == 0)
        def _():
            dg_ref[...] = jnp.zeros_like(dg_ref)
            sq[...] = jnp.zeros_like(sq)

        xf = h_ref[...]
        gv = g_ref[...]
        r = lax.rsqrt(jnp.mean(xf * xf, axis=-1, keepdims=True) + EPS)
        xhat = xf * r
        rows = i * tm + lax.broadcasted_iota(jnp.int32, (tm, 1), 0)
        err = jnp.where(rows >= PAD0 + N_META, xhat * gv - t_ref[...], 0.0)
        sq[...] += jnp.sum(err * err, axis=0, keepdims=True)
        dy = err * (1.0 / C)
        dg_ref[...] += jnp.sum(dy * xhat, axis=0, keepdims=True)
        dxh = dy * gv
        dh_ref[...] = r * (dxh - xhat * jnp.mean(dxh * xhat, axis=-1, keepdims=True))

        @pl.when(i == nt - 1)
        def _():
            loss_ref[...] = jnp.broadcast_to(jnp.sum(sq[...], axis=1, keepdims=True) * (0.5 / C), (1, LANES))

    row = pl.BlockSpec((tm, C), lambda i: (i, 0))
    vec = pl.BlockSpec((1, C), lambda i: (0, 0))
    return pl.pallas_call(
        body,
        out_shape=(jax.ShapeDtypeStruct((1, LANES), F32), jax.ShapeDtypeStruct((lp, C), F32), jax.ShapeDtypeStruct((1, C), F32)),
        grid=(nt,),
        in_specs=[row, vec, row],
        out_specs=(pl.BlockSpec((1, LANES), lambda i: (0, 0)), row, vec),
        scratch_shapes=[pltpu.VMEM((1, C), F32)],
        name=name,
        compiler_params=_cp(("arbitrary",)),
    )(h, g.reshape(1, C), target)


def _adamw(w, g, m, v, name):
    shape = w.shape
    C = shape[-1]
    R = w.size // C
    tr = R
    if R % 8 == 0:
        for cand in range(8, R + 1, 8):
            if R % cand == 0 and cand * C * 4 <= (1 << 20):
                tr = cand
    c1 = 1.0 - ADAM_B1 ** ADAM_STEP
    c2 = 1.0 - ADAM_B2 ** ADAM_STEP

    def body(w_ref, g_ref, m_ref, v_ref, d_ref, nm_ref, nv_ref):
        gv = g_ref[...]
        nm = ADAM_B1 * m_ref[...] + (1.0 - ADAM_B1) * gv
        nv = ADAM_B2 * v_ref[...] + (1.0 - ADAM_B2) * (gv * gv)
        nm_ref[...] = nm
        nv_ref[...] = nv
        d_ref[...] = -ADAM_LR * ((nm / c1) / (jnp.sqrt(nv / c2) + ADAM_EPS) + ADAM_WD * w_ref[...])

    blk = pl.BlockSpec((tr, C), lambda i: (i, 0))
    out = jax.ShapeDtypeStruct((R, C), F32)
    outs = pl.pallas_call(
        body,
        out_shape=(out, out, out),
        grid=(R // tr,),
        in_specs=[blk] * 4,
        out_specs=(blk, blk, blk),
        name=name,
        compiler_params=_cp(("parallel",)),
    )(*(t.reshape(R, C) for t in (w, g, m, v)))
    return tuple(t.reshape(shape) for t in outs)


def _exchange(send, axes, same, name):
    na = len(axes)
    n = 1 << na
    _, R, C = send.shape
    parts = max(p for p in (8, 4, 2, 1) if R % (16 * p) == 0 or p == 1)
    pr = R // parts

    def body(send_ref, recv_ref, send_sems, recv_sems, local_sem):
        coords = {ax: lax.axis_index(ax) for ax in MESH_AXES}
        me = 0
        for ax in axes:
            me = me * 2 + coords[ax]

        def member(r):
            dev = dict(coords)
            for b, ax in enumerate(axes):
                if (r >> (na - 1 - b)) & 1:
                    dev[ax] = 1 - dev[ax]
            return tuple(dev[ax] for ax in MESH_AXES)

        def chunk(j, p):
            return (send_ref.at[0] if same else send_ref.at[j]).at[pl.ds(p * pr, pr)]

        def slot(j, p):
            return recv_ref.at[j].at[pl.ds(p * pr, pr)]

        own = pltpu.make_async_copy(send_ref.at[0] if same else send_ref.at[me], recv_ref.at[me], local_sem)
        own.start()
        copies = []
        for r in range(1, n):
            peer = me ^ r
            for p in range(parts):
                cp = pltpu.make_async_remote_copy(
                    src_ref=chunk(peer, p), dst_ref=slot(me, p), send_sem=send_sems.at[r, p], recv_sem=recv_sems.at[r, p],
                    device_id=member(r), device_id_type=pl.DeviceIdType.MESH)
                cp.start()
                copies.append(cp)
        for r in range(1, n):
            for p in range(parts):
                arrival = pltpu.make_async_remote_copy(
                    src_ref=chunk(me, p), dst_ref=slot(me ^ r, p), send_sem=send_sems.at[r, p], recv_sem=recv_sems.at[r, p],
                    device_id=member(r), device_id_type=pl.DeviceIdType.MESH)
                arrival.wait_recv()
        for cp in copies:
            cp.wait_send()
        own.wait()

    any_spec = pl.BlockSpec(memory_space=pl.ANY)
    return pl.pallas_call(
        body,
        out_shape=jax.ShapeDtypeStruct((n, R, C), send.dtype),
        in_specs=[any_spec],
        out_specs=any_spec,
        scratch_shapes=[pltpu.SemaphoreType.DMA((n, parts)), pltpu.SemaphoreType.DMA((n, parts)), pltpu.SemaphoreType.DMA],
        name=name,
        compiler_params=pltpu.CompilerParams(has_side_effects=True),
    )(send)


def _sum_chunks(x, name, out_dtype=F32):
    n, R, C = x.shape
    tr = _row_tile(R, 512)

    def body(x_ref, o_ref):
        acc = x_ref[0].astype(F32)
        for j in range(1, n):
            acc = acc + x_ref[j].astype(F32)
        o_ref[...] = acc.astype(o_ref.dtype)

    return pl.pallas_call(
        body,
        out_shape=jax.ShapeDtypeStruct((R, C), out_dtype),
        grid=(R // tr,),
        in_specs=[pl.BlockSpec((n, tr, C), lambda i: (0, i, 0))],
        out_specs=pl.BlockSpec((tr, C), lambda i: (i, 0)),
        name=name,
        compiler_params=_cp(("parallel",)),
    )(x)


def _pad_heads_cols(w, groups, d):
    k = w.shape[0]
    w = w.reshape(k, groups * N_HEADS, d)
    return jnp.pad(w, ((0, 0), (0, 0), (0, LANES - d))).reshape(k, groups * N_HEADS * LANES)


def _unpad_heads_cols(w, groups, d):
    k = w.shape[0]
    return w.reshape(k, groups * N_HEADS, LANES)[:, :, :d].reshape(k, groups * N_HEADS * d)


def _pad_heads_rows(w, d):
    n = w.shape[1]
    return jnp.pad(w.reshape(N_HEADS, d, n), ((0, 0), (0, LANES - d), (0, 0))).reshape(N_HEADS * LANES, n)


def _unpad_heads_rows(w, d):
    n = w.shape[1]
    return w.reshape(N_HEADS, LANES, n)[:, :d].reshape(N_HEADS * d, n)


Q_SCALE = HEAD_DIM ** -0.5


def _scale_q_cols(w):
    nq = N_HEADS * LANES
    return jnp.concatenate([w[:, :nq] * Q_SCALE, w[:, nq:]], axis=1)


def _kernel_weights(W):
    P = dict(W)
    pw = W["pool_w"][0]
    bd = jnp.zeros((D_MODEL, D_MODEL), pw.dtype)
    for g in range(len(POOL_WINDOWS)):
        bd = lax.dynamic_update_slice(bd, pw[g], (g * POOL_GROUP, g * POOL_GROUP))
    P["pool_bd"] = bd
    P["sb_qkv"] = _scale_q_cols(_pad_heads_cols(W["sb_w_qkv"][0], 3, HEAD_DIM))
    P["sb_o"] = _pad_heads_rows(W["sb_w_o"][0], HEAD_DIM)
    nq = 3 * N_HEADS * HEAD_DIM
    P["fox_qkv"] = _scale_q_cols(_pad_heads_cols(W["fox_w_qkvf"][0][:, :nq], 3, HEAD_DIM))
    P["fox_f"] = jnp.pad(W["fox_w_qkvf"][0][:, nq:], ((0, 0), (0, LANES - N_HEADS)))
    P["fox_o"] = _pad_heads_rows(W["fox_w_o"][0], HEAD_DIM)
    P["fox_b"] = jnp.pad(W["fox_b_f"], ((0, 0), (0, LANES - N_HEADS)))
    P["mla_down"] = jnp.pad(W["mla_w_down"][0], ((0, 0), (0, MLA_DOWN_PAD - W["mla_w_down"].shape[2])))
    P["mla_uq"] = _pad_heads_cols(W["mla_w_uq"][0], 1, MLA_NOPE + MLA_ROPE)
    ukv = W["mla_w_ukv"][0].reshape(MLA_KV_RANK, N_HEADS, 2 * HEAD_DIM)
    padk = ((0, 0), (0, 0), (0, LANES - HEAD_DIM))
    P["mla_ukv"] = jnp.concatenate(
        [jnp.pad(ukv[:, :, :MLA_NOPE], padk).reshape(MLA_KV_RANK, -1), jnp.pad(ukv[:, :, MLA_NOPE:], padk).reshape(MLA_KV_RANK, -1)], axis=1)
    P["mla_o"] = _pad_heads_rows(W["mla_w_o"][0], HEAD_DIM)
    return P


def _rope_tables(lp):
    pos = (jnp.arange(lp) - PAD0).astype(F32)
    inv = ROPE_THETA ** (-jnp.arange(0, MLA_ROPE, 2, dtype=F32) / MLA_ROPE)
    ang = pos[:, None] * inv[None, :]
    cos, sin = jnp.cos(ang), jnp.sin(ang)
    half = MLA_ROPE // 2
    z = lambda n: jnp.zeros((lp, n), F32)
    c = jnp.concatenate([jnp.ones((lp, MLA_NOPE), F32), cos, cos, z(LANES - MLA_NOPE - MLA_ROPE)], axis=1)
    a = jnp.concatenate([z(MLA_NOPE), -sin, z(LANES - MLA_NOPE - half)], axis=1)
    b = jnp.concatenate([z(MLA_NOPE + half), sin, z(LANES - MLA_NOPE - MLA_ROPE)], axis=1)
    return c, a, b


def _key_bias(lp, t, per_head=None):
    pad = jnp.arange(lp)[None, :] < PAD0
    body = jnp.zeros((N_HEADS, lp), F32) if per_head is None else per_head
    return jnp.where(pad, NEG, body).reshape(N_HEADS, lp // t, 1, t)


def _ffn_fwd(h, i, P):
    b = _rms_fwd(h, P["norm_ffn"][i], BF16, "ffn_norm")
    g, u, hd = _ffn_up(b, P["ffn_w_gate"][i], P["ffn_w_up"][i], "ffn_gate_up")
    return _mm(hd, P["ffn_w_down"][i], "nn", "ffn_down", add=h), (h, b, g, u, hd)


def _ffn_bwd(dh, i, P, saved):
    h, b, g, u, hd = saved
    dwd = _mm(hd, dh, "tn", "ffn_down_dw")
    dg, du = _ffn_down_bwd(dh, P["ffn_w_down"][i], g, u, "ffn_down_dx")
    dwg = _mm(b, dg, "tn", "ffn_gate_dw")
    dwu = _mm(b, du, "tn", "ffn_up_dw")
    db = _mm(dg, P["ffn_w_gate"][i], "nt", "ffn_gate_dx")
    db = _mm(du, P["ffn_w_up"][i], "nt", "ffn_up_dx", add=db)
    dh_in, dgain = _rms_bwd(h, P["norm_ffn"][i], db, dh, "ffn_norm_bwd")
    return dh_in, dgain, dwg, dwu, dwd


def _pool_layer_fwd(h, P):
    a = _rms_fwd(h, P["norm_mix"][0], F32, "pool_norm")
    pooled = _pool_fwd(a, "pool_window")
    pre = _mm(pooled, P["pool_bd"], "nn", "pool_mix")
    return _scale_add(h, pre, P["pool_scale"][0], "pool_scale_add"), (h, pooled, pre)


def _pool_layer_bwd(dh, P, saved):
    h, pooled, pre = saved
    dpre, dscale = _scale_bwd(dh, pre, P["pool_scale"][0], "pool_scale_bwd")
    dbd = _mm(pooled, dpre, "tn", "pool_mix_dw")
    dpooled = _mm(dpre, P["pool_bd"], "nt", "pool_mix_dx")
    da = _pool_bwd(dpooled, "pool_window_bwd")
    dh_in, dgain = _rms_bwd(h, P["norm_mix"][0], da, dh, "mix_norm_bwd")
    dw = jnp.stack([dbd[g * POOL_GROUP:(g + 1) * POOL_GROUP, g * POOL_GROUP:(g + 1) * POOL_GROUP] for g in range(len(POOL_WINDOWS))])
    return dh_in, {"norm_mix0": dgain, "pool_w": dw[None], "pool_scale": dscale}


def _out_proj_bwd(o, dh, wo, tag):
    return _mm(o, dh, "tn", tag + "_o_dw"), _mm(dh, wo, "nt", tag + "_o_dx", out_dtype=BF16)


def _sb_layer_fwd(h, P):
    lp = h.shape[0]
    a = _rms_fwd(h, P["norm_mix"][1], BF16, "mix_norm")
    qkv = _mm(a, P["sb_qkv"], "nn", "sb_qkv", out_dtype=BF16)
    kb = _key_bias(lp, SB_TK)
    o = _sb_fwd(qkv, kb, nh=N_HEADS, name="sb_attn")
    return _mm(o, P["sb_o"], "nn", "attn_out", add=h), (h, a, qkv, kb, o)


def _sb_layer_bwd(dh, P, saved):
    h, a, qkv, kb, o = saved
    dwo, do = _out_proj_bwd(o, dh, P["sb_o"], "attn")
    dq, dk, dv = _sb_bwd(qkv, kb, do, nh=N_HEADS, name="sb_attn_bwd")
    dqkv = jnp.concatenate([dq, dk, dv], axis=1)
    dw = _scale_q_cols(_mm(a, dqkv, "tn", "qkv_dw"))
    da = _mm(dqkv, P["sb_qkv"], "nt", "qkv_dx")
    dh_in, dgain = _rms_bwd(h, P["norm_mix"][1], da, dh, "mix_norm_bwd")
    return dh_in, {"norm_mix1": dgain, "sb_w_qkv": _unpad_heads_cols(dw, 3, HEAD_DIM)[None], "sb_w_o": _unpad_heads_rows(dwo, HEAD_DIM)[None]}


def _fox_layer_fwd(h, P):
    lp = h.shape[0]
    t = _attn_tile(lp)
    a = _rms_fwd(h, P["norm_mix"][3], BF16, "mix_norm")
    qkv = _mm(a, P["fox_qkv"], "nn", "sb_qkv", out_dtype=BF16)
    f = _mm(a, P["fox_f"], "nn", "fox_gate_proj")
    fc = _gate_fwd(f, P["fox_b"], "fox_gate")[:, :N_HEADS]
    kb = _key_bias(lp, t, -fc.T)
    fq = jnp.broadcast_to(fc[:, :, None], (lp, N_HEADS, LANES)).reshape(lp, N_HEADS * LANES)
    o, lse = _attn_fwd(qkv, qkv, qkv, kb, fq, nh=N_HEADS, offs=(0, N_HEADS, 2 * N_HEADS), name="fox_attn")
    return _mm(o, P["fox_o"], "nn", "attn_out", add=h), (h, a, qkv, f, kb, fq, o, lse)


def _fox_layer_bwd(dh, P, saved):
    h, a, qkv, f, kb, fq, o, lse = saved
    lp = h.shape[0]
    dwo, do = _out_proj_bwd(o, dh, P["fox_o"], "attn")
    dq, dk, dv, dkb, dqb = _attn_bwd(qkv, qkv, qkv, kb, fq, o, do, lse, nh=N_HEADS, offs=(0, N_HEADS, 2 * N_HEADS),
                                     name="fox_attn_bwd")
    dfc = jnp.pad(dqb.reshape(lp, N_HEADS, LANES)[:, :, 0] - dkb.reshape(N_HEADS, lp).T, ((0, 0), (0, LANES - N_HEADS)))
    df, dbf = _gate_bwd(f, P["fox_b"], dfc, "fox_gate_bwd")
    dqkv = jnp.concatenate([dq, dk, dv], axis=1)
    dw = _scale_q_cols(_mm(a, dqkv, "tn", "qkv_dw"))
    dwf = _mm(a, df, "tn", "fox_gate_dw")
    da = _mm(dqkv, P["fox_qkv"], "nt", "qkv_dx")
    da = _mm(df, P["fox_f"], "nt", "fox_gate_dx", add=da)
    dh_in, dgain = _rms_bwd(h, P["norm_mix"][3], da, dh, "mix_norm_bwd")
    dwqkvf = jnp.concatenate([_unpad_heads_cols(dw, 3, HEAD_DIM), dwf[:, :N_HEADS]], axis=1)
    return dh_in, {"norm_mix3": dgain, "fox_w_qkvf": dwqkvf[None], "fox_b_f": dbf[:, :N_HEADS], "fox_w_o": _unpad_heads_rows(dwo, HEAD_DIM)[None]}


def _mla_layer_fwd(h, P):
    lp = h.shape[0]
    a = _rms_fwd(h, P["norm_mix"][2], BF16, "mix_norm")
    down = _mm(a, P["mla_down"], "nn", "mla_down")
    cq_pre = down[:, :MLA_Q_RANK]
    ckv_pre = down[:, MLA_Q_RANK:MLA_Q_RANK + MLA_KV_RANK]
    kr = jnp.pad(down[:, MLA_Q_RANK + MLA_KV_RANK:MLA_Q_RANK + MLA_KV_RANK + MLA_ROPE], ((0, 0), (MLA_NOPE, LANES - MLA_NOPE - MLA_ROPE)))
    cq = _rms_fwd(cq_pre, P["mla_q_norm"][0], BF16, "mla_q_norm")
    ckv = _rms_fwd(ckv_pre, P["mla_kv_norm"][0], BF16, "mla_kv_norm")
    q = _mm(cq, P["mla_uq"], "nn", "mla_uq")
    kv = _mm(ckv, P["mla_ukv"], "nn", "mla_ukv", out_dtype=BF16)
    tabs = _rope_tables(lp)
    qr, kc = _mla_prep_fwd(q, kv, kr, *tabs, "mla_rope")
    kb = _key_bias(lp, _attn_tile(lp))
    o, lse = _attn_fwd(qr, kc, kv, kb, None, nh=N_HEADS, offs=(0, 0, N_HEADS), name="mla_attn")
    return _mm(o, P["mla_o"], "nn", "attn_out", add=h), (h, a, cq_pre, ckv_pre, cq, ckv, qr, kc, kv, tabs, kb, o, lse)


def _mla_layer_bwd(dh, P, saved):
    h, a, cq_pre, ckv_pre, cq, ckv, qr, kc, kv, tabs, kb, o, lse = saved
    lp = h.shape[0]
    dwo, do = _out_proj_bwd(o, dh, P["mla_o"], "attn")
    dqr, dkc, dv = _attn_bwd(qr, kc, kv, kb, None, o, do, lse, nh=N_HEADS, offs=(0, 0, N_HEADS),
                             name="mla_attn_bwd")
    dq, dkr = _mla_prep_bwd(dqr, dkc, *tabs, "mla_rope_bwd")
    dkv = jnp.concatenate([dkc, dv], axis=1)
    dwuq = _mm(cq, dq, "tn", "mla_uq_dw")
    dcq = _mm(dq, P["mla_uq"], "nt", "mla_uq_dx")
    dwukv = _mm(ckv, dkv, "tn", "mla_ukv_dw")
    dckv = _mm(dkv, P["mla_ukv"], "nt", "mla_ukv_dx")
    dcq_pre, dqn = _rms_bwd(cq_pre, P["mla_q_norm"][0], dcq, None, "mla_q_norm_bwd")
    dckv_pre, dkvn = _rms_bwd(ckv_pre, P["mla_kv_norm"][0], dckv, None, "mla_kv_norm_bwd")
    used = MLA_Q_RANK + MLA_KV_RANK + MLA_ROPE
    ddown = jnp.concatenate([dcq_pre, dckv_pre, dkr[:, MLA_NOPE:MLA_NOPE + MLA_ROPE], jnp.zeros((lp, MLA_DOWN_PAD - used), F32)], axis=1)
    dwdown = _mm(a, ddown, "tn", "mla_down_dw")
    da = _mm(ddown, P["mla_down"], "nt", "mla_down_dx")
    dh_in, dgain = _rms_bwd(h, P["norm_mix"][2], da, dh, "mix_norm_bwd")
    dukv = dwukv.reshape(MLA_KV_RANK, 2, N_HEADS, LANES)[:, :, :, :HEAD_DIM]
    dukv = jnp.concatenate([dukv[:, 0], dukv[:, 1]], axis=-1).reshape(MLA_KV_RANK, N_HEADS * 2 * HEAD_DIM)
    return dh_in, {
        "norm_mix2": dgain, "mla_w_down": dwdown[:, :used][None], "mla_q_norm": dqn, "mla_kv_norm": dkvn,
        "mla_w_uq": _unpad_heads_cols(dwuq, 1, MLA_NOPE + MLA_ROPE)[None], "mla_w_ukv": dukv[None],
        "mla_w_o": _unpad_heads_rows(dwo, HEAD_DIM)[None]}


_MIXERS = ((_pool_layer_fwd, _pool_layer_bwd), (_sb_layer_fwd, _sb_layer_bwd), (_mla_layer_fwd, _mla_layer_bwd), (_fox_layer_fwd, _fox_layer_bwd))


def _step_local(x, target, W):
    seq = x.shape[0]
    P = _kernel_weights(W)
    h = jnp.concatenate([jnp.zeros((PAD0, D_MODEL), F32), W["meta"], x], axis=0)
    tpad = jnp.pad(target, ((PAD0 + N_META, 0), (0, 0)))
    saved = []
    for i in range(4):
        h, s_mix = _MIXERS[i][0](h, P)
        h, s_ffn = _ffn_fwd(h, i, P)
        saved.append((s_mix, s_ffn))
    loss, dh, dfinal = _loss_head(h, W["final_norm"], tpad, "loss_head")
    grads = {"final_norm": dfinal.reshape(-1)}
    gains_mix, gains_ffn, dwg, dwu, dwd = [None] * 4, [None] * 4, [None] * 4, [None] * 4, [None] * 4
    for i in reversed(range(4)):
        s_mix, s_ffn = saved[i]
        dh, gains_ffn[i], dwg[i], dwu[i], dwd[i] = _ffn_bwd(dh, i, P, s_ffn)
        dh, g = _MIXERS[i][1](dh, P, s_mix)
        gains_mix[i] = g.pop("norm_mix%d" % i)
        grads.update(g)
    grads["norm_mix"] = jnp.concatenate(gains_mix, axis=0)
    grads["norm_ffn"] = jnp.concatenate(gains_ffn, axis=0)
    grads["ffn_w_gate"] = jnp.stack(dwg)
    grads["ffn_w_up"] = jnp.stack(dwu)
    grads["ffn_w_down"] = jnp.stack(dwd)
    grads["meta"] = dh[PAD0:PAD0 + N_META]
    return loss, dh[PAD0 + N_META:], grads


_WEIGHTS = ("meta", "norm_mix", "norm_ffn", "pool_w", "pool_scale", "sb_w_qkv", "sb_w_o", "mla_w_down", "mla_q_norm",
            "mla_kv_norm", "mla_w_uq", "mla_w_ukv", "mla_w_o", "fox_w_qkvf", "fox_b_f", "fox_w_o", "ffn_w_gate",
            "ffn_w_up", "ffn_w_down", "final_norm")
_SHARD_AXIS = {"meta": 1, "pool_w": 2, "sb_w_qkv": 2, "sb_w_o": 1, "mla_w_down": 1, "mla_q_norm": 1, "mla_kv_norm": 1,
               "mla_w_uq": 2, "mla_w_ukv": 2, "mla_w_o": 1, "fox_w_qkvf": 2, "fox_b_f": None, "fox_w_o": 1,
               "ffn_w_gate": 2, "ffn_w_up": 2, "ffn_w_down": 1}
_SHARDED = tuple(n for n in _WEIGHTS if _SHARD_AXIS.get(n) is not None)
_REPLICATED = tuple(n for n in _WEIGHTS if _SHARD_AXIS.get(n) is None)
_EXACT = ("meta", "mla_q_norm", "mla_kv_norm")
N_CHIPS = 4
GRAD_ROW_TILE = 512


PACK_ROWS = 16


def _piece_rows(t):
    return -(-t.size // (LANES * PACK_ROWS)) * PACK_ROWS


def _flat_rows(parts, dtype, row_multiple):
    pieces = []
    for p in parts:
        flat = p.astype(dtype).reshape(-1)
        pieces.append(jnp.pad(flat, (0, _piece_rows(p) * LANES - flat.shape[0])).reshape(-1, LANES))
    rows = sum(q.shape[0] for q in pieces)
    pad = -(-rows // row_multiple) * row_multiple - rows
    if pad:
        pieces.append(jnp.zeros((pad, LANES), dtype))
    return jnp.concatenate(pieces, axis=0)


def _split_flat(flat, like):
    out, off = [], 0
    for t in like:
        out.append(flat[off:off + _piece_rows(t)].reshape(-1)[:t.size].reshape(t.shape))
        off += _piece_rows(t)
    return out


def _gather_shards(local, names, dtype, name):
    blocks = [local[n] for n in names]
    recv = _exchange(_flat_rows(blocks, dtype, PACK_ROWS)[None], ("x", "y"), True, name)
    per_chip = [_split_flat(recv[s], blocks) for s in range(N_CHIPS)]
    return {n: jnp.concatenate([per_chip[s][k] for s in range(N_CHIPS)], axis=_SHARD_AXIS[n]) for k, n in enumerate(names)}


def _shard_of(g, n, s):
    w = g.shape[_SHARD_AXIS[n]] // N_CHIPS
    return lax.slice_in_dim(g, s * w, (s + 1) * w, axis=_SHARD_AXIS[n])


def _train_step(a):
    local = {n: a[n] for n in _WEIGHTS}
    full = {n: local[n] for n in _REPLICATED}
    full.update(_gather_shards(local, [n for n in _SHARDED if n not in _EXACT], BF16, "gather_weights"))
    full.update(_gather_shards(local, list(_EXACT), F32, "gather_exact"))

    loss, grad_x, grads = _step_local(a["x"][0], a["loss_target"][0], full)

    send = jnp.stack([
        _flat_rows([_shard_of(grads[n], n, s) for n in _SHARDED], BF16, 2 * GRAD_ROW_TILE).reshape(2, -1, LANES)
        for s in range(N_CHIPS)]).reshape(2 * N_CHIPS, -1, LANES)
    mine = _sum_chunks(_exchange(send, MESH_AXES, False, "scatter_grads"), "sum_grads", out_dtype=BF16)
    both = _exchange(mine[None], ("c",), True, "pair_grads").reshape(-1, LANES).astype(F32)
    reduced = dict(zip(_SHARDED, _split_flat(both, [local[n] for n in _SHARDED])))
    small = _flat_rows([grads[n] for n in _REPLICATED], F32, 8)
    small = _sum_chunks(_exchange(small[None], MESH_AXES, True, "gather_small_grads"), "sum_small_grads")
    reduced.update(zip(_REPLICATED, _split_flat(small, [local[n] for n in _REPLICATED])))

    deltas, new_m, new_v = {}, {}, {}
    for n in _WEIGHTS:
        deltas[n], new_m[n], new_v[n] = _adamw(local[n], reduced[n], a["m_" + n], a["v_" + n], "adamw")
    total = lax.psum(loss[0, 0], MESH_AXES)
    return (total, grad_x[None], *[reduced[n] for n in _WEIGHTS], *[deltas[n] for n in _WEIGHTS],
            *[new_m[n] for n in _WEIGHTS], *[new_v[n] for n in _WEIGHTS])


def kernel(x, meta, norm_mix, norm_ffn, pool_w, pool_scale, sb_w_qkv, sb_w_o, mla_w_down, mla_q_norm, mla_kv_norm, mla_w_uq, mla_w_ukv, mla_w_o, fox_w_qkvf, fox_b_f, fox_w_o, ffn_w_gate, ffn_w_up, ffn_w_down, final_norm, loss_target, m_meta, m_norm_mix, m_norm_ffn, m_pool_w, m_pool_scale, m_sb_w_qkv, m_sb_w_o, m_mla_w_down, m_mla_q_norm, m_mla_kv_norm, m_mla_w_uq, m_mla_w_ukv, m_mla_w_o, m_fox_w_qkvf, m_fox_b_f, m_fox_w_o, m_ffn_w_gate, m_ffn_w_up, m_ffn_w_down, m_final_norm, v_meta, v_norm_mix, v_norm_ffn, v_pool_w, v_pool_scale, v_sb_w_qkv, v_sb_w_o, v_mla_w_down, v_mla_q_norm, v_mla_kv_norm, v_mla_w_uq, v_mla_w_ukv, v_mla_w_o, v_fox_w_qkvf, v_fox_b_f, v_fox_w_o, v_ffn_w_gate, v_ffn_w_up, v_ffn_w_down, v_final_norm):
    return _train_step(dict(locals()))
```

```python
import functools

import jax
import jax.numpy as jnp
from jax import lax
from jax.experimental import pallas as pl
from jax.experimental.pallas import tpu as pltpu

F32 = jnp.float32
BF16 = jnp.bfloat16

D_MODEL = 1024
N_META = 16
PAD0 = 112
LANES = 128
N_HEADS = 16
HEAD_DIM = 64
POOL_WINDOWS = (2, 4, 8, 16)
POOL_GROUP = 256
POOL_HALO = 16
MLA_Q_RANK = 384
MLA_KV_RANK = 256
MLA_NOPE = 64
MLA_ROPE = 32
MLA_DOWN_PAD = 768
ROPE_THETA = 10000.0
D_FF = 2816
EPS = 1e-6
NEG = -1e30
ADAM_LR = 0.001
ADAM_B1 = 0.9
ADAM_B2 = 0.999
ADAM_EPS = 1e-08
ADAM_WD = 0.01
ADAM_STEP = 10
VMEM_LIMIT = 56 * 1024 * 1024
MESH_AXES = ("x", "y", "c")


def _cp(sem, **kw):
    return pltpu.CompilerParams(dimension_semantics=sem, vmem_limit_bytes=VMEM_LIMIT, **kw)


def _row_tile(m, target):
    best = None
    for t in range(16, min(m, target) + 1, 16):
        if m % t == 0:
            best = t
    return best or m


def _col_tile(n, target):
    best = None
    for t in range(LANES, min(n, target) + 1, LANES):
        if n % t == 0:
            best = t
    return best or n


def _mm(a, b, mode, name, out_dtype=F32, add=None, tm=640, tn=1536, tk=2048):
    if mode == "nn":
        (M, K), (K2, N) = a.shape, b.shape
    elif mode == "nt":
        (M, K), (N, K2) = a.shape, b.shape
    else:
        (K, M), (K2, N) = a.shape, b.shape
    assert K == K2, (mode, a.shape, b.shape)
    if mode == "tn":
        tm_ = _col_tile(M, 1408)
        tk_ = _row_tile(K, 1664)
    else:
        tm_ = _row_tile(M, tm)
        tk_ = _col_tile(K, tk) if K > tk else K
    tn_ = _col_tile(N, tn)
    nk = K // tk_
    if mode == "nn":
        a_spec = pl.BlockSpec((tm_, tk_), lambda i, j, k: (i, k))
        b_spec = pl.BlockSpec((tk_, tn_), lambda i, j, k: (k, j))
        dims = (((1,), (0,)), ((), ()))
    elif mode == "nt":
        a_spec = pl.BlockSpec((tm_, tk_), lambda i, j, k: (i, k))
        b_spec = pl.BlockSpec((tn_, tk_), lambda i, j, k: (j, k))
        dims = (((1,), (1,)), ((), ()))
    else:
        a_spec = pl.BlockSpec((tk_, tm_), lambda i, j, k: (k, i))
        b_spec = pl.BlockSpec((tk_, tn_), lambda i, j, k: (k, j))
        dims = (((0,), (0,)), ((), ()))
    o_spec = pl.BlockSpec((tm_, tn_), lambda i, j, k: (i, j))
    has_add = add is not None

    def body(*refs):
        if has_add:
            a_ref, b_ref, add_ref, o_ref, acc_ref = refs
        else:
            a_ref, b_ref, o_ref, acc_ref = refs
        k = pl.program_id(2)
        part = lax.dot_general(a_ref[...].astype(BF16), b_ref[...].astype(BF16), dims, preferred_element_type=F32)

        @pl.when(k == 0)
        def _():
            acc_ref[...] = part

        @pl.when(k > 0)
        def _():
            acc_ref[...] += part

        @pl.when(k == nk - 1)
        def _():
            r = acc_ref[...]
            if has_add:
                r = r + add_ref[...]
            o_ref[...] = r.astype(o_ref.dtype)

    ins = [a, b] + ([add] if has_add else [])
    in_specs = [a_spec, b_spec] + ([o_spec] if has_add else [])
    return pl.pallas_call(
        body,
        out_shape=jax.ShapeDtypeStruct((M, N), out_dtype),
        grid=(M // tm_, N // tn_, nk),
        in_specs=in_specs,
        out_specs=o_spec,
        scratch_shapes=[pltpu.VMEM((tm_, tn_), F32)],
        name=name,
        compiler_params=_cp(("parallel", "parallel", "arbitrary")),
    )(*ins)


def _rms_fwd(x, g, out_dtype, name):
    M, C = x.shape
    tm = _row_tile(M, 640)

    def body(x_ref, g_ref, o_ref):
        xf = x_ref[...]
        r = lax.rsqrt(jnp.mean(xf * xf, axis=-1, keepdims=True) + EPS)
        o_ref[...] = ((xf * r) * g_ref[...]).astype(o_ref.dtype)

    return pl.pallas_call(
        body,
        out_shape=jax.ShapeDtypeStruct((M, C), out_dtype),
        grid=(M // tm,),
        in_specs=[pl.BlockSpec((tm, C), lambda i: (i, 0)), pl.BlockSpec((1, C), lambda i: (0, 0))],
        out_specs=pl.BlockSpec((tm, C), lambda i: (i, 0)),
        name=name,
        compiler_params=_cp(("parallel",)),
    )(x, g.reshape(1, C))


def _rms_bwd(x, g, dy, dres, name):
    M, C = x.shape
    tm = _row_tile(M, 640)
    has_res = dres is not None

    def body(*refs):
        if has_res:
            x_ref, g_ref, dy_ref, dres_ref, dx_ref, dg_ref = refs
        else:
            x_ref, g_ref, dy_ref, dx_ref, dg_ref = refs
        xf = x_ref[...]
        r = lax.rsqrt(jnp.mean(xf * xf, axis=-1, keepdims=True) + EPS)
        xhat = xf * r
        dyf = dy_ref[...].astype(F32)

        @pl.when(pl.program_id(0) == 0)
        def _():
            dg_ref[...] = jnp.zeros_like(dg_ref)

        dg_ref[...] += jnp.sum(dyf * xhat, axis=0, keepdims=True)
        dxh = dyf * g_ref[...]
        dx = r * (dxh - xhat * jnp.mean(dxh * xhat, axis=-1, keepdims=True))
        if has_res:
            dx = dx + dres_ref[...]
        dx_ref[...] = dx

    row = pl.BlockSpec((tm, C), lambda i: (i, 0))
    vec = pl.BlockSpec((1, C), lambda i: (0, 0))
    ins = [x, g.reshape(1, C), dy] + ([dres] if has_res else [])
    return pl.pallas_call(
        body,
        out_shape=(jax.ShapeDtypeStruct((M, C), F32), jax.ShapeDtypeStruct((1, C), F32)),
        grid=(M // tm,),
        in_specs=[row, vec, row] + ([row] if has_res else []),
        out_specs=(row, vec),
        name=name,
        compiler_params=_cp(("arbitrary",)),
    )(*ins)


def _sigmoid(x):
    return 1.0 / (1.0 + jnp.exp(-x))


def _ffn_up(b, wg, wu, name):
    M, K = b.shape
    N = wg.shape[1]
    tm, tn = _row_tile(M, 640), _col_tile(N, 1536)

    def body(b_ref, wg_ref, wu_ref, g_ref, u_ref, h_ref):
        bv = b_ref[...]
        g = jnp.dot(bv, wg_ref[...], preferred_element_type=F32)
        u = jnp.dot(bv, wu_ref[...], preferred_element_type=F32)
        g_ref[...] = g
        u_ref[...] = u
        h_ref[...] = ((g * _sigmoid(g)) * u).astype(h_ref.dtype)

    w_spec = pl.BlockSpec((K, tn), lambda i, j: (0, j))
    o_spec = pl.BlockSpec((tm, tn), lambda i, j: (i, j))
    f32 = jax.ShapeDtypeStruct((M, N), F32)
    return pl.pallas_call(
        body,
        out_shape=(f32, f32, jax.ShapeDtypeStruct((M, N), BF16)),
        grid=(M // tm, N // tn),
        in_specs=[pl.BlockSpec((tm, K), lambda i, j: (i, 0)), w_spec, w_spec],
        out_specs=(o_spec, o_spec, o_spec),
        name=name,
        compiler_params=_cp(("parallel", "parallel")),
    )(b, wg, wu)


def _ffn_down_bwd(dh, wd, g, u, name):
    M, K = dh.shape
    N = wd.shape[0]
    tm, tn = _row_tile(M, 640), _col_tile(N, 1536)

    def body(dh_ref, wd_ref, g_ref, u_ref, dg_ref, du_ref):
        d = lax.dot_general(dh_ref[...].astype(BF16), wd_ref[...], _NT, preferred_element_type=F32)
        gv = g_ref[...]
        sg = _sigmoid(gv)
        du_ref[...] = (d * (gv * sg)).astype(du_ref.dtype)
        dg_ref[...] = ((d * u_ref[...]) * (sg * (1.0 + gv * (1.0 - sg)))).astype(dg_ref.dtype)

    o_spec = pl.BlockSpec((tm, tn), lambda i, j: (i, j))
    bf = jax.ShapeDtypeStruct((M, N), BF16)
    return pl.pallas_call(
        body,
        out_shape=(bf, bf),
        grid=(M // tm, N // tn),
        in_specs=[pl.BlockSpec((tm, K), lambda i, j: (i, 0)), pl.BlockSpec((tn, K), lambda i, j: (j, 0)), o_spec, o_spec],
        out_specs=(o_spec, o_spec),
        name=name,
        compiler_params=_cp(("parallel", "parallel")),
    )(dh, wd, g, u)


def _attn_tile(lp):
    return _col_tile(lp, 640)


def _head_specs(lp, t, offs):
    q_spec = pl.BlockSpec((t, LANES), lambda h, i: (i, offs[0] + h))
    k_spec = pl.BlockSpec((lp, LANES), lambda h, i: (0, offs[1] + h))
    v_spec = pl.BlockSpec((lp, LANES), lambda h, i: (0, offs[2] + h))
    return q_spec, k_spec, v_spec


def _attn_fwd(qa, ka, va, kb, fq, *, nh, offs, name, tile=640):
    lp = qa.shape[0]
    t = _col_tile(lp, tile)
    nq = lp // t
    has_fq = fq is not None

    def body(*refs):
        if has_fq:
            q_ref, k_ref, v_ref, kb_ref, fq_ref, o_ref, lse_ref = refs
        else:
            q_ref, k_ref, v_ref, kb_ref, o_ref, lse_ref = refs
        i = pl.program_id(1)
        q = q_ref[...]
        fqc = fq_ref[:, 0:1] if has_fq else None
        causal = lax.broadcasted_iota(jnp.int32, (t, t), 1) <= lax.broadcasted_iota(jnp.int32, (t, t), 0)

        def step(j, carry, masked):
            rows = pl.ds(pl.multiple_of(j * t, t), t)
            s = biased(lax.dot_general(q, k_ref[rows, :], _NT, preferred_element_type=F32), j)
            if masked:
                s = jnp.where(causal, s, NEG)
            return update(s, v_ref[rows, :], carry)

        def update(s, v, carry):
            m, l, acc = carry
            m_new = jnp.maximum(m, jnp.max(s, axis=1, keepdims=True))
            p = jnp.exp(s - m_new)
            alpha = jnp.exp(m - m_new)
            l = alpha * l + jnp.sum(p, axis=1, keepdims=True)
            acc = alpha * acc + jnp.dot(p.astype(BF16), v, preferred_element_type=F32)
            return m_new, l, acc

        def biased(s, j):
            bias = kb_ref[j]
            if has_fq:
                bias = fqc + bias
            return s + bias

        def pair(m_, carry):
            j0, j1 = 2 * m_, 2 * m_ + 1
            r0 = pl.ds(pl.multiple_of(j0 * t, t), t)
            r1 = pl.ds(pl.multiple_of(j1 * t, t), t)
            s0 = lax.dot_general(q, k_ref[r0, :], _NT, preferred_element_type=F32)
            s1 = lax.dot_general(q, k_ref[r1, :], _NT, preferred_element_type=F32)
            carry = update(biased(s0, j0), v_ref[r0, :], carry)
            return update(biased(s1, j1), v_ref[r1, :], carry)

        init = (jnp.full((t, 1), NEG, F32), jnp.zeros((t, 1), F32), jnp.zeros((t, LANES), F32))
        carry = lax.fori_loop(0, i // 2, pair, init)
        carry = lax.fori_loop(0, i % 2, lambda _, c: step(i - 1, c, False), carry)
        m, l, acc = step(i, carry, True)
        valid = (i * t + lax.broadcasted_iota(jnp.int32, (t, 1), 0)) >= PAD0
        o_ref[...] = jnp.where(valid, acc / l, 0.0).astype(o_ref.dtype)
        lse_ref[...] = jnp.broadcast_to(m + jnp.log(l), (t, LANES))

    q_spec, k_spec, v_spec = _head_specs(lp, t, offs)
    kb_spec = pl.BlockSpec((None, nq, 1, t), lambda h, i: (h, 0, 0, 0))
    row_spec = pl.BlockSpec((t, LANES), lambda h, i: (i, h))
    ins = [qa, ka, va, kb] + ([fq] if has_fq else [])
    return pl.pallas_call(
        body,
        out_shape=(jax.ShapeDtypeStruct((lp, nh * LANES), BF16), jax.ShapeDtypeStruct((lp, nh * LANES), F32)),
        grid=(nh, nq),
        in_specs=[q_spec, k_spec, v_spec, kb_spec] + ([row_spec] if has_fq else []),
        out_specs=(row_spec, row_spec),
        name=name,
        compiler_params=_cp(("parallel", "arbitrary")),
    )(*ins)


def _attn_bwd(qa, ka, va, kb, fq, o, do, lse, *, nh, offs, name, tile=640):
    lp = qa.shape[0]
    t = _col_tile(lp, tile)
    nq = lp // t
    has_fq = fq is not None

    def body(*refs):
        if has_fq:
            q_ref, k_ref, v_ref, kb_ref, fq_ref, o_ref, do_ref, lse_ref, dq_ref, dk_ref, dv_ref, dkb_ref, dqb_ref, dk_acc, dv_acc = refs
        else:
            q_ref, k_ref, v_ref, kb_ref, o_ref, do_ref, lse_ref, dq_ref, dk_ref, dv_ref, dk_acc, dv_acc = refs
        i = pl.program_id(1)

        @pl.when(i == 0)
        def _():
            dk_acc[...] = jnp.zeros_like(dk_acc)
            dv_acc[...] = jnp.zeros_like(dv_acc)
            if has_fq:
                dkb_ref[...] = jnp.zeros_like(dkb_ref)

        q = q_ref[...]
        dov = do_ref[...]
        delta = jnp.sum(o_ref[...].astype(F32) * dov.astype(F32), axis=1, keepdims=True)
        lse_c = lse_ref[:, 0:1]
        fqc = fq_ref[:, 0:1] if has_fq else None
        causal = lax.broadcasted_iota(jnp.int32, (t, t), 1) <= lax.broadcasted_iota(jnp.int32, (t, t), 0)

        def step(j, carry, masked):
            dq_acc, rs = carry
            st = pl.multiple_of(j * t, t)
            k = k_ref[pl.ds(st, t), :]
            v = v_ref[pl.ds(st, t), :]
            s = lax.dot_general(q, k, _NT, preferred_element_type=F32)
            bias = kb_ref[j]
            if has_fq:
                bias = fqc + bias
            s = s + bias
            if masked:
                s = jnp.where(causal, s, NEG)
            p = jnp.exp(s - lse_c)
            dp = lax.dot_general(dov, v, (((1,), (1,)), ((), ())), preferred_element_type=F32)
            ds = p * (dp - delta)
            dv_acc[pl.ds(st, t), :] += lax.dot_general(p.astype(BF16), dov, (((0,), (0,)), ((), ())), preferred_element_type=F32)
            dsb = ds.astype(BF16)
            dk_acc[pl.ds(st, t), :] += lax.dot_general(dsb, q, (((0,), (0,)), ((), ())), preferred_element_type=F32)
            if has_fq:
                dkb_ref[j] += jnp.sum(ds, axis=0, keepdims=True)
                rs = rs + jnp.sum(ds, axis=1, keepdims=True)
            return dq_acc + jnp.dot(dsb, k, preferred_element_type=F32), rs

        carry = (jnp.zeros((t, LANES), F32), jnp.zeros((t, 1), F32))
        carry = lax.fori_loop(0, i, lambda j, c: step(j, c, False), carry)
        dq_acc, rs = step(i, carry, True)
        dq_ref[...] = dq_acc.astype(dq_ref.dtype)
        if has_fq:
            dqb_ref[...] = jnp.broadcast_to(rs, (t, LANES))

        @pl.when(i == nq - 1)
        def _():
            dk_ref[...] = dk_acc[...].astype(dk_ref.dtype)
            dv_ref[...] = dv_acc[...].astype(dv_ref.dtype)

    q_spec, k_spec, v_spec = _head_specs(lp, t, offs)
    kb_spec = pl.BlockSpec((None, nq, 1, t), lambda h, i: (h, 0, 0, 0))
    row_spec = pl.BlockSpec((t, LANES), lambda h, i: (i, h))
    col_spec = pl.BlockSpec((lp, LANES), lambda h, i: (0, h))
    ins = [qa, ka, va, kb] + ([fq] if has_fq else []) + [o, do, lse]
    wide = jax.ShapeDtypeStruct((lp, nh * LANES), BF16)
    extra_shapes = (jax.ShapeDtypeStruct(kb.shape, F32), jax.ShapeDtypeStruct((lp, nh * LANES), F32)) if has_fq else ()
    extra_specs = (kb_spec, row_spec) if has_fq else ()
    return pl.pallas_call(
        body,
        out_shape=(wide, wide, wide) + extra_shapes,
        grid=(nh, nq),
        in_specs=[q_spec, k_spec, v_spec, kb_spec] + ([row_spec] if has_fq else []) + [row_spec, row_spec, row_spec],
        out_specs=(row_spec, col_spec, col_spec) + extra_specs,
        scratch_shapes=[pltpu.VMEM((lp, LANES), F32), pltpu.VMEM((lp, LANES), F32)],
        name=name,
        compiler_params=_cp(("parallel", "arbitrary")),
    )(*ins)


SB_TK = 128


def _split3(x):
    hi = x.astype(BF16)
    r1 = x - hi.astype(F32)
    mid = r1.astype(BF16)
    lo = (r1 - mid.astype(F32)).astype(BF16)
    return hi, mid, lo


SB_RC = 128
SB_UNROLL = 3


_NT = (((1,), (1,)), ((), ()))
_TN = (((0,), (0,)), ((), ()))


def _sb_logits(zraw, kbj, mask):
    z = kbj + zraw
    if mask is not None:
        z = jnp.where(mask, z, NEG)
    e = jnp.exp(-jnp.abs(z))
    g = jnp.minimum(z, 0.0) - jnp.log(1.0 + e)
    lk = g - z
    return z, e, g, lk


def _dot3_parts(parts, tri):
    d = functools.partial(jnp.dot, preferred_element_type=F32)
    return d(parts[0], tri) + d(parts[1], tri) + d(parts[2], tri)


def _split2(x):
    hi = x.astype(BF16)
    return hi, (x - hi.astype(F32)).astype(BF16)


def _dot_parts(parts, m):
    out = jnp.dot(parts[0], m, preferred_element_type=F32)
    for p in parts[1:]:
        out = out + jnp.dot(p, m, preferred_element_type=F32)
    return out


def _tri(n, pred):
    return pred(lax.broadcasted_iota(jnp.int32, (n, n), 0), lax.broadcasted_iota(jnp.int32, (n, n), 1)).astype(BF16)


def _sb_diag_chunks(jj, nrc, rc, tk):
    plan = []
    for r in range(nrc):
        lo_row, hi_row = r * rc, (r + 1) * rc - 1
        lo_col, hi_col = jj * tk, (jj + 1) * tk - 1
        if hi_row <= lo_col:
            plan.append(None)
        elif lo_row > hi_col:
            plan.append("all")
        else:
            plan.append(lo_col - lo_row)
    return plan


def _sb_fwd(qa, kb, *, nh, name, tq=640):
    lp = qa.shape[0]
    tq = _col_tile(lp, tq)
    tk = SB_TK
    rc = min(SB_RC, tq)
    nq, sub, nrc = lp // tq, tq // tk, tq // rc

    def body(q_ref, k_ref, v_ref, kb_ref, o_ref, c_scr, acc_scr):
        i = pl.program_id(1)
        c_scr[...] = jnp.zeros_like(c_scr)
        acc_scr[...] = jnp.zeros_like(acc_scr)
        tri = _tri(tk, lambda r, c: r > c)
        row_io = lax.broadcasted_iota(jnp.int32, (rc, tk), 0)
        col_io = lax.broadcasted_iota(jnp.int32, (rc, tk), 1)

        def scores(j, rows):
            k = k_ref[pl.ds(pl.multiple_of(j * tk, tk), tk), :]
            return [lax.dot_general(q_ref[rs, :], k, _NT, preferred_element_type=F32) for rs in rows]

        def weights(j, rows, masks, zs):
            kbj = kb_ref[j]
            gs, splits, firsts = [], [], []
            for mask, zraw in zip(masks, zs):
                _, _, g, lk = _sb_logits(zraw, kbj, mask)
                gs.append(g)
                firsts.append(lk[:, 0:1])
                splits.append(_split2(lk))
            sums = [_dot_parts(p, tri) for p in splits]
            avs = [jnp.exp(g + (sm + c_scr[rs, :])).astype(BF16) for g, sm, rs in zip(gs, sums, rows)]
            for rs, sm, first in zip(rows, sums, firsts):
                c_scr[rs, :] += jnp.broadcast_to(sm[:, 0:1] + first, (rc, tk))
            return avs

        def values(j, rows, avs):
            v = v_ref[pl.ds(pl.multiple_of(j * tk, tk), tk), :]
            pvs = [jnp.dot(a, v, preferred_element_type=F32) for a in avs]
            for rs, pv in zip(rows, pvs):
                acc_scr[rs, :] += pv

        for jj in reversed(range(sub)):
            plan = _sb_diag_chunks(jj, nrc, rc, tk)
            live = [r for r, what in enumerate(plan) if what is not None]
            rows = [pl.ds(r * rc, rc) for r in live]
            masks = [None if plan[r] == "all" else (col_io + plan[r]) < row_io for r in live]
            j = i * sub + jj
            values(j, rows, weights(j, rows, masks, scores(j, rows)))

        n = i * sub
        rows = [pl.ds(r * rc, rc) for r in range(nrc)]
        nomask = [None] * nrc

        def left(m, carry):
            js = [n - 1 - SB_UNROLL * m - t for t in range(SB_UNROLL)]
            zss = [scores(j, rows) for j in js]
            avss = [weights(j, rows, nomask, zs) for j, zs in zip(js, zss)]
            for j, avs in zip(js, avss):
                values(j, rows, avs)
            return carry

        lax.fori_loop(0, n // SB_UNROLL, left, 0)

        def tail(t, carry):
            j = n % SB_UNROLL - 1 - t
            values(j, rows, weights(j, rows, nomask, scores(j, rows)))
            return carry

        lax.fori_loop(0, n % SB_UNROLL, tail, 0)

        o_ref[...] = acc_scr[...].astype(o_ref.dtype)

    q_spec, k_spec, v_spec = _head_specs(lp, tq, (0, nh, 2 * nh))
    kb_spec = pl.BlockSpec((None, lp // tk, 1, tk), lambda h, i: (h, 0, 0, 0))
    row_spec = pl.BlockSpec((tq, LANES), lambda h, i: (i, h))
    return pl.pallas_call(
        body,
        out_shape=jax.ShapeDtypeStruct((lp, nh * LANES), BF16),
        grid=(nh, nq),
        in_specs=[q_spec, k_spec, v_spec, kb_spec],
        out_specs=row_spec,
        scratch_shapes=[pltpu.VMEM((tq, tk), F32), pltpu.VMEM((tq, LANES), F32)],
        name=name,
        compiler_params=_cp(("parallel", "arbitrary")),
    )(qa, qa, qa, kb)


def _sb_bwd(qa, kb, do, *, nh, name, tq=640):
    lp = qa.shape[0]
    tq = _col_tile(lp, tq)
    tk = SB_TK
    rc = min(SB_RC, tq)
    nq, nk, sub, nrc = lp // tq, lp // tk, tq // tk, tq // rc

    def body(q_ref, k_ref, v_ref, kb_ref, do_ref, dq_ref, dk_ref, dv_ref, dkt_acc, dvt_acc, w_scr, b_scr, c_scr, u_scr, dq_scr):
        i = pl.program_id(1)

        @pl.when(i == 0)
        def _():
            dkt_acc[...] = jnp.zeros_like(dkt_acc)
            dvt_acc[...] = jnp.zeros_like(dvt_acc)

        c_scr[...] = jnp.zeros_like(c_scr)
        u_scr[...] = jnp.zeros_like(u_scr)
        dq_scr[...] = jnp.zeros_like(dq_scr)
        tri_gt = _tri(tk, lambda r, c: r > c)
        tri_lt = _tri(tk, lambda r, c: r < c)
        row_io = lax.broadcasted_iota(jnp.int32, (rc, tk), 0)
        col_io = lax.broadcasted_iota(jnp.int32, (rc, tk), 1)
        qt = q_ref[...].astype(F32).T.astype(BF16)
        dot_t = do_ref[...].astype(F32).T.astype(BF16)
        zero_blk = jnp.zeros((rc, tk), BF16)

        def full_rows(plan, parts):
            it = iter(parts)
            return jnp.concatenate([zero_blk if what is None else next(it) for what in plan], axis=0)

        def key_rows(j):
            return pl.ds(pl.multiple_of(j * tk, tk), tk)

        def scores(j, rows):
            k = k_ref[key_rows(j), :]
            v = v_ref[key_rows(j), :]
            zs = [lax.dot_general(q_ref[rs, :], k, _NT, preferred_element_type=F32) for rs in rows]
            das = [lax.dot_general(do_ref[rs, :], v, _NT, preferred_element_type=F32) for rs in rows]
            return zs, das

        def weights(j, rows, masks, zs, das):
            kbj = kb_ref[j]
            gs, splits, firsts = [], [], []
            for rs, mask, zraw in zip(rows, masks, zs):
                z, e, g, lk = _sb_logits(zraw, kbj, mask)
                b_scr[j, rs, :] = (jnp.where(z >= 0.0, 1.0, e) / (1.0 + e)).astype(BF16)
                gs.append(g)
                firsts.append(lk[:, 0:1])
                splits.append(_split2(lk))
            sums = [_dot_parts(p, tri_gt) for p in splits]
            avs = []
            for rs, g, sm, da, first in zip(rows, gs, sums, das, firsts):
                a = jnp.exp(g + (sm + c_scr[rs, :]))
                w_scr[j, rs, :] = (a * da).astype(BF16)
                avs.append(a.astype(BF16))
                c_scr[rs, :] += jnp.broadcast_to(sm[:, 0:1] + first, (rc, tk))
            return avs

        def dv_update(j, plan, avs):
            dvt_acc[j] += jnp.dot(dot_t, full_rows(plan, avs), preferred_element_type=F32)

        for jj in reversed(range(sub)):
            plan = _sb_diag_chunks(jj, nrc, rc, tk)
            live = [r for r, what in enumerate(plan) if what is not None]
            rows = [pl.ds(r * rc, rc) for r in live]
            masks = [None if plan[r] == "all" else (col_io + plan[r]) < row_io for r in live]
            j = i * sub + jj
            dv_update(j, plan, weights(j, rows, masks, *scores(j, rows)))

        n = i * sub
        everything = ["all"] * nrc
        rows = [pl.ds(r * rc, rc) for r in range(nrc)]
        nomask = [None] * nrc
        def left1(m, carry):
            js = [n - 1 - SB_UNROLL * m - t for t in range(SB_UNROLL)]
            scs = [scores(j, rows) for j in js]
            avs = [weights(j, rows, nomask, *sc) for j, sc in zip(js, scs)]
            for j, av in zip(js, avs):
                dv_update(j, everything, av)
            return carry

        lax.fori_loop(0, n // SB_UNROLL, left1, 0)

        def tail1(t, carry):
            j = n % SB_UNROLL - 1 - t
            dv_update(j, everything, weights(j, rows, nomask, *scores(j, rows)))
            return carry

        lax.fori_loop(0, n % SB_UNROLL, tail1, 0)

        def prefix(j, rows):
            return [jnp.dot(w_scr[j, rs, :], tri_lt, preferred_element_type=F32) for rs in rows]

        def dlogits(j, rows, sums):
            dzs = []
            for rs, sm in zip(rows, sums):
                w = w_scr[j, rs, :].astype(F32)
                beta = b_scr[j, rs, :].astype(F32)
                dzs.append((w - beta * ((w + sm) + u_scr[rs, :])).astype(BF16))
                u_scr[rs, :] += jnp.broadcast_to(sm[:, tk - 1:tk] + w[:, tk - 1:tk], (rc, tk))
            return dzs

        def dqk_update(j, plan, rows, dzs):
            k = k_ref[key_rows(j), :]
            dqs = [jnp.dot(dz, k, preferred_element_type=F32) for dz in dzs]
            for rs, dq in zip(rows, dqs):
                dq_scr[rs, :] += dq
            dkt_acc[j] += jnp.dot(qt, full_rows(plan, dzs), preferred_element_type=F32)

        def left2(m, carry):
            js = [SB_UNROLL * m + t for t in range(SB_UNROLL)]
            sums = [prefix(j, rows) for j in js]
            dzs = [dlogits(j, rows, sm) for j, sm in zip(js, sums)]
            for j, dz in zip(js, dzs):
                dqk_update(j, everything, rows, dz)
            return carry

        lax.fori_loop(0, n // SB_UNROLL, left2, 0)

        def tail2(t, carry):
            j = (n // SB_UNROLL) * SB_UNROLL + t
            dqk_update(j, everything, rows, dlogits(j, rows, prefix(j, rows)))
            return carry

        lax.fori_loop(0, n % SB_UNROLL, tail2, 0)
        for jj in range(sub):
            plan = _sb_diag_chunks(jj, nrc, rc, tk)
            live_rows = [pl.ds(r * rc, rc) for r, what in enumerate(plan) if what is not None]
            j = i * sub + jj
            dqk_update(j, plan, live_rows, dlogits(j, live_rows, prefix(j, live_rows)))
        dq_ref[...] = dq_scr[...].astype(dq_ref.dtype)

        @pl.when(i == nq - 1)
        def _():
            def flush(j, carry):
                dk_ref[key_rows(j), :] = dkt_acc[j].T.astype(dk_ref.dtype)
                dv_ref[key_rows(j), :] = dvt_acc[j].T.astype(dv_ref.dtype)
                return carry

            lax.fori_loop(0, nk, flush, 0)

    q_spec, k_spec, v_spec = _head_specs(lp, tq, (0, nh, 2 * nh))
    kb_spec = pl.BlockSpec((None, nk, 1, tk), lambda h, i: (h, 0, 0, 0))
    row_spec = pl.BlockSpec((tq, LANES), lambda h, i: (i, h))
    col_spec = pl.BlockSpec((lp, LANES), lambda h, i: (0, h))
    wide = jax.ShapeDtypeStruct((lp, nh * LANES), BF16)
    return pl.pallas_call(
        body,
        out_shape=(wide, wide, wide),
        grid=(nh, nq),
        in_specs=[q_spec, k_spec, v_spec, kb_spec, row_spec],
        out_specs=(row_spec, col_spec, col_spec),
        scratch_shapes=[
            pltpu.VMEM((nk, LANES, tk), F32),
            pltpu.VMEM((nk, LANES, tk), F32),
            pltpu.VMEM((nk, tq, tk), BF16),
            pltpu.VMEM((nk, tq, tk), BF16),
            pltpu.VMEM((tq, tk), F32),
            pltpu.VMEM((tq, tk), F32),
            pltpu.VMEM((tq, LANES), F32),
        ],
        name=name,
        compiler_params=_cp(("parallel", "arbitrary")),
    )(qa, qa, qa, kb, do)


def _pool_counts(pos, win):
    return jnp.clip(pos + 1, 1, win).astype(F32)


def _pool_fwd(a, name):
    lp, C = a.shape
    tm = _row_tile(lp, 640)
    hb = tm // POOL_HALO

    def body(prev_ref, cur_ref, o_ref, xs):
        i = pl.program_id(0)
        xs[pl.ds(0, POOL_HALO), :] = jnp.where(i > 0, prev_ref[...], 0.0)
        xs[pl.ds(POOL_HALO, tm), :] = cur_ref[...]
        pos = i * tm + lax.broadcasted_iota(jnp.int32, (tm, 1), 0) - PAD0
        for g, win in enumerate(POOL_WINDOWS):
            cols = pl.ds(g * POOL_GROUP, POOL_GROUP)
            s = xs[pl.ds(POOL_HALO, tm), cols]
            for k in range(1, win):
                s = s + xs[pl.ds(POOL_HALO - k, tm), cols]
            o_ref[:, cols] = (s / _pool_counts(pos, win) - xs[pl.ds(POOL_HALO, tm), cols]).astype(o_ref.dtype)

    return pl.pallas_call(
        body,
        out_shape=jax.ShapeDtypeStruct((lp, C), BF16),
        grid=(lp // tm,),
        in_specs=[
            pl.BlockSpec((POOL_HALO, C), lambda i: (jnp.maximum(i * hb - 1, 0), 0)),
            pl.BlockSpec((tm, C), lambda i: (i, 0)),
        ],
        out_specs=pl.BlockSpec((tm, C), lambda i: (i, 0)),
        scratch_shapes=[pltpu.VMEM((tm + POOL_HALO, C), F32)],
        name=name,
        compiler_params=_cp(("parallel",)),
    )(a, a)


def _pool_bwd(dp, name):
    lp, C = dp.shape
    tm = _row_tile(lp, 640)
    hb = tm // POOL_HALO
    nt = lp // tm
    last_halo = lp // POOL_HALO - 1

    def body(cur_ref, next_ref, o_ref, xs):
        i = pl.program_id(0)
        pos = i * tm + lax.broadcasted_iota(jnp.int32, (tm, 1), 0) - PAD0
        pos_h = (i + 1) * tm + lax.broadcasted_iota(jnp.int32, (POOL_HALO, 1), 0) - PAD0
        for g, win in enumerate(POOL_WINDOWS):
            cols = pl.ds(g * POOL_GROUP, POOL_GROUP)
            cur = cur_ref[:, cols]
            xs[pl.ds(0, tm), cols] = cur / _pool_counts(pos, win)
            xs[pl.ds(tm, POOL_HALO), cols] = jnp.where(i < nt - 1, next_ref[:, cols], 0.0) / _pool_counts(pos_h, win)
            s = xs[pl.ds(0, tm), cols]
            for k in range(1, win):
                s = s + xs[pl.ds(k, tm), cols]
            o_ref[:, cols] = jnp.where(pos >= 0, s - cur, 0.0)

    return pl.pallas_call(
        body,
        out_shape=jax.ShapeDtypeStruct((lp, C), F32),
        grid=(nt,),
        in_specs=[
            pl.BlockSpec((tm, C), lambda i: (i, 0)),
            pl.BlockSpec((POOL_HALO, C), lambda i: (jnp.minimum((i + 1) * hb, last_halo), 0)),
        ],
        out_specs=pl.BlockSpec((tm, C), lambda i: (i, 0)),
        scratch_shapes=[pltpu.VMEM((tm + POOL_HALO, C), F32)],
        name=name,
        compiler_params=_cp(("parallel",)),
    )(dp, dp)


def _scale_add(h, pre, scale, name):
    M, C = h.shape
    tm = _row_tile(M, 640)

    def body(h_ref, p_ref, s_ref, o_ref):
        o_ref[...] = h_ref[...] + p_ref[...] * s_ref[...]

    row = pl.BlockSpec((tm, C), lambda i: (i, 0))
    return pl.pallas_call(
        body,
        out_shape=jax.ShapeDtypeStruct((M, C), F32),
        grid=(M // tm,),
        in_specs=[row, row, pl.BlockSpec((1, C), lambda i: (0, 0))],
        out_specs=row,
        name=name,
        compiler_params=_cp(("parallel",)),
    )(h, pre, scale.reshape(1, C))


def _scale_bwd(dh, pre, scale, name):
    M, C = dh.shape
    tm = _row_tile(M, 640)

    def body(dh_ref, p_ref, s_ref, dp_ref, ds_ref):
        @pl.when(pl.program_id(0) == 0)
        def _():
            ds_ref[...] = jnp.zeros_like(ds_ref)

        d = dh_ref[...]
        ds_ref[...] += jnp.sum(d * p_ref[...], axis=0, keepdims=True)
        dp_ref[...] = (d * s_ref[...]).astype(dp_ref.dtype)

    row = pl.BlockSpec((tm, C), lambda i: (i, 0))
    vec = pl.BlockSpec((1, C), lambda i: (0, 0))
    return pl.pallas_call(
        body,
        out_shape=(jax.ShapeDtypeStruct((M, C), BF16), jax.ShapeDtypeStruct((1, C), F32)),
        grid=(M // tm,),
        in_specs=[row, row, vec],
        out_specs=(row, vec),
        name=name,
        compiler_params=_cp(("arbitrary",)),
    )(dh, pre, scale.reshape(1, C))


def _gate_parts(z):
    e = jnp.exp(-jnp.abs(z))
    return e, jnp.minimum(z, 0.0) - jnp.log(1.0 + e)


def _tri_dot3(tri, x):
    hi, mid, lo = _split3(x)
    d = functools.partial(jnp.dot, preferred_element_type=F32)
    return d(tri, hi) + d(tri, mid) + d(tri, lo)


def _gate_fwd(x, b, name):
    lp, C = x.shape
    tm = _row_tile(lp, 640)

    def body(x_ref, b_ref, o_ref, carry):
        i = pl.program_id(0)

        @pl.when(i == 0)
        def _():
            carry[...] = jnp.zeros_like(carry)

        _, ls = _gate_parts(x_ref[...] + b_ref[...])
        rows = i * tm + lax.broadcasted_iota(jnp.int32, (tm, 1), 0)
        ls = jnp.where(rows >= PAD0, ls, 0.0)
        tri = (lax.broadcasted_iota(jnp.int32, (tm, tm), 0) >= lax.broadcasted_iota(jnp.int32, (tm, tm), 1)).astype(BF16)
        f = _tri_dot3(tri, ls) + carry[...]
        o_ref[...] = f
        carry[...] = f[tm - 1:tm, :]

    return pl.pallas_call(
        body,
        out_shape=jax.ShapeDtypeStruct((lp, C), F32),
        grid=(lp // tm,),
        in_specs=[pl.BlockSpec((tm, C), lambda i: (i, 0)), pl.BlockSpec((1, C), lambda i: (0, 0))],
        out_specs=pl.BlockSpec((tm, C), lambda i: (i, 0)),
        scratch_shapes=[pltpu.VMEM((1, C), F32)],
        name=name,
        compiler_params=_cp(("arbitrary",)),
    )(x, b)


def _gate_bwd(x, b, df, name):
    lp, C = x.shape
    tm = _row_tile(lp, 640)
    nt = lp // tm

    def body(x_ref, b_ref, df_ref, dx_ref, db_ref, carry):
        i = pl.program_id(0)

        @pl.when(i == 0)
        def _():
            carry[...] = jnp.zeros_like(carry)
            db_ref[...] = jnp.zeros_like(db_ref)

        z = x_ref[...] + b_ref[...]
        e, _ = _gate_parts(z)
        tri = (lax.broadcasted_iota(jnp.int32, (tm, tm), 0) <= lax.broadcasted_iota(jnp.int32, (tm, tm), 1)).astype(BF16)
        r = _tri_dot3(tri, df_ref[...]) + carry[...]
        carry[...] = r[0:1, :]
        rows = (nt - 1 - i) * tm + lax.broadcasted_iota(jnp.int32, (tm, 1), 0)
        dx = jnp.where(rows >= PAD0, r * (jnp.where(z >= 0.0, e, 1.0) / (1.0 + e)), 0.0)
        dx_ref[...] = dx
        db_ref[...] += jnp.sum(dx, axis=0, keepdims=True)

    rev = pl.BlockSpec((tm, C), lambda i: (nt - 1 - i, 0))
    vec = pl.BlockSpec((1, C), lambda i: (0, 0))
    return pl.pallas_call(
        body,
        out_shape=(jax.ShapeDtypeStruct((lp, C), F32), jax.ShapeDtypeStruct((1, C), F32)),
        grid=(nt,),
        in_specs=[rev, vec, rev],
        out_specs=(rev, vec),
        scratch_shapes=[pltpu.VMEM((1, C), F32)],
        name=name,
        compiler_params=_cp(("arbitrary",)),
    )(x, b, df)


MLA_SCALE = (MLA_NOPE + MLA_ROPE) ** -0.5


def _rope_apply(x, c, a, b):
    return x * c + pltpu.roll(x, LANES - 16, 1) * a + pltpu.roll(x, 16, 1) * b


def _rope_transpose(dy, c, a, b):
    return dy * c + pltpu.roll(dy * a, 16, 1) + pltpu.roll(dy * b, LANES - 16, 1)


def _mla_prep_fwd(q, kmat, kr, c, a, b, name):
    lp, W = q.shape
    nh = W // LANES
    tm = _row_tile(lp, 640)

    def body(q_ref, k_ref, kr_ref, c_ref, a_ref, b_ref, qo_ref, ko_ref):
        cv, av, bv = c_ref[...], a_ref[...], b_ref[...]
        qo_ref[...] = (_rope_apply(q_ref[...], cv, av, bv) * MLA_SCALE).astype(qo_ref.dtype)
        ko_ref[...] = (k_ref[...] + _rope_apply(kr_ref[...], cv, av, bv)).astype(ko_ref.dtype)

    head = pl.BlockSpec((tm, LANES), lambda i, h: (i, h))
    tab = pl.BlockSpec((tm, LANES), lambda i, h: (i, 0))
    wide = jax.ShapeDtypeStruct((lp, W), BF16)
    return pl.pallas_call(
        body,
        out_shape=(wide, wide),
        grid=(lp // tm, nh),
        in_specs=[head, head, tab, tab, tab, tab],
        out_specs=(head, head),
        name=name,
        compiler_params=_cp(("parallel", "parallel")),
    )(q, kmat, kr, c, a, b)


def _mla_prep_bwd(dq, dk, c, a, b, name):
    lp, W = dq.shape
    nh = W // LANES
    tm = _row_tile(lp, 640)

    def body(dq_ref, dk_ref, c_ref, a_ref, b_ref, dqo_ref, dkr_ref):
        cv, av, bv = c_ref[...], a_ref[...], b_ref[...]
        ksum = jnp.zeros((tm, LANES), F32)
        for h in range(nh):
            cols = pl.ds(h * LANES, LANES)
            dqo_ref[:, cols] = _rope_transpose(dq_ref[:, cols].astype(F32) * MLA_SCALE, cv, av, bv).astype(dqo_ref.dtype)
            ksum = ksum + dk_ref[:, cols].astype(F32)
        dkr_ref[...] = _rope_transpose(ksum, cv, av, bv)

    wide = pl.BlockSpec((tm, W), lambda i: (i, 0))
    tab = pl.BlockSpec((tm, LANES), lambda i: (i, 0))
    return pl.pallas_call(
        body,
        out_shape=(jax.ShapeDtypeStruct((lp, W), BF16), jax.ShapeDtypeStruct((lp, LANES), F32)),
        grid=(lp // tm,),
        in_specs=[wide, wide, tab, tab, tab],
        out_specs=(wide, tab),
        name=name,
        compiler_params=_cp(("parallel",)),
    )(dq, dk, c, a, b)


def _loss_head(h, g, target, name):
    lp, C = h.shape
    tm = _row_tile(lp, 640)
    nt = lp // tm

    def body(h_ref, g_ref, t_ref, loss_ref, dh_ref, dg_ref, sq):
        i = pl.program_id(0)

        @pl.when(i == 0)
        def _():
            dg_ref[...] = jnp.zeros_like(dg_ref)
            sq[...] = jnp.zeros_like(sq)

        xf = h_ref[...]
        gv = g_ref[...]
        r = lax.rsqrt(jnp.mean(xf * xf, axis=-1, keepdims=True) + EPS)
        xhat = xf * r
        rows = i * tm + lax.broadcasted_iota(jnp.int32, (tm, 1), 0)
        err = jnp.where(rows >= PAD0 + N_META, xhat * gv - t_ref[...], 0.0)
        sq[...] += jnp.sum(err * err, axis=0, keepdims=True)
        dy = err * (1.0 / C)
        dg_ref[...] += jnp.sum(dy * xhat, axis=0, keepdims=True)
        dxh = dy * gv
        dh_ref[...] = r * (dxh - xhat * jnp.mean(dxh * xhat, axis=-1, keepdims=True))

        @pl.when(i == nt - 1)
        def _():
            loss_ref[...] = jnp.broadcast_to(jnp.sum(sq[...], axis=1, keepdims=True) * (0.5 / C), (1, LANES))

    row = pl.BlockSpec((tm, C), lambda i: (i, 0))
    vec = pl.BlockSpec((1, C), lambda i: (0, 0))
    return pl.pallas_call(
        body,
        out_shape=(jax.ShapeDtypeStruct((1, LANES), F32), jax.ShapeDtypeStruct((lp, C), F32), jax.ShapeDtypeStruct((1, C), F32)),
        grid=(nt,),
        in_specs=[row, vec, row],
        out_specs=(pl.BlockSpec((1, LANES), lambda i: (0, 0)), row, vec),
        scratch_shapes=[pltpu.VMEM((1, C), F32)],
        name=name,
        compiler_params=_cp(("arbitrary",)),
    )(h, g.reshape(1, C), target)


def _adamw(w, g, m, v, name):
    shape = w.shape
    C = shape[-1]
    R = w.size // C
    tr = R
    if R % 8 == 0:
        for cand in range(8, R + 1, 8):
            if R % cand == 0 and cand * C * 4 <= (1 << 20):
                tr = cand
    c1 = 1.0 - ADAM_B1 ** ADAM_STEP
    c2 = 1.0 - ADAM_B2 ** ADAM_STEP

    def body(w_ref, g_ref, m_ref, v_ref, d_ref, nm_ref, nv_ref):
        gv = g_ref[...]
        nm = ADAM_B1 * m_ref[...] + (1.0 - ADAM_B1) * gv
        nv = ADAM_B2 * v_ref[...] + (1.0 - ADAM_B2) * (gv * gv)
        nm_ref[...] = nm
        nv_ref[...] = nv
        d_ref[...] = -ADAM_LR * ((nm / c1) / (jnp.sqrt(nv / c2) + ADAM_EPS) + ADAM_WD * w_ref[...])

    blk = pl.BlockSpec((tr, C), lambda i: (i, 0))
    out = jax.ShapeDtypeStruct((R, C), F32)
    outs = pl.pallas_call(
        body,
        out_shape=(out, out, out),
        grid=(R // tr,),
        in_specs=[blk] * 4,
        out_specs=(blk, blk, blk),
        name=name,
        compiler_params=_cp(("parallel",)),
    )(*(t.reshape(R, C) for t in (w, g, m, v)))
    return tuple(t.reshape(shape) for t in outs)


def _exchange(send, axes, same, name):
    na = len(axes)
    n = 1 << na
    _, R, C = send.shape
    parts = max(p for p in (8, 4, 2, 1) if R % (16 * p) == 0 or p == 1)
    pr = R // parts

    def body(send_ref, recv_ref, send_sems, recv_sems, local_sem):
        coords = {ax: lax.axis_index(ax) for ax in MESH_AXES}
        me = 0
        for ax in axes:
            me = me * 2 + coords[ax]

        def member(r):
            dev = dict(coords)
            for b, ax in enumerate(axes):
                if (r >> (na - 1 - b)) & 1:
                    dev[ax] = 1 - dev[ax]
            return tuple(dev[ax] for ax in MESH_AXES)

        def chunk(j, p):
            return (send_ref.at[0] if same else send_ref.at[j]).at[pl.ds(p * pr, pr)]

        def slot(j, p):
            return recv_ref.at[j].at[pl.ds(p * pr, pr)]

        own = pltpu.make_async_copy(send_ref.at[0] if same else send_ref.at[me], recv_ref.at[me], local_sem)
        own.start()
        copies = []
        for r in range(1, n):
            peer = me ^ r
            for p in range(parts):
                cp = pltpu.make_async_remote_copy(
                    src_ref=chunk(peer, p), dst_ref=slot(me, p), send_sem=send_sems.at[r, p], recv_sem=recv_sems.at[r, p],
                    device_id=member(r), device_id_type=pl.DeviceIdType.MESH)
                cp.start()
                copies.append(cp)
        for r in range(1, n):
            for p in range(parts):
                arrival = pltpu.make_async_remote_copy(
                    src_ref=chunk(me, p), dst_ref=slot(me ^ r, p), send_sem=send_sems.at[r, p], recv_sem=recv_sems.at[r, p],
                    device_id=member(r), device_id_type=pl.DeviceIdType.MESH)
                arrival.wait_recv()
        for cp in copies:
            cp.wait_send()
        own.wait()

    any_spec = pl.BlockSpec(memory_space=pl.ANY)
    return pl.pallas_call(
        body,
        out_shape=jax.ShapeDtypeStruct((n, R, C), send.dtype),
        in_specs=[any_spec],
        out_specs=any_spec,
        scratch_shapes=[pltpu.SemaphoreType.DMA((n, parts)), pltpu.SemaphoreType.DMA((n, parts)), pltpu.SemaphoreType.DMA],
        name=name,
        compiler_params=pltpu.CompilerParams(has_side_effects=True),
    )(send)


def _sum_chunks(x, name, out_dtype=F32):
    n, R, C = x.shape
    tr = _row_tile(R, 512)

    def body(x_ref, o_ref):
        acc = x_ref[0].astype(F32)
        for j in range(1, n):
            acc = acc + x_ref[j].astype(F32)
        o_ref[...] = acc.astype(o_ref.dtype)

    return pl.pallas_call(
        body,
        out_shape=jax.ShapeDtypeStruct((R, C), out_dtype),
        grid=(R // tr,),
        in_specs=[pl.BlockSpec((n, tr, C), lambda i: (0, i, 0))],
        out_specs=pl.BlockSpec((tr, C), lambda i: (i, 0)),
        name=name,
        compiler_params=_cp(("parallel",)),
    )(x)


def _pad_heads_cols(w, groups, d):
    k = w.shape[0]
    w = w.reshape(k, groups * N_HEADS, d)
    return jnp.pad(w, ((0, 0), (0, 0), (0, LANES - d))).reshape(k, groups * N_HEADS * LANES)


def _unpad_heads_cols(w, groups, d):
    k = w.shape[0]
    return w.reshape(k, groups * N_HEADS, LANES)[:, :, :d].reshape(k, groups * N_HEADS * d)


def _pad_heads_rows(w, d):
    n = w.shape[1]
    return jnp.pad(w.reshape(N_HEADS, d, n), ((0, 0), (0, LANES - d), (0, 0))).reshape(N_HEADS * LANES, n)


def _unpad_heads_rows(w, d):
    n = w.shape[1]
    return w.reshape(N_HEADS, LANES, n)[:, :d].reshape(N_HEADS * d, n)


Q_SCALE = HEAD_DIM ** -0.5


def _scale_q_cols(w):
    nq = N_HEADS * LANES
    return jnp.concatenate([w[:, :nq] * Q_SCALE, w[:, nq:]], axis=1)


def _kernel_weights(W):
    P = dict(W)
    pw = W["pool_w"][0]
    bd = jnp.zeros((D_MODEL, D_MODEL), pw.dtype)
    for g in range(len(POOL_WINDOWS)):
        bd = lax.dynamic_update_slice(bd, pw[g], (g * POOL_GROUP, g * POOL_GROUP))
    P["pool_bd"] = bd
    P["sb_qkv"] = _scale_q_cols(_pad_heads_cols(W["sb_w_qkv"][0], 3, HEAD_DIM))
    P["sb_o"] = _pad_heads_rows(W["sb_w_o"][0], HEAD_DIM)
    nq = 3 * N_HEADS * HEAD_DIM
    P["fox_qkv"] = _scale_q_cols(_pad_heads_cols(W["fox_w_qkvf"][0][:, :nq], 3, HEAD_DIM))
    P["fox_f"] = jnp.pad(W["fox_w_qkvf"][0][:, nq:], ((0, 0), (0, LANES - N_HEADS)))
    P["fox_o"] = _pad_heads_rows(W["fox_w_o"][0], HEAD_DIM)
    P["fox_b"] = jnp.pad(W["fox_b_f"], ((0, 0), (0, LANES - N_HEADS)))
    P["mla_down"] = jnp.pad(W["mla_w_down"][0], ((0, 0), (0, MLA_DOWN_PAD - W["mla_w_down"].shape[2])))
    P["mla_uq"] = _pad_heads_cols(W["mla_w_uq"][0], 1, MLA_NOPE + MLA_ROPE)
    ukv = W["mla_w_ukv"][0].reshape(MLA_KV_RANK, N_HEADS, 2 * HEAD_DIM)
    padk = ((0, 0), (0, 0), (0, LANES - HEAD_DIM))
    P["mla_ukv"] = jnp.concatenate(
        [jnp.pad(ukv[:, :, :MLA_NOPE], padk).reshape(MLA_KV_RANK, -1), jnp.pad(ukv[:, :, MLA_NOPE:], padk).reshape(MLA_KV_RANK, -1)], axis=1)
    P["mla_o"] = _pad_heads_rows(W["mla_w_o"][0], HEAD_DIM)
    return P


def _rope_tables(lp):
    pos = (jnp.arange(lp) - PAD0).astype(F32)
    inv = ROPE_THETA ** (-jnp.arange(0, MLA_ROPE, 2, dtype=F32) / MLA_ROPE)
    ang = pos[:, None] * inv[None, :]
    cos, sin = jnp.cos(ang), jnp.sin(ang)
    half = MLA_ROPE // 2
    z = lambda n: jnp.zeros((lp, n), F32)
    c = jnp.concatenate([jnp.ones((lp, MLA_NOPE), F32), cos, cos, z(LANES - MLA_NOPE - MLA_ROPE)], axis=1)
    a = jnp.concatenate([z(MLA_NOPE), -sin, z(LANES - MLA_NOPE - half)], axis=1)
    b = jnp.concatenate([z(MLA_NOPE + half), sin, z(LANES - MLA_NOPE - MLA_ROPE)], axis=1)
    return c, a, b


def _key_bias(lp, t, per_head=None):
    pad = jnp.arange(lp)[None, :] < PAD0
    body = jnp.zeros((N_HEADS, lp), F32) if per_head is None else per_head
    return jnp.where(pad, NEG, body).reshape(N_HEADS, lp // t, 1, t)


def _ffn_fwd(h, i, P):
    b = _rms_fwd(h, P["norm_ffn"][i], BF16, "ffn_norm")
    g, u, hd = _ffn_up(b, P["ffn_w_gate"][i], P["ffn_w_up"][i], "ffn_gate_up")
    return _mm(hd, P["ffn_w_down"][i], "nn", "ffn_down", add=h), (h, b, g, u, hd)


def _ffn_bwd(dh, i, P, saved):
    h, b, g, u, hd = saved
    dwd = _mm(hd, dh, "tn", "ffn_down_dw")
    dg, du = _ffn_down_bwd(dh, P["ffn_w_down"][i], g, u, "ffn_down_dx")
    dwg = _mm(b, dg, "tn", "ffn_gate_dw")
    dwu = _mm(b, du, "tn", "ffn_up_dw")
    db = _mm(dg, P["ffn_w_gate"][i], "nt", "ffn_gate_dx")
    db = _mm(du, P["ffn_w_up"][i], "nt", "ffn_up_dx", add=db)
    dh_in, dgain = _rms_bwd(h, P["norm_ffn"][i], db, dh, "ffn_norm_bwd")
    return dh_in, dgain, dwg, dwu, dwd


def _pool_layer_fwd(h, P):
    a = _rms_fwd(h, P["norm_mix"][0], F32, "pool_norm")
    pooled = _pool_fwd(a, "pool_window")
    pre = _mm(pooled, P["pool_bd"], "nn", "pool_mix")
    return _scale_add(h, pre, P["pool_scale"][0], "pool_scale_add"), (h, pooled, pre)


def _pool_layer_bwd(dh, P, saved):
    h, pooled, pre = saved
    dpre, dscale = _scale_bwd(dh, pre, P["pool_scale"][0], "pool_scale_bwd")
    dbd = _mm(pooled, dpre, "tn", "pool_mix_dw")
    dpooled = _mm(dpre, P["pool_bd"], "nt", "pool_mix_dx")
    da = _pool_bwd(dpooled, "pool_window_bwd")
    dh_in, dgain = _rms_bwd(h, P["norm_mix"][0], da, dh, "mix_norm_bwd")
    dw = jnp.stack([dbd[g * POOL_GROUP:(g + 1) * POOL_GROUP, g * POOL_GROUP:(g + 1) * POOL_GROUP] for g in range(len(POOL_WINDOWS))])
    return dh_in, {"norm_mix0": dgain, "pool_w": dw[None], "pool_scale": dscale}


def _out_proj_bwd(o, dh, wo, tag):
    return _mm(o, dh, "tn", tag + "_o_dw"), _mm(dh, wo, "nt", tag + "_o_dx", out_dtype=BF16)


def _sb_layer_fwd(h, P):
    lp = h.shape[0]
    a = _rms_fwd(h, P["norm_mix"][1], BF16, "mix_norm")
    qkv = _mm(a, P["sb_qkv"], "nn", "sb_qkv", out_dtype=BF16)
    kb = _key_bias(lp, SB_TK)
    o = _sb_fwd(qkv, kb, nh=N_HEADS, name="sb_attn")
    return _mm(o, P["sb_o"], "nn", "attn_out", add=h), (h, a, qkv, kb, o)


def _sb_layer_bwd(dh, P, saved):
    h, a, qkv, kb, o = saved
    dwo, do = _out_proj_bwd(o, dh, P["sb_o"], "attn")
    dq, dk, dv = _sb_bwd(qkv, kb, do, nh=N_HEADS, name="sb_attn_bwd")
    dqkv = jnp.concatenate([dq, dk, dv], axis=1)
    dw = _scale_q_cols(_mm(a, dqkv, "tn", "qkv_dw"))
    da = _mm(dqkv, P["sb_qkv"], "nt", "qkv_dx")
    dh_in, dgain = _rms_bwd(h, P["norm_mix"][1], da, dh, "mix_norm_bwd")
    return dh_in, {"norm_mix1": dgain, "sb_w_qkv": _unpad_heads_cols(dw, 3, HEAD_DIM)[None], "sb_w_o": _unpad_heads_rows(dwo, HEAD_DIM)[None]}


def _fox_layer_fwd(h, P):
    lp = h.shape[0]
    t = _attn_tile(lp)
    a = _rms_fwd(h, P["norm_mix"][3], BF16, "mix_norm")
    qkv = _mm(a, P["fox_qkv"], "nn", "sb_qkv", out_dtype=BF16)
    f = _mm(a, P["fox_f"], "nn", "fox_gate_proj")
    fc = _gate_fwd(f, P["fox_b"], "fox_gate")[:, :N_HEADS]
    kb = _key_bias(lp, t, -fc.T)
    fq = jnp.broadcast_to(fc[:, :, None], (lp, N_HEADS, LANES)).reshape(lp, N_HEADS * LANES)
    o, lse = _attn_fwd(qkv, qkv, qkv, kb, fq, nh=N_HEADS, offs=(0, N_HEADS, 2 * N_HEADS), name="fox_attn")
    return _mm(o, P["fox_o"], "nn", "attn_out", add=h), (h, a, qkv, f, kb, fq, o, lse)


def _fox_layer_bwd(dh, P, saved):
    h, a, qkv, f, kb, fq, o, lse = saved
    lp = h.shape[0]
    dwo, do = _out_proj_bwd(o, dh, P["fox_o"], "attn")
    dq, dk, dv, dkb, dqb = _attn_bwd(qkv, qkv, qkv, kb, fq, o, do, lse, nh=N_HEADS, offs=(0, N_HEADS, 2 * N_HEADS),
                                     name="fox_attn_bwd")
    dfc = jnp.pad(dqb.reshape(lp, N_HEADS, LANES)[:, :, 0] - dkb.reshape(N_HEADS, lp).T, ((0, 0), (0, LANES - N_HEADS)))
    df, dbf = _gate_bwd(f, P["fox_b"], dfc, "fox_gate_bwd")
    dqkv = jnp.concatenate([dq, dk, dv], axis=1)
    dw = _scale_q_cols(_mm(a, dqkv, "tn", "qkv_dw"))
    dwf = _mm(a, df, "tn", "fox_gate_dw")
    da = _mm(dqkv, P["fox_qkv"], "nt", "qkv_dx")
    da = _mm(df, P["fox_f"], "nt", "fox_gate_dx", add=da)
    dh_in, dgain = _rms_bwd(h, P["norm_mix"][3], da, dh, "mix_norm_bwd")
    dwqkvf = jnp.concatenate([_unpad_heads_cols(dw, 3, HEAD_DIM), dwf[:, :N_HEADS]], axis=1)
    return dh_in, {"norm_mix3": dgain, "fox_w_qkvf": dwqkvf[None], "fox_b_f": dbf[:, :N_HEADS], "fox_w_o": _unpad_heads_rows(dwo, HEAD_DIM)[None]}


def _mla_layer_fwd(h, P):
    lp = h.shape[0]
    a = _rms_fwd(h, P["norm_mix"][2], BF16, "mix_norm")
    down = _mm(a, P["mla_down"], "nn", "mla_down")
    cq_pre = down[:, :MLA_Q_RANK]
    ckv_pre = down[:, MLA_Q_RANK:MLA_Q_RANK + MLA_KV_RANK]
    kr = jnp.pad(down[:, MLA_Q_RANK + MLA_KV_RANK:MLA_Q_RANK + MLA_KV_RANK + MLA_ROPE], ((0, 0), (MLA_NOPE, LANES - MLA_NOPE - MLA_ROPE)))
    cq = _rms_fwd(cq_pre, P["mla_q_norm"][0], BF16, "mla_q_norm")
    ckv = _rms_fwd(ckv_pre, P["mla_kv_norm"][0], BF16, "mla_kv_norm")
    q = _mm(cq, P["mla_uq"], "nn", "mla_uq")
    kv = _mm(ckv, P["mla_ukv"], "nn", "mla_ukv", out_dtype=BF16)
    tabs = _rope_tables(lp)
    qr, kc = _mla_prep_fwd(q, kv, kr, *tabs, "mla_rope")
    kb = _key_bias(lp, _attn_tile(lp))
    o, lse = _attn_fwd(qr, kc, kv, kb, None, nh=N_HEADS, offs=(0, 0, N_HEADS), name="mla_attn")
    return _mm(o, P["mla_o"], "nn", "attn_out", add=h), (h, a, cq_pre, ckv_pre, cq, ckv, qr, kc, kv, tabs, kb, o, lse)


def _mla_layer_bwd(dh, P, saved):
    h, a, cq_pre, ckv_pre, cq, ckv, qr, kc, kv, tabs, kb, o, lse = saved
    lp = h.shape[0]
    dwo, do = _out_proj_bwd(o, dh, P["mla_o"], "attn")
    dqr, dkc, dv = _attn_bwd(qr, kc, kv, kb, None, o, do, lse, nh=N_HEADS, offs=(0, 0, N_HEADS),
                             name="mla_attn_bwd")
    dq, dkr = _mla_prep_bwd(dqr, dkc, *tabs, "mla_rope_bwd")
    dkv = jnp.concatenate([dkc, dv], axis=1)
    dwuq = _mm(cq, dq, "tn", "mla_uq_dw")
    dcq = _mm(dq, P["mla_uq"], "nt", "mla_uq_dx")
    dwukv = _mm(ckv, dkv, "tn", "mla_ukv_dw")
    dckv = _mm(dkv, P["mla_ukv"], "nt", "mla_ukv_dx")
    dcq_pre, dqn = _rms_bwd(cq_pre, P["mla_q_norm"][0], dcq, None, "mla_q_norm_bwd")
    dckv_pre, dkvn = _rms_bwd(ckv_pre, P["mla_kv_norm"][0], dckv, None, "mla_kv_norm_bwd")
    used = MLA_Q_RANK + MLA_KV_RANK + MLA_ROPE
    ddown = jnp.concatenate([dcq_pre, dckv_pre, dkr[:, MLA_NOPE:MLA_NOPE + MLA_ROPE], jnp.zeros((lp, MLA_DOWN_PAD - used), F32)], axis=1)
    dwdown = _mm(a, ddown, "tn", "mla_down_dw")
    da = _mm(ddown, P["mla_down"], "nt", "mla_down_dx")
    dh_in, dgain = _rms_bwd(h, P["norm_mix"][2], da, dh, "mix_norm_bwd")
    dukv = dwukv.reshape(MLA_KV_RANK, 2, N_HEADS, LANES)[:, :, :, :HEAD_DIM]
    dukv = jnp.concatenate([dukv[:, 0], dukv[:, 1]], axis=-1).reshape(MLA_KV_RANK, N_HEADS * 2 * HEAD_DIM)
    return dh_in, {
        "norm_mix2": dgain, "mla_w_down": dwdown[:, :used][None], "mla_q_norm": dqn, "mla_kv_norm": dkvn,
        "mla_w_uq": _unpad_heads_cols(dwuq, 1, MLA_NOPE + MLA_ROPE)[None], "mla_w_ukv": dukv[None],
        "mla_w_o": _unpad_heads_rows(dwo, HEAD_DIM)[None]}


_MIXERS = ((_pool_layer_fwd, _pool_layer_bwd), (_sb_layer_fwd, _sb_layer_bwd), (_mla_layer_fwd, _mla_layer_bwd), (_fox_layer_fwd, _fox_layer_bwd))


def _step_local(x, target, W):
    seq = x.shape[0]
    P = _kernel_weights(W)
    h = jnp.concatenate([jnp.zeros((PAD0, D_MODEL), F32), W["meta"], x], axis=0)
    tpad = jnp.pad(target, ((PAD0 + N_META, 0), (0, 0)))
    saved = []
    for i in range(4):
        h, s_mix = _MIXERS[i][0](h, P)
        h, s_ffn = _ffn_fwd(h, i, P)
        saved.append((s_mix, s_ffn))
    loss, dh, dfinal = _loss_head(h, W["final_norm"], tpad, "loss_head")
    grads = {"final_norm": dfinal.reshape(-1)}
    gains_mix, gains_ffn, dwg, dwu, dwd = [None] * 4, [None] * 4, [None] * 4, [None] * 4, [None] * 4
    for i in reversed(range(4)):
        s_mix, s_ffn = saved[i]
        dh, gains_ffn[i], dwg[i], dwu[i], dwd[i] = _ffn_bwd(dh, i, P, s_ffn)
        dh, g = _MIXERS[i][1](dh, P, s_mix)
        gains_mix[i] = g.pop("norm_mix%d" % i)
        grads.update(g)
    grads["norm_mix"] = jnp.concatenate(gains_mix, axis=0)
    grads["norm_ffn"] = jnp.concatenate(gains_ffn, axis=0)
    grads["ffn_w_gate"] = jnp.stack(dwg)
    grads["ffn_w_up"] = jnp.stack(dwu)
    grads["ffn_w_down"] = jnp.stack(dwd)
    grads["meta"] = dh[PAD0:PAD0 + N_META]
    return loss, dh[PAD0 + N_META:], grads


_WEIGHTS = ("meta", "norm_mix", "norm_ffn", "pool_w", "pool_scale", "sb_w_qkv", "sb_w_o", "mla_w_down", "mla_q_norm",
            "mla_kv_norm", "mla_w_uq", "mla_w_ukv", "mla_w_o", "fox_w_qkvf", "fox_b_f", "fox_w_o", "ffn_w_gate",
            "ffn_w_up", "ffn_w_down", "final_norm")
_SHARD_AXIS = {"meta": 1, "pool_w": 2, "sb_w_qkv": 2, "sb_w_o": 1, "mla_w_down": 1, "mla_q_norm": 1, "mla_kv_norm": 1,
               "mla_w_uq": 2, "mla_w_ukv": 2, "mla_w_o": 1, "fox_w_qkvf": 2, "fox_b_f": None, "fox_w_o": 1,
               "ffn_w_gate": 2, "ffn_w_up": 2, "ffn_w_down": 1}
_SHARDED = tuple(n for n in _WEIGHTS if _SHARD_AXIS.get(n) is not None)
_REPLICATED = tuple(n for n in _WEIGHTS if _SHARD_AXIS.get(n) is None)
_EXACT = ("meta", "mla_q_norm", "mla_kv_norm")
N_CHIPS = 4
GRAD_ROW_TILE = 512


PACK_ROWS = 16


def _piece_rows(t):
    return -(-t.size // (LANES * PACK_ROWS)) * PACK_ROWS


def _flat_rows(parts, dtype, row_multiple):
    pieces = []
    for p in parts:
        flat = p.astype(dtype).reshape(-1)
        pieces.append(jnp.pad(flat, (0, _piece_rows(p) * LANES - flat.shape[0])).reshape(-1, LANES))
    rows = sum(q.shape[0] for q in pieces)
    pad = -(-rows // row_multiple) * row_multiple - rows
    if pad:
        pieces.append(jnp.zeros((pad, LANES), dtype))
    return jnp.concatenate(pieces, axis=0)


def _split_flat(flat, like):
    out, off = [], 0
    for t in like:
        out.append(flat[off:off + _piece_rows(t)].reshape(-1)[:t.size].reshape(t.shape))
        off += _piece_rows(t)
    return out


def _gather_shards(local, names, dtype, name):
    blocks = [local[n] for n in names]
    recv = _exchange(_flat_rows(blocks, dtype, PACK_ROWS)[None], ("x", "y"), True, name)
    per_chip = [_split_flat(recv[s], blocks) for s in range(N_CHIPS)]
    return {n: jnp.concatenate([per_chip[s][k] for s in range(N_CHIPS)], axis=_SHARD_AXIS[n]) for k, n in enumerate(names)}


def _shard_of(g, n, s):
    w = g.shape[_SHARD_AXIS[n]] // N_CHIPS
    return lax.slice_in_dim(g, s * w, (s + 1) * w, axis=_SHARD_AXIS[n])


def _train_step(a):
    local = {n: a[n] for n in _WEIGHTS}
    full = {n: local[n] for n in _REPLICATED}
    full.update(_gather_shards(local, [n for n in _SHARDED if n not in _EXACT], BF16, "gather_weights"))
    full.update(_gather_shards(local, list(_EXACT), F32, "gather_exact"))

    loss, grad_x, grads = _step_local(a["x"][0], a["loss_target"][0], full)

    send = jnp.stack([
        _flat_rows([_shard_of(grads[n], n, s) for n in _SHARDED], BF16, 2 * GRAD_ROW_TILE).reshape(2, -1, LANES)
        for s in range(N_CHIPS)]).reshape(2 * N_CHIPS, -1, LANES)
    mine = _sum_chunks(_exchange(send, MESH_AXES, False, "scatter_grads"), "sum_grads", out_dtype=BF16)
    both = _exchange(mine[None], ("c",), True, "pair_grads").reshape(-1, LANES).astype(F32)
    reduced = dict(zip(_SHARDED, _split_flat(both, [local[n] for n in _SHARDED])))
    small = _flat_rows([grads[n] for n in _REPLICATED], F32, 8)
    small = _sum_chunks(_exchange(small[None], MESH_AXES, True, "gather_small_grads"), "sum_small_grads")
    reduced.update(zip(_REPLICATED, _split_flat(small, [local[n] for n in _REPLICATED])))

    deltas, new_m, new_v = {}, {}, {}
    for n in _WEIGHTS:
        deltas[n], new_m[n], new_v[n] = _adamw(local[n], reduced[n], a["m_" + n], a["v_" + n], "adamw")
    total = lax.psum(loss[0, 0], MESH_AXES)
    return (total, grad_x[None], *[reduced[n] for n in _WEIGHTS], *[deltas[n] for n in _WEIGHTS],
            *[new_m[n] for n in _WEIGHTS], *[new_v[n] for n in _WEIGHTS])


def kernel(x, meta, norm_mix, norm_ffn, pool_w, pool_scale, sb_w_qkv, sb_w_o, mla_w_down, mla_q_norm, mla_kv_norm, mla_w_uq, mla_w_ukv, mla_w_o, fox_w_qkvf, fox_b_f, fox_w_o, ffn_w_gate, ffn_w_up, ffn_w_down, final_norm, loss_target, m_meta, m_norm_mix, m_norm_ffn, m_pool_w, m_pool_scale, m_sb_w_qkv, m_sb_w_o, m_mla_w_down, m_mla_q_norm, m_mla_kv_norm, m_mla_w_uq, m_mla_w_ukv, m_mla_w_o, m_fox_w_qkvf, m_fox_b_f, m_fox_w_o, m_ffn_w_gate, m_ffn_w_up, m_ffn_w_down, m_final_norm, v_meta, v_norm_mix, v_norm_ffn, v_pool_w, v_pool_scale, v_sb_w_qkv, v_sb_w_o, v_mla_w_down, v_mla_q_norm, v_mla_kv_norm, v_mla_w_uq, v_mla_w_ukv, v_mla_w_o, v_fox_w_qkvf, v_fox_b_f, v_fox_w_o, v_ffn_w_gate, v_ffn_w_up, v_ffn_w_down, v_final_norm):
    return _train_step(dict(locals()))
```

```python
import functools

import jax
import jax.numpy as jnp
from jax import lax
from jax.experimental import pallas as pl
from jax.experimental.pallas import tpu as pltpu

F32 = jnp.float32
BF16 = jnp.bfloat16

D_MODEL = 1024
N_META = 16
PAD0 = 112
LANES = 128
N_HEADS = 16
HEAD_DIM = 64
POOL_WINDOWS = (2, 4, 8, 16)
POOL_GROUP = 256
POOL_HALO = 16
MLA_Q_RANK = 384
MLA_KV_RANK = 256
MLA_NOPE = 64
MLA_ROPE = 32
MLA_DOWN_PAD = 768
ROPE_THETA = 10000.0
D_FF = 2816
EPS = 1e-6
NEG = -1e30
ADAM_LR = 0.001
ADAM_B1 = 0.9
ADAM_B2 = 0.999
ADAM_EPS = 1e-08
ADAM_WD = 0.01
ADAM_STEP = 10
VMEM_LIMIT = 56 * 1024 * 1024
MESH_AXES = ("x", "y", "c")


def _cp(sem, **kw):
    return pltpu.CompilerParams(dimension_semantics=sem, vmem_limit_bytes=VMEM_LIMIT, **kw)


def _row_tile(m, target):
    best = None
    for t in range(16, min(m, target) + 1, 16):
        if m % t == 0:
            best = t
    return best or m


def _col_tile(n, target):
    best = None
    for t in range(LANES, min(n, target) + 1, LANES):
        if n % t == 0:
            best = t
    return best or n


def _mm(a, b, mode, name, out_dtype=F32, add=None, tm=640, tn=1536, tk=2048):
    if mode == "nn":
        (M, K), (K2, N) = a.shape, b.shape
    elif mode == "nt":
        (M, K), (N, K2) = a.shape, b.shape
    else:
        (K, M), (K2, N) = a.shape, b.shape
    assert K == K2, (mode, a.shape, b.shape)
    if mode == "tn":
        tm_ = _col_tile(M, 1408)
        tk_ = _row_tile(K, 1664)
    else:
        tm_ = _row_tile(M, tm)
        tk_ = _col_tile(K, tk) if K > tk else K
    tn_ = _col_tile(N, tn)
    nk = K // tk_
    if mode == "nn":
        a_spec = pl.BlockSpec((tm_, tk_), lambda i, j, k: (i, k))
        b_spec = pl.BlockSpec((tk_, tn_), lambda i, j, k: (k, j))
        dims = (((1,), (0,)), ((), ()))
    elif mode == "nt":
        a_spec = pl.BlockSpec((tm_, tk_), lambda i, j, k: (i, k))
        b_spec = pl.BlockSpec((tn_, tk_), lambda i, j, k: (j, k))
        dims = (((1,), (1,)), ((), ()))
    else:
        a_spec = pl.BlockSpec((tk_, tm_), lambda i, j, k: (k, i))
        b_spec = pl.BlockSpec((tk_, tn_), lambda i, j, k: (k, j))
        dims = (((0,), (0,)), ((), ()))
    o_spec = pl.BlockSpec((tm_, tn_), lambda i, j, k: (i, j))
    has_add = add is not None

    def body(*refs):
        if has_add:
            a_ref, b_ref, add_ref, o_ref, acc_ref = refs
        else:
            a_ref, b_ref, o_ref, acc_ref = refs
        k = pl.program_id(2)
        part = lax.dot_general(a_ref[...].astype(BF16), b_ref[...].astype(BF16), dims, preferred_element_type=F32)

        @pl.when(k == 0)
        def _():
            acc_ref[...] = part

        @pl.when(k > 0)
        def _():
            acc_ref[...] += part

        @pl.when(k == nk - 1)
        def _():
            r = acc_ref[...]
            if has_add:
                r = r + add_ref[...]
            o_ref[...] = r.astype(o_ref.dtype)

    ins = [a, b] + ([add] if has_add else [])
    in_specs = [a_spec, b_spec] + ([o_spec] if has_add else [])
    return pl.pallas_call(
        body,
        out_shape=jax.ShapeDtypeStruct((M, N), out_dtype),
        grid=(M // tm_, N // tn_, nk),
        in_specs=in_specs,
        out_specs=o_spec,
        scratch_shapes=[pltpu.VMEM((tm_, tn_), F32)],
        name=name,
        compiler_params=_cp(("parallel", "parallel", "arbitrary")),
    )(*ins)


def _rms_fwd(x, g, out_dtype, name):
    M, C = x.shape
    tm = _row_tile(M, 640)

    def body(x_ref, g_ref, o_ref):
        xf = x_ref[...]
        r = lax.rsqrt(jnp.mean(xf * xf, axis=-1, keepdims=True) + EPS)
        o_ref[...] = ((xf * r) * g_ref[...]).astype(o_ref.dtype)

    return pl.pallas_call(
        body,
        out_shape=jax.ShapeDtypeStruct((M, C), out_dtype),
        grid=(M // tm,),
        in_specs=[pl.BlockSpec((tm, C), lambda i: (i, 0)), pl.BlockSpec((1, C), lambda i: (0, 0))],
        out_specs=pl.BlockSpec((tm, C), lambda i: (i, 0)),
        name=name,
        compiler_params=_cp(("parallel",)),
    )(x, g.reshape(1, C))


def _rms_bwd(x, g, dy, dres, name):
    M, C = x.shape
    tm = _row_tile(M, 640)
    has_res = dres is not None

    def body(*refs):
        if has_res:
            x_ref, g_ref, dy_ref, dres_ref, dx_ref, dg_ref = refs
        else:
            x_ref, g_ref, dy_ref, dx_ref, dg_ref = refs
        xf = x_ref[...]
        r = lax.rsqrt(jnp.mean(xf * xf, axis=-1, keepdims=True) + EPS)
        xhat = xf * r
        dyf = dy_ref[...].astype(F32)

        @pl.when(pl.program_id(0) == 0)
        def _():
            dg_ref[...] = jnp.zeros_like(dg_ref)

        dg_ref[...] += jnp.sum(dyf * xhat, axis=0, keepdims=True)
        dxh = dyf * g_ref[...]
        dx = r * (dxh - xhat * jnp.mean(dxh * xhat, axis=-1, keepdims=True))
        if has_res:
            dx = dx + dres_ref[...]
        dx_ref[...] = dx

    row = pl.BlockSpec((tm, C), lambda i: (i, 0))
    vec = pl.BlockSpec((1, C), lambda i: (0, 0))
    ins = [x, g.reshape(1, C), dy] + ([dres] if has_res else [])
    return pl.pallas_call(
        body,
        out_shape=(jax.ShapeDtypeStruct((M, C), F32), jax.ShapeDtypeStruct((1, C), F32)),
        grid=(M // tm,),
        in_specs=[row, vec, row] + ([row] if has_res else []),
        out_specs=(row, vec),
        name=name,
        compiler_params=_cp(("arbitrary",)),
    )(*ins)


def _sigmoid(x):
    return 1.0 / (1.0 + jnp.exp(-x))


def _ffn_up(b, wg, wu, name):
    M, K = b.shape
    N = wg.shape[1]
    tm, tn = _row_tile(M, 640), _col_tile(N, 1536)

    def body(b_ref, wg_ref, wu_ref, g_ref, u_ref, h_ref):
        bv = b_ref[...]
        g = jnp.dot(bv, wg_ref[...], preferred_element_type=F32)
        u = jnp.dot(bv, wu_ref[...], preferred_element_type=F32)
        g_ref[...] = g
        u_ref[...] = u
        h_ref[...] = ((g * _sigmoid(g)) * u).astype(h_ref.dtype)

    w_spec = pl.BlockSpec((K, tn), lambda i, j: (0, j))
    o_spec = pl.BlockSpec((tm, tn), lambda i, j: (i, j))
    f32 = jax.ShapeDtypeStruct((M, N), F32)
    return pl.pallas_call(
        body,
        out_shape=(f32, f32, jax.ShapeDtypeStruct((M, N), BF16)),
        grid=(M // tm, N // tn),
        in_specs=[pl.BlockSpec((tm, K), lambda i, j: (i, 0)), w_spec, w_spec],
        out_specs=(o_spec, o_spec, o_spec),
        name=name,
        compiler_params=_cp(("parallel", "parallel")),
    )(b, wg, wu)


def _ffn_down_bwd(dh, wd, g, u, name):
    M, K = dh.shape
    N = wd.shape[0]
    tm, tn = _row_tile(M, 640), _col_tile(N, 1536)

    def body(dh_ref, wd_ref, g_ref, u_ref, dg_ref, du_ref):
        d = lax.dot_general(dh_ref[...].astype(BF16), wd_ref[...], _NT, preferred_element_type=F32)
        gv = g_ref[...]
        sg = _sigmoid(gv)
        du_ref[...] = (d * (gv * sg)).astype(du_ref.dtype)
        dg_ref[...] = ((d * u_ref[...]) * (sg * (1.0 + gv * (1.0 - sg)))).astype(dg_ref.dtype)

    o_spec = pl.BlockSpec((tm, tn), lambda i, j: (i, j))
    bf = jax.ShapeDtypeStruct((M, N), BF16)
    return pl.pallas_call(
        body,
        out_shape=(bf, bf),
        grid=(M // tm, N // tn),
        in_specs=[pl.BlockSpec((tm, K), lambda i, j: (i, 0)), pl.BlockSpec((tn, K), lambda i, j: (j, 0)), o_spec, o_spec],
        out_specs=(o_spec, o_spec),
        name=name,
        compiler_params=_cp(("parallel", "parallel")),
    )(dh, wd, g, u)


ATTN_GROUP = 3


def _attn_tile(lp):
    return _col_tile(lp, 640)


def _head_specs(lp, t, offs):
    q_spec = pl.BlockSpec((t, LANES), lambda h, i: (i, offs[0] + h))
    k_spec = pl.BlockSpec((lp, LANES), lambda h, i: (0, offs[1] + h))
    v_spec = pl.BlockSpec((lp, LANES), lambda h, i: (0, offs[2] + h))
    return q_spec, k_spec, v_spec


def _attn_fwd(qa, ka, va, kb, fq, *, nh, offs, name, tile=640):
    lp = qa.shape[0]
    t = _col_tile(lp, tile)
    nq = lp // t
    has_fq = fq is not None

    def body(*refs):
        if has_fq:
            q_ref, k_ref, v_ref, kb_ref, fq_ref, o_ref, lse_ref = refs
        else:
            q_ref, k_ref, v_ref, kb_ref, o_ref, lse_ref = refs
        i = pl.program_id(1)
        q = q_ref[...]
        fqc = fq_ref[:, 0:1] if has_fq else None
        causal = lax.broadcasted_iota(jnp.int32, (t, t), 1) <= lax.broadcasted_iota(jnp.int32, (t, t), 0)

        def step(j, carry, masked):
            rows = pl.ds(pl.multiple_of(j * t, t), t)
            s = biased(lax.dot_general(q, k_ref[rows, :], _NT, preferred_element_type=F32), j)
            if masked:
                s = jnp.where(causal, s, NEG)
            return update(s, v_ref[rows, :], carry)

        def update(s, v, carry):
            m, l, acc = carry
            m_new = jnp.maximum(m, jnp.max(s, axis=1, keepdims=True))
            p = jnp.exp(s - m_new)
            alpha = jnp.exp(m - m_new)
            l = alpha * l + jnp.sum(p, axis=1, keepdims=True)
            acc = alpha * acc + jnp.dot(p.astype(BF16), v, preferred_element_type=F32)
            return m_new, l, acc

        def biased(s, j):
            bias = kb_ref[j]
            if has_fq:
                bias = fqc + bias
            return s + bias

        def group(g, carry):
            js = [ATTN_GROUP * g + u for u in range(ATTN_GROUP)]
            rws = [pl.ds(pl.multiple_of(j * t, t), t) for j in js]
            scores = [lax.dot_general(q, k_ref[r, :], _NT, preferred_element_type=F32) for r in rws]
            for j, r, s in zip(js, rws, scores):
                carry = update(biased(s, j), v_ref[r, :], carry)
            return carry

        init = (jnp.full((t, 1), NEG, F32), jnp.zeros((t, 1), F32), jnp.zeros((t, LANES), F32))
        carry = lax.fori_loop(0, i // ATTN_GROUP, group, init)
        done = (i // ATTN_GROUP) * ATTN_GROUP
        carry = lax.fori_loop(0, i % ATTN_GROUP, lambda u, c: step(done + u, c, False), carry)
        m, l, acc = step(i, carry, True)
        valid = (i * t + lax.broadcasted_iota(jnp.int32, (t, 1), 0)) >= PAD0
        o_ref[...] = jnp.where(valid, acc / l, 0.0).astype(o_ref.dtype)
        lse_ref[...] = jnp.broadcast_to(m + jnp.log(l), (t, LANES))

    q_spec, k_spec, v_spec = _head_specs(lp, t, offs)
    kb_spec = pl.BlockSpec((None, nq, 1, t), lambda h, i: (h, 0, 0, 0))
    row_spec = pl.BlockSpec((t, LANES), lambda h, i: (i, h))
    ins = [qa, ka, va, kb] + ([fq] if has_fq else [])
    return pl.pallas_call(
        body,
        out_shape=(jax.ShapeDtypeStruct((lp, nh * LANES), BF16), jax.ShapeDtypeStruct((lp, nh * LANES), F32)),
        grid=(nh, nq),
        in_specs=[q_spec, k_spec, v_spec, kb_spec] + ([row_spec] if has_fq else []),
        out_specs=(row_spec, row_spec),
        name=name,
        compiler_params=_cp(("parallel", "arbitrary")),
    )(*ins)


def _attn_bwd(qa, ka, va, kb, fq, o, do, lse, *, nh, offs, name, tile=640):
    lp = qa.shape[0]
    t = _col_tile(lp, tile)
    nq = lp // t
    has_fq = fq is not None

    def body(*refs):
        if has_fq:
            q_ref, k_ref, v_ref, kb_ref, fq_ref, o_ref, do_ref, lse_ref, dq_ref, dk_ref, dv_ref, dkb_ref, dqb_ref, dk_acc, dv_acc = refs
        else:
            q_ref, k_ref, v_ref, kb_ref, o_ref, do_ref, lse_ref, dq_ref, dk_ref, dv_ref, dk_acc, dv_acc = refs
        i = pl.program_id(1)

        @pl.when(i == 0)
        def _():
            dk_acc[...] = jnp.zeros_like(dk_acc)
            dv_acc[...] = jnp.zeros_like(dv_acc)
            if has_fq:
                dkb_ref[...] = jnp.zeros_like(dkb_ref)

        q = q_ref[...]
        dov = do_ref[...]
        delta = jnp.sum(o_ref[...].astype(F32) * dov.astype(F32), axis=1, keepdims=True)
        lse_c = lse_ref[:, 0:1]
        fqc = fq_ref[:, 0:1] if has_fq else None
        causal = lax.broadcasted_iota(jnp.int32, (t, t), 1) <= lax.broadcasted_iota(jnp.int32, (t, t), 0)

        def step(j, carry, masked):
            dq_acc, rs = carry
            st = pl.multiple_of(j * t, t)
            k = k_ref[pl.ds(st, t), :]
            v = v_ref[pl.ds(st, t), :]
            s = lax.dot_general(q, k, _NT, preferred_element_type=F32)
            bias = kb_ref[j]
            if has_fq:
                bias = fqc + bias
            s = s + bias
            if masked:
                s = jnp.where(causal, s, NEG)
            p = jnp.exp(s - lse_c)
            dp = lax.dot_general(dov, v, (((1,), (1,)), ((), ())), preferred_element_type=F32)
            ds = p * (dp - delta)
            dv_acc[pl.ds(st, t), :] += lax.dot_general(p.astype(BF16), dov, (((0,), (0,)), ((), ())), preferred_element_type=F32)
            dsb = ds.astype(BF16)
            dk_acc[pl.ds(st, t), :] += lax.dot_general(dsb, q, (((0,), (0,)), ((), ())), preferred_element_type=F32)
            if has_fq:
                dkb_ref[j] += jnp.sum(ds, axis=0, keepdims=True)
                rs = rs + jnp.sum(ds, axis=1, keepdims=True)
            return dq_acc + jnp.dot(dsb, k, preferred_element_type=F32), rs

        carry = (jnp.zeros((t, LANES), F32), jnp.zeros((t, 1), F32))
        carry = lax.fori_loop(0, i, lambda j, c: step(j, c, False), carry)
        dq_acc, rs = step(i, carry, True)
        dq_ref[...] = dq_acc.astype(dq_ref.dtype)
        if has_fq:
            dqb_ref[...] = jnp.broadcast_to(rs, (t, LANES))

        @pl.when(i == nq - 1)
        def _():
            dk_ref[...] = dk_acc[...].astype(dk_ref.dtype)
            dv_ref[...] = dv_acc[...].astype(dv_ref.dtype)

    q_spec, k_spec, v_spec = _head_specs(lp, t, offs)
    kb_spec = pl.BlockSpec((None, nq, 1, t), lambda h, i: (h, 0, 0, 0))
    row_spec = pl.BlockSpec((t, LANES), lambda h, i: (i, h))
    col_spec = pl.BlockSpec((lp, LANES), lambda h, i: (0, h))
    ins = [qa, ka, va, kb] + ([fq] if has_fq else []) + [o, do, lse]
    wide = jax.ShapeDtypeStruct((lp, nh * LANES), BF16)
    extra_shapes = (jax.ShapeDtypeStruct(kb.shape, F32), jax.ShapeDtypeStruct((lp, nh * LANES), F32)) if has_fq else ()
    extra_specs = (kb_spec, row_spec) if has_fq else ()
    return pl.pallas_call(
        body,
        out_shape=(wide, wide, wide) + extra_shapes,
        grid=(nh, nq),
        in_specs=[q_spec, k_spec, v_spec, kb_spec] + ([row_spec] if has_fq else []) + [row_spec, row_spec, row_spec],
        out_specs=(row_spec, col_spec, col_spec) + extra_specs,
        scratch_shapes=[pltpu.VMEM((lp, LANES), F32), pltpu.VMEM((lp, LANES), F32)],
        name=name,
        compiler_params=_cp(("parallel", "arbitrary")),
    )(*ins)


SB_TK = 128


def _split3(x):
    hi = x.astype(BF16)
    r1 = x - hi.astype(F32)
    mid = r1.astype(BF16)
    lo = (r1 - mid.astype(F32)).astype(BF16)
    return hi, mid, lo


SB_RC = 128
SB_UNROLL = 5


_NT = (((1,), (1,)), ((), ()))
_TN = (((0,), (0,)), ((), ()))


def _sb_logits(zraw, kbj, mask):
    z = kbj + zraw
    if mask is not None:
        z = jnp.where(mask, z, NEG)
    e = jnp.exp(-jnp.abs(z))
    g = jnp.minimum(z, 0.0) - jnp.log(1.0 + e)
    lk = g - z
    return z, e, g, lk


def _dot3_parts(parts, tri):
    d = functools.partial(jnp.dot, preferred_element_type=F32)
    return d(parts[0], tri) + d(parts[1], tri) + d(parts[2], tri)


def _split2(x):
    hi = x.astype(BF16)
    return hi, (x - hi.astype(F32)).astype(BF16)


def _dot_parts(parts, m):
    out = jnp.dot(parts[0], m, preferred_element_type=F32)
    for p in parts[1:]:
        out = out + jnp.dot(p, m, preferred_element_type=F32)
    return out


def _tri(n, pred):
    return pred(lax.broadcasted_iota(jnp.int32, (n, n), 0), lax.broadcasted_iota(jnp.int32, (n, n), 1)).astype(BF16)


def _sb_diag_chunks(jj, nrc, rc, tk):
    plan = []
    for r in range(nrc):
        lo_row, hi_row = r * rc, (r + 1) * rc - 1
        lo_col, hi_col = jj * tk, (jj + 1) * tk - 1
        if hi_row <= lo_col:
            plan.append(None)
        elif lo_row > hi_col:
            plan.append("all")
        else:
            plan.append(lo_col - lo_row)
    return plan


def _sb_fwd(qa, kb, *, nh, name, tq=640):
    lp = qa.shape[0]
    tq = _col_tile(lp, tq)
    tk = SB_TK
    rc = min(SB_RC, tq)
    nq, sub, nrc = lp // tq, tq // tk, tq // rc

    def body(q_ref, k_ref, v_ref, kb_ref, o_ref, c_scr, acc_scr):
        i = pl.program_id(1)
        c_scr[...] = jnp.zeros_like(c_scr)
        acc_scr[...] = jnp.zeros_like(acc_scr)
        tri = _tri(tk, lambda r, c: r > c)
        row_io = lax.broadcasted_iota(jnp.int32, (rc, tk), 0)
        col_io = lax.broadcasted_iota(jnp.int32, (rc, tk), 1)

        def scores(j, rows):
            k = k_ref[pl.ds(pl.multiple_of(j * tk, tk), tk), :]
            return [lax.dot_general(q_ref[rs, :], k, _NT, preferred_element_type=F32) for rs in rows]

        def weights(j, rows, masks, zs):
            kbj = kb_ref[j]
            gs, splits, firsts = [], [], []
            for mask, zraw in zip(masks, zs):
                _, _, g, lk = _sb_logits(zraw, kbj, mask)
                gs.append(g)
                firsts.append(lk[:, 0:1])
                splits.append(_split2(lk))
            sums = [_dot_parts(p, tri) for p in splits]
            avs = [jnp.exp(g + (sm + c_scr[rs, :])).astype(BF16) for g, sm, rs in zip(gs, sums, rows)]
            for rs, sm, first in zip(rows, sums, firsts):
                c_scr[rs, :] += jnp.broadcast_to(sm[:, 0:1] + first, (rc, tk))
            return avs

        def values(j, rows, avs):
            v = v_ref[pl.ds(pl.multiple_of(j * tk, tk), tk), :]
            pvs = [jnp.dot(a, v, preferred_element_type=F32) for a in avs]
            for rs, pv in zip(rows, pvs):
                acc_scr[rs, :] += pv

        for jj in reversed(range(sub)):
            plan = _sb_diag_chunks(jj, nrc, rc, tk)
            live = [r for r, what in enumerate(plan) if what is not None]
            rows = [pl.ds(r * rc, rc) for r in live]
            masks = [None if plan[r] == "all" else (col_io + plan[r]) < row_io for r in live]
            j = i * sub + jj
            values(j, rows, weights(j, rows, masks, scores(j, rows)))

        n = i * sub
        rows = [pl.ds(r * rc, rc) for r in range(nrc)]
        nomask = [None] * nrc

        def left(m, carry):
            js = [n - 1 - SB_UNROLL * m - t for t in range(SB_UNROLL)]
            zss = [scores(j, rows) for j in js]
            avss = [weights(j, rows, nomask, zs) for j, zs in zip(js, zss)]
            for j, avs in zip(js, avss):
                values(j, rows, avs)
            return carry

        lax.fori_loop(0, n // SB_UNROLL, left, 0)

        def tail(t, carry):
            j = n % SB_UNROLL - 1 - t
            values(j, rows, weights(j, rows, nomask, scores(j, rows)))
            return carry

        lax.fori_loop(0, n % SB_UNROLL, tail, 0)

        o_ref[...] = acc_scr[...].astype(o_ref.dtype)

    q_spec, k_spec, v_spec = _head_specs(lp, tq, (0, nh, 2 * nh))
    kb_spec = pl.BlockSpec((None, lp // tk, 1, tk), lambda h, i: (h, 0, 0, 0))
    row_spec = pl.BlockSpec((tq, LANES), lambda h, i: (i, h))
    return pl.pallas_call(
        body,
        out_shape=jax.ShapeDtypeStruct((lp, nh * LANES), BF16),
        grid=(nh, nq),
        in_specs=[q_spec, k_spec, v_spec, kb_spec],
        out_specs=row_spec,
        scratch_shapes=[pltpu.VMEM((tq, tk), F32), pltpu.VMEM((tq, LANES), F32)],
        name=name,
        compiler_params=_cp(("parallel", "arbitrary")),
    )(qa, qa, qa, kb)


def _sb_bwd(qa, kb, do, *, nh, name, tq=640):
    lp = qa.shape[0]
    tq = _col_tile(lp, tq)
    tk = SB_TK
    rc = min(SB_RC, tq)
    nq, nk, sub, nrc = lp // tq, lp // tk, tq // tk, tq // rc

    def body(q_ref, k_ref, v_ref, kb_ref, do_ref, dq_ref, dk_ref, dv_ref, dkt_acc, dvt_acc, w_scr, b_scr, c_scr, u_scr, dq_scr):
        i = pl.program_id(1)

        @pl.when(i == 0)
        def _():
            dkt_acc[...] = jnp.zeros_like(dkt_acc)
            dvt_acc[...] = jnp.zeros_like(dvt_acc)

        c_scr[...] = jnp.zeros_like(c_scr)
        u_scr[...] = jnp.zeros_like(u_scr)
        dq_scr[...] = jnp.zeros_like(dq_scr)
        tri_gt = _tri(tk, lambda r, c: r > c)
        tri_lt = _tri(tk, lambda r, c: r < c)
        row_io = lax.broadcasted_iota(jnp.int32, (rc, tk), 0)
        col_io = lax.broadcasted_iota(jnp.int32, (rc, tk), 1)
        qt = q_ref[...].astype(F32).T.astype(BF16)
        dot_t = do_ref[...].astype(F32).T.astype(BF16)
        zero_blk = jnp.zeros((rc, tk), BF16)

        def full_rows(plan, parts):
            it = iter(parts)
            return jnp.concatenate([zero_blk if what is None else next(it) for what in plan], axis=0)

        def key_rows(j):
            return pl.ds(pl.multiple_of(j * tk, tk), tk)

        def scores(j, rows):
            k = k_ref[key_rows(j), :]
            v = v_ref[key_rows(j), :]
            zs = [lax.dot_general(q_ref[rs, :], k, _NT, preferred_element_type=F32) for rs in rows]
            das = [lax.dot_general(do_ref[rs, :], v, _NT, preferred_element_type=F32) for rs in rows]
            return zs, das

        def weights(j, rows, masks, zs, das):
            kbj = kb_ref[j]
            gs, splits, firsts = [], [], []
            for rs, mask, zraw in zip(rows, masks, zs):
                z, e, g, lk = _sb_logits(zraw, kbj, mask)
                b_scr[j, rs, :] = (jnp.where(z >= 0.0, 1.0, e) / (1.0 + e)).astype(BF16)
                gs.append(g)
                firsts.append(lk[:, 0:1])
                splits.append(_split2(lk))
            sums = [_dot_parts(p, tri_gt) for p in splits]
            avs = []
            for rs, g, sm, da, first in zip(rows, gs, sums, das, firsts):
                a = jnp.exp(g + (sm + c_scr[rs, :]))
                w_scr[j, rs, :] = (a * da).astype(BF16)
                avs.append(a.astype(BF16))
                c_scr[rs, :] += jnp.broadcast_to(sm[:, 0:1] + first, (rc, tk))
            return avs

        def dv_update(j, plan, avs):
            dvt_acc[j] += jnp.dot(dot_t, full_rows(plan, avs), preferred_element_type=F32)

        for jj in reversed(range(sub)):
            plan = _sb_diag_chunks(jj, nrc, rc, tk)
            live = [r for r, what in enumerate(plan) if what is not None]
            rows = [pl.ds(r * rc, rc) for r in live]
            masks = [None if plan[r] == "all" else (col_io + plan[r]) < row_io for r in live]
            j = i * sub + jj
            dv_update(j, plan, weights(j, rows, masks, *scores(j, rows)))

        n = i * sub
        everything = ["all"] * nrc
        rows = [pl.ds(r * rc, rc) for r in range(nrc)]
        nomask = [None] * nrc
        def left1(m, carry):
            js = [n - 1 - SB_UNROLL * m - t for t in range(SB_UNROLL)]
            scs = [scores(j, rows) for j in js]
            avs = [weights(j, rows, nomask, *sc) for j, sc in zip(js, scs)]
            for j, av in zip(js, avs):
                dv_update(j, everything, av)
            return carry

        lax.fori_loop(0, n // SB_UNROLL, left1, 0)

        def tail1(t, carry):
            j = n % SB_UNROLL - 1 - t
            dv_update(j, everything, weights(j, rows, nomask, *scores(j, rows)))
            return carry

        lax.fori_loop(0, n % SB_UNROLL, tail1, 0)

        def prefix(j, rows):
            return [jnp.dot(w_scr[j, rs, :], tri_lt, preferred_element_type=F32) for rs in rows]

        def dlogits(j, rows, sums):
            dzs = []
            for rs, sm in zip(rows, sums):
                w = w_scr[j, rs, :].astype(F32)
                beta = b_scr[j, rs, :].astype(F32)
                dzs.append((w - beta * ((w + sm) + u_scr[rs, :])).astype(BF16))
                u_scr[rs, :] += jnp.broadcast_to(sm[:, tk - 1:tk] + w[:, tk - 1:tk], (rc, tk))
            return dzs

        def dqk_update(j, plan, rows, dzs):
            k = k_ref[key_rows(j), :]
            dqs = [jnp.dot(dz, k, preferred_element_type=F32) for dz in dzs]
            for rs, dq in zip(rows, dqs):
                dq_scr[rs, :] += dq
            dkt_acc[j] += jnp.dot(qt, full_rows(plan, dzs), preferred_element_type=F32)

        def left2(m, carry):
            js = [SB_UNROLL * m + t for t in range(SB_UNROLL)]
            sums = [prefix(j, rows) for j in js]
            dzs = [dlogits(j, rows, sm) for j, sm in zip(js, sums)]
            for j, dz in zip(js, dzs):
                dqk_update(j, everything, rows, dz)
            return carry

        lax.fori_loop(0, n // SB_UNROLL, left2, 0)

        def tail2(t, carry):
            j = (n // SB_UNROLL) * SB_UNROLL + t
            dqk_update(j, everything, rows, dlogits(j, rows, prefix(j, rows)))
            return carry

        lax.fori_loop(0, n % SB_UNROLL, tail2, 0)
        for jj in range(sub):
            plan = _sb_diag_chunks(jj, nrc, rc, tk)
            live_rows = [pl.ds(r * rc, rc) for r, what in enumerate(plan) if what is not None]
            j = i * sub + jj
            dqk_update(j, plan, live_rows, dlogits(j, live_rows, prefix(j, live_rows)))
        dq_ref[...] = dq_scr[...].astype(dq_ref.dtype)

        @pl.when(i == nq - 1)
        def _():
            def flush(j, carry):
                dk_ref[key_rows(j), :] = dkt_acc[j].T.astype(dk_ref.dtype)
                dv_ref[key_rows(j), :] = dvt_acc[j].T.astype(dv_ref.dtype)
                return carry

            lax.fori_loop(0, nk, flush, 0)

    q_spec, k_spec, v_spec = _head_specs(lp, tq, (0, nh, 2 * nh))
    kb_spec = pl.BlockSpec((None, nk, 1, tk), lambda h, i: (h, 0, 0, 0))
    row_spec = pl.BlockSpec((tq, LANES), lambda h, i: (i, h))
    col_spec = pl.BlockSpec((lp, LANES), lambda h, i: (0, h))
    wide = jax.ShapeDtypeStruct((lp, nh * LANES), BF16)
    return pl.pallas_call(
        body,
        out_shape=(wide, wide, wide),
        grid=(nh, nq),
        in_specs=[q_spec, k_spec, v_spec, kb_spec, row_spec],
        out_specs=(row_spec, col_spec, col_spec),
        scratch_shapes=[
            pltpu.VMEM((nk, LANES, tk), F32),
            pltpu.VMEM((nk, LANES, tk), F32),
            pltpu.VMEM((nk, tq, tk), BF16),
            pltpu.VMEM((nk, tq, tk), BF16),
            pltpu.VMEM((tq, tk), F32),
            pltpu.VMEM((tq, tk), F32),
            pltpu.VMEM((tq, LANES), F32),
        ],
        name=name,
        compiler_params=_cp(("parallel", "arbitrary")),
    )(qa, qa, qa, kb, do)


def _pool_counts(pos, win):
    return jnp.clip(pos + 1, 1, win).astype(F32)


def _pool_fwd(a, name):
    lp, C = a.shape
    tm = _row_tile(lp, 640)
    hb = tm // POOL_HALO

    def body(prev_ref, cur_ref, o_ref, xs):
        i = pl.program_id(0)
        xs[pl.ds(0, POOL_HALO), :] = jnp.where(i > 0, prev_ref[...], 0.0)
        xs[pl.ds(POOL_HALO, tm), :] = cur_ref[...]
        pos = i * tm + lax.broadcasted_iota(jnp.int32, (tm, 1), 0) - PAD0
        for g, win in enumerate(POOL_WINDOWS):
            cols = pl.ds(g * POOL_GROUP, POOL_GROUP)
            s = xs[pl.ds(POOL_HALO, tm), cols]
            for k in range(1, win):
                s = s + xs[pl.ds(POOL_HALO - k, tm), cols]
            o_ref[:, cols] = (s / _pool_counts(pos, win) - xs[pl.ds(POOL_HALO, tm), cols]).astype(o_ref.dtype)

    return pl.pallas_call(
        body,
        out_shape=jax.ShapeDtypeStruct((lp, C), BF16),
        grid=(lp // tm,),
        in_specs=[
            pl.BlockSpec((POOL_HALO, C), lambda i: (jnp.maximum(i * hb - 1, 0), 0)),
            pl.BlockSpec((tm, C), lambda i: (i, 0)),
        ],
        out_specs=pl.BlockSpec((tm, C), lambda i: (i, 0)),
        scratch_shapes=[pltpu.VMEM((tm + POOL_HALO, C), F32)],
        name=name,
        compiler_params=_cp(("parallel",)),
    )(a, a)


def _pool_bwd(dp, name):
    lp, C = dp.shape
    tm = _row_tile(lp, 640)
    hb = tm // POOL_HALO
    nt = lp // tm
    last_halo = lp // POOL_HALO - 1

    def body(cur_ref, next_ref, o_ref, xs):
        i = pl.program_id(0)
        pos = i * tm + lax.broadcasted_iota(jnp.int32, (tm, 1), 0) - PAD0
        pos_h = (i + 1) * tm + lax.broadcasted_iota(jnp.int32, (POOL_HALO, 1), 0) - PAD0
        for g, win in enumerate(POOL_WINDOWS):
            cols = pl.ds(g * POOL_GROUP, POOL_GROUP)
            cur = cur_ref[:, cols]
            xs[pl.ds(0, tm), cols] = cur / _pool_counts(pos, win)
            xs[pl.ds(tm, POOL_HALO), cols] = jnp.where(i < nt - 1, next_ref[:, cols], 0.0) / _pool_counts(pos_h, win)
            s = xs[pl.ds(0, tm), cols]
            for k in range(1, win):
                s = s + xs[pl.ds(k, tm), cols]
            o_ref[:, cols] = jnp.where(pos >= 0, s - cur, 0.0)

    return pl.pallas_call(
        body,
        out_shape=jax.ShapeDtypeStruct((lp, C), F32),
        grid=(nt,),
        in_specs=[
            pl.BlockSpec((tm, C), lambda i: (i, 0)),
            pl.BlockSpec((POOL_HALO, C), lambda i: (jnp.minimum((i + 1) * hb, last_halo), 0)),
        ],
        out_specs=pl.BlockSpec((tm, C), lambda i: (i, 0)),
        scratch_shapes=[pltpu.VMEM((tm + POOL_HALO, C), F32)],
        name=name,
        compiler_params=_cp(("parallel",)),
    )(dp, dp)


def _scale_add(h, pre, scale, name):
    M, C = h.shape
    tm = _row_tile(M, 640)

    def body(h_ref, p_ref, s_ref, o_ref):
        o_ref[...] = h_ref[...] + p_ref[...] * s_ref[...]

    row = pl.BlockSpec((tm, C), lambda i: (i, 0))
    return pl.pallas_call(
        body,
        out_shape=jax.ShapeDtypeStruct((M, C), F32),
        grid=(M // tm,),
        in_specs=[row, row, pl.BlockSpec((1, C), lambda i: (0, 0))],
        out_specs=row,
        name=name,
        compiler_params=_cp(("parallel",)),
    )(h, pre, scale.reshape(1, C))


def _scale_bwd(dh, pre, scale, name):
    M, C = dh.shape
    tm = _row_tile(M, 640)

    def body(dh_ref, p_ref, s_ref, dp_ref, ds_ref):
        @pl.when(pl.program_id(0) == 0)
        def _():
            ds_ref[...] = jnp.zeros_like(ds_ref)

        d = dh_ref[...]
        ds_ref[...] += jnp.sum(d * p_ref[...], axis=0, keepdims=True)
        dp_ref[...] = (d * s_ref[...]).astype(dp_ref.dtype)

    row = pl.BlockSpec((tm, C), lambda i: (i, 0))
    vec = pl.BlockSpec((1, C), lambda i: (0, 0))
    return pl.pallas_call(
        body,
        out_shape=(jax.ShapeDtypeStruct((M, C), BF16), jax.ShapeDtypeStruct((1, C), F32)),
        grid=(M // tm,),
        in_specs=[row, row, vec],
        out_specs=(row, vec),
        name=name,
        compiler_params=_cp(("arbitrary",)),
    )(dh, pre, scale.reshape(1, C))


def _gate_parts(z):
    e = jnp.exp(-jnp.abs(z))
    return e, jnp.minimum(z, 0.0) - jnp.log(1.0 + e)


def _tri_dot3(tri, x):
    hi, mid, lo = _split3(x)
    d = functools.partial(jnp.dot, preferred_element_type=F32)
    return d(tri, hi) + d(tri, mid) + d(tri, lo)


def _gate_fwd(x, b, name):
    lp, C = x.shape
    tm = _row_tile(lp, 640)

    def body(x_ref, b_ref, o_ref, carry):
        i = pl.program_id(0)

        @pl.when(i == 0)
        def _():
            carry[...] = jnp.zeros_like(carry)

        _, ls = _gate_parts(x_ref[...] + b_ref[...])
        rows = i * tm + lax.broadcasted_iota(jnp.int32, (tm, 1), 0)
        ls = jnp.where(rows >= PAD0, ls, 0.0)
        tri = (lax.broadcasted_iota(jnp.int32, (tm, tm), 0) >= lax.broadcasted_iota(jnp.int32, (tm, tm), 1)).astype(BF16)
        f = _tri_dot3(tri, ls) + carry[...]
        o_ref[...] = f
        carry[...] = f[tm - 1:tm, :]

    return pl.pallas_call(
        body,
        out_shape=jax.ShapeDtypeStruct((lp, C), F32),
        grid=(lp // tm,),
        in_specs=[pl.BlockSpec((tm, C), lambda i: (i, 0)), pl.BlockSpec((1, C), lambda i: (0, 0))],
        out_specs=pl.BlockSpec((tm, C), lambda i: (i, 0)),
        scratch_shapes=[pltpu.VMEM((1, C), F32)],
        name=name,
        compiler_params=_cp(("arbitrary",)),
    )(x, b)


def _gate_bwd(x, b, df, name):
    lp, C = x.shape
    tm = _row_tile(lp, 640)
    nt = lp // tm

    def body(x_ref, b_ref, df_ref, dx_ref, db_ref, carry):
        i = pl.program_id(0)

        @pl.when(i == 0)
        def _():
            carry[...] = jnp.zeros_like(carry)
            db_ref[...] = jnp.zeros_like(db_ref)

        z = x_ref[...] + b_ref[...]
        e, _ = _gate_parts(z)
        tri = (lax.broadcasted_iota(jnp.int32, (tm, tm), 0) <= lax.broadcasted_iota(jnp.int32, (tm, tm), 1)).astype(BF16)
        r = _tri_dot3(tri, df_ref[...]) + carry[...]
        carry[...] = r[0:1, :]
        rows = (nt - 1 - i) * tm + lax.broadcasted_iota(jnp.int32, (tm, 1), 0)
        dx = jnp.where(rows >= PAD0, r * (jnp.where(z >= 0.0, e, 1.0) / (1.0 + e)), 0.0)
        dx_ref[...] = dx
        db_ref[...] += jnp.sum(dx, axis=0, keepdims=True)

    rev = pl.BlockSpec((tm, C), lambda i: (nt - 1 - i, 0))
    vec = pl.BlockSpec((1, C), lambda i: (0, 0))
    return pl.pallas_call(
        body,
        out_shape=(jax.ShapeDtypeStruct((lp, C), F32), jax.ShapeDtypeStruct((1, C), F32)),
        grid=(nt,),
        in_specs=[rev, vec, rev],
        out_specs=(rev, vec),
        scratch_shapes=[pltpu.VMEM((1, C), F32)],
        name=name,
        compiler_params=_cp(("arbitrary",)),
    )(x, b, df)


MLA_SCALE = (MLA_NOPE + MLA_ROPE) ** -0.5


def _rope_apply(x, c, a, b):
    return x * c + pltpu.roll(x, LANES - 16, 1) * a + pltpu.roll(x, 16, 1) * b


def _rope_transpose(dy, c, a, b):
    return dy * c + pltpu.roll(dy * a, 16, 1) + pltpu.roll(dy * b, LANES - 16, 1)


def _mla_prep_fwd(q, kmat, kr, c, a, b, name):
    lp, W = q.shape
    nh = W // LANES
    tm = _row_tile(lp, 640)

    def body(q_ref, k_ref, kr_ref, c_ref, a_ref, b_ref, qo_ref, ko_ref):
        cv, av, bv = c_ref[...], a_ref[...], b_ref[...]
        qo_ref[...] = (_rope_apply(q_ref[...], cv, av, bv) * MLA_SCALE).astype(qo_ref.dtype)
        ko_ref[...] = (k_ref[...] + _rope_apply(kr_ref[...], cv, av, bv)).astype(ko_ref.dtype)

    head = pl.BlockSpec((tm, LANES), lambda i, h: (i, h))
    tab = pl.BlockSpec((tm, LANES), lambda i, h: (i, 0))
    wide = jax.ShapeDtypeStruct((lp, W), BF16)
    return pl.pallas_call(
        body,
        out_shape=(wide, wide),
        grid=(lp // tm, nh),
        in_specs=[head, head, tab, tab, tab, tab],
        out_specs=(head, head),
        name=name,
        compiler_params=_cp(("parallel", "parallel")),
    )(q, kmat, kr, c, a, b)


def _mla_prep_bwd(dq, dk, c, a, b, name):
    lp, W = dq.shape
    nh = W // LANES
    tm = _row_tile(lp, 640)

    def body(dq_ref, dk_ref, c_ref, a_ref, b_ref, dqo_ref, dkr_ref):
        cv, av, bv = c_ref[...], a_ref[...], b_ref[...]
        ksum = jnp.zeros((tm, LANES), F32)
        for h in range(nh):
            cols = pl.ds(h * LANES, LANES)
            dqo_ref[:, cols] = _rope_transpose(dq_ref[:, cols].astype(F32) * MLA_SCALE, cv, av, bv).astype(dqo_ref.dtype)
            ksum = ksum + dk_ref[:, cols].astype(F32)
        dkr_ref[...] = _rope_transpose(ksum, cv, av, bv)

    wide = pl.BlockSpec((tm, W), lambda i: (i, 0))
    tab = pl.BlockSpec((tm, LANES), lambda i: (i, 0))
    return pl.pallas_call(
        body,
        out_shape=(jax.ShapeDtypeStruct((lp, W), BF16), jax.ShapeDtypeStruct((lp, LANES), F32)),
        grid=(lp // tm,),
        in_specs=[wide, wide, tab, tab, tab],
        out_specs=(wide, tab),
        name=name,
        compiler_params=_cp(("parallel",)),
    )(dq, dk, c, a, b)


def _loss_head(h, g, target, name):
    lp, C = h.shape
    tm = _row_tile(lp, 640)
    nt = lp // tm

    def body(h_ref, g_ref, t_ref, loss_ref, dh_ref, dg_ref, sq):
        i = pl.program_id(0)

        @pl.when(i == 0)
        def _():
            dg_ref[...] = jnp.zeros_like(dg_ref)
            sq[...] = jnp.zeros_like(sq)

        xf = h_ref[...]
        gv = g_ref[...]
        r = lax.rsqrt(jnp.mean(xf * xf, axis=-1, keepdims=True) + EPS)
        xhat = xf * r
        rows = i * tm + lax.broadcasted_iota(jnp.int32, (tm, 1), 0)
        err = jnp.where(rows >= PAD0 + N_META, xhat * gv - t_ref[...], 0.0)
        sq[...] += jnp.sum(err * err, axis=0, keepdims=True)
        dy = err * (1.0 / C)
        dg_ref[...] += jnp.sum(dy * xhat, axis=0, keepdims=True)
        dxh = dy * gv
        dh_ref[...] = r * (dxh - xhat * jnp.mean(dxh * xhat, axis=-1, keepdims=True))

        @pl.when(i == nt - 1)
        def _():
            loss_ref[...] = jnp.broadcast_to(jnp.sum(sq[...], axis=1, keepdims=True) * (0.5 / C), (1, LANES))

    row = pl.BlockSpec((tm, C), lambda i: (i, 0))
    vec = pl.BlockSpec((1, C), lambda i: (0, 0))
    return pl.pallas_call(
        body,
        out_shape=(jax.ShapeDtypeStruct((1, LANES), F32), jax.ShapeDtypeStruct((lp, C), F32), jax.ShapeDtypeStruct((1, C), F32)),
        grid=(nt,),
        in_specs=[row, vec, row],
        out_specs=(pl.BlockSpec((1, LANES), lambda i: (0, 0)), row, vec),
        scratch_shapes=[pltpu.VMEM((1, C), F32)],
        name=name,
        compiler_params=_cp(("arbitrary",)),
    )(h, g.reshape(1, C), target)


def _adamw(w, g, m, v, name):
    shape = w.shape
    C = shape[-1]
    R = w.size // C
    tr = R
    if R % 8 == 0:
        for cand in range(8, R + 1, 8):
            if R % cand == 0 and cand * C * 4 <= (1 << 20):
                tr = cand
    c1 = 1.0 - ADAM_B1 ** ADAM_STEP
    c2 = 1.0 - ADAM_B2 ** ADAM_STEP

    def body(w_ref, g_ref, m_ref, v_ref, d_ref, nm_ref, nv_ref):
        gv = g_ref[...]
        nm = ADAM_B1 * m_ref[...] + (1.0 - ADAM_B1) * gv
        nv = ADAM_B2 * v_ref[...] + (1.0 - ADAM_B2) * (gv * gv)
        nm_ref[...] = nm
        nv_ref[...] = nv
        d_ref[...] = -ADAM_LR * ((nm / c1) / (jnp.sqrt(nv / c2) + ADAM_EPS) + ADAM_WD * w_ref[...])

    blk = pl.BlockSpec((tr, C), lambda i: (i, 0))
    out = jax.ShapeDtypeStruct((R, C), F32)
    outs = pl.pallas_call(
        body,
        out_shape=(out, out, out),
        grid=(R // tr,),
        in_specs=[blk] * 4,
        out_specs=(blk, blk, blk),
        name=name,
        compiler_params=_cp(("parallel",)),
    )(*(t.reshape(R, C) for t in (w, g, m, v)))
    return tuple(t.reshape(shape) for t in outs)


def _exchange(send, axes, same, name):
    na = len(axes)
    n = 1 << na
    _, R, C = send.shape
    parts = max(p for p in (8, 4, 2, 1) if R % (16 * p) == 0 or p == 1)
    pr = R // parts

    def body(send_ref, recv_ref, send_sems, recv_sems, local_sem):
        coords = {ax: lax.axis_index(ax) for ax in MESH_AXES}
        me = 0
        for ax in axes:
            me = me * 2 + coords[ax]

        def member(r):
            dev = dict(coords)
            for b, ax in enumerate(axes):
                if (r >> (na - 1 - b)) & 1:
                    dev[ax] = 1 - dev[ax]
            return tuple(dev[ax] for ax in MESH_AXES)

        def chunk(j, p):
            return (send_ref.at[0] if same else send_ref.at[j]).at[pl.ds(p * pr, pr)]

        def slot(j, p):
            return recv_ref.at[j].at[pl.ds(p * pr, pr)]

        own = pltpu.make_async_copy(send_ref.at[0] if same else send_ref.at[me], recv_ref.at[me], local_sem)
        own.start()
        copies = []
        for r in range(1, n):
            peer = me ^ r
            for p in range(parts):
                cp = pltpu.make_async_remote_copy(
                    src_ref=chunk(peer, p), dst_ref=slot(me, p), send_sem=send_sems.at[r, p], recv_sem=recv_sems.at[r, p],
                    device_id=member(r), device_id_type=pl.DeviceIdType.MESH)
                cp.start()
                copies.append(cp)
        for r in range(1, n):
            for p in range(parts):
                arrival = pltpu.make_async_remote_copy(
                    src_ref=chunk(me, p), dst_ref=slot(me ^ r, p), send_sem=send_sems.at[r, p], recv_sem=recv_sems.at[r, p],
                    device_id=member(r), device_id_type=pl.DeviceIdType.MESH)
                arrival.wait_recv()
        for cp in copies:
            cp.wait_send()
        own.wait()

    any_spec = pl.BlockSpec(memory_space=pl.ANY)
    return pl.pallas_call(
        body,
        out_shape=jax.ShapeDtypeStruct((n, R, C), send.dtype),
        in_specs=[any_spec],
        out_specs=any_spec,
        scratch_shapes=[pltpu.SemaphoreType.DMA((n, parts)), pltpu.SemaphoreType.DMA((n, parts)), pltpu.SemaphoreType.DMA],
        name=name,
        compiler_params=pltpu.CompilerParams(has_side_effects=True),
    )(send)


def _sum_chunks(x, name, out_dtype=F32):
    n, R, C = x.shape
    tr = _row_tile(R, 512)

    def body(x_ref, o_ref):
        acc = x_ref[0].astype(F32)
        for j in range(1, n):
            acc = acc + x_ref[j].astype(F32)
        o_ref[...] = acc.astype(o_ref.dtype)

    return pl.pallas_call(
        body,
        out_shape=jax.ShapeDtypeStruct((R, C), out_dtype),
        grid=(R // tr,),
        in_specs=[pl.BlockSpec((n, tr, C), lambda i: (0, i, 0))],
        out_specs=pl.BlockSpec((tr, C), lambda i: (i, 0)),
        name=name,
        compiler_params=_cp(("parallel",)),
    )(x)


def _pad_heads_cols(w, groups, d):
    k = w.shape[0]
    w = w.reshape(k, groups * N_HEADS, d)
    return jnp.pad(w, ((0, 0), (0, 0), (0, LANES - d))).reshape(k, groups * N_HEADS * LANES)


def _unpad_heads_cols(w, groups, d):
    k = w.shape[0]
    return w.reshape(k, groups * N_HEADS, LANES)[:, :, :d].reshape(k, groups * N_HEADS * d)


def _pad_heads_rows(w, d):
    n = w.shape[1]
    return jnp.pad(w.reshape(N_HEADS, d, n), ((0, 0), (0, LANES - d), (0, 0))).reshape(N_HEADS * LANES, n)


def _unpad_heads_rows(w, d):
    n = w.shape[1]
    return w.reshape(N_HEADS, LANES, n)[:, :d].reshape(N_HEADS * d, n)


Q_SCALE = HEAD_DIM ** -0.5


def _scale_q_cols(w):
    nq = N_HEADS * LANES
    return jnp.concatenate([w[:, :nq] * Q_SCALE, w[:, nq:]], axis=1)


def _kernel_weights(W):
    P = dict(W)
    pw = W["pool_w"][0]
    bd = jnp.zeros((D_MODEL, D_MODEL), pw.dtype)
    for g in range(len(POOL_WINDOWS)):
        bd = lax.dynamic_update_slice(bd, pw[g], (g * POOL_GROUP, g * POOL_GROUP))
    P["pool_bd"] = bd
    P["sb_qkv"] = _scale_q_cols(_pad_heads_cols(W["sb_w_qkv"][0], 3, HEAD_DIM))
    P["sb_o"] = _pad_heads_rows(W["sb_w_o"][0], HEAD_DIM)
    nq = 3 * N_HEADS * HEAD_DIM
    P["fox_qkv"] = _scale_q_cols(_pad_heads_cols(W["fox_w_qkvf"][0][:, :nq], 3, HEAD_DIM))
    P["fox_f"] = jnp.pad(W["fox_w_qkvf"][0][:, nq:], ((0, 0), (0, LANES - N_HEADS)))
    P["fox_o"] = _pad_heads_rows(W["fox_w_o"][0], HEAD_DIM)
    P["fox_b"] = jnp.pad(W["fox_b_f"], ((0, 0), (0, LANES - N_HEADS)))
    P["mla_down"] = jnp.pad(W["mla_w_down"][0], ((0, 0), (0, MLA_DOWN_PAD - W["mla_w_down"].shape[2])))
    P["mla_uq"] = _pad_heads_cols(W["mla_w_uq"][0], 1, MLA_NOPE + MLA_ROPE)
    ukv = W["mla_w_ukv"][0].reshape(MLA_KV_RANK, N_HEADS, 2 * HEAD_DIM)
    padk = ((0, 0), (0, 0), (0, LANES - HEAD_DIM))
    P["mla_ukv"] = jnp.concatenate(
        [jnp.pad(ukv[:, :, :MLA_NOPE], padk).reshape(MLA_KV_RANK, -1), jnp.pad(ukv[:, :, MLA_NOPE:], padk).reshape(MLA_KV_RANK, -1)], axis=1)
    P["mla_o"] = _pad_heads_rows(W["mla_w_o"][0], HEAD_DIM)
    return P


def _rope_tables(lp):
    pos = (jnp.arange(lp) - PAD0).astype(F32)
    inv = ROPE_THETA ** (-jnp.arange(0, MLA_ROPE, 2, dtype=F32) / MLA_ROPE)
    ang = pos[:, None] * inv[None, :]
    cos, sin = jnp.cos(ang), jnp.sin(ang)
    half = MLA_ROPE // 2
    z = lambda n: jnp.zeros((lp, n), F32)
    c = jnp.concatenate([jnp.ones((lp, MLA_NOPE), F32), cos, cos, z(LANES - MLA_NOPE - MLA_ROPE)], axis=1)
    a = jnp.concatenate([z(MLA_NOPE), -sin, z(LANES - MLA_NOPE - half)], axis=1)
    b = jnp.concatenate([z(MLA_NOPE + half), sin, z(LANES - MLA_NOPE - MLA_ROPE)], axis=1)
    return c, a, b


def _key_bias(lp, t, per_head=None):
    pad = jnp.arange(lp)[None, :] < PAD0
    body = jnp.zeros((N_HEADS, lp), F32) if per_head is None else per_head
    return jnp.where(pad, NEG, body).reshape(N_HEADS, lp // t, 1, t)


def _ffn_fwd(h, i, P):
    b = _rms_fwd(h, P["norm_ffn"][i], BF16, "ffn_norm")
    g, u, hd = _ffn_up(b, P["ffn_w_gate"][i], P["ffn_w_up"][i], "ffn_gate_up")
    return _mm(hd, P["ffn_w_down"][i], "nn", "ffn_down", add=h), (h, b, g, u, hd)


def _ffn_bwd(dh, i, P, saved):
    h, b, g, u, hd = saved
    dwd = _mm(hd, dh, "tn", "ffn_down_dw")
    dg, du = _ffn_down_bwd(dh, P["ffn_w_down"][i], g, u, "ffn_down_dx")
    dwg = _mm(b, dg, "tn", "ffn_gate_dw")
    dwu = _mm(b, du, "tn", "ffn_up_dw")
    db = _mm(dg, P["ffn_w_gate"][i], "nt", "ffn_gate_dx")
    db = _mm(du, P["ffn_w_up"][i], "nt", "ffn_up_dx", add=db)
    dh_in, dgain = _rms_bwd(h, P["norm_ffn"][i], db, dh, "ffn_norm_bwd")
    return dh_in, dgain, dwg, dwu, dwd


def _pool_layer_fwd(h, P):
    a = _rms_fwd(h, P["norm_mix"][0], F32, "pool_norm")
    pooled = _pool_fwd(a, "pool_window")
    pre = _mm(pooled, P["pool_bd"], "nn", "pool_mix")
    return _scale_add(h, pre, P["pool_scale"][0], "pool_scale_add"), (h, pooled, pre)


def _pool_layer_bwd(dh, P, saved):
    h, pooled, pre = saved
    dpre, dscale = _scale_bwd(dh, pre, P["pool_scale"][0], "pool_scale_bwd")
    dbd = _mm(pooled, dpre, "tn", "pool_mix_dw")
    dpooled = _mm(dpre, P["pool_bd"], "nt", "pool_mix_dx")
    da = _pool_bwd(dpooled, "pool_window_bwd")
    dh_in, dgain = _rms_bwd(h, P["norm_mix"][0], da, dh, "mix_norm_bwd")
    dw = jnp.stack([dbd[g * POOL_GROUP:(g + 1) * POOL_GROUP, g * POOL_GROUP:(g + 1) * POOL_GROUP] for g in range(len(POOL_WINDOWS))])
    return dh_in, {"norm_mix0": dgain, "pool_w": dw[None], "pool_scale": dscale}


def _out_proj_bwd(o, dh, wo, tag):
    return _mm(o, dh, "tn", tag + "_o_dw"), _mm(dh, wo, "nt", tag + "_o_dx", out_dtype=BF16)


def _sb_layer_fwd(h, P):
    lp = h.shape[0]
    a = _rms_fwd(h, P["norm_mix"][1], BF16, "mix_norm")
    qkv = _mm(a, P["sb_qkv"], "nn", "sb_qkv", out_dtype=BF16)
    kb = _key_bias(lp, SB_TK)
    o = _sb_fwd(qkv, kb, nh=N_HEADS, name="sb_attn")
    return _mm(o, P["sb_o"], "nn", "attn_out", add=h), (h, a, qkv, kb, o)


def _sb_layer_bwd(dh, P, saved):
    h, a, qkv, kb, o = saved
    dwo, do = _out_proj_bwd(o, dh, P["sb_o"], "attn")
    dq, dk, dv = _sb_bwd(qkv, kb, do, nh=N_HEADS, name="sb_attn_bwd")
    dqkv = jnp.concatenate([dq, dk, dv], axis=1)
    dw = _scale_q_cols(_mm(a, dqkv, "tn", "qkv_dw"))
    da = _mm(dqkv, P["sb_qkv"], "nt", "qkv_dx")
    dh_in, dgain = _rms_bwd(h, P["norm_mix"][1], da, dh, "mix_norm_bwd")
    return dh_in, {"norm_mix1": dgain, "sb_w_qkv": _unpad_heads_cols(dw, 3, HEAD_DIM)[None], "sb_w_o": _unpad_heads_rows(dwo, HEAD_DIM)[None]}


def _fox_layer_fwd(h, P):
    lp = h.shape[0]
    t = _attn_tile(lp)
    a = _rms_fwd(h, P["norm_mix"][3], BF16, "mix_norm")
    qkv = _mm(a, P["fox_qkv"], "nn", "sb_qkv", out_dtype=BF16)
    f = _mm(a, P["fox_f"], "nn", "fox_gate_proj")
    fc = _gate_fwd(f, P["fox_b"], "fox_gate")[:, :N_HEADS]
    kb = _key_bias(lp, t, -fc.T)
    fq = jnp.broadcast_to(fc[:, :, None], (lp, N_HEADS, LANES)).reshape(lp, N_HEADS * LANES)
    o, lse = _attn_fwd(qkv, qkv, qkv, kb, fq, nh=N_HEADS, offs=(0, N_HEADS, 2 * N_HEADS), name="fox_attn")
    return _mm(o, P["fox_o"], "nn", "attn_out", add=h), (h, a, qkv, f, kb, fq, o, lse)


def _fox_layer_bwd(dh, P, saved):
    h, a, qkv, f, kb, fq, o, lse = saved
    lp = h.shape[0]
    dwo, do = _out_proj_bwd(o, dh, P["fox_o"], "attn")
    dq, dk, dv, dkb, dqb = _attn_bwd(qkv, qkv, qkv, kb, fq, o, do, lse, nh=N_HEADS, offs=(0, N_HEADS, 2 * N_HEADS),
                                     name="fox_attn_bwd")
    dfc = jnp.pad(dqb.reshape(lp, N_HEADS, LANES)[:, :, 0] - dkb.reshape(N_HEADS, lp).T, ((0, 0), (0, LANES - N_HEADS)))
    df, dbf = _gate_bwd(f, P["fox_b"], dfc, "fox_gate_bwd")
    dqkv = jnp.concatenate([dq, dk, dv], axis=1)
    dw = _scale_q_cols(_mm(a, dqkv, "tn", "qkv_dw"))
    dwf = _mm(a, df, "tn", "fox_gate_dw")
    da = _mm(dqkv, P["fox_qkv"], "nt", "qkv_dx")
    da = _mm(df, P["fox_f"], "nt", "fox_gate_dx", add=da)
    dh_in, dgain = _rms_bwd(h, P["norm_mix"][3], da, dh, "mix_norm_bwd")
    dwqkvf = jnp.concatenate([_unpad_heads_cols(dw, 3, HEAD_DIM), dwf[:, :N_HEADS]], axis=1)
    return dh_in, {"norm_mix3": dgain, "fox_w_qkvf": dwqkvf[None], "fox_b_f": dbf[:, :N_HEADS], "fox_w_o": _unpad_heads_rows(dwo, HEAD_DIM)[None]}


def _mla_layer_fwd(h, P):
    lp = h.shape[0]
    a = _rms_fwd(h, P["norm_mix"][2], BF16, "mix_norm")
    down = _mm(a, P["mla_down"], "nn", "mla_down")
    cq_pre = down[:, :MLA_Q_RANK]
    ckv_pre = down[:, MLA_Q_RANK:MLA_Q_RANK + MLA_KV_RANK]
    kr = jnp.pad(down[:, MLA_Q_RANK + MLA_KV_RANK:MLA_Q_RANK + MLA_KV_RANK + MLA_ROPE], ((0, 0), (MLA_NOPE, LANES - MLA_NOPE - MLA_ROPE)))
    cq = _rms_fwd(cq_pre, P["mla_q_norm"][0], BF16, "mla_q_norm")
    ckv = _rms_fwd(ckv_pre, P["mla_kv_norm"][0], BF16, "mla_kv_norm")
    q = _mm(cq, P["mla_uq"], "nn", "mla_uq")
    kv = _mm(ckv, P["mla_ukv"], "nn", "mla_ukv", out_dtype=BF16)
    tabs = _rope_tables(lp)
    qr, kc = _mla_prep_fwd(q, kv, kr, *tabs, "mla_rope")
    kb = _key_bias(lp, _attn_tile(lp))
    o, lse = _attn_fwd(qr, kc, kv, kb, None, nh=N_HEADS, offs=(0, 0, N_HEADS), name="mla_attn")
    return _mm(o, P["mla_o"], "nn", "attn_out", add=h), (h, a, cq_pre, ckv_pre, cq, ckv, qr, kc, kv, tabs, kb, o, lse)


def _mla_layer_bwd(dh, P, saved):
    h, a, cq_pre, ckv_pre, cq, ckv, qr, kc, kv, tabs, kb, o, lse = saved
    lp = h.shape[0]
    dwo, do = _out_proj_bwd(o, dh, P["mla_o"], "attn")
    dqr, dkc, dv = _attn_bwd(qr, kc, kv, kb, None, o, do, lse, nh=N_HEADS, offs=(0, 0, N_HEADS),
                             name="mla_attn_bwd")
    dq, dkr = _mla_prep_bwd(dqr, dkc, *tabs, "mla_rope_bwd")
    dkv = jnp.concatenate([dkc, dv], axis=1)
    dwuq = _mm(cq, dq, "tn", "mla_uq_dw")
    dcq = _mm(dq, P["mla_uq"], "nt", "mla_uq_dx")
    dwukv = _mm(ckv, dkv, "tn", "mla_ukv_dw")
    dckv = _mm(dkv, P["mla_ukv"], "nt", "mla_ukv_dx")
    dcq_pre, dqn = _rms_bwd(cq_pre, P["mla_q_norm"][0], dcq, None, "mla_q_norm_bwd")
    dckv_pre, dkvn = _rms_bwd(ckv_pre, P["mla_kv_norm"][0], dckv, None, "mla_kv_norm_bwd")
    used = MLA_Q_RANK + MLA_KV_RANK + MLA_ROPE
    ddown = jnp.concatenate([dcq_pre, dckv_pre, dkr[:, MLA_NOPE:MLA_NOPE + MLA_ROPE], jnp.zeros((lp, MLA_DOWN_PAD - used), F32)], axis=1)
    dwdown = _mm(a, ddown, "tn", "mla_down_dw")
    da = _mm(ddown, P["mla_down"], "nt", "mla_down_dx")
    dh_in, dgain = _rms_bwd(h, P["norm_mix"][2], da, dh, "mix_norm_bwd")
    dukv = dwukv.reshape(MLA_KV_RANK, 2, N_HEADS, LANES)[:, :, :, :HEAD_DIM]
    dukv = jnp.concatenate([dukv[:, 0], dukv[:, 1]], axis=-1).reshape(MLA_KV_RANK, N_HEADS * 2 * HEAD_DIM)
    return dh_in, {
        "norm_mix2": dgain, "mla_w_down": dwdown[:, :used][None], "mla_q_norm": dqn, "mla_kv_norm": dkvn,
        "mla_w_uq": _unpad_heads_cols(dwuq, 1, MLA_NOPE + MLA_ROPE)[None], "mla_w_ukv": dukv[None],
        "mla_w_o": _unpad_heads_rows(dwo, HEAD_DIM)[None]}


_MIXERS = ((_pool_layer_fwd, _pool_layer_bwd), (_sb_layer_fwd, _sb_layer_bwd), (_mla_layer_fwd, _mla_layer_bwd), (_fox_layer_fwd, _fox_layer_bwd))


def _step_local(x, target, W):
    seq = x.shape[0]
    P = _kernel_weights(W)
    h = jnp.concatenate([jnp.zeros((PAD0, D_MODEL), F32), W["meta"], x], axis=0)
    tpad = jnp.pad(target, ((PAD0 + N_META, 0), (0, 0)))
    saved = []
    for i in range(4):
        h, s_mix = _MIXERS[i][0](h, P)
        h, s_ffn = _ffn_fwd(h, i, P)
        saved.append((s_mix, s_ffn))
    loss, dh, dfinal = _loss_head(h, W["final_norm"], tpad, "loss_head")
    grads = {"final_norm": dfinal.reshape(-1)}
    gains_mix, gains_ffn, dwg, dwu, dwd = [None] * 4, [None] * 4, [None] * 4, [None] * 4, [None] * 4
    for i in reversed(range(4)):
        s_mix, s_ffn = saved[i]
        dh, gains_ffn[i], dwg[i], dwu[i], dwd[i] = _ffn_bwd(dh, i, P, s_ffn)
        dh, g = _MIXERS[i][1](dh, P, s_mix)
        gains_mix[i] = g.pop("norm_mix%d" % i)
        grads.update(g)
    grads["norm_mix"] = jnp.concatenate(gains_mix, axis=0)
    grads["norm_ffn"] = jnp.concatenate(gains_ffn, axis=0)
    grads["ffn_w_gate"] = jnp.stack(dwg)
    grads["ffn_w_up"] = jnp.stack(dwu)
    grads["ffn_w_down"] = jnp.stack(dwd)
    grads["meta"] = dh[PAD0:PAD0 + N_META]
    return loss, dh[PAD0 + N_META:], grads


_WEIGHTS = ("meta", "norm_mix", "norm_ffn", "pool_w", "pool_scale", "sb_w_qkv", "sb_w_o", "mla_w_down", "mla_q_norm",
            "mla_kv_norm", "mla_w_uq", "mla_w_ukv", "mla_w_o", "fox_w_qkvf", "fox_b_f", "fox_w_o", "ffn_w_gate",
            "ffn_w_up", "ffn_w_down", "final_norm")
_SHARD_AXIS = {"meta": 1, "pool_w": 2, "sb_w_qkv": 2, "sb_w_o": 1, "mla_w_down": 1, "mla_q_norm": 1, "mla_kv_norm": 1,
               "mla_w_uq": 2, "mla_w_ukv": 2, "mla_w_o": 1, "fox_w_qkvf": 2, "fox_b_f": None, "fox_w_o": 1,
               "ffn_w_gate": 2, "ffn_w_up": 2, "ffn_w_down": 1}
_SHARDED = tuple(n for n in _WEIGHTS if _SHARD_AXIS.get(n) is not None)
_REPLICATED = tuple(n for n in _WEIGHTS if _SHARD_AXIS.get(n) is None)
_EXACT = ("meta", "mla_q_norm", "mla_kv_norm")
N_CHIPS = 4
GRAD_ROW_TILE = 512


PACK_ROWS = 16


def _piece_rows(t):
    return -(-t.size // (LANES * PACK_ROWS)) * PACK_ROWS


def _flat_rows(parts, dtype, row_multiple):
    pieces = []
    for p in parts:
        flat = p.astype(dtype).reshape(-1)
        pieces.append(jnp.pad(flat, (0, _piece_rows(p) * LANES - flat.shape[0])).reshape(-1, LANES))
    rows = sum(q.shape[0] for q in pieces)
    pad = -(-rows // row_multiple) * row_multiple - rows
    if pad:
        pieces.append(jnp.zeros((pad, LANES), dtype))
    return jnp.concatenate(pieces, axis=0)


def _split_flat(flat, like):
    out, off = [], 0
    for t in like:
        out.append(flat[off:off + _piece_rows(t)].reshape(-1)[:t.size].reshape(t.shape))
        off += _piece_rows(t)
    return out


def _gather_shards(local, names, dtype, name):
    blocks = [local[n] for n in names]
    recv = _exchange(_flat_rows(blocks, dtype, PACK_ROWS)[None], ("x", "y"), True, name)
    per_chip = [_split_flat(recv[s], blocks) for s in range(N_CHIPS)]
    return {n: jnp.concatenate([per_chip[s][k] for s in range(N_CHIPS)], axis=_SHARD_AXIS[n]) for k, n in enumerate(names)}


def _shard_of(g, n, s):
    w = g.shape[_SHARD_AXIS[n]] // N_CHIPS
    return lax.slice_in_dim(g, s * w, (s + 1) * w, axis=_SHARD_AXIS[n])


def _train_step(a):
    local = {n: a[n] for n in _WEIGHTS}
    full = {n: local[n] for n in _REPLICATED}
    full.update(_gather_shards(local, [n for n in _SHARDED if n not in _EXACT], BF16, "gather_weights"))
    full.update(_gather_shards(local, list(_EXACT), F32, "gather_exact"))

    loss, grad_x, grads = _step_local(a["x"][0], a["loss_target"][0], full)

    send = jnp.stack([
        _flat_rows([_shard_of(grads[n], n, s) for n in _SHARDED], BF16, 2 * GRAD_ROW_TILE).reshape(2, -1, LANES)
        for s in range(N_CHIPS)]).reshape(2 * N_CHIPS, -1, LANES)
    mine = _sum_chunks(_exchange(send, MESH_AXES, False, "scatter_grads"), "sum_grads", out_dtype=BF16)
    both = _exchange(mine[None], ("c",), True, "pair_grads").reshape(-1, LANES).astype(F32)
    reduced = dict(zip(_SHARDED, _split_flat(both, [local[n] for n in _SHARDED])))
    small = _flat_rows([grads[n] for n in _REPLICATED], F32, 8)
    small = _sum_chunks(_exchange(small[None], MESH_AXES, True, "gather_small_grads"), "sum_small_grads")
    reduced.update(zip(_REPLICATED, _split_flat(small, [local[n] for n in _REPLICATED])))

    deltas, new_m, new_v = {}, {}, {}
    for n in _WEIGHTS:
        deltas[n], new_m[n], new_v[n] = _adamw(local[n], reduced[n], a["m_" + n], a["v_" + n], "adamw")
    total = lax.psum(loss[0, 0], MESH_AXES)
    return (total, grad_x[None], *[reduced[n] for n in _WEIGHTS], *[deltas[n] for n in _WEIGHTS],
            *[new_m[n] for n in _WEIGHTS], *[new_v[n] for n in _WEIGHTS])


def kernel(x, meta, norm_mix, norm_ffn, pool_w, pool_scale, sb_w_qkv, sb_w_o, mla_w_down, mla_q_norm, mla_kv_norm, mla_w_uq, mla_w_ukv, mla_w_o, fox_w_qkvf, fox_b_f, fox_w_o, ffn_w_gate, ffn_w_up, ffn_w_down, final_norm, loss_target, m_meta, m_norm_mix, m_norm_ffn, m_pool_w, m_pool_scale, m_sb_w_qkv, m_sb_w_o, m_mla_w_down, m_mla_q_norm, m_mla_kv_norm, m_mla_w_uq, m_mla_w_ukv, m_mla_w_o, m_fox_w_qkvf, m_fox_b_f, m_fox_w_o, m_ffn_w_gate, m_ffn_w_up, m_ffn_w_down, m_final_norm, v_meta, v_norm_mix, v_norm_ffn, v_pool_w, v_pool_scale, v_sb_w_qkv, v_sb_w_o, v_mla_w_down, v_mla_q_norm, v_mla_kv_norm, v_mla_w_uq, v_mla_w_ukv, v_mla_w_o, v_fox_w_qkvf, v_fox_b_f, v_fox_w_o, v_ffn_w_gate, v_ffn_w_up, v_ffn_w_down, v_final_norm):
    return _train_step(dict(locals()))
```

```python
import functools

import jax
import jax.numpy as jnp
from jax import lax
from jax.experimental import pallas as pl
from jax.experimental.pallas import tpu as pltpu

F32 = jnp.float32
BF16 = jnp.bfloat16

D_MODEL = 1024
N_META = 16
PAD0 = 112
LANES = 128
N_HEADS = 16
HEAD_DIM = 64
POOL_WINDOWS = (2, 4, 8, 16)
POOL_GROUP = 256
POOL_HALO = 16
MLA_Q_RANK = 384
MLA_KV_RANK = 256
MLA_NOPE = 64
MLA_ROPE = 32
MLA_DOWN_PAD = 768
ROPE_THETA = 10000.0
D_FF = 2816
EPS = 1e-6
NEG = -1e30
ADAM_LR = 0.001
ADAM_B1 = 0.9
ADAM_B2 = 0.999
ADAM_EPS = 1e-08
ADAM_WD = 0.01
ADAM_STEP = 10
VMEM_LIMIT = 56 * 1024 * 1024
MESH_AXES = ("x", "y", "c")


def _cp(sem, **kw):
    return pltpu.CompilerParams(dimension_semantics=sem, vmem_limit_bytes=VMEM_LIMIT, **kw)


def _row_tile(m, target):
    best = None
    for t in range(16, min(m, target) + 1, 16):
        if m % t == 0:
            best = t
    return best or m


def _col_tile(n, target):
    best = None
    for t in range(LANES, min(n, target) + 1, LANES):
        if n % t == 0:
            best = t
    return best or n


def _mm(a, b, mode, name, out_dtype=F32, add=None, tm=640, tn=1536, tk=2048):
    if mode == "nn":
        (M, K), (K2, N) = a.shape, b.shape
    elif mode == "nt":
        (M, K), (N, K2) = a.shape, b.shape
    else:
        (K, M), (K2, N) = a.shape, b.shape
    assert K == K2, (mode, a.shape, b.shape)
    if mode == "tn":
        tm_ = _col_tile(M, 1408)
        tk_ = _row_tile(K, 1664)
    else:
        tm_ = _row_tile(M, tm)
        tk_ = _col_tile(K, tk) if K > tk else K
    tn_ = _col_tile(N, tn)
    nk = K // tk_
    if mode == "nn":
        a_spec = pl.BlockSpec((tm_, tk_), lambda i, j, k: (i, k))
        b_spec = pl.BlockSpec((tk_, tn_), lambda i, j, k: (k, j))
        dims = (((1,), (0,)), ((), ()))
    elif mode == "nt":
        a_spec = pl.BlockSpec((tm_, tk_), lambda i, j, k: (i, k))
        b_spec = pl.BlockSpec((tn_, tk_), lambda i, j, k: (j, k))
        dims = (((1,), (1,)), ((), ()))
    else:
        a_spec = pl.BlockSpec((tk_, tm_), lambda i, j, k: (k, i))
        b_spec = pl.BlockSpec((tk_, tn_), lambda i, j, k: (k, j))
        dims = (((0,), (0,)), ((), ()))
    o_spec = pl.BlockSpec((tm_, tn_), lambda i, j, k: (i, j))
    has_add = add is not None

    def body(*refs):
        if has_add:
            a_ref, b_ref, add_ref, o_ref, acc_ref = refs
        else:
            a_ref, b_ref, o_ref, acc_ref = refs
        k = pl.program_id(2)
        part = lax.dot_general(a_ref[...].astype(BF16), b_ref[...].astype(BF16), dims, preferred_element_type=F32)

        @pl.when(k == 0)
        def _():
            acc_ref[...] = part

        @pl.when(k > 0)
        def _():
            acc_ref[...] += part

        @pl.when(k == nk - 1)
        def _():
            r = acc_ref[...]
            if has_add:
                r = r + add_ref[...]
            o_ref[...] = r.astype(o_ref.dtype)

    ins = [a, b] + ([add] if has_add else [])
    in_specs = [a_spec, b_spec] + ([o_spec] if has_add else [])
    return pl.pallas_call(
        body,
        out_shape=jax.ShapeDtypeStruct((M, N), out_dtype),
        grid=(M // tm_, N // tn_, nk),
        in_specs=in_specs,
        out_specs=o_spec,
        scratch_shapes=[pltpu.VMEM((tm_, tn_), F32)],
        name=name,
        compiler_params=_cp(("parallel", "parallel", "arbitrary")),
    )(*ins)


def _rms_fwd(x, g, out_dtype, name):
    M, C = x.shape
    tm = _row_tile(M, 640)

    def body(x_ref, g_ref, o_ref):
        xf = x_ref[...]
        r = lax.rsqrt(jnp.mean(xf * xf, axis=-1, keepdims=True) + EPS)
        o_ref[...] = ((xf * r) * g_ref[...]).astype(o_ref.dtype)

    return pl.pallas_call(
        body,
        out_shape=jax.ShapeDtypeStruct((M, C), out_dtype),
        grid=(M // tm,),
        in_specs=[pl.BlockSpec((tm, C), lambda i: (i, 0)), pl.BlockSpec((1, C), lambda i: (0, 0))],
        out_specs=pl.BlockSpec((tm, C), lambda i: (i, 0)),
        name=name,
        compiler_params=_cp(("parallel",)),
    )(x, g.reshape(1, C))


def _rms_bwd(x, g, dy, dres, name):
    M, C = x.shape
    tm = _row_tile(M, 640)
    has_res = dres is not None

    def body(*refs):
        if has_res:
            x_ref, g_ref, dy_ref, dres_ref, dx_ref, dg_ref = refs
        else:
            x_ref, g_ref, dy_ref, dx_ref, dg_ref = refs
        xf = x_ref[...]
        r = lax.rsqrt(jnp.mean(xf * xf, axis=-1, keepdims=True) + EPS)
        xhat = xf * r
        dyf = dy_ref[...].astype(F32)

        @pl.when(pl.program_id(0) == 0)
        def _():
            dg_ref[...] = jnp.zeros_like(dg_ref)

        dg_ref[...] += jnp.sum(dyf * xhat, axis=0, keepdims=True)
        dxh = dyf * g_ref[...]
        dx = r * (dxh - xhat * jnp.mean(dxh * xhat, axis=-1, keepdims=True))
        if has_res:
            dx = dx + dres_ref[...]
        dx_ref[...] = dx

    row = pl.BlockSpec((tm, C), lambda i: (i, 0))
    vec = pl.BlockSpec((1, C), lambda i: (0, 0))
    ins = [x, g.reshape(1, C), dy] + ([dres] if has_res else [])
    return pl.pallas_call(
        body,
        out_shape=(jax.ShapeDtypeStruct((M, C), F32), jax.ShapeDtypeStruct((1, C), F32)),
        grid=(M // tm,),
        in_specs=[row, vec, row] + ([row] if has_res else []),
        out_specs=(row, vec),
        name=name,
        compiler_params=_cp(("arbitrary",)),
    )(*ins)


def _sigmoid(x):
    return 1.0 / (1.0 + jnp.exp(-x))


def _ffn_up(b, wg, wu, name):
    M, K = b.shape
    N = wg.shape[1]
    tm, tn = _row_tile(M, 640), _col_tile(N, 1536)

    def body(b_ref, wg_ref, wu_ref, g_ref, u_ref, h_ref):
        bv = b_ref[...]
        g = jnp.dot(bv, wg_ref[...], preferred_element_type=F32)
        u = jnp.dot(bv, wu_ref[...], preferred_element_type=F32)
        g_ref[...] = g
        u_ref[...] = u
        h_ref[...] = ((g * _sigmoid(g)) * u).astype(h_ref.dtype)

    w_spec = pl.BlockSpec((K, tn), lambda i, j: (0, j))
    o_spec = pl.BlockSpec((tm, tn), lambda i, j: (i, j))
    f32 = jax.ShapeDtypeStruct((M, N), F32)
    return pl.pallas_call(
        body,
        out_shape=(f32, f32, jax.ShapeDtypeStruct((M, N), BF16)),
        grid=(M // tm, N // tn),
        in_specs=[pl.BlockSpec((tm, K), lambda i, j: (i, 0)), w_spec, w_spec],
        out_specs=(o_spec, o_spec, o_spec),
        name=name,
        compiler_params=_cp(("parallel", "parallel")),
    )(b, wg, wu)


def _ffn_down_bwd(dh, wd, g, u, name):
    M, K = dh.shape
    N = wd.shape[0]
    tm, tn = _row_tile(M, 640), _col_tile(N, 1536)

    def body(dh_ref, wd_ref, g_ref, u_ref, dg_ref, du_ref):
        d = lax.dot_general(dh_ref[...].astype(BF16), wd_ref[...], _NT, preferred_element_type=F32)
        gv = g_ref[...]
        sg = _sigmoid(gv)
        du_ref[...] = (d * (gv * sg)).astype(du_ref.dtype)
        dg_ref[...] = ((d * u_ref[...]) * (sg * (1.0 + gv * (1.0 - sg)))).astype(dg_ref.dtype)

    o_spec = pl.BlockSpec((tm, tn), lambda i, j: (i, j))
    bf = jax.ShapeDtypeStruct((M, N), BF16)
    return pl.pallas_call(
        body,
        out_shape=(bf, bf),
        grid=(M // tm, N // tn),
        in_specs=[pl.BlockSpec((tm, K), lambda i, j: (i, 0)), pl.BlockSpec((tn, K), lambda i, j: (j, 0)), o_spec, o_spec],
        out_specs=(o_spec, o_spec),
        name=name,
        compiler_params=_cp(("parallel", "parallel")),
    )(dh, wd, g, u)


ATTN_GROUP = 3


def _attn_tile(lp):
    return _col_tile(lp, 640)


def _head_specs(lp, t, offs):
    q_spec = pl.BlockSpec((t, LANES), lambda h, i: (i, offs[0] + h))
    k_spec = pl.BlockSpec((lp, LANES), lambda h, i: (0, offs[1] + h))
    v_spec = pl.BlockSpec((lp, LANES), lambda h, i: (0, offs[2] + h))
    return q_spec, k_spec, v_spec


def _attn_fwd(qa, ka, va, kb, fq, *, nh, offs, name, tile=640):
    lp = qa.shape[0]
    t = _col_tile(lp, tile)
    nq = lp // t
    has_fq = fq is not None

    def body(*refs):
        if has_fq:
            q_ref, k_ref, v_ref, kb_ref, fq_ref, o_ref, lse_ref = refs
        else:
            q_ref, k_ref, v_ref, kb_ref, o_ref, lse_ref = refs
        i = pl.program_id(1)
        q = q_ref[...]
        fqc = fq_ref[:, 0:1] if has_fq else None
        causal = lax.broadcasted_iota(jnp.int32, (t, t), 1) <= lax.broadcasted_iota(jnp.int32, (t, t), 0)

        def step(j, carry, masked):
            rows = pl.ds(pl.multiple_of(j * t, t), t)
            s = biased(lax.dot_general(q, k_ref[rows, :], _NT, preferred_element_type=F32), j)
            if masked:
                s = jnp.where(causal, s, NEG)
            return update(s, v_ref[rows, :], carry)

        def update(s, v, carry):
            m, l, acc = carry
            m_new = jnp.maximum(m, jnp.max(s, axis=1, keepdims=True))
            p = jnp.exp(s - m_new)
            alpha = jnp.exp(m - m_new)
            l = alpha * l + jnp.sum(p, axis=1, keepdims=True)
            acc = alpha * acc + jnp.dot(p.astype(BF16), v, preferred_element_type=F32)
            return m_new, l, acc

        def biased(s, j):
            bias = kb_ref[j]
            if has_fq:
                bias = fqc + bias
            return s + bias

        def group(g, carry):
            js = [ATTN_GROUP * g + u for u in range(ATTN_GROUP)]
            rws = [pl.ds(pl.multiple_of(j * t, t), t) for j in js]
            scores = [lax.dot_general(q, k_ref[r, :], _NT, preferred_element_type=F32) for r in rws]
            for j, r, s in zip(js, rws, scores):
                carry = update(biased(s, j), v_ref[r, :], carry)
            return carry

        init = (jnp.full((t, 1), NEG, F32), jnp.zeros((t, 1), F32), jnp.zeros((t, LANES), F32))
        carry = lax.fori_loop(0, i // ATTN_GROUP, group, init)
        done = (i // ATTN_GROUP) * ATTN_GROUP
        carry = lax.fori_loop(0, i % ATTN_GROUP, lambda u, c: step(done + u, c, False), carry)
        m, l, acc = step(i, carry, True)
        valid = (i * t + lax.broadcasted_iota(jnp.int32, (t, 1), 0)) >= PAD0
        o_ref[...] = jnp.where(valid, acc / l, 0.0).astype(o_ref.dtype)
        lse_ref[...] = jnp.broadcast_to(m + jnp.log(l), (t, LANES))

    q_spec, k_spec, v_spec = _head_specs(lp, t, offs)
    kb_spec = pl.BlockSpec((None, nq, 1, t), lambda h, i: (h, 0, 0, 0))
    row_spec = pl.BlockSpec((t, LANES), lambda h, i: (i, h))
    ins = [qa, ka, va, kb] + ([fq] if has_fq else [])
    return pl.pallas_call(
        body,
        out_shape=(jax.ShapeDtypeStruct((lp, nh * LANES), BF16), jax.ShapeDtypeStruct((lp, nh * LANES), F32)),
        grid=(nh, nq),
        in_specs=[q_spec, k_spec, v_spec, kb_spec] + ([row_spec] if has_fq else []),
        out_specs=(row_spec, row_spec),
        name=name,
        compiler_params=_cp(("parallel", "arbitrary")),
    )(*ins)


def _attn_bwd(qa, ka, va, kb, fq, o, do, lse, *, nh, offs, name, tile=640):
    lp = qa.shape[0]
    t = _col_tile(lp, tile)
    nq = lp // t
    has_fq = fq is not None

    def body(*refs):
        if has_fq:
            q_ref, k_ref, v_ref, kb_ref, fq_ref, o_ref, do_ref, lse_ref, dq_ref, dk_ref, dv_ref, dkb_ref, dqb_ref, dk_acc, dv_acc = refs
        else:
            q_ref, k_ref, v_ref, kb_ref, o_ref, do_ref, lse_ref, dq_ref, dk_ref, dv_ref, dk_acc, dv_acc = refs
        i = pl.program_id(1)

        @pl.when(i == 0)
        def _():
            dk_acc[...] = jnp.zeros_like(dk_acc)
            dv_acc[...] = jnp.zeros_like(dv_acc)
            if has_fq:
                dkb_ref[...] = jnp.zeros_like(dkb_ref)

        q = q_ref[...]
        dov = do_ref[...]
        delta = jnp.sum(o_ref[...].astype(F32) * dov.astype(F32), axis=1, keepdims=True)
        lse_c = lse_ref[:, 0:1]
        fqc = fq_ref[:, 0:1] if has_fq else None
        causal = lax.broadcasted_iota(jnp.int32, (t, t), 1) <= lax.broadcasted_iota(jnp.int32, (t, t), 0)

        def step(j, carry, masked):
            dq_acc, rs = carry
            st = pl.multiple_of(j * t, t)
            k = k_ref[pl.ds(st, t), :]
            v = v_ref[pl.ds(st, t), :]
            s = lax.dot_general(q, k, _NT, preferred_element_type=F32)
            bias = kb_ref[j]
            if has_fq:
                bias = fqc + bias
            s = s + bias
            if masked:
                s = jnp.where(causal, s, NEG)
            p = jnp.exp(s - lse_c)
            dp = lax.dot_general(dov, v, (((1,), (1,)), ((), ())), preferred_element_type=F32)
            ds = p * (dp - delta)
            dv_acc[pl.ds(st, t), :] += lax.dot_general(p.astype(BF16), dov, (((0,), (0,)), ((), ())), preferred_element_type=F32)
            dsb = ds.astype(BF16)
            dk_acc[pl.ds(st, t), :] += lax.dot_general(dsb, q, (((0,), (0,)), ((), ())), preferred_element_type=F32)
            if has_fq:
                dkb_ref[j] += jnp.sum(ds, axis=0, keepdims=True)
                rs = rs + jnp.sum(ds, axis=1, keepdims=True)
            return dq_acc + jnp.dot(dsb, k, preferred_element_type=F32), rs

        carry = (jnp.zeros((t, LANES), F32), jnp.zeros((t, 1), F32))
        carry = lax.fori_loop(0, i, lambda j, c: step(j, c, False), carry)
        dq_acc, rs = step(i, carry, True)
        dq_ref[...] = dq_acc.astype(dq_ref.dtype)
        if has_fq:
            dqb_ref[...] = jnp.broadcast_to(rs, (t, LANES))

        @pl.when(i == nq - 1)
        def _():
            dk_ref[...] = dk_acc[...].astype(dk_ref.dtype)
            dv_ref[...] = dv_acc[...].astype(dv_ref.dtype)

    q_spec, k_spec, v_spec = _head_specs(lp, t, offs)
    kb_spec = pl.BlockSpec((None, nq, 1, t), lambda h, i: (h, 0, 0, 0))
    row_spec = pl.BlockSpec((t, LANES), lambda h, i: (i, h))
    col_spec = pl.BlockSpec((lp, LANES), lambda h, i: (0, h))
    ins = [qa, ka, va, kb] + ([fq] if has_fq else []) + [o, do, lse]
    wide = jax.ShapeDtypeStruct((lp, nh * LANES), BF16)
    extra_shapes = (jax.ShapeDtypeStruct(kb.shape, F32), jax.ShapeDtypeStruct((lp, nh * LANES), F32)) if has_fq else ()
    extra_specs = (kb_spec, row_spec) if has_fq else ()
    return pl.pallas_call(
        body,
        out_shape=(wide, wide, wide) + extra_shapes,
        grid=(nh, nq),
        in_specs=[q_spec, k_spec, v_spec, kb_spec] + ([row_spec] if has_fq else []) + [row_spec, row_spec, row_spec],
        out_specs=(row_spec, col_spec, col_spec) + extra_specs,
        scratch_shapes=[pltpu.VMEM((lp, LANES), F32), pltpu.VMEM((lp, LANES), F32)],
        name=name,
        compiler_params=_cp(("parallel", "arbitrary")),
    )(*ins)


SB_TK = 128


def _split3(x):
    hi = x.astype(BF16)
    r1 = x - hi.astype(F32)
    mid = r1.astype(BF16)
    lo = (r1 - mid.astype(F32)).astype(BF16)
    return hi, mid, lo


SB_RC = 128
SB_UNROLL = 5


_NT = (((1,), (1,)), ((), ()))
_TN = (((0,), (0,)), ((), ()))


def _sb_logits(zraw, kbj, mask):
    z = kbj + zraw
    if mask is not None:
        z = jnp.where(mask, z, NEG)
    e = jnp.exp(-jnp.abs(z))
    g = jnp.minimum(z, 0.0) - jnp.log(1.0 + e)
    lk = g - z
    return z, e, g, lk


def _dot3_parts(parts, tri):
    d = functools.partial(jnp.dot, preferred_element_type=F32)
    return d(parts[0], tri) + d(parts[1], tri) + d(parts[2], tri)


def _split2(x):
    hi = x.astype(BF16)
    return hi, (x - hi.astype(F32)).astype(BF16)


def _dot_parts(parts, m):
    out = jnp.dot(parts[0], m, preferred_element_type=F32)
    for p in parts[1:]:
        out = out + jnp.dot(p, m, preferred_element_type=F32)
    return out


def _tri(n, pred):
    return pred(lax.broadcasted_iota(jnp.int32, (n, n), 0), lax.broadcasted_iota(jnp.int32, (n, n), 1)).astype(BF16)


def _sb_diag_chunks(jj, nrc, rc, tk):
    plan = []
    for r in range(nrc):
        lo_row, hi_row = r * rc, (r + 1) * rc - 1
        lo_col, hi_col = jj * tk, (jj + 1) * tk - 1
        if hi_row <= lo_col:
            plan.append(None)
        elif lo_row > hi_col:
            plan.append("all")
        else:
            plan.append(lo_col - lo_row)
    return plan


def _sb_fwd(qa, kb, *, nh, name, tq=640):
    lp = qa.shape[0]
    tq = _col_tile(lp, tq)
    tk = SB_TK
    rc = min(SB_RC, tq)
    nq, sub, nrc = lp // tq, tq // tk, tq // rc

    def body(q_ref, k_ref, v_ref, kb_ref, o_ref, c_scr, acc_scr):
        i = pl.program_id(1)
        c_scr[...] = jnp.zeros_like(c_scr)
        acc_scr[...] = jnp.zeros_like(acc_scr)
        tri = _tri(tk, lambda r, c: r > c)
        row_io = lax.broadcasted_iota(jnp.int32, (rc, tk), 0)
        col_io = lax.broadcasted_iota(jnp.int32, (rc, tk), 1)

        def scores(j, rows):
            k = k_ref[pl.ds(pl.multiple_of(j * tk, tk), tk), :]
            return [lax.dot_general(q_ref[rs, :], k, _NT, preferred_element_type=F32) for rs in rows]

        def weights(j, rows, masks, zs):
            kbj = kb_ref[j]
            gs, splits, firsts = [], [], []
            for mask, zraw in zip(masks, zs):
                _, _, g, lk = _sb_logits(zraw, kbj, mask)
                gs.append(g)
                firsts.append(lk[:, 0:1])
                splits.append(_split2(lk))
            sums = [_dot_parts(p, tri) for p in splits]
            avs = [jnp.exp(g + (sm + c_scr[rs, :])).astype(BF16) for g, sm, rs in zip(gs, sums, rows)]
            for rs, sm, first in zip(rows, sums, firsts):
                c_scr[rs, :] += jnp.broadcast_to(sm[:, 0:1] + first, (rc, tk))
            return avs

        def values(j, rows, avs):
            v = v_ref[pl.ds(pl.multiple_of(j * tk, tk), tk), :]
            pvs = [jnp.dot(a, v, preferred_element_type=F32) for a in avs]
            for rs, pv in zip(rows, pvs):
                acc_scr[rs, :] += pv

        for jj in reversed(range(sub)):
            plan = _sb_diag_chunks(jj, nrc, rc, tk)
            live = [r for r, what in enumerate(plan) if what is not None]
            rows = [pl.ds(r * rc, rc) for r in live]
            masks = [None if plan[r] == "all" else (col_io + plan[r]) < row_io for r in live]
            j = i * sub + jj
            values(j, rows, weights(j, rows, masks, scores(j, rows)))

        n = i * sub
        rows = [pl.ds(r * rc, rc) for r in range(nrc)]
        nomask = [None] * nrc

        def left(m, carry):
            js = [n - 1 - SB_UNROLL * m - t for t in range(SB_UNROLL)]
            zss = [scores(j, rows) for j in js]
            avss = [weights(j, rows, nomask, zs) for j, zs in zip(js, zss)]
            for j, avs in zip(js, avss):
                values(j, rows, avs)
            return carry

        lax.fori_loop(0, n // SB_UNROLL, left, 0)

        def tail(t, carry):
            j = n % SB_UNROLL - 1 - t
            values(j, rows, weights(j, rows, nomask, scores(j, rows)))
            return carry

        lax.fori_loop(0, n % SB_UNROLL, tail, 0)

        o_ref[...] = acc_scr[...].astype(o_ref.dtype)

    q_spec, k_spec, v_spec = _head_specs(lp, tq, (0, nh, 2 * nh))
    kb_spec = pl.BlockSpec((None, lp // tk, 1, tk), lambda h, i: (h, 0, 0, 0))
    row_spec = pl.BlockSpec((tq, LANES), lambda h, i: (i, h))
    return pl.pallas_call(
        body,
        out_shape=jax.ShapeDtypeStruct((lp, nh * LANES), BF16),
        grid=(nh, nq),
        in_specs=[q_spec, k_spec, v_spec, kb_spec],
        out_specs=row_spec,
        scratch_shapes=[pltpu.VMEM((tq, tk), F32), pltpu.VMEM((tq, LANES), F32)],
        name=name,
        compiler_params=_cp(("parallel", "arbitrary")),
    )(qa, qa, qa, kb)


def _sb_bwd(qa, kb, do, *, nh, name, tq=640):
    lp = qa.shape[0]
    tq = _col_tile(lp, tq)
    tk = SB_TK
    rc = min(SB_RC, tq)
    nq, nk, sub, nrc = lp // tq, lp // tk, tq // tk, tq // rc

    def body(q_ref, k_ref, v_ref, kb_ref, do_ref, dq_ref, dk_ref, dv_ref, dkt_acc, dvt_acc, w_scr, b_scr, c_scr, u_scr, dq_scr):
        i = pl.program_id(1)

        @pl.when(i == 0)
        def _():
            dkt_acc[...] = jnp.zeros_like(dkt_acc)
            dvt_acc[...] = jnp.zeros_like(dvt_acc)

        c_scr[...] = jnp.zeros_like(c_scr)
        u_scr[...] = jnp.zeros_like(u_scr)
        dq_scr[...] = jnp.zeros_like(dq_scr)
        tri_gt = _tri(tk, lambda r, c: r > c)
        tri_lt = _tri(tk, lambda r, c: r < c)
        row_io = lax.broadcasted_iota(jnp.int32, (rc, tk), 0)
        col_io = lax.broadcasted_iota(jnp.int32, (rc, tk), 1)
        qt = q_ref[...].astype(F32).T.astype(BF16)
        dot_t = do_ref[...].astype(F32).T.astype(BF16)
        zero_blk = jnp.zeros((rc, tk), BF16)

        def full_rows(plan, parts):
            it = iter(parts)
            return jnp.concatenate([zero_blk if what is None else next(it) for what in plan], axis=0)

        def key_rows(j):
            return pl.ds(pl.multiple_of(j * tk, tk), tk)

        def scores(j, rows):
            k = k_ref[key_rows(j), :]
            v = v_ref[key_rows(j), :]
            zs = [lax.dot_general(q_ref[rs, :], k, _NT, preferred_element_type=F32) for rs in rows]
            das = [lax.dot_general(do_ref[rs, :], v, _NT, preferred_element_type=F32) for rs in rows]
            return zs, das

        def weights(j, rows, masks, zs, das):
            kbj = kb_ref[j]
            gs, splits, firsts = [], [], []
            for rs, mask, zraw in zip(rows, masks, zs):
                _, _, g, lk = _sb_logits(zraw, kbj, mask)
                b_scr[j, rs, :] = jnp.exp(g).astype(BF16)
                gs.append(g)
                firsts.append(lk[:, 0:1])
                splits.append(_split2(lk))
            sums = [_dot_parts(p, tri_gt) for p in splits]
            avs = []
            for rs, g, sm, da, first in zip(rows, gs, sums, das, firsts):
                a = jnp.exp(g + (sm + c_scr[rs, :]))
                w_scr[j, rs, :] = (a * da).astype(BF16)
                avs.append(a.astype(BF16))
                c_scr[rs, :] += jnp.broadcast_to(sm[:, 0:1] + first, (rc, tk))
            return avs

        def dv_update(j, plan, avs):
            dvt_acc[j] += jnp.dot(dot_t, full_rows(plan, avs), preferred_element_type=F32)

        for jj in reversed(range(sub)):
            plan = _sb_diag_chunks(jj, nrc, rc, tk)
            live = [r for r, what in enumerate(plan) if what is not None]
            rows = [pl.ds(r * rc, rc) for r in live]
            masks = [None if plan[r] == "all" else (col_io + plan[r]) < row_io for r in live]
            j = i * sub + jj
            dv_update(j, plan, weights(j, rows, masks, *scores(j, rows)))

        n = i * sub
        everything = ["all"] * nrc
        rows = [pl.ds(r * rc, rc) for r in range(nrc)]
        nomask = [None] * nrc
        def left1(m, carry):
            js = [n - 1 - SB_UNROLL * m - t for t in range(SB_UNROLL)]
            scs = [scores(j, rows) for j in js]
            avs = [weights(j, rows, nomask, *sc) for j, sc in zip(js, scs)]
            for j, av in zip(js, avs):
                dv_update(j, everything, av)
            return carry

        lax.fori_loop(0, n // SB_UNROLL, left1, 0)

        def tail1(t, carry):
            j = n % SB_UNROLL - 1 - t
            dv_update(j, everything, weights(j, rows, nomask, *scores(j, rows)))
            return carry

        lax.fori_loop(0, n % SB_UNROLL, tail1, 0)

        def prefix(j, rows):
            return [jnp.dot(w_scr[j, rs, :], tri_lt, preferred_element_type=F32) for rs in rows]

        def dlogits(j, rows, sums):
            dzs = []
            for rs, sm in zip(rows, sums):
                w = w_scr[j, rs, :].astype(F32)
                beta = b_scr[j, rs, :].astype(F32)
                dzs.append((w - beta * ((w + sm) + u_scr[rs, :])).astype(BF16))
                u_scr[rs, :] += jnp.broadcast_to(sm[:, tk - 1:tk] + w[:, tk - 1:tk], (rc, tk))
            return dzs

        def dqk_update(j, plan, rows, dzs):
            k = k_ref[key_rows(j), :]
            dqs = [jnp.dot(dz, k, preferred_element_type=F32) for dz in dzs]
            for rs, dq in zip(rows, dqs):
                dq_scr[rs, :] += dq
            dkt_acc[j] += jnp.dot(qt, full_rows(plan, dzs), preferred_element_type=F32)

        def left2(m, carry):
            js = [SB_UNROLL * m + t for t in range(SB_UNROLL)]
            sums = [prefix(j, rows) for j in js]
            dzs = [dlogits(j, rows, sm) for j, sm in zip(js, sums)]
            for j, dz in zip(js, dzs):
                dqk_update(j, everything, rows, dz)
            return carry

        lax.fori_loop(0, n // SB_UNROLL, left2, 0)

        def tail2(t, carry):
            j = (n // SB_UNROLL) * SB_UNROLL + t
            dqk_update(j, everything, rows, dlogits(j, rows, prefix(j, rows)))
            return carry

        lax.fori_loop(0, n % SB_UNROLL, tail2, 0)
        for jj in range(sub):
            plan = _sb_diag_chunks(jj, nrc, rc, tk)
            live_rows = [pl.ds(r * rc, rc) for r, what in enumerate(plan) if what is not None]
            j = i * sub + jj
            dqk_update(j, plan, live_rows, dlogits(j, live_rows, prefix(j, live_rows)))
        dq_ref[...] = dq_scr[...].astype(dq_ref.dtype)

        @pl.when(i == nq - 1)
        def _():
            def flush(j, carry):
                dk_ref[key_rows(j), :] = dkt_acc[j].T.astype(dk_ref.dtype)
                dv_ref[key_rows(j), :] = dvt_acc[j].T.astype(dv_ref.dtype)
                return carry

            lax.fori_loop(0, nk, flush, 0)

    q_spec, k_spec, v_spec = _head_specs(lp, tq, (0, nh, 2 * nh))
    kb_spec = pl.BlockSpec((None, nk, 1, tk), lambda h, i: (h, 0, 0, 0))
    row_spec = pl.BlockSpec((tq, LANES), lambda h, i: (i, h))
    col_spec = pl.BlockSpec((lp, LANES), lambda h, i: (0, h))
    wide = jax.ShapeDtypeStruct((lp, nh * LANES), BF16)
    return pl.pallas_call(
        body,
        out_shape=(wide, wide, wide),
        grid=(nh, nq),
        in_specs=[q_spec, k_spec, v_spec, kb_spec, row_spec],
        out_specs=(row_spec, col_spec, col_spec),
        scratch_shapes=[
            pltpu.VMEM((nk, LANES, tk), F32),
            pltpu.VMEM((nk, LANES, tk), F32),
            pltpu.VMEM((nk, tq, tk), BF16),
            pltpu.VMEM((nk, tq, tk), BF16),
            pltpu.VMEM((tq, tk), F32),
            pltpu.VMEM((tq, tk), F32),
            pltpu.VMEM((tq, LANES), F32),
        ],
        name=name,
        compiler_params=_cp(("parallel", "arbitrary")),
    )(qa, qa, qa, kb, do)


def _pool_counts(pos, win):
    return jnp.clip(pos + 1, 1, win).astype(F32)


def _pool_fwd(a, name):
    lp, C = a.shape
    tm = _row_tile(lp, 640)
    hb = tm // POOL_HALO

    def body(prev_ref, cur_ref, o_ref, xs):
        i = pl.program_id(0)
        xs[pl.ds(0, POOL_HALO), :] = jnp.where(i > 0, prev_ref[...], 0.0)
        xs[pl.ds(POOL_HALO, tm), :] = cur_ref[...]
        pos = i * tm + lax.broadcasted_iota(jnp.int32, (tm, 1), 0) - PAD0
        for g, win in enumerate(POOL_WINDOWS):
            cols = pl.ds(g * POOL_GROUP, POOL_GROUP)
            s = xs[pl.ds(POOL_HALO, tm), cols]
            for k in range(1, win):
                s = s + xs[pl.ds(POOL_HALO - k, tm), cols]
            o_ref[:, cols] = (s / _pool_counts(pos, win) - xs[pl.ds(POOL_HALO, tm), cols]).astype(o_ref.dtype)

    return pl.pallas_call(
        body,
        out_shape=jax.ShapeDtypeStruct((lp, C), BF16),
        grid=(lp // tm,),
        in_specs=[
            pl.BlockSpec((POOL_HALO, C), lambda i: (jnp.maximum(i * hb - 1, 0), 0)),
            pl.BlockSpec((tm, C), lambda i: (i, 0)),
        ],
        out_specs=pl.BlockSpec((tm, C), lambda i: (i, 0)),
        scratch_shapes=[pltpu.VMEM((tm + POOL_HALO, C), F32)],
        name=name,
        compiler_params=_cp(("parallel",)),
    )(a, a)


def _pool_bwd(dp, name):
    lp, C = dp.shape
    tm = _row_tile(lp, 640)
    hb = tm // POOL_HALO
    nt = lp // tm
    last_halo = lp // POOL_HALO - 1

    def body(cur_ref, next_ref, o_ref, xs):
        i = pl.program_id(0)
        pos = i * tm + lax.broadcasted_iota(jnp.int32, (tm, 1), 0) - PAD0
        pos_h = (i + 1) * tm + lax.broadcasted_iota(jnp.int32, (POOL_HALO, 1), 0) - PAD0
        for g, win in enumerate(POOL_WINDOWS):
            cols = pl.ds(g * POOL_GROUP, POOL_GROUP)
            cur = cur_ref[:, cols]
            xs[pl.ds(0, tm), cols] = cur / _pool_counts(pos, win)
            xs[pl.ds(tm, POOL_HALO), cols] = jnp.where(i < nt - 1, next_ref[:, cols], 0.0) / _pool_counts(pos_h, win)
            s = xs[pl.ds(0, tm), cols]
            for k in range(1, win):
                s = s + xs[pl.ds(k, tm), cols]
            o_ref[:, cols] = jnp.where(pos >= 0, s - cur, 0.0)

    return pl.pallas_call(
        body,
        out_shape=jax.ShapeDtypeStruct((lp, C), F32),
        grid=(nt,),
        in_specs=[
            pl.BlockSpec((tm, C), lambda i: (i, 0)),
            pl.BlockSpec((POOL_HALO, C), lambda i: (jnp.minimum((i + 1) * hb, last_halo), 0)),
        ],
        out_specs=pl.BlockSpec((tm, C), lambda i: (i, 0)),
        scratch_shapes=[pltpu.VMEM((tm + POOL_HALO, C), F32)],
        name=name,
        compiler_params=_cp(("parallel",)),
    )(dp, dp)


def _scale_add(h, pre, scale, name):
    M, C = h.shape
    tm = _row_tile(M, 640)

    def body(h_ref, p_ref, s_ref, o_ref):
        o_ref[...] = h_ref[...] + p_ref[...] * s_ref[...]

    row = pl.BlockSpec((tm, C), lambda i: (i, 0))
    return pl.pallas_call(
        body,
        out_shape=jax.ShapeDtypeStruct((M, C), F32),
        grid=(M // tm,),
        in_specs=[row, row, pl.BlockSpec((1, C), lambda i: (0, 0))],
        out_specs=row,
        name=name,
        compiler_params=_cp(("parallel",)),
    )(h, pre, scale.reshape(1, C))


def _scale_bwd(dh, pre, scale, name):
    M, C = dh.shape
    tm = _row_tile(M, 640)

    def body(dh_ref, p_ref, s_ref, dp_ref, ds_ref):
        @pl.when(pl.program_id(0) == 0)
        def _():
            ds_ref[...] = jnp.zeros_like(ds_ref)

        d = dh_ref[...]
        ds_ref[...] += jnp.sum(d * p_ref[...], axis=0, keepdims=True)
        dp_ref[...] = (d * s_ref[...]).astype(dp_ref.dtype)

    row = pl.BlockSpec((tm, C), lambda i: (i, 0))
    vec = pl.BlockSpec((1, C), lambda i: (0, 0))
    return pl.pallas_call(
        body,
        out_shape=(jax.ShapeDtypeStruct((M, C), BF16), jax.ShapeDtypeStruct((1, C), F32)),
        grid=(M // tm,),
        in_specs=[row, row, vec],
        out_specs=(row, vec),
        name=name,
        compiler_params=_cp(("arbitrary",)),
    )(dh, pre, scale.reshape(1, C))


def _gate_parts(z):
    e = jnp.exp(-jnp.abs(z))
    return e, jnp.minimum(z, 0.0) - jnp.log(1.0 + e)


def _tri_dot3(tri, x):
    hi, mid, lo = _split3(x)
    d = functools.partial(jnp.dot, preferred_element_type=F32)
    return d(tri, hi) + d(tri, mid) + d(tri, lo)


def _gate_fwd(x, b, name):
    lp, C = x.shape
    tm = _row_tile(lp, 640)

    def body(x_ref, b_ref, o_ref, carry):
        i = pl.program_id(0)

        @pl.when(i == 0)
        def _():
            carry[...] = jnp.zeros_like(carry)

        _, ls = _gate_parts(x_ref[...] + b_ref[...])
        rows = i * tm + lax.broadcasted_iota(jnp.int32, (tm, 1), 0)
        ls = jnp.where(rows >= PAD0, ls, 0.0)
        tri = (lax.broadcasted_iota(jnp.int32, (tm, tm), 0) >= lax.broadcasted_iota(jnp.int32, (tm, tm), 1)).astype(BF16)
        f = _tri_dot3(tri, ls) + carry[...]
        o_ref[...] = f
        carry[...] = f[tm - 1:tm, :]

    return pl.pallas_call(
        body,
        out_shape=jax.ShapeDtypeStruct((lp, C), F32),
        grid=(lp // tm,),
        in_specs=[pl.BlockSpec((tm, C), lambda i: (i, 0)), pl.BlockSpec((1, C), lambda i: (0, 0))],
        out_specs=pl.BlockSpec((tm, C), lambda i: (i, 0)),
        scratch_shapes=[pltpu.VMEM((1, C), F32)],
        name=name,
        compiler_params=_cp(("arbitrary",)),
    )(x, b)


def _gate_bwd(x, b, df, name):
    lp, C = x.shape
    tm = _row_tile(lp, 640)
    nt = lp // tm

    def body(x_ref, b_ref, df_ref, dx_ref, db_ref, carry):
        i = pl.program_id(0)

        @pl.when(i == 0)
        def _():
            carry[...] = jnp.zeros_like(carry)
            db_ref[...] = jnp.zeros_like(db_ref)

        z = x_ref[...] + b_ref[...]
        e, _ = _gate_parts(z)
        tri = (lax.broadcasted_iota(jnp.int32, (tm, tm), 0) <= lax.broadcasted_iota(jnp.int32, (tm, tm), 1)).astype(BF16)
        r = _tri_dot3(tri, df_ref[...]) + carry[...]
        carry[...] = r[0:1, :]
        rows = (nt - 1 - i) * tm + lax.broadcasted_iota(jnp.int32, (tm, 1), 0)
        dx = jnp.where(rows >= PAD0, r * (jnp.where(z >= 0.0, e, 1.0) / (1.0 + e)), 0.0)
        dx_ref[...] = dx
        db_ref[...] += jnp.sum(dx, axis=0, keepdims=True)

    rev = pl.BlockSpec((tm, C), lambda i: (nt - 1 - i, 0))
    vec = pl.BlockSpec((1, C), lambda i: (0, 0))
    return pl.pallas_call(
        body,
        out_shape=(jax.ShapeDtypeStruct((lp, C), F32), jax.ShapeDtypeStruct((1, C), F32)),
        grid=(nt,),
        in_specs=[rev, vec, rev],
        out_specs=(rev, vec),
        scratch_shapes=[pltpu.VMEM((1, C), F32)],
        name=name,
        compiler_params=_cp(("arbitrary",)),
    )(x, b, df)


MLA_SCALE = (MLA_NOPE + MLA_ROPE) ** -0.5


def _rope_apply(x, c, a, b):
    return x * c + pltpu.roll(x, LANES - 16, 1) * a + pltpu.roll(x, 16, 1) * b


def _rope_transpose(dy, c, a, b):
    return dy * c + pltpu.roll(dy * a, 16, 1) + pltpu.roll(dy * b, LANES - 16, 1)


def _mla_prep_fwd(q, kmat, kr, c, a, b, name):
    lp, W = q.shape
    nh = W // LANES
    tm = _row_tile(lp, 640)

    def body(q_ref, k_ref, kr_ref, c_ref, a_ref, b_ref, qo_ref, ko_ref):
        cv, av, bv = c_ref[...], a_ref[...], b_ref[...]
        kr_roped = _rope_apply(kr_ref[...], cv, av, bv)
        for h in range(nh):
            cols = pl.ds(h * LANES, LANES)
            qo_ref[:, cols] = (_rope_apply(q_ref[:, cols], cv, av, bv) * MLA_SCALE).astype(qo_ref.dtype)
            ko_ref[:, cols] = (k_ref[:, cols] + kr_roped).astype(ko_ref.dtype)

    heads = pl.BlockSpec((tm, W), lambda i: (i, 0))
    tab = pl.BlockSpec((tm, LANES), lambda i: (i, 0))
    wide = jax.ShapeDtypeStruct((lp, W), BF16)
    return pl.pallas_call(
        body,
        out_shape=(wide, wide),
        grid=(lp // tm,),
        in_specs=[heads, heads, tab, tab, tab, tab],
        out_specs=(heads, heads),
        name=name,
        compiler_params=_cp(("parallel",)),
    )(q, kmat, kr, c, a, b)


def _mla_prep_bwd(dq, dk, c, a, b, name):
    lp, W = dq.shape
    nh = W // LANES
    tm = _row_tile(lp, 640)

    def body(dq_ref, dk_ref, c_ref, a_ref, b_ref, dqo_ref, dkr_ref):
        cv, av, bv = c_ref[...], a_ref[...], b_ref[...]
        ksum = jnp.zeros((tm, LANES), F32)
        for h in range(nh):
            cols = pl.ds(h * LANES, LANES)
            dqo_ref[:, cols] = _rope_transpose(dq_ref[:, cols].astype(F32) * MLA_SCALE, cv, av, bv).astype(dqo_ref.dtype)
            ksum = ksum + dk_ref[:, cols].astype(F32)
        dkr_ref[...] = _rope_transpose(ksum, cv, av, bv)

    wide = pl.BlockSpec((tm, W), lambda i: (i, 0))
    tab = pl.BlockSpec((tm, LANES), lambda i: (i, 0))
    return pl.pallas_call(
        body,
        out_shape=(jax.ShapeDtypeStruct((lp, W), BF16), jax.ShapeDtypeStruct((lp, LANES), F32)),
        grid=(lp // tm,),
        in_specs=[wide, wide, tab, tab, tab],
        out_specs=(wide, tab),
        name=name,
        compiler_params=_cp(("parallel",)),
    )(dq, dk, c, a, b)


def _loss_head(h, g, target, name):
    lp, C = h.shape
    tm = _row_tile(lp, 640)
    nt = lp // tm

    def body(h_ref, g_ref, t_ref, loss_ref, dh_ref, dg_ref, sq):
        i = pl.program_id(0)

        @pl.when(i == 0)
        def _():
            dg_ref[...] = jnp.zeros_like(dg_ref)
            sq[...] = jnp.zeros_like(sq)

        xf = h_ref[...]
        gv = g_ref[...]
        r = lax.rsqrt(jnp.mean(xf * xf, axis=-1, keepdims=True) + EPS)
        xhat = xf * r
        rows = i * tm + lax.broadcasted_iota(jnp.int32, (tm, 1), 0)
        err = jnp.where(rows >= PAD0 + N_META, xhat * gv - t_ref[...], 0.0)
        sq[...] += jnp.sum(err * err, axis=0, keepdims=True)
        dy = err * (1.0 / C)
        dg_ref[...] += jnp.sum(dy * xhat, axis=0, keepdims=True)
        dxh = dy * gv
        dh_ref[...] = r * (dxh - xhat * jnp.mean(dxh * xhat, axis=-1, keepdims=True))

        @pl.when(i == nt - 1)
        def _():
            loss_ref[...] = jnp.broadcast_to(jnp.sum(sq[...], axis=1, keepdims=True) * (0.5 / C), (1, LANES))

    row = pl.BlockSpec((tm, C), lambda i: (i, 0))
    vec = pl.BlockSpec((1, C), lambda i: (0, 0))
    return pl.pallas_call(
        body,
        out_shape=(jax.ShapeDtypeStruct((1, LANES), F32), jax.ShapeDtypeStruct((lp, C), F32), jax.ShapeDtypeStruct((1, C), F32)),
        grid=(nt,),
        in_specs=[row, vec, row],
        out_specs=(pl.BlockSpec((1, LANES), lambda i: (0, 0)), row, vec),
        scratch_shapes=[pltpu.VMEM((1, C), F32)],
        name=name,
        compiler_params=_cp(("arbitrary",)),
    )(h, g.reshape(1, C), target)


def _adamw(w, g, m, v, name):
    shape = w.shape
    C = shape[-1]
    R = w.size // C
    tr = R
    if R % 8 == 0:
        for cand in range(8, R + 1, 8):
            if R % cand == 0 and cand * C * 4 <= (1 << 20):
                tr = cand
    c1 = 1.0 - ADAM_B1 ** ADAM_STEP
    c2 = 1.0 - ADAM_B2 ** ADAM_STEP

    def body(w_ref, g_ref, m_ref, v_ref, d_ref, nm_ref, nv_ref):
        gv = g_ref[...]
        nm = ADAM_B1 * m_ref[...] + (1.0 - ADAM_B1) * gv
        nv = ADAM_B2 * v_ref[...] + (1.0 - ADAM_B2) * (gv * gv)
        nm_ref[...] = nm
        nv_ref[...] = nv
        d_ref[...] = -ADAM_LR * ((nm / c1) / (jnp.sqrt(nv / c2) + ADAM_EPS) + ADAM_WD * w_ref[...])

    blk = pl.BlockSpec((tr, C), lambda i: (i, 0))
    out = jax.ShapeDtypeStruct((R, C), F32)
    outs = pl.pallas_call(
        body,
        out_shape=(out, out, out),
        grid=(R // tr,),
        in_specs=[blk] * 4,
        out_specs=(blk, blk, blk),
        name=name,
        compiler_params=_cp(("parallel",)),
    )(*(t.reshape(R, C) for t in (w, g, m, v)))
    return tuple(t.reshape(shape) for t in outs)


def _exchange(send, axes, same, name):
    na = len(axes)
    n = 1 << na
    _, R, C = send.shape
    parts = max(p for p in (8, 4, 2, 1) if R % (16 * p) == 0 or p == 1)
    pr = R // parts

    def body(send_ref, recv_ref, send_sems, recv_sems, local_sem):
        coords = {ax: lax.axis_index(ax) for ax in MESH_AXES}
        me = 0
        for ax in axes:
            me = me * 2 + coords[ax]

        def member(r):
            dev = dict(coords)
            for b, ax in enumerate(axes):
                if (r >> (na - 1 - b)) & 1:
                    dev[ax] = 1 - dev[ax]
            return tuple(dev[ax] for ax in MESH_AXES)

        def chunk(j, p):
            return (send_ref.at[0] if same else send_ref.at[j]).at[pl.ds(p * pr, pr)]

        def slot(j, p):
            return recv_ref.at[j].at[pl.ds(p * pr, pr)]

        own = pltpu.make_async_copy(send_ref.at[0] if same else send_ref.at[me], recv_ref.at[me], local_sem)
        own.start()
        copies = []
        for r in range(1, n):
            peer = me ^ r
            for p in range(parts):
                cp = pltpu.make_async_remote_copy(
                    src_ref=chunk(peer, p), dst_ref=slot(me, p), send_sem=send_sems.at[r, p], recv_sem=recv_sems.at[r, p],
                    device_id=member(r), device_id_type=pl.DeviceIdType.MESH)
                cp.start()
                copies.append(cp)
        for r in range(1, n):
            for p in range(parts):
                arrival = pltpu.make_async_remote_copy(
                    src_ref=chunk(me, p), dst_ref=slot(me ^ r, p), send_sem=send_sems.at[r, p], recv_sem=recv_sems.at[r, p],
                    device_id=member(r), device_id_type=pl.DeviceIdType.MESH)
                arrival.wait_recv()
        for cp in copies:
            cp.wait_send()
        own.wait()

    any_spec = pl.BlockSpec(memory_space=pl.ANY)
    return pl.pallas_call(
        body,
        out_shape=jax.ShapeDtypeStruct((n, R, C), send.dtype),
        in_specs=[any_spec],
        out_specs=any_spec,
        scratch_shapes=[pltpu.SemaphoreType.DMA((n, parts)), pltpu.SemaphoreType.DMA((n, parts)), pltpu.SemaphoreType.DMA],
        name=name,
        compiler_params=pltpu.CompilerParams(has_side_effects=True),
    )(send)


def _sum_chunks(x, name, out_dtype=F32):
    n, R, C = x.shape
    tr = _row_tile(R, 512)

    def body(x_ref, o_ref):
        acc = x_ref[0].astype(F32)
        for j in range(1, n):
            acc = acc + x_ref[j].astype(F32)
        o_ref[...] = acc.astype(o_ref.dtype)

    return pl.pallas_call(
        body,
        out_shape=jax.ShapeDtypeStruct((R, C), out_dtype),
        grid=(R // tr,),
        in_specs=[pl.BlockSpec((n, tr, C), lambda i: (0, i, 0))],
        out_specs=pl.BlockSpec((tr, C), lambda i: (i, 0)),
        name=name,
        compiler_params=_cp(("parallel",)),
    )(x)


def _pad_heads_cols(w, groups, d):
    k = w.shape[0]
    w = w.reshape(k, groups * N_HEADS, d)
    return jnp.pad(w, ((0, 0), (0, 0), (0, LANES - d))).reshape(k, groups * N_HEADS * LANES)


def _unpad_heads_cols(w, groups, d):
    k = w.shape[0]
    return w.reshape(k, groups * N_HEADS, LANES)[:, :, :d].reshape(k, groups * N_HEADS * d)


def _pad_heads_rows(w, d):
    n = w.shape[1]
    return jnp.pad(w.reshape(N_HEADS, d, n), ((0, 0), (0, LANES - d), (0, 0))).reshape(N_HEADS * LANES, n)


def _unpad_heads_rows(w, d):
    n = w.shape[1]
    return w.reshape(N_HEADS, LANES, n)[:, :d].reshape(N_HEADS * d, n)


Q_SCALE = HEAD_DIM ** -0.5


def _scale_q_cols(w):
    nq = N_HEADS * LANES
    return jnp.concatenate([w[:, :nq] * Q_SCALE, w[:, nq:]], axis=1)


def _kernel_weights(W):
    P = dict(W)
    pw = W["pool_w"][0]
    bd = jnp.zeros((D_MODEL, D_MODEL), pw.dtype)
    for g in range(len(POOL_WINDOWS)):
        bd = lax.dynamic_update_slice(bd, pw[g], (g * POOL_GROUP, g * POOL_GROUP))
    P["pool_bd"] = bd
    P["sb_qkv"] = _scale_q_cols(_pad_heads_cols(W["sb_w_qkv"][0], 3, HEAD_DIM))
    P["sb_o"] = _pad_heads_rows(W["sb_w_o"][0], HEAD_DIM)
    nq = 3 * N_HEADS * HEAD_DIM
    P["fox_qkv"] = _scale_q_cols(_pad_heads_cols(W["fox_w_qkvf"][0][:, :nq], 3, HEAD_DIM))
    P["fox_f"] = jnp.pad(W["fox_w_qkvf"][0][:, nq:], ((0, 0), (0, LANES - N_HEADS)))
    P["fox_o"] = _pad_heads_rows(W["fox_w_o"][0], HEAD_DIM)
    P["fox_b"] = jnp.pad(W["fox_b_f"], ((0, 0), (0, LANES - N_HEADS)))
    P["mla_down"] = jnp.pad(W["mla_w_down"][0], ((0, 0), (0, MLA_DOWN_PAD - W["mla_w_down"].shape[2])))
    P["mla_uq"] = _pad_heads_cols(W["mla_w_uq"][0], 1, MLA_NOPE + MLA_ROPE)
    ukv = W["mla_w_ukv"][0].reshape(MLA_KV_RANK, N_HEADS, 2 * HEAD_DIM)
    padk = ((0, 0), (0, 0), (0, LANES - HEAD_DIM))
    P["mla_ukv"] = jnp.concatenate(
        [jnp.pad(ukv[:, :, :MLA_NOPE], padk).reshape(MLA_KV_RANK, -1), jnp.pad(ukv[:, :, MLA_NOPE:], padk).reshape(MLA_KV_RANK, -1)], axis=1)
    P["mla_o"] = _pad_heads_rows(W["mla_w_o"][0], HEAD_DIM)
    return P


def _rope_tables(lp):
    pos = (jnp.arange(lp) - PAD0).astype(F32)
    inv = ROPE_THETA ** (-jnp.arange(0, MLA_ROPE, 2, dtype=F32) / MLA_ROPE)
    ang = pos[:, None] * inv[None, :]
    cos, sin = jnp.cos(ang), jnp.sin(ang)
    half = MLA_ROPE // 2
    z = lambda n: jnp.zeros((lp, n), F32)
    c = jnp.concatenate([jnp.ones((lp, MLA_NOPE), F32), cos, cos, z(LANES - MLA_NOPE - MLA_ROPE)], axis=1)
    a = jnp.concatenate([z(MLA_NOPE), -sin, z(LANES - MLA_NOPE - half)], axis=1)
    b = jnp.concatenate([z(MLA_NOPE + half), sin, z(LANES - MLA_NOPE - MLA_ROPE)], axis=1)
    return c, a, b


def _key_bias(lp, t, per_head=None):
    pad = jnp.arange(lp)[None, :] < PAD0
    body = jnp.zeros((N_HEADS, lp), F32) if per_head is None else per_head
    return jnp.where(pad, NEG, body).reshape(N_HEADS, lp // t, 1, t)


def _ffn_fwd(h, i, P):
    b = _rms_fwd(h, P["norm_ffn"][i], BF16, "ffn_norm")
    g, u, hd = _ffn_up(b, P["ffn_w_gate"][i], P["ffn_w_up"][i], "ffn_gate_up")
    return _mm(hd, P["ffn_w_down"][i], "nn", "ffn_down", add=h), (h, b, g, u, hd)


def _ffn_bwd(dh, i, P, saved):
    h, b, g, u, hd = saved
    dwd = _mm(hd, dh, "tn", "ffn_down_dw")
    dg, du = _ffn_down_bwd(dh, P["ffn_w_down"][i], g, u, "ffn_down_dx")
    dwg = _mm(b, dg, "tn", "ffn_gate_dw")
    dwu = _mm(b, du, "tn", "ffn_up_dw")
    db = _mm(dg, P["ffn_w_gate"][i], "nt", "ffn_gate_dx")
    db = _mm(du, P["ffn_w_up"][i], "nt", "ffn_up_dx", add=db)
    dh_in, dgain = _rms_bwd(h, P["norm_ffn"][i], db, dh, "ffn_norm_bwd")
    return dh_in, dgain, dwg, dwu, dwd


def _pool_layer_fwd(h, P):
    a = _rms_fwd(h, P["norm_mix"][0], F32, "pool_norm")
    pooled = _pool_fwd(a, "pool_window")
    pre = _mm(pooled, P["pool_bd"], "nn", "pool_mix")
    return _scale_add(h, pre, P["pool_scale"][0], "pool_scale_add"), (h, pooled, pre)


def _pool_layer_bwd(dh, P, saved):
    h, pooled, pre = saved
    dpre, dscale = _scale_bwd(dh, pre, P["pool_scale"][0], "pool_scale_bwd")
    dbd = _mm(pooled, dpre, "tn", "pool_mix_dw")
    dpooled = _mm(dpre, P["pool_bd"], "nt", "pool_mix_dx")
    da = _pool_bwd(dpooled, "pool_window_bwd")
    dh_in, dgain = _rms_bwd(h, P["norm_mix"][0], da, dh, "mix_norm_bwd")
    dw = jnp.stack([dbd[g * POOL_GROUP:(g + 1) * POOL_GROUP, g * POOL_GROUP:(g + 1) * POOL_GROUP] for g in range(len(POOL_WINDOWS))])
    return dh_in, {"norm_mix0": dgain, "pool_w": dw[None], "pool_scale": dscale}


def _out_proj_bwd(o, dh, wo, tag):
    return _mm(o, dh, "tn", tag + "_o_dw"), _mm(dh, wo, "nt", tag + "_o_dx", out_dtype=BF16)


def _sb_layer_fwd(h, P):
    lp = h.shape[0]
    a = _rms_fwd(h, P["norm_mix"][1], BF16, "mix_norm")
    qkv = _mm(a, P["sb_qkv"], "nn", "sb_qkv", out_dtype=BF16)
    kb = _key_bias(lp, SB_TK)
    o = _sb_fwd(qkv, kb, nh=N_HEADS, name="sb_attn")
    return _mm(o, P["sb_o"], "nn", "attn_out", add=h), (h, a, qkv, kb, o)


def _sb_layer_bwd(dh, P, saved):
    h, a, qkv, kb, o = saved
    dwo, do = _out_proj_bwd(o, dh, P["sb_o"], "attn")
    dq, dk, dv = _sb_bwd(qkv, kb, do, nh=N_HEADS, name="sb_attn_bwd")
    dqkv = jnp.concatenate([dq, dk, dv], axis=1)
    dw = _scale_q_cols(_mm(a, dqkv, "tn", "qkv_dw"))
    da = _mm(dqkv, P["sb_qkv"], "nt", "qkv_dx")
    dh_in, dgain = _rms_bwd(h, P["norm_mix"][1], da, dh, "mix_norm_bwd")
    return dh_in, {"norm_mix1": dgain, "sb_w_qkv": _unpad_heads_cols(dw, 3, HEAD_DIM)[None], "sb_w_o": _unpad_heads_rows(dwo, HEAD_DIM)[None]}


def _fox_layer_fwd(h, P):
    lp = h.shape[0]
    t = _attn_tile(lp)
    a = _rms_fwd(h, P["norm_mix"][3], BF16, "mix_norm")
    qkv = _mm(a, P["fox_qkv"], "nn", "sb_qkv", out_dtype=BF16)
    f = _mm(a, P["fox_f"], "nn", "fox_gate_proj")
    fc = _gate_fwd(f, P["fox_b"], "fox_gate")[:, :N_HEADS]
    kb = _key_bias(lp, t, -fc.T)
    fq = jnp.broadcast_to(fc[:, :, None], (lp, N_HEADS, LANES)).reshape(lp, N_HEADS * LANES)
    o, lse = _attn_fwd(qkv, qkv, qkv, kb, fq, nh=N_HEADS, offs=(0, N_HEADS, 2 * N_HEADS), name="fox_attn")
    return _mm(o, P["fox_o"], "nn", "attn_out", add=h), (h, a, qkv, f, kb, fq, o, lse)


def _fox_layer_bwd(dh, P, saved):
    h, a, qkv, f, kb, fq, o, lse = saved
    lp = h.shape[0]
    dwo, do = _out_proj_bwd(o, dh, P["fox_o"], "attn")
    dq, dk, dv, dkb, dqb = _attn_bwd(qkv, qkv, qkv, kb, fq, o, do, lse, nh=N_HEADS, offs=(0, N_HEADS, 2 * N_HEADS),
                                     name="fox_attn_bwd")
    dfc = jnp.pad(dqb.reshape(lp, N_HEADS, LANES)[:, :, 0] - dkb.reshape(N_HEADS, lp).T, ((0, 0), (0, LANES - N_HEADS)))
    df, dbf = _gate_bwd(f, P["fox_b"], dfc, "fox_gate_bwd")
    dqkv = jnp.concatenate([dq, dk, dv], axis=1)
    dw = _scale_q_cols(_mm(a, dqkv, "tn", "qkv_dw"))
    dwf = _mm(a, df, "tn", "fox_gate_dw")
    da = _mm(dqkv, P["fox_qkv"], "nt", "qkv_dx")
    da = _mm(df, P["fox_f"], "nt", "fox_gate_dx", add=da)
    dh_in, dgain = _rms_bwd(h, P["norm_mix"][3], da, dh, "mix_norm_bwd")
    dwqkvf = jnp.concatenate([_unpad_heads_cols(dw, 3, HEAD_DIM), dwf[:, :N_HEADS]], axis=1)
    return dh_in, {"norm_mix3": dgain, "fox_w_qkvf": dwqkvf[None], "fox_b_f": dbf[:, :N_HEADS], "fox_w_o": _unpad_heads_rows(dwo, HEAD_DIM)[None]}


def _mla_layer_fwd(h, P):
    lp = h.shape[0]
    a = _rms_fwd(h, P["norm_mix"][2], BF16, "mix_norm")
    down = _mm(a, P["mla_down"], "nn", "mla_down")
    cq_pre = down[:, :MLA_Q_RANK]
    ckv_pre = down[:, MLA_Q_RANK:MLA_Q_RANK + MLA_KV_RANK]
    kr = jnp.pad(down[:, MLA_Q_RANK + MLA_KV_RANK:MLA_Q_RANK + MLA_KV_RANK + MLA_ROPE], ((0, 0), (MLA_NOPE, LANES - MLA_NOPE - MLA_ROPE)))
    cq = _rms_fwd(cq_pre, P["mla_q_norm"][0], BF16, "mla_q_norm")
    ckv = _rms_fwd(ckv_pre, P["mla_kv_norm"][0], BF16, "mla_kv_norm")
    q = _mm(cq, P["mla_uq"], "nn", "mla_uq")
    kv = _mm(ckv, P["mla_ukv"], "nn", "mla_ukv", out_dtype=BF16)
    tabs = _rope_tables(lp)
    qr, kc = _mla_prep_fwd(q, kv, kr, *tabs, "mla_rope")
    kb = _key_bias(lp, _attn_tile(lp))
    o, lse = _attn_fwd(qr, kc, kv, kb, None, nh=N_HEADS, offs=(0, 0, N_HEADS), name="mla_attn")
    return _mm(o, P["mla_o"], "nn", "attn_out", add=h), (h, a, cq_pre, ckv_pre, cq, ckv, qr, kc, kv, tabs, kb, o, lse)


def _mla_layer_bwd(dh, P, saved):
    h, a, cq_pre, ckv_pre, cq, ckv, qr, kc, kv, tabs, kb, o, lse = saved
    lp = h.shape[0]
    dwo, do = _out_proj_bwd(o, dh, P["mla_o"], "attn")
    dqr, dkc, dv = _attn_bwd(qr, kc, kv, kb, None, o, do, lse, nh=N_HEADS, offs=(0, 0, N_HEADS),
                             name="mla_attn_bwd")
    dq, dkr = _mla_prep_bwd(dqr, dkc, *tabs, "mla_rope_bwd")
    dkv = jnp.concatenate([dkc, dv], axis=1)
    dwuq = _mm(cq, dq, "tn", "mla_uq_dw")
    dcq = _mm(dq, P["mla_uq"], "nt", "mla_uq_dx")
    dwukv = _mm(ckv, dkv, "tn", "mla_ukv_dw")
    dckv = _mm(dkv, P["mla_ukv"], "nt", "mla_ukv_dx")
    dcq_pre, dqn = _rms_bwd(cq_pre, P["mla_q_norm"][0], dcq, None, "mla_q_norm_bwd")
    dckv_pre, dkvn = _rms_bwd(ckv_pre, P["mla_kv_norm"][0], dckv, None, "mla_kv_norm_bwd")
    used = MLA_Q_RANK + MLA_KV_RANK + MLA_ROPE
    ddown = jnp.concatenate([dcq_pre, dckv_pre, dkr[:, MLA_NOPE:MLA_NOPE + MLA_ROPE], jnp.zeros((lp, MLA_DOWN_PAD - used), F32)], axis=1)
    dwdown = _mm(a, ddown, "tn", "mla_down_dw")
    da = _mm(ddown, P["mla_down"], "nt", "mla_down_dx")
    dh_in, dgain = _rms_bwd(h, P["norm_mix"][2], da, dh, "mix_norm_bwd")
    dukv = dwukv.reshape(MLA_KV_RANK, 2, N_HEADS, LANES)[:, :, :, :HEAD_DIM]
    dukv = jnp.concatenate([dukv[:, 0], dukv[:, 1]], axis=-1).reshape(MLA_KV_RANK, N_HEADS * 2 * HEAD_DIM)
    return dh_in, {
        "norm_mix2": dgain, "mla_w_down": dwdown[:, :used][None], "mla_q_norm": dqn, "mla_kv_norm": dkvn,
        "mla_w_uq": _unpad_heads_cols(dwuq, 1, MLA_NOPE + MLA_ROPE)[None], "mla_w_ukv": dukv[None],
        "mla_w_o": _unpad_heads_rows(dwo, HEAD_DIM)[None]}


_MIXERS = ((_pool_layer_fwd, _pool_layer_bwd), (_sb_layer_fwd, _sb_layer_bwd), (_mla_layer_fwd, _mla_layer_bwd), (_fox_layer_fwd, _fox_layer_bwd))


def _step_local(x, target, W):
    seq = x.shape[0]
    P = _kernel_weights(W)
    h = jnp.concatenate([jnp.zeros((PAD0, D_MODEL), F32), W["meta"], x], axis=0)
    tpad = jnp.pad(target, ((PAD0 + N_META, 0), (0, 0)))
    saved = []
    for i in range(4):
        h, s_mix = _MIXERS[i][0](h, P)
        h, s_ffn = _ffn_fwd(h, i, P)
        saved.append((s_mix, s_ffn))
    loss, dh, dfinal = _loss_head(h, W["final_norm"], tpad, "loss_head")
    grads = {"final_norm": dfinal.reshape(-1)}
    gains_mix, gains_ffn, dwg, dwu, dwd = [None] * 4, [None] * 4, [None] * 4, [None] * 4, [None] * 4
    for i in reversed(range(4)):
        s_mix, s_ffn = saved[i]
        dh, gains_ffn[i], dwg[i], dwu[i], dwd[i] = _ffn_bwd(dh, i, P, s_ffn)
        dh, g = _MIXERS[i][1](dh, P, s_mix)
        gains_mix[i] = g.pop("norm_mix%d" % i)
        grads.update(g)
    grads["norm_mix"] = jnp.concatenate(gains_mix, axis=0)
    grads["norm_ffn"] = jnp.concatenate(gains_ffn, axis=0)
    grads["ffn_w_gate"] = jnp.stack(dwg)
    grads["ffn_w_up"] = jnp.stack(dwu)
    grads["ffn_w_down"] = jnp.stack(dwd)
    grads["meta"] = dh[PAD0:PAD0 + N_META]
    return loss, dh[PAD0 + N_META:], grads


_WEIGHTS = ("meta", "norm_mix", "norm_ffn", "pool_w", "pool_scale", "sb_w_qkv", "sb_w_o", "mla_w_down", "mla_q_norm",
            "mla_kv_norm", "mla_w_uq", "mla_w_ukv", "mla_w_o", "fox_w_qkvf", "fox_b_f", "fox_w_o", "ffn_w_gate",
            "ffn_w_up", "ffn_w_down", "final_norm")
_SHARD_AXIS = {"meta": 1, "pool_w": 2, "sb_w_qkv": 2, "sb_w_o": 1, "mla_w_down": 1, "mla_q_norm": 1, "mla_kv_norm": 1,
               "mla_w_uq": 2, "mla_w_ukv": 2, "mla_w_o": 1, "fox_w_qkvf": 2, "fox_b_f": None, "fox_w_o": 1,
               "ffn_w_gate": 2, "ffn_w_up": 2, "ffn_w_down": 1}
_SHARDED = tuple(n for n in _WEIGHTS if _SHARD_AXIS.get(n) is not None)
_REPLICATED = tuple(n for n in _WEIGHTS if _SHARD_AXIS.get(n) is None)
_EXACT = ("meta", "mla_q_norm", "mla_kv_norm")
N_CHIPS = 4
GRAD_ROW_TILE = 512


PACK_ROWS = 16


def _piece_rows(t):
    return -(-t.size // (LANES * PACK_ROWS)) * PACK_ROWS


def _flat_rows(parts, dtype, row_multiple):
    pieces = []
    for p in parts:
        flat = p.astype(dtype).reshape(-1)
        pieces.append(jnp.pad(flat, (0, _piece_rows(p) * LANES - flat.shape[0])).reshape(-1, LANES))
    rows = sum(q.shape[0] for q in pieces)
    pad = -(-rows // row_multiple) * row_multiple - rows
    if pad:
        pieces.append(jnp.zeros((pad, LANES), dtype))
    return jnp.concatenate(pieces, axis=0)


def _split_flat(flat, like):
    out, off = [], 0
    for t in like:
        out.append(flat[off:off + _piece_rows(t)].reshape(-1)[:t.size].reshape(t.shape))
        off += _piece_rows(t)
    return out


def _gather_shards(local, names, dtype, name):
    blocks = [local[n] for n in names]
    recv = _exchange(_flat_rows(blocks, dtype, PACK_ROWS)[None], ("x", "y"), True, name)
    per_chip = [_split_flat(recv[s], blocks) for s in range(N_CHIPS)]
    return {n: jnp.concatenate([per_chip[s][k] for s in range(N_CHIPS)], axis=_SHARD_AXIS[n]) for k, n in enumerate(names)}


def _shard_of(g, n, s):
    w = g.shape[_SHARD_AXIS[n]] // N_CHIPS
    return lax.slice_in_dim(g, s * w, (s + 1) * w, axis=_SHARD_AXIS[n])


def _train_step(a):
    local = {n: a[n] for n in _WEIGHTS}
    full = {n: local[n] for n in _REPLICATED}
    full.update(_gather_shards(local, [n for n in _SHARDED if n not in _EXACT], BF16, "gather_weights"))
    full.update(_gather_shards(local, list(_EXACT), F32, "gather_exact"))

    loss, grad_x, grads = _step_local(a["x"][0], a["loss_target"][0], full)

    send = jnp.stack([
        _flat_rows([_shard_of(grads[n], n, s) for n in _SHARDED], BF16, 2 * GRAD_ROW_TILE).reshape(2, -1, LANES)
        for s in range(N_CHIPS)]).reshape(2 * N_CHIPS, -1, LANES)
    mine = _sum_chunks(_exchange(send, MESH_AXES, False, "scatter_grads"), "sum_grads", out_dtype=BF16)
    both = _exchange(mine[None], ("c",), True, "pair_grads").reshape(-1, LANES).astype(F32)
    reduced = dict(zip(_SHARDED, _split_flat(both, [local[n] for n in _SHARDED])))
    small = _flat_rows([grads[n] for n in _REPLICATED], F32, 8)
    small = _sum_chunks(_exchange(small[None], MESH_AXES, True, "gather_small_grads"), "sum_small_grads")
    reduced.update(zip(_REPLICATED, _split_flat(small, [local[n] for n in _REPLICATED])))

    deltas, new_m, new_v = {}, {}, {}
    for n in _WEIGHTS:
        deltas[n], new_m[n], new_v[n] = _adamw(local[n], reduced[n], a["m_" + n], a["v_" + n], "adamw")
    total = lax.psum(loss[0, 0], MESH_AXES)
    return (total, grad_x[None], *[reduced[n] for n in _WEIGHTS], *[deltas[n] for n in _WEIGHTS],
            *[new_m[n] for n in _WEIGHTS], *[new_v[n] for n in _WEIGHTS])


def kernel(x, meta, norm_mix, norm_ffn, pool_w, pool_scale, sb_w_qkv, sb_w_o, mla_w_down, mla_q_norm, mla_kv_norm, mla_w_uq, mla_w_ukv, mla_w_o, fox_w_qkvf, fox_b_f, fox_w_o, ffn_w_gate, ffn_w_up, ffn_w_down, final_norm, loss_target, m_meta, m_norm_mix, m_norm_ffn, m_pool_w, m_pool_scale, m_sb_w_qkv, m_sb_w_o, m_mla_w_down, m_mla_q_norm, m_mla_kv_norm, m_mla_w_uq, m_mla_w_ukv, m_mla_w_o, m_fox_w_qkvf, m_fox_b_f, m_fox_w_o, m_ffn_w_gate, m_ffn_w_up, m_ffn_w_down, m_final_norm, v_meta, v_norm_mix, v_norm_ffn, v_pool_w, v_pool_scale, v_sb_w_qkv, v_sb_w_o, v_mla_w_down, v_mla_q_norm, v_mla_kv_norm, v_mla_w_uq, v_mla_w_ukv, v_mla_w_o, v_fox_w_qkvf, v_fox_b_f, v_fox_w_o, v_ffn_w_gate, v_ffn_w_up, v_ffn_w_down, v_final_norm):
    return _train_step(dict(locals()))
```

```python
import functools

import jax
import jax.numpy as jnp
from jax import lax
from jax.experimental import pallas as pl
from jax.experimental.pallas import tpu as pltpu

F32 = jnp.float32
BF16 = jnp.bfloat16

D_MODEL = 1024
N_META = 16
PAD0 = 112
LANES = 128
N_HEADS = 16
HEAD_DIM = 64
POOL_WINDOWS = (2, 4, 8, 16)
POOL_GROUP = 256
POOL_HALO = 16
MLA_Q_RANK = 384
MLA_KV_RANK = 256
MLA_NOPE = 64
MLA_ROPE = 32
MLA_DOWN_PAD = 768
ROPE_THETA = 10000.0
D_FF = 2816
EPS = 1e-6
NEG = -1e30
ADAM_LR = 0.001
ADAM_B1 = 0.9
ADAM_B2 = 0.999
ADAM_EPS = 1e-08
ADAM_WD = 0.01
ADAM_STEP = 10
VMEM_LIMIT = 56 * 1024 * 1024
MESH_AXES = ("x", "y", "c")


def _cp(sem, **kw):
    return pltpu.CompilerParams(dimension_semantics=sem, vmem_limit_bytes=VMEM_LIMIT, **kw)


def _row_tile(m, target):
    best = None
    for t in range(16, min(m, target) + 1, 16):
        if m % t == 0:
            best = t
    return best or m


def _col_tile(n, target):
    best = None
    for t in range(LANES, min(n, target) + 1, LANES):
        if n % t == 0:
            best = t
    return best or n


def _mm(a, b, mode, name, out_dtype=F32, add=None, tm=640, tn=1536, tk=2048):
    if mode == "nn":
        (M, K), (K2, N) = a.shape, b.shape
    elif mode == "nt":
        (M, K), (N, K2) = a.shape, b.shape
    else:
        (K, M), (K2, N) = a.shape, b.shape
    assert K == K2, (mode, a.shape, b.shape)
    if mode == "tn":
        tm_ = _col_tile(M, 1408)
        tk_ = _row_tile(K, 1664)
    else:
        tm_ = _row_tile(M, tm)
        tk_ = _col_tile(K, tk) if K > tk else K
    tn_ = _col_tile(N, tn)
    nk = K // tk_
    if mode == "nn":
        a_spec = pl.BlockSpec((tm_, tk_), lambda i, j, k: (i, k))
        b_spec = pl.BlockSpec((tk_, tn_), lambda i, j, k: (k, j))
        dims = (((1,), (0,)), ((), ()))
    elif mode == "nt":
        a_spec = pl.BlockSpec((tm_, tk_), lambda i, j, k: (i, k))
        b_spec = pl.BlockSpec((tn_, tk_), lambda i, j, k: (j, k))
        dims = (((1,), (1,)), ((), ()))
    else:
        a_spec = pl.BlockSpec((tk_, tm_), lambda i, j, k: (k, i))
        b_spec = pl.BlockSpec((tk_, tn_), lambda i, j, k: (k, j))
        dims = (((0,), (0,)), ((), ()))
    o_spec = pl.BlockSpec((tm_, tn_), lambda i, j, k: (i, j))
    has_add = add is not None

    def body(*refs):
        if has_add:
            a_ref, b_ref, add_ref, o_ref, acc_ref = refs
        else:
            a_ref, b_ref, o_ref, acc_ref = refs
        k = pl.program_id(2)
        part = lax.dot_general(a_ref[...].astype(BF16), b_ref[...].astype(BF16), dims, preferred_element_type=F32)

        @pl.when(k == 0)
        def _():
            acc_ref[...] = part

        @pl.when(k > 0)
        def _():
            acc_ref[...] += part

        @pl.when(k == nk - 1)
        def _():
            r = acc_ref[...]
            if has_add:
                r = r + add_ref[...]
            o_ref[...] = r.astype(o_ref.dtype)

    ins = [a, b] + ([add] if has_add else [])
    in_specs = [a_spec, b_spec] + ([o_spec] if has_add else [])
    return pl.pallas_call(
        body,
        out_shape=jax.ShapeDtypeStruct((M, N), out_dtype),
        grid=(M // tm_, N // tn_, nk),
        in_specs=in_specs,
        out_specs=o_spec,
        scratch_shapes=[pltpu.VMEM((tm_, tn_), F32)],
        name=name,
        compiler_params=_cp(("parallel", "parallel", "arbitrary")),
    )(*ins)


def _rms_fwd(x, g, out_dtype, name):
    M, C = x.shape
    tm = _row_tile(M, 640)

    def body(x_ref, g_ref, o_ref):
        xf = x_ref[...]
        r = lax.rsqrt(jnp.mean(xf * xf, axis=-1, keepdims=True) + EPS)
        o_ref[...] = ((xf * r) * g_ref[...]).astype(o_ref.dtype)

    return pl.pallas_call(
        body,
        out_shape=jax.ShapeDtypeStruct((M, C), out_dtype),
        grid=(M // tm,),
        in_specs=[pl.BlockSpec((tm, C), lambda i: (i, 0)), pl.BlockSpec((1, C), lambda i: (0, 0))],
        out_specs=pl.BlockSpec((tm, C), lambda i: (i, 0)),
        name=name,
        compiler_params=_cp(("parallel",)),
    )(x, g.reshape(1, C))


def _rms_bwd(x, g, dy, dres, name):
    M, C = x.shape
    tm = _row_tile(M, 640)
    has_res = dres is not None

    def body(*refs):
        if has_res:
            x_ref, g_ref, dy_ref, dres_ref, dx_ref, dg_ref = refs
        else:
            x_ref, g_ref, dy_ref, dx_ref, dg_ref = refs
        xf = x_ref[...]
        r = lax.rsqrt(jnp.mean(xf * xf, axis=-1, keepdims=True) + EPS)
        xhat = xf * r
        dyf = dy_ref[...].astype(F32)

        @pl.when(pl.program_id(0) == 0)
        def _():
            dg_ref[...] = jnp.zeros_like(dg_ref)

        dg_ref[...] += jnp.sum(dyf * xhat, axis=0, keepdims=True)
        dxh = dyf * g_ref[...]
        dx = r * (dxh - xhat * jnp.mean(dxh * xhat, axis=-1, keepdims=True))
        if has_res:
            dx = dx + dres_ref[...]
        dx_ref[...] = dx

    row = pl.BlockSpec((tm, C), lambda i: (i, 0))
    vec = pl.BlockSpec((1, C), lambda i: (0, 0))
    ins = [x, g.reshape(1, C), dy] + ([dres] if has_res else [])
    return pl.pallas_call(
        body,
        out_shape=(jax.ShapeDtypeStruct((M, C), F32), jax.ShapeDtypeStruct((1, C), F32)),
        grid=(M // tm,),
        in_specs=[row, vec, row] + ([row] if has_res else []),
        out_specs=(row, vec),
        name=name,
        compiler_params=_cp(("arbitrary",)),
    )(*ins)


def _sigmoid(x):
    return 1.0 / (1.0 + jnp.exp(-x))


def _ffn_up(b, wg, wu, name):
    M, K = b.shape
    N = wg.shape[1]
    tm, tn = _row_tile(M, 640), _col_tile(N, 1536)

    def body(b_ref, wg_ref, wu_ref, g_ref, u_ref, h_ref):
        bv = b_ref[...]
        g = jnp.dot(bv, wg_ref[...], preferred_element_type=F32)
        u = jnp.dot(bv, wu_ref[...], preferred_element_type=F32)
        g_ref[...] = g
        u_ref[...] = u
        h_ref[...] = ((g * _sigmoid(g)) * u).astype(h_ref.dtype)

    w_spec = pl.BlockSpec((K, tn), lambda i, j: (0, j))
    o_spec = pl.BlockSpec((tm, tn), lambda i, j: (i, j))
    f32 = jax.ShapeDtypeStruct((M, N), F32)
    return pl.pallas_call(
        body,
        out_shape=(f32, f32, jax.ShapeDtypeStruct((M, N), BF16)),
        grid=(M // tm, N // tn),
        in_specs=[pl.BlockSpec((tm, K), lambda i, j: (i, 0)), w_spec, w_spec],
        out_specs=(o_spec, o_spec, o_spec),
        name=name,
        compiler_params=_cp(("parallel", "parallel")),
    )(b, wg, wu)


def _ffn_down_bwd(dh, wd, g, u, name):
    M, K = dh.shape
    N = wd.shape[0]
    tm, tn = _row_tile(M, 640), _col_tile(N, 1536)

    def body(dh_ref, wd_ref, g_ref, u_ref, dg_ref, du_ref):
        d = lax.dot_general(dh_ref[...].astype(BF16), wd_ref[...], _NT, preferred_element_type=F32)
        gv = g_ref[...]
        sg = _sigmoid(gv)
        du_ref[...] = (d * (gv * sg)).astype(du_ref.dtype)
        dg_ref[...] = ((d * u_ref[...]) * (sg * (1.0 + gv * (1.0 - sg)))).astype(dg_ref.dtype)

    o_spec = pl.BlockSpec((tm, tn), lambda i, j: (i, j))
    bf = jax.ShapeDtypeStruct((M, N), BF16)
    return pl.pallas_call(
        body,
        out_shape=(bf, bf),
        grid=(M // tm, N // tn),
        in_specs=[pl.BlockSpec((tm, K), lambda i, j: (i, 0)), pl.BlockSpec((tn, K), lambda i, j: (j, 0)), o_spec, o_spec],
        out_specs=(o_spec, o_spec),
        name=name,
        compiler_params=_cp(("parallel", "parallel")),
    )(dh, wd, g, u)


ATTN_GROUP = 3


def _attn_tile(lp):
    return _col_tile(lp, 640)


def _head_specs(lp, t, offs):
    q_spec = pl.BlockSpec((t, LANES), lambda h, i: (i, offs[0] + h))
    k_spec = pl.BlockSpec((lp, LANES), lambda h, i: (0, offs[1] + h))
    v_spec = pl.BlockSpec((lp, LANES), lambda h, i: (0, offs[2] + h))
    return q_spec, k_spec, v_spec


def _attn_fwd(qa, ka, va, kb, fq, *, nh, offs, name, tile=640):
    lp = qa.shape[0]
    t = _col_tile(lp, tile)
    nq = lp // t
    has_fq = fq is not None

    def body(*refs):
        if has_fq:
            q_ref, k_ref, v_ref, kb_ref, fq_ref, o_ref, lse_ref = refs
        else:
            q_ref, k_ref, v_ref, kb_ref, o_ref, lse_ref = refs
        i = pl.program_id(1)
        q = q_ref[...]
        fqc = fq_ref[:, 0:1] if has_fq else None
        causal = lax.broadcasted_iota(jnp.int32, (t, t), 1) <= lax.broadcasted_iota(jnp.int32, (t, t), 0)

        def step(j, carry, masked):
            rows = pl.ds(pl.multiple_of(j * t, t), t)
            s = biased(lax.dot_general(q, k_ref[rows, :], _NT, preferred_element_type=F32), j)
            if masked:
                s = jnp.where(causal, s, NEG)
            return update(s, v_ref[rows, :], carry)

        def update(s, v, carry):
            m, l, acc = carry
            m_new = jnp.maximum(m, jnp.max(s, axis=1, keepdims=True))
            p = jnp.exp(s - m_new)
            alpha = jnp.exp(m - m_new)
            l = alpha * l + jnp.sum(p, axis=1, keepdims=True)
            acc = alpha * acc + jnp.dot(p.astype(BF16), v, preferred_element_type=F32)
            return m_new, l, acc

        def biased(s, j):
            bias = kb_ref[j]
            if has_fq:
                bias = fqc + bias
            return s + bias

        def group(g, carry):
            js = [ATTN_GROUP * g + u for u in range(ATTN_GROUP)]
            rws = [pl.ds(pl.multiple_of(j * t, t), t) for j in js]
            scores = [lax.dot_general(q, k_ref[r, :], _NT, preferred_element_type=F32) for r in rws]
            for j, r, s in zip(js, rws, scores):
                carry = update(biased(s, j), v_ref[r, :], carry)
            return carry

        init = (jnp.full((t, 1), NEG, F32), jnp.zeros((t, 1), F32), jnp.zeros((t, LANES), F32))
        carry = lax.fori_loop(0, i // ATTN_GROUP, group, init)
        done = (i // ATTN_GROUP) * ATTN_GROUP
        carry = lax.fori_loop(0, i % ATTN_GROUP, lambda u, c: step(done + u, c, False), carry)
        m, l, acc = step(i, carry, True)
        valid = (i * t + lax.broadcasted_iota(jnp.int32, (t, 1), 0)) >= PAD0
        o_ref[...] = jnp.where(valid, acc / l, 0.0).astype(o_ref.dtype)
        lse_ref[...] = jnp.broadcast_to(m + jnp.log(l), (t, LANES))

    q_spec, k_spec, v_spec = _head_specs(lp, t, offs)
    kb_spec = pl.BlockSpec((None, nq, 1, t), lambda h, i: (h, 0, 0, 0))
    row_spec = pl.BlockSpec((t, LANES), lambda h, i: (i, h))
    ins = [qa, ka, va, kb] + ([fq] if has_fq else [])
    return pl.pallas_call(
        body,
        out_shape=(jax.ShapeDtypeStruct((lp, nh * LANES), BF16), jax.ShapeDtypeStruct((lp, nh * LANES), F32)),
        grid=(nh, nq),
        in_specs=[q_spec, k_spec, v_spec, kb_spec] + ([row_spec] if has_fq else []),
        out_specs=(row_spec, row_spec),
        name=name,
        compiler_params=_cp(("parallel", "arbitrary")),
    )(*ins)


def _attn_bwd(qa, ka, va, kb, fq, o, do, lse, *, nh, offs, name, tile=640):
    lp = qa.shape[0]
    t = _col_tile(lp, tile)
    nq = lp // t
    has_fq = fq is not None

    def body(*refs):
        if has_fq:
            q_ref, k_ref, v_ref, kb_ref, fq_ref, o_ref, do_ref, lse_ref, dq_ref, dk_ref, dv_ref, dkb_ref, dqb_ref, dk_acc, dv_acc = refs
        else:
            q_ref, k_ref, v_ref, kb_ref, o_ref, do_ref, lse_ref, dq_ref, dk_ref, dv_ref, dk_acc, dv_acc = refs
        i = pl.program_id(1)

        @pl.when(i == 0)
        def _():
            dk_acc[...] = jnp.zeros_like(dk_acc)
            dv_acc[...] = jnp.zeros_like(dv_acc)
            if has_fq:
                dkb_ref[...] = jnp.zeros_like(dkb_ref)

        q = q_ref[...]
        dov = do_ref[...]
        delta = jnp.sum(o_ref[...].astype(F32) * dov.astype(F32), axis=1, keepdims=True)
        lse_c = lse_ref[:, 0:1]
        fqc = fq_ref[:, 0:1] if has_fq else None
        causal = lax.broadcasted_iota(jnp.int32, (t, t), 1) <= lax.broadcasted_iota(jnp.int32, (t, t), 0)

        def step(j, carry, masked):
            dq_acc, rs = carry
            st = pl.multiple_of(j * t, t)
            k = k_ref[pl.ds(st, t), :]
            v = v_ref[pl.ds(st, t), :]
            s = lax.dot_general(q, k, _NT, preferred_element_type=F32)
            bias = kb_ref[j]
            if has_fq:
                bias = fqc + bias
            s = s + bias
            if masked:
                s = jnp.where(causal, s, NEG)
            p = jnp.exp(s - lse_c)
            dp = lax.dot_general(dov, v, (((1,), (1,)), ((), ())), preferred_element_type=F32)
            ds = p * (dp - delta)
            dv_acc[pl.ds(st, t), :] += lax.dot_general(p.astype(BF16), dov, (((0,), (0,)), ((), ())), preferred_element_type=F32)
            dsb = ds.astype(BF16)
            dk_acc[pl.ds(st, t), :] += lax.dot_general(dsb, q, (((0,), (0,)), ((), ())), preferred_element_type=F32)
            if has_fq:
                dkb_ref[j] += jnp.sum(ds, axis=0, keepdims=True)
                rs = rs + jnp.sum(ds, axis=1, keepdims=True)
            return dq_acc + jnp.dot(dsb, k, preferred_element_type=F32), rs

        carry = (jnp.zeros((t, LANES), F32), jnp.zeros((t, 1), F32))
        carry = lax.fori_loop(0, i, lambda j, c: step(j, c, False), carry)
        dq_acc, rs = step(i, carry, True)
        dq_ref[...] = dq_acc.astype(dq_ref.dtype)
        if has_fq:
            dqb_ref[...] = jnp.broadcast_to(rs, (t, LANES))

        @pl.when(i == nq - 1)
        def _():
            dk_ref[...] = dk_acc[...].astype(dk_ref.dtype)
            dv_ref[...] = dv_acc[...].astype(dv_ref.dtype)

    q_spec, k_spec, v_spec = _head_specs(lp, t, offs)
    kb_spec = pl.BlockSpec((None, nq, 1, t), lambda h, i: (h, 0, 0, 0))
    row_spec = pl.BlockSpec((t, LANES), lambda h, i: (i, h))
    col_spec = pl.BlockSpec((lp, LANES), lambda h, i: (0, h))
    ins = [qa, ka, va, kb] + ([fq] if has_fq else []) + [o, do, lse]
    wide = jax.ShapeDtypeStruct((lp, nh * LANES), BF16)
    extra_shapes = (jax.ShapeDtypeStruct(kb.shape, F32), jax.ShapeDtypeStruct((lp, nh * LANES), F32)) if has_fq else ()
    extra_specs = (kb_spec, row_spec) if has_fq else ()
    return pl.pallas_call(
        body,
        out_shape=(wide, wide, wide) + extra_shapes,
        grid=(nh, nq),
        in_specs=[q_spec, k_spec, v_spec, kb_spec] + ([row_spec] if has_fq else []) + [row_spec, row_spec, row_spec],
        out_specs=(row_spec, col_spec, col_spec) + extra_specs,
        scratch_shapes=[pltpu.VMEM((lp, LANES), F32), pltpu.VMEM((lp, LANES), F32)],
        name=name,
        compiler_params=_cp(("parallel", "arbitrary")),
    )(*ins)


SB_TK = 128


def _split3(x):
    hi = x.astype(BF16)
    r1 = x - hi.astype(F32)
    mid = r1.astype(BF16)
    lo = (r1 - mid.astype(F32)).astype(BF16)
    return hi, mid, lo


SB_RC = 128
SB_UNROLL = 5


_NT = (((1,), (1,)), ((), ()))
_TN = (((0,), (0,)), ((), ()))


def _sb_logits(zraw, kbj, mask):
    z = kbj + zraw
    if mask is not None:
        z = jnp.where(mask, z, NEG)
    e = jnp.exp(-jnp.abs(z))
    g = jnp.minimum(z, 0.0) - jnp.log(1.0 + e)
    lk = g - z
    return z, e, g, lk


def _dot3_parts(parts, tri):
    d = functools.partial(jnp.dot, preferred_element_type=F32)
    return d(parts[0], tri) + d(parts[1], tri) + d(parts[2], tri)


def _split2(x):
    hi = x.astype(BF16)
    return hi, (x - hi.astype(F32)).astype(BF16)


def _dot_parts(parts, m):
    out = jnp.dot(parts[0], m, preferred_element_type=F32)
    for p in parts[1:]:
        out = out + jnp.dot(p, m, preferred_element_type=F32)
    return out


def _tri(n, pred):
    return pred(lax.broadcasted_iota(jnp.int32, (n, n), 0), lax.broadcasted_iota(jnp.int32, (n, n), 1)).astype(BF16)


def _sb_diag_chunks(jj, nrc, rc, tk):
    plan = []
    for r in range(nrc):
        lo_row, hi_row = r * rc, (r + 1) * rc - 1
        lo_col, hi_col = jj * tk, (jj + 1) * tk - 1
        if hi_row <= lo_col:
            plan.append(None)
        elif lo_row > hi_col:
            plan.append("all")
        else:
            plan.append(lo_col - lo_row)
    return plan


def _sb_fwd(qa, kb, *, nh, name, tq=640):
    lp = qa.shape[0]
    tq = _col_tile(lp, tq)
    tk = SB_TK
    rc = min(SB_RC, tq)
    nq, sub, nrc = lp // tq, tq // tk, tq // rc

    def body(q_ref, k_ref, v_ref, kb_ref, o_ref, c_scr, acc_scr):
        i = pl.program_id(1)
        c_scr[...] = jnp.zeros_like(c_scr)
        acc_scr[...] = jnp.zeros_like(acc_scr)
        tri = _tri(tk, lambda r, c: r > c)
        row_io = lax.broadcasted_iota(jnp.int32, (rc, tk), 0)
        col_io = lax.broadcasted_iota(jnp.int32, (rc, tk), 1)

        def scores(j, rows):
            k = k_ref[pl.ds(pl.multiple_of(j * tk, tk), tk), :]
            return [lax.dot_general(q_ref[rs, :], k, _NT, preferred_element_type=F32) for rs in rows]

        def weights(j, rows, masks, zs):
            kbj = kb_ref[j]
            gs, splits, firsts = [], [], []
            for mask, zraw in zip(masks, zs):
                _, _, g, lk = _sb_logits(zraw, kbj, mask)
                gs.append(g)
                firsts.append(lk[:, 0:1])
                splits.append(_split2(lk))
            sums = [_dot_parts(p, tri) for p in splits]
            avs = [jnp.exp(g + (sm + c_scr[rs, :])).astype(BF16) for g, sm, rs in zip(gs, sums, rows)]
            for rs, sm, first in zip(rows, sums, firsts):
                c_scr[rs, :] += jnp.broadcast_to(sm[:, 0:1] + first, (rc, tk))
            return avs

        def values(j, rows, avs):
            v = v_ref[pl.ds(pl.multiple_of(j * tk, tk), tk), :]
            pvs = [jnp.dot(a, v, preferred_element_type=F32) for a in avs]
            for rs, pv in zip(rows, pvs):
                acc_scr[rs, :] += pv

        diag = []
        for jj in reversed(range(sub)):
            plan = _sb_diag_chunks(jj, nrc, rc, tk)
            live = [r for r, what in enumerate(plan) if what is not None]
            diag.append((i * sub + jj, [pl.ds(r * rc, rc) for r in live],
                         [None if plan[r] == "all" else (col_io + plan[r]) < row_io for r in live]))
        diag_scores = [scores(j, rows) for j, rows, _ in diag]
        diag_avs = [weights(j, rows, masks, zs) for (j, rows, masks), zs in zip(diag, diag_scores)]
        for (j, rows, _), avs in zip(diag, diag_avs):
            values(j, rows, avs)

        n = i * sub
        rows = [pl.ds(r * rc, rc) for r in range(nrc)]
        nomask = [None] * nrc

        def left(m, carry):
            js = [n - 1 - SB_UNROLL * m - t for t in range(SB_UNROLL)]
            zss = [scores(j, rows) for j in js]
            avss = [weights(j, rows, nomask, zs) for j, zs in zip(js, zss)]
            for j, avs in zip(js, avss):
                values(j, rows, avs)
            return carry

        lax.fori_loop(0, n // SB_UNROLL, left, 0)

        def tail(t, carry):
            j = n % SB_UNROLL - 1 - t
            values(j, rows, weights(j, rows, nomask, scores(j, rows)))
            return carry

        lax.fori_loop(0, n % SB_UNROLL, tail, 0)

        o_ref[...] = acc_scr[...].astype(o_ref.dtype)

    q_spec, k_spec, v_spec = _head_specs(lp, tq, (0, nh, 2 * nh))
    kb_spec = pl.BlockSpec((None, lp // tk, 1, tk), lambda h, i: (h, 0, 0, 0))
    row_spec = pl.BlockSpec((tq, LANES), lambda h, i: (i, h))
    return pl.pallas_call(
        body,
        out_shape=jax.ShapeDtypeStruct((lp, nh * LANES), BF16),
        grid=(nh, nq),
        in_specs=[q_spec, k_spec, v_spec, kb_spec],
        out_specs=row_spec,
        scratch_shapes=[pltpu.VMEM((tq, tk), F32), pltpu.VMEM((tq, LANES), F32)],
        name=name,
        compiler_params=_cp(("parallel", "arbitrary")),
    )(qa, qa, qa, kb)


def _sb_bwd(qa, kb, do, *, nh, name, tq=640):
    lp = qa.shape[0]
    tq = _col_tile(lp, tq)
    tk = SB_TK
    rc = min(SB_RC, tq)
    nq, nk, sub, nrc = lp // tq, lp // tk, tq // tk, tq // rc

    def body(q_ref, k_ref, v_ref, kb_ref, do_ref, dq_ref, dk_ref, dv_ref, dkt_acc, dvt_acc, w_scr, b_scr, c_scr, u_scr, dq_scr):
        i = pl.program_id(1)

        @pl.when(i == 0)
        def _():
            dkt_acc[...] = jnp.zeros_like(dkt_acc)
            dvt_acc[...] = jnp.zeros_like(dvt_acc)

        c_scr[...] = jnp.zeros_like(c_scr)
        u_scr[...] = jnp.zeros_like(u_scr)
        dq_scr[...] = jnp.zeros_like(dq_scr)
        tri_gt = _tri(tk, lambda r, c: r > c)
        tri_lt = _tri(tk, lambda r, c: r < c)
        row_io = lax.broadcasted_iota(jnp.int32, (rc, tk), 0)
        col_io = lax.broadcasted_iota(jnp.int32, (rc, tk), 1)
        qt = q_ref[...].astype(F32).T.astype(BF16)
        dot_t = do_ref[...].astype(F32).T.astype(BF16)
        zero_blk = jnp.zeros((rc, tk), BF16)

        def full_rows(plan, parts):
            it = iter(parts)
            return jnp.concatenate([zero_blk if what is None else next(it) for what in plan], axis=0)

        def key_rows(j):
            return pl.ds(pl.multiple_of(j * tk, tk), tk)

        def scores(j, rows):
            k = k_ref[key_rows(j), :]
            v = v_ref[key_rows(j), :]
            zs = [lax.dot_general(q_ref[rs, :], k, _NT, preferred_element_type=F32) for rs in rows]
            das = [lax.dot_general(do_ref[rs, :], v, _NT, preferred_element_type=F32) for rs in rows]
            return zs, das

        def weights(j, rows, masks, zs, das):
            kbj = kb_ref[j]
            gs, splits, firsts = [], [], []
            for rs, mask, zraw in zip(rows, masks, zs):
                _, _, g, lk = _sb_logits(zraw, kbj, mask)
                b_scr[j, rs, :] = jnp.exp(g).astype(BF16)
                gs.append(g)
                firsts.append(lk[:, 0:1])
                splits.append(_split2(lk))
            sums = [_dot_parts(p, tri_gt) for p in splits]
            avs = []
            for rs, g, sm, da, first in zip(rows, gs, sums, das, firsts):
                a = jnp.exp(g + (sm + c_scr[rs, :]))
                w_scr[j, rs, :] = (a * da).astype(BF16)
                avs.append(a.astype(BF16))
                c_scr[rs, :] += jnp.broadcast_to(sm[:, 0:1] + first, (rc, tk))
            return avs

        def dv_update(j, plan, avs):
            dvt_acc[j] += jnp.dot(dot_t, full_rows(plan, avs), preferred_element_type=F32)

        diag = []
        for jj in reversed(range(sub)):
            plan = _sb_diag_chunks(jj, nrc, rc, tk)
            live = [r for r, what in enumerate(plan) if what is not None]
            diag.append((i * sub + jj, plan, [pl.ds(r * rc, rc) for r in live],
                         [None if plan[r] == "all" else (col_io + plan[r]) < row_io for r in live]))
        diag_scores = [scores(j, rows) for j, _, rows, _ in diag]
        diag_avs = [weights(j, rows, masks, *sc) for (j, _, rows, masks), sc in zip(diag, diag_scores)]
        for (j, plan, _, _), avs in zip(diag, diag_avs):
            dv_update(j, plan, avs)

        n = i * sub
        everything = ["all"] * nrc
        rows = [pl.ds(r * rc, rc) for r in range(nrc)]
        nomask = [None] * nrc
        def left1(m, carry):
            js = [n - 1 - SB_UNROLL * m - t for t in range(SB_UNROLL)]
            scs = [scores(j, rows) for j in js]
            avs = [weights(j, rows, nomask, *sc) for j, sc in zip(js, scs)]
            for j, av in zip(js, avs):
                dv_update(j, everything, av)
            return carry

        lax.fori_loop(0, n // SB_UNROLL, left1, 0)

        def tail1(t, carry):
            j = n % SB_UNROLL - 1 - t
            dv_update(j, everything, weights(j, rows, nomask, *scores(j, rows)))
            return carry

        lax.fori_loop(0, n % SB_UNROLL, tail1, 0)

        def prefix(j, rows):
            return [jnp.dot(w_scr[j, rs, :], tri_lt, preferred_element_type=F32) for rs in rows]

        def dlogits(j, rows, sums):
            dzs = []
            for rs, sm in zip(rows, sums):
                w = w_scr[j, rs, :].astype(F32)
                beta = b_scr[j, rs, :].astype(F32)
                dzs.append((w - beta * ((w + sm) + u_scr[rs, :])).astype(BF16))
                u_scr[rs, :] += jnp.broadcast_to(sm[:, tk - 1:tk] + w[:, tk - 1:tk], (rc, tk))
            return dzs

        def dqk_update(j, plan, rows, dzs):
            k = k_ref[key_rows(j), :]
            dqs = [jnp.dot(dz, k, preferred_element_type=F32) for dz in dzs]
            for rs, dq in zip(rows, dqs):
                dq_scr[rs, :] += dq
            dkt_acc[j] += jnp.dot(qt, full_rows(plan, dzs), preferred_element_type=F32)

        def left2(m, carry):
            js = [SB_UNROLL * m + t for t in range(SB_UNROLL)]
            sums = [prefix(j, rows) for j in js]
            dzs = [dlogits(j, rows, sm) for j, sm in zip(js, sums)]
            for j, dz in zip(js, dzs):
                dqk_update(j, everything, rows, dz)
            return carry

        lax.fori_loop(0, n // SB_UNROLL, left2, 0)

        def tail2(t, carry):
            j = (n // SB_UNROLL) * SB_UNROLL + t
            dqk_update(j, everything, rows, dlogits(j, rows, prefix(j, rows)))
            return carry

        lax.fori_loop(0, n % SB_UNROLL, tail2, 0)
        diag = []
        for jj in range(sub):
            plan = _sb_diag_chunks(jj, nrc, rc, tk)
            diag.append((i * sub + jj, plan, [pl.ds(r * rc, rc) for r, what in enumerate(plan) if what is not None]))
        diag_sums = [prefix(j, live_rows) for j, _, live_rows in diag]
        diag_dzs = [dlogits(j, live_rows, sm) for (j, _, live_rows), sm in zip(diag, diag_sums)]
        for (j, plan, live_rows), dzs in zip(diag, diag_dzs):
            dqk_update(j, plan, live_rows, dzs)
        dq_ref[...] = dq_scr[...].astype(dq_ref.dtype)

        @pl.when(i == nq - 1)
        def _():
            def flush(j, carry):
                dk_ref[key_rows(j), :] = dkt_acc[j].T.astype(dk_ref.dtype)
                dv_ref[key_rows(j), :] = dvt_acc[j].T.astype(dv_ref.dtype)
                return carry

            lax.fori_loop(0, nk, flush, 0)

    q_spec, k_spec, v_spec = _head_specs(lp, tq, (0, nh, 2 * nh))
    kb_spec = pl.BlockSpec((None, nk, 1, tk), lambda h, i: (h, 0, 0, 0))
    row_spec = pl.BlockSpec((tq, LANES), lambda h, i: (i, h))
    col_spec = pl.BlockSpec((lp, LANES), lambda h, i: (0, h))
    wide = jax.ShapeDtypeStruct((lp, nh * LANES), BF16)
    return pl.pallas_call(
        body,
        out_shape=(wide, wide, wide),
        grid=(nh, nq),
        in_specs=[q_spec, k_spec, v_spec, kb_spec, row_spec],
        out_specs=(row_spec, col_spec, col_spec),
        scratch_shapes=[
            pltpu.VMEM((nk, LANES, tk), F32),
            pltpu.VMEM((nk, LANES, tk), F32),
            pltpu.VMEM((nk, tq, tk), BF16),
            pltpu.VMEM((nk, tq, tk), BF16),
            pltpu.VMEM((tq, tk), F32),
            pltpu.VMEM((tq, tk), F32),
            pltpu.VMEM((tq, LANES), F32),
        ],
        name=name,
        compiler_params=_cp(("parallel", "arbitrary")),
    )(qa, qa, qa, kb, do)


def _pool_counts(pos, win):
    return jnp.clip(pos + 1, 1, win).astype(F32)


def _pool_fwd(a, name):
    lp, C = a.shape
    tm = _row_tile(lp, 640)
    hb = tm // POOL_HALO

    def body(prev_ref, cur_ref, o_ref, xs):
        i = pl.program_id(0)
        xs[pl.ds(0, POOL_HALO), :] = jnp.where(i > 0, prev_ref[...], 0.0)
        xs[pl.ds(POOL_HALO, tm), :] = cur_ref[...]
        pos = i * tm + lax.broadcasted_iota(jnp.int32, (tm, 1), 0) - PAD0
        for g, win in enumerate(POOL_WINDOWS):
            cols = pl.ds(g * POOL_GROUP, POOL_GROUP)
            s = xs[pl.ds(POOL_HALO, tm), cols]
            for k in range(1, win):
                s = s + xs[pl.ds(POOL_HALO - k, tm), cols]
            o_ref[:, cols] = (s / _pool_counts(pos, win) - xs[pl.ds(POOL_HALO, tm), cols]).astype(o_ref.dtype)

    return pl.pallas_call(
        body,
        out_shape=jax.ShapeDtypeStruct((lp, C), BF16),
        grid=(lp // tm,),
        in_specs=[
            pl.BlockSpec((POOL_HALO, C), lambda i: (jnp.maximum(i * hb - 1, 0), 0)),
            pl.BlockSpec((tm, C), lambda i: (i, 0)),
        ],
        out_specs=pl.BlockSpec((tm, C), lambda i: (i, 0)),
        scratch_shapes=[pltpu.VMEM((tm + POOL_HALO, C), F32)],
        name=name,
        compiler_params=_cp(("parallel",)),
    )(a, a)


def _pool_bwd(dp, name):
    lp, C = dp.shape
    tm = _row_tile(lp, 640)
    hb = tm // POOL_HALO
    nt = lp // tm
    last_halo = lp // POOL_HALO - 1

    def body(cur_ref, next_ref, o_ref, xs):
        i = pl.program_id(0)
        pos = i * tm + lax.broadcasted_iota(jnp.int32, (tm, 1), 0) - PAD0
        pos_h = (i + 1) * tm + lax.broadcasted_iota(jnp.int32, (POOL_HALO, 1), 0) - PAD0
        for g, win in enumerate(POOL_WINDOWS):
            cols = pl.ds(g * POOL_GROUP, POOL_GROUP)
            cur = cur_ref[:, cols]
            xs[pl.ds(0, tm), cols] = cur / _pool_counts(pos, win)
            xs[pl.ds(tm, POOL_HALO), cols] = jnp.where(i < nt - 1, next_ref[:, cols], 0.0) / _pool_counts(pos_h, win)
            s = xs[pl.ds(0, tm), cols]
            for k in range(1, win):
                s = s + xs[pl.ds(k, tm), cols]
            o_ref[:, cols] = jnp.where(pos >= 0, s - cur, 0.0)

    return pl.pallas_call(
        body,
        out_shape=jax.ShapeDtypeStruct((lp, C), F32),
        grid=(nt,),
        in_specs=[
            pl.BlockSpec((tm, C), lambda i: (i, 0)),
            pl.BlockSpec((POOL_HALO, C), lambda i: (jnp.minimum((i + 1) * hb, last_halo), 0)),
        ],
        out_specs=pl.BlockSpec((tm, C), lambda i: (i, 0)),
        scratch_shapes=[pltpu.VMEM((tm + POOL_HALO, C), F32)],
        name=name,
        compiler_params=_cp(("parallel",)),
    )(dp, dp)


def _scale_add(h, pre, scale, name):
    M, C = h.shape
    tm = _row_tile(M, 640)

    def body(h_ref, p_ref, s_ref, o_ref):
        o_ref[...] = h_ref[...] + p_ref[...] * s_ref[...]

    row = pl.BlockSpec((tm, C), lambda i: (i, 0))
    return pl.pallas_call(
        body,
        out_shape=jax.ShapeDtypeStruct((M, C), F32),
        grid=(M // tm,),
        in_specs=[row, row, pl.BlockSpec((1, C), lambda i: (0, 0))],
        out_specs=row,
        name=name,
        compiler_params=_cp(("parallel",)),
    )(h, pre, scale.reshape(1, C))


def _scale_bwd(dh, pre, scale, name):
    M, C = dh.shape
    tm = _row_tile(M, 640)

    def body(dh_ref, p_ref, s_ref, dp_ref, ds_ref):
        @pl.when(pl.program_id(0) == 0)
        def _():
            ds_ref[...] = jnp.zeros_like(ds_ref)

        d = dh_ref[...]
        ds_ref[...] += jnp.sum(d * p_ref[...], axis=0, keepdims=True)
        dp_ref[...] = (d * s_ref[...]).astype(dp_ref.dtype)

    row = pl.BlockSpec((tm, C), lambda i: (i, 0))
    vec = pl.BlockSpec((1, C), lambda i: (0, 0))
    return pl.pallas_call(
        body,
        out_shape=(jax.ShapeDtypeStruct((M, C), BF16), jax.ShapeDtypeStruct((1, C), F32)),
        grid=(M // tm,),
        in_specs=[row, row, vec],
        out_specs=(row, vec),
        name=name,
        compiler_params=_cp(("arbitrary",)),
    )(dh, pre, scale.reshape(1, C))


def _gate_parts(z):
    e = jnp.exp(-jnp.abs(z))
    return e, jnp.minimum(z, 0.0) - jnp.log(1.0 + e)


def _tri_dot3(tri, x):
    hi, mid, lo = _split3(x)
    d = functools.partial(jnp.dot, preferred_element_type=F32)
    return d(tri, hi) + d(tri, mid) + d(tri, lo)


def _gate_fwd(x, b, name):
    lp, C = x.shape
    tm = _row_tile(lp, 640)

    def body(x_ref, b_ref, o_ref, carry):
        i = pl.program_id(0)

        @pl.when(i == 0)
        def _():
            carry[...] = jnp.zeros_like(carry)

        _, ls = _gate_parts(x_ref[...] + b_ref[...])
        rows = i * tm + lax.broadcasted_iota(jnp.int32, (tm, 1), 0)
        ls = jnp.where(rows >= PAD0, ls, 0.0)
        tri = (lax.broadcasted_iota(jnp.int32, (tm, tm), 0) >= lax.broadcasted_iota(jnp.int32, (tm, tm), 1)).astype(BF16)
        f = _tri_dot3(tri, ls) + carry[...]
        o_ref[...] = f
        carry[...] = f[tm - 1:tm, :]

    return pl.pallas_call(
        body,
        out_shape=jax.ShapeDtypeStruct((lp, C), F32),
        grid=(lp // tm,),
        in_specs=[pl.BlockSpec((tm, C), lambda i: (i, 0)), pl.BlockSpec((1, C), lambda i: (0, 0))],
        out_specs=pl.BlockSpec((tm, C), lambda i: (i, 0)),
        scratch_shapes=[pltpu.VMEM((1, C), F32)],
        name=name,
        compiler_params=_cp(("arbitrary",)),
    )(x, b)


def _gate_bwd(x, b, df, name):
    lp, C = x.shape
    tm = _row_tile(lp, 640)
    nt = lp // tm

    def body(x_ref, b_ref, df_ref, dx_ref, db_ref, carry):
        i = pl.program_id(0)

        @pl.when(i == 0)
        def _():
            carry[...] = jnp.zeros_like(carry)
            db_ref[...] = jnp.zeros_like(db_ref)

        z = x_ref[...] + b_ref[...]
        e, _ = _gate_parts(z)
        tri = (lax.broadcasted_iota(jnp.int32, (tm, tm), 0) <= lax.broadcasted_iota(jnp.int32, (tm, tm), 1)).astype(BF16)
        r = _tri_dot3(tri, df_ref[...]) + carry[...]
        carry[...] = r[0:1, :]
        rows = (nt - 1 - i) * tm + lax.broadcasted_iota(jnp.int32, (tm, 1), 0)
        dx = jnp.where(rows >= PAD0, r * (jnp.where(z >= 0.0, e, 1.0) / (1.0 + e)), 0.0)
        dx_ref[...] = dx
        db_ref[...] += jnp.sum(dx, axis=0, keepdims=True)

    rev = pl.BlockSpec((tm, C), lambda i: (nt - 1 - i, 0))
    vec = pl.BlockSpec((1, C), lambda i: (0, 0))
    return pl.pallas_call(
        body,
        out_shape=(jax.ShapeDtypeStruct((lp, C), F32), jax.ShapeDtypeStruct((1, C), F32)),
        grid=(nt,),
        in_specs=[rev, vec, rev],
        out_specs=(rev, vec),
        scratch_shapes=[pltpu.VMEM((1, C), F32)],
        name=name,
        compiler_params=_cp(("arbitrary",)),
    )(x, b, df)


MLA_SCALE = (MLA_NOPE + MLA_ROPE) ** -0.5


def _rope_apply(x, c, a, b):
    return x * c + pltpu.roll(x, LANES - 16, 1) * a + pltpu.roll(x, 16, 1) * b


def _rope_transpose(dy, c, a, b):
    return dy * c + pltpu.roll(dy * a, 16, 1) + pltpu.roll(dy * b, LANES - 16, 1)


def _mla_prep_fwd(q, kmat, kr, c, a, b, name):
    lp, W = q.shape
    nh = W // LANES
    tm = _row_tile(lp, 640)

    def body(q_ref, k_ref, kr_ref, c_ref, a_ref, b_ref, qo_ref, ko_ref):
        cv, av, bv = c_ref[...], a_ref[...], b_ref[...]
        kr_roped = _rope_apply(kr_ref[...], cv, av, bv)
        for h in range(nh):
            cols = pl.ds(h * LANES, LANES)
            qo_ref[:, cols] = (_rope_apply(q_ref[:, cols], cv, av, bv) * MLA_SCALE).astype(qo_ref.dtype)
            ko_ref[:, cols] = (k_ref[:, cols] + kr_roped).astype(ko_ref.dtype)

    heads = pl.BlockSpec((tm, W), lambda i: (i, 0))
    tab = pl.BlockSpec((tm, LANES), lambda i: (i, 0))
    wide = jax.ShapeDtypeStruct((lp, W), BF16)
    return pl.pallas_call(
        body,
        out_shape=(wide, wide),
        grid=(lp // tm,),
        in_specs=[heads, heads, tab, tab, tab, tab],
        out_specs=(heads, heads),
        name=name,
        compiler_params=_cp(("parallel",)),
    )(q, kmat, kr, c, a, b)


def _mla_prep_bwd(dq, dk, c, a, b, name):
    lp, W = dq.shape
    nh = W // LANES
    tm = _row_tile(lp, 640)

    def body(dq_ref, dk_ref, c_ref, a_ref, b_ref, dqo_ref, dkr_ref):
        cv, av, bv = c_ref[...], a_ref[...], b_ref[...]
        ksum = jnp.zeros((tm, LANES), F32)
        for h in range(nh):
            cols = pl.ds(h * LANES, LANES)
            dqo_ref[:, cols] = _rope_transpose(dq_ref[:, cols].astype(F32) * MLA_SCALE, cv, av, bv).astype(dqo_ref.dtype)
            ksum = ksum + dk_ref[:, cols].astype(F32)
        dkr_ref[...] = _rope_transpose(ksum, cv, av, bv)

    wide = pl.BlockSpec((tm, W), lambda i: (i, 0))
    tab = pl.BlockSpec((tm, LANES), lambda i: (i, 0))
    return pl.pallas_call(
        body,
        out_shape=(jax.ShapeDtypeStruct((lp, W), BF16), jax.ShapeDtypeStruct((lp, LANES), F32)),
        grid=(lp // tm,),
        in_specs=[wide, wide, tab, tab, tab],
        out_specs=(wide, tab),
        name=name,
        compiler_params=_cp(("parallel",)),
    )(dq, dk, c, a, b)


def _loss_head(h, g, target, name):
    lp, C = h.shape
    tm = _row_tile(lp, 640)
    nt = lp // tm

    def body(h_ref, g_ref, t_ref, loss_ref, dh_ref, dg_ref, sq):
        i = pl.program_id(0)

        @pl.when(i == 0)
        def _():
            dg_ref[...] = jnp.zeros_like(dg_ref)
            sq[...] = jnp.zeros_like(sq)

        xf = h_ref[...]
        gv = g_ref[...]
        r = lax.rsqrt(jnp.mean(xf * xf, axis=-1, keepdims=True) + EPS)
        xhat = xf * r
        rows = i * tm + lax.broadcasted_iota(jnp.int32, (tm, 1), 0)
        err = jnp.where(rows >= PAD0 + N_META, xhat * gv - t_ref[...], 0.0)
        sq[...] += jnp.sum(err * err, axis=0, keepdims=True)
        dy = err * (1.0 / C)
        dg_ref[...] += jnp.sum(dy * xhat, axis=0, keepdims=True)
        dxh = dy * gv
        dh_ref[...] = r * (dxh - xhat * jnp.mean(dxh * xhat, axis=-1, keepdims=True))

        @pl.when(i == nt - 1)
        def _():
            loss_ref[...] = jnp.broadcast_to(jnp.sum(sq[...], axis=1, keepdims=True) * (0.5 / C), (1, LANES))

    row = pl.BlockSpec((tm, C), lambda i: (i, 0))
    vec = pl.BlockSpec((1, C), lambda i: (0, 0))
    return pl.pallas_call(
        body,
        out_shape=(jax.ShapeDtypeStruct((1, LANES), F32), jax.ShapeDtypeStruct((lp, C), F32), jax.ShapeDtypeStruct((1, C), F32)),
        grid=(nt,),
        in_specs=[row, vec, row],
        out_specs=(pl.BlockSpec((1, LANES), lambda i: (0, 0)), row, vec),
        scratch_shapes=[pltpu.VMEM((1, C), F32)],
        name=name,
        compiler_params=_cp(("arbitrary",)),
    )(h, g.reshape(1, C), target)


def _adamw(w, g, m, v, name):
    shape = w.shape
    C = shape[-1]
    R = w.size // C
    tr = R
    if R % 8 == 0:
        for cand in range(8, R + 1, 8):
            if R % cand == 0 and cand * C * 4 <= (1 << 20):
                tr = cand
    c1 = 1.0 - ADAM_B1 ** ADAM_STEP
    c2 = 1.0 - ADAM_B2 ** ADAM_STEP

    def body(w_ref, g_ref, m_ref, v_ref, d_ref, nm_ref, nv_ref):
        gv = g_ref[...]
        nm = ADAM_B1 * m_ref[...] + (1.0 - ADAM_B1) * gv
        nv = ADAM_B2 * v_ref[...] + (1.0 - ADAM_B2) * (gv * gv)
        nm_ref[...] = nm
        nv_ref[...] = nv
        d_ref[...] = -ADAM_LR * ((nm / c1) / (jnp.sqrt(nv / c2) + ADAM_EPS) + ADAM_WD * w_ref[...])

    blk = pl.BlockSpec((tr, C), lambda i: (i, 0))
    out = jax.ShapeDtypeStruct((R, C), F32)
    outs = pl.pallas_call(
        body,
        out_shape=(out, out, out),
        grid=(R // tr,),
        in_specs=[blk] * 4,
        out_specs=(blk, blk, blk),
        name=name,
        compiler_params=_cp(("parallel",)),
    )(*(t.reshape(R, C) for t in (w, g, m, v)))
    return tuple(t.reshape(shape) for t in outs)


def _exchange(send, axes, same, name):
    na = len(axes)
    n = 1 << na
    _, R, C = send.shape
    parts = max(p for p in (8, 4, 2, 1) if R % (16 * p) == 0 or p == 1)
    pr = R // parts

    def body(send_ref, recv_ref, send_sems, recv_sems, local_sem):
        coords = {ax: lax.axis_index(ax) for ax in MESH_AXES}
        me = 0
        for ax in axes:
            me = me * 2 + coords[ax]

        def member(r):
            dev = dict(coords)
            for b, ax in enumerate(axes):
                if (r >> (na - 1 - b)) & 1:
                    dev[ax] = 1 - dev[ax]
            return tuple(dev[ax] for ax in MESH_AXES)

        def chunk(j, p):
            return (send_ref.at[0] if same else send_ref.at[j]).at[pl.ds(p * pr, pr)]

        def slot(j, p):
            return recv_ref.at[j].at[pl.ds(p * pr, pr)]

        own = pltpu.make_async_copy(send_ref.at[0] if same else send_ref.at[me], recv_ref.at[me], local_sem)
        own.start()
        copies = []
        for r in range(1, n):
            peer = me ^ r
            for p in range(parts):
                cp = pltpu.make_async_remote_copy(
                    src_ref=chunk(peer, p), dst_ref=slot(me, p), send_sem=send_sems.at[r, p], recv_sem=recv_sems.at[r, p],
                    device_id=member(r), device_id_type=pl.DeviceIdType.MESH)
                cp.start()
                copies.append(cp)
        for r in range(1, n):
            for p in range(parts):
                arrival = pltpu.make_async_remote_copy(
                    src_ref=chunk(me, p), dst_ref=slot(me ^ r, p), send_sem=send_sems.at[r, p], recv_sem=recv_sems.at[r, p],
                    device_id=member(r), device_id_type=pl.DeviceIdType.MESH)
                arrival.wait_recv()
        for cp in copies:
            cp.wait_send()
        own.wait()

    any_spec = pl.BlockSpec(memory_space=pl.ANY)
    return pl.pallas_call(
        body,
        out_shape=jax.ShapeDtypeStruct((n, R, C), send.dtype),
        in_specs=[any_spec],
        out_specs=any_spec,
        scratch_shapes=[pltpu.SemaphoreType.DMA((n, parts)), pltpu.SemaphoreType.DMA((n, parts)), pltpu.SemaphoreType.DMA],
        name=name,
        compiler_params=pltpu.CompilerParams(has_side_effects=True),
    )(send)


def _sum_chunks(x, name, out_dtype=F32):
    n, R, C = x.shape
    tr = _row_tile(R, 512)

    def body(x_ref, o_ref):
        acc = x_ref[0].astype(F32)
        for j in range(1, n):
            acc = acc + x_ref[j].astype(F32)
        o_ref[...] = acc.astype(o_ref.dtype)

    return pl.pallas_call(
        body,
        out_shape=jax.ShapeDtypeStruct((R, C), out_dtype),
        grid=(R // tr,),
        in_specs=[pl.BlockSpec((n, tr, C), lambda i: (0, i, 0))],
        out_specs=pl.BlockSpec((tr, C), lambda i: (i, 0)),
        name=name,
        compiler_params=_cp(("parallel",)),
    )(x)


def _pad_heads_cols(w, groups, d):
    k = w.shape[0]
    w = w.reshape(k, groups * N_HEADS, d)
    return jnp.pad(w, ((0, 0), (0, 0), (0, LANES - d))).reshape(k, groups * N_HEADS * LANES)


def _unpad_heads_cols(w, groups, d):
    k = w.shape[0]
    return w.reshape(k, groups * N_HEADS, LANES)[:, :, :d].reshape(k, groups * N_HEADS * d)


def _pad_heads_rows(w, d):
    n = w.shape[1]
    return jnp.pad(w.reshape(N_HEADS, d, n), ((0, 0), (0, LANES - d), (0, 0))).reshape(N_HEADS * LANES, n)


def _unpad_heads_rows(w, d):
    n = w.shape[1]
    return w.reshape(N_HEADS, LANES, n)[:, :d].reshape(N_HEADS * d, n)


Q_SCALE = HEAD_DIM ** -0.5


def _scale_q_cols(w):
    nq = N_HEADS * LANES
    return jnp.concatenate([w[:, :nq] * Q_SCALE, w[:, nq:]], axis=1)


def _kernel_weights(W):
    P = dict(W)
    pw = W["pool_w"][0]
    bd = jnp.zeros((D_MODEL, D_MODEL), pw.dtype)
    for g in range(len(POOL_WINDOWS)):
        bd = lax.dynamic_update_slice(bd, pw[g], (g * POOL_GROUP, g * POOL_GROUP))
    P["pool_bd"] = bd
    P["sb_qkv"] = _scale_q_cols(_pad_heads_cols(W["sb_w_qkv"][0], 3, HEAD_DIM))
    P["sb_o"] = _pad_heads_rows(W["sb_w_o"][0], HEAD_DIM)
    nq = 3 * N_HEADS * HEAD_DIM
    P["fox_qkv"] = _scale_q_cols(_pad_heads_cols(W["fox_w_qkvf"][0][:, :nq], 3, HEAD_DIM))
    P["fox_f"] = jnp.pad(W["fox_w_qkvf"][0][:, nq:], ((0, 0), (0, LANES - N_HEADS)))
    P["fox_o"] = _pad_heads_rows(W["fox_w_o"][0], HEAD_DIM)
    P["fox_b"] = jnp.pad(W["fox_b_f"], ((0, 0), (0, LANES - N_HEADS)))
    P["mla_down"] = jnp.pad(W["mla_w_down"][0], ((0, 0), (0, MLA_DOWN_PAD - W["mla_w_down"].shape[2])))
    P["mla_uq"] = _pad_heads_cols(W["mla_w_uq"][0], 1, MLA_NOPE + MLA_ROPE)
    ukv = W["mla_w_ukv"][0].reshape(MLA_KV_RANK, N_HEADS, 2 * HEAD_DIM)
    padk = ((0, 0), (0, 0), (0, LANES - HEAD_DIM))
    P["mla_ukv"] = jnp.concatenate(
        [jnp.pad(ukv[:, :, :MLA_NOPE], padk).reshape(MLA_KV_RANK, -1), jnp.pad(ukv[:, :, MLA_NOPE:], padk).reshape(MLA_KV_RANK, -1)], axis=1)
    P["mla_o"] = _pad_heads_rows(W["mla_w_o"][0], HEAD_DIM)
    return P


def _rope_tables(lp):
    pos = (jnp.arange(lp) - PAD0).astype(F32)
    inv = ROPE_THETA ** (-jnp.arange(0, MLA_ROPE, 2, dtype=F32) / MLA_ROPE)
    ang = pos[:, None] * inv[None, :]
    cos, sin = jnp.cos(ang), jnp.sin(ang)
    half = MLA_ROPE // 2
    z = lambda n: jnp.zeros((lp, n), F32)
    c = jnp.concatenate([jnp.ones((lp, MLA_NOPE), F32), cos, cos, z(LANES - MLA_NOPE - MLA_ROPE)], axis=1)
    a = jnp.concatenate([z(MLA_NOPE), -sin, z(LANES - MLA_NOPE - half)], axis=1)
    b = jnp.concatenate([z(MLA_NOPE + half), sin, z(LANES - MLA_NOPE - MLA_ROPE)], axis=1)
    return c, a, b


def _key_bias(lp, t, per_head=None):
    pad = jnp.arange(lp)[None, :] < PAD0
    body = jnp.zeros((N_HEADS, lp), F32) if per_head is None else per_head
    return jnp.where(pad, NEG, body).reshape(N_HEADS, lp // t, 1, t)


def _ffn_fwd(h, i, P):
    b = _rms_fwd(h, P["norm_ffn"][i], BF16, "ffn_norm")
    g, u, hd = _ffn_up(b, P["ffn_w_gate"][i], P["ffn_w_up"][i], "ffn_gate_up")
    return _mm(hd, P["ffn_w_down"][i], "nn", "ffn_down", add=h), (h, b, g, u, hd)


def _ffn_bwd(dh, i, P, saved):
    h, b, g, u, hd = saved
    dwd = _mm(hd, dh, "tn", "ffn_down_dw")
    dg, du = _ffn_down_bwd(dh, P["ffn_w_down"][i], g, u, "ffn_down_dx")
    dwg = _mm(b, dg, "tn", "ffn_gate_dw")
    dwu = _mm(b, du, "tn", "ffn_up_dw")
    db = _mm(dg, P["ffn_w_gate"][i], "nt", "ffn_gate_dx")
    db = _mm(du, P["ffn_w_up"][i], "nt", "ffn_up_dx", add=db)
    dh_in, dgain = _rms_bwd(h, P["norm_ffn"][i], db, dh, "ffn_norm_bwd")
    return dh_in, dgain, dwg, dwu, dwd


def _pool_layer_fwd(h, P):
    a = _rms_fwd(h, P["norm_mix"][0], F32, "pool_norm")
    pooled = _pool_fwd(a, "pool_window")
    pre = _mm(pooled, P["pool_bd"], "nn", "pool_mix")
    return _scale_add(h, pre, P["pool_scale"][0], "pool_scale_add"), (h, pooled, pre)


def _pool_layer_bwd(dh, P, saved):
    h, pooled, pre = saved
    dpre, dscale = _scale_bwd(dh, pre, P["pool_scale"][0], "pool_scale_bwd")
    dbd = _mm(pooled, dpre, "tn", "pool_mix_dw")
    dpooled = _mm(dpre, P["pool_bd"], "nt", "pool_mix_dx")
    da = _pool_bwd(dpooled, "pool_window_bwd")
    dh_in, dgain = _rms_bwd(h, P["norm_mix"][0], da, dh, "mix_norm_bwd")
    dw = jnp.stack([dbd[g * POOL_GROUP:(g + 1) * POOL_GROUP, g * POOL_GROUP:(g + 1) * POOL_GROUP] for g in range(len(POOL_WINDOWS))])
    return dh_in, {"norm_mix0": dgain, "pool_w": dw[None], "pool_scale": dscale}


def _out_proj_bwd(o, dh, wo, tag):
    return _mm(o, dh, "tn", tag + "_o_dw"), _mm(dh, wo, "nt", tag + "_o_dx", out_dtype=BF16)


def _sb_layer_fwd(h, P):
    lp = h.shape[0]
    a = _rms_fwd(h, P["norm_mix"][1], BF16, "mix_norm")
    qkv = _mm(a, P["sb_qkv"], "nn", "sb_qkv", out_dtype=BF16)
    kb = _key_bias(lp, SB_TK)
    o = _sb_fwd(qkv, kb, nh=N_HEADS, name="sb_attn")
    return _mm(o, P["sb_o"], "nn", "attn_out", add=h), (h, a, qkv, kb, o)


def _sb_layer_bwd(dh, P, saved):
    h, a, qkv, kb, o = saved
    dwo, do = _out_proj_bwd(o, dh, P["sb_o"], "attn")
    dq, dk, dv = _sb_bwd(qkv, kb, do, nh=N_HEADS, name="sb_attn_bwd")
    dqkv = jnp.concatenate([dq, dk, dv], axis=1)
    dw = _scale_q_cols(_mm(a, dqkv, "tn", "qkv_dw"))
    da = _mm(dqkv, P["sb_qkv"], "nt", "qkv_dx")
    dh_in, dgain = _rms_bwd(h, P["norm_mix"][1], da, dh, "mix_norm_bwd")
    return dh_in, {"norm_mix1": dgain, "sb_w_qkv": _unpad_heads_cols(dw, 3, HEAD_DIM)[None], "sb_w_o": _unpad_heads_rows(dwo, HEAD_DIM)[None]}


def _fox_layer_fwd(h, P):
    lp = h.shape[0]
    t = _attn_tile(lp)
    a = _rms_fwd(h, P["norm_mix"][3], BF16, "mix_norm")
    qkv = _mm(a, P["fox_qkv"], "nn", "sb_qkv", out_dtype=BF16)
    f = _mm(a, P["fox_f"], "nn", "fox_gate_proj")
    fc = _gate_fwd(f, P["fox_b"], "fox_gate")[:, :N_HEADS]
    kb = _key_bias(lp, t, -fc.T)
    fq = jnp.broadcast_to(fc[:, :, None], (lp, N_HEADS, LANES)).reshape(lp, N_HEADS * LANES)
    o, lse = _attn_fwd(qkv, qkv, qkv, kb, fq, nh=N_HEADS, offs=(0, N_HEADS, 2 * N_HEADS), name="fox_attn")
    return _mm(o, P["fox_o"], "nn", "attn_out", add=h), (h, a, qkv, f, kb, fq, o, lse)


def _fox_layer_bwd(dh, P, saved):
    h, a, qkv, f, kb, fq, o, lse = saved
    lp = h.shape[0]
    dwo, do = _out_proj_bwd(o, dh, P["fox_o"], "attn")
    dq, dk, dv, dkb, dqb = _attn_bwd(qkv, qkv, qkv, kb, fq, o, do, lse, nh=N_HEADS, offs=(0, N_HEADS, 2 * N_HEADS),
                                     name="fox_attn_bwd")
    dfc = jnp.pad(dqb.reshape(lp, N_HEADS, LANES)[:, :, 0] - dkb.reshape(N_HEADS, lp).T, ((0, 0), (0, LANES - N_HEADS)))
    df, dbf = _gate_bwd(f, P["fox_b"], dfc, "fox_gate_bwd")
    dqkv = jnp.concatenate([dq, dk, dv], axis=1)
    dw = _scale_q_cols(_mm(a, dqkv, "tn", "qkv_dw"))
    dwf = _mm(a, df, "tn", "fox_gate_dw")
    da = _mm(dqkv, P["fox_qkv"], "nt", "qkv_dx")
    da = _mm(df, P["fox_f"], "nt", "fox_gate_dx", add=da)
    dh_in, dgain = _rms_bwd(h, P["norm_mix"][3], da, dh, "mix_norm_bwd")
    dwqkvf = jnp.concatenate([_unpad_heads_cols(dw, 3, HEAD_DIM), dwf[:, :N_HEADS]], axis=1)
    return dh_in, {"norm_mix3": dgain, "fox_w_qkvf": dwqkvf[None], "fox_b_f": dbf[:, :N_HEADS], "fox_w_o": _unpad_heads_rows(dwo, HEAD_DIM)[None]}


def _mla_layer_fwd(h, P):
    lp = h.shape[0]
    a = _rms_fwd(h, P["norm_mix"][2], BF16, "mix_norm")
    down = _mm(a, P["mla_down"], "nn", "mla_down")
    cq_pre = down[:, :MLA_Q_RANK]
    ckv_pre = down[:, MLA_Q_RANK:MLA_Q_RANK + MLA_KV_RANK]
    kr = jnp.pad(down[:, MLA_Q_RANK + MLA_KV_RANK:MLA_Q_RANK + MLA_KV_RANK + MLA_ROPE], ((0, 0), (MLA_NOPE, LANES - MLA_NOPE - MLA_ROPE)))
    cq = _rms_fwd(cq_pre, P["mla_q_norm"][0], BF16, "mla_q_norm")
    ckv = _rms_fwd(ckv_pre, P["mla_kv_norm"][0], BF16, "mla_kv_norm")
    q = _mm(cq, P["mla_uq"], "nn", "mla_uq")
    kv = _mm(ckv, P["mla_ukv"], "nn", "mla_ukv", out_dtype=BF16)
    tabs = _rope_tables(lp)
    qr, kc = _mla_prep_fwd(q, kv, kr, *tabs, "mla_rope")
    kb = _key_bias(lp, _attn_tile(lp))
    o, lse = _attn_fwd(qr, kc, kv, kb, None, nh=N_HEADS, offs=(0, 0, N_HEADS), name="mla_attn")
    return _mm(o, P["mla_o"], "nn", "attn_out", add=h), (h, a, cq_pre, ckv_pre, cq, ckv, qr, kc, kv, tabs, kb, o, lse)


def _mla_layer_bwd(dh, P, saved):
    h, a, cq_pre, ckv_pre, cq, ckv, qr, kc, kv, tabs, kb, o, lse = saved
    lp = h.shape[0]
    dwo, do = _out_proj_bwd(o, dh, P["mla_o"], "attn")
    dqr, dkc, dv = _attn_bwd(qr, kc, kv, kb, None, o, do, lse, nh=N_HEADS, offs=(0, 0, N_HEADS),
                             name="mla_attn_bwd")
    dq, dkr = _mla_prep_bwd(dqr, dkc, *tabs, "mla_rope_bwd")
    dkv = jnp.concatenate([dkc, dv], axis=1)
    dwuq = _mm(cq, dq, "tn", "mla_uq_dw")
    dcq = _mm(dq, P["mla_uq"], "nt", "mla_uq_dx")
    dwukv = _mm(ckv, dkv, "tn", "mla_ukv_dw")
    dckv = _mm(dkv, P["mla_ukv"], "nt", "mla_ukv_dx")
    dcq_pre, dqn = _rms_bwd(cq_pre, P["mla_q_norm"][0], dcq, None, "mla_q_norm_bwd")
    dckv_pre, dkvn = _rms_bwd(ckv_pre, P["mla_kv_norm"][0], dckv, None, "mla_kv_norm_bwd")
    used = MLA_Q_RANK + MLA_KV_RANK + MLA_ROPE
    ddown = jnp.concatenate([dcq_pre, dckv_pre, dkr[:, MLA_NOPE:MLA_NOPE + MLA_ROPE], jnp.zeros((lp, MLA_DOWN_PAD - used), F32)], axis=1)
    dwdown = _mm(a, ddown, "tn", "mla_down_dw")
    da = _mm(ddown, P["mla_down"], "nt", "mla_down_dx")
    dh_in, dgain = _rms_bwd(h, P["norm_mix"][2], da, dh, "mix_norm_bwd")
    dukv = dwukv.reshape(MLA_KV_RANK, 2, N_HEADS, LANES)[:, :, :, :HEAD_DIM]
    dukv = jnp.concatenate([dukv[:, 0], dukv[:, 1]], axis=-1).reshape(MLA_KV_RANK, N_HEADS * 2 * HEAD_DIM)
    return dh_in, {
        "norm_mix2": dgain, "mla_w_down": dwdown[:, :used][None], "mla_q_norm": dqn, "mla_kv_norm": dkvn,
        "mla_w_uq": _unpad_heads_cols(dwuq, 1, MLA_NOPE + MLA_ROPE)[None], "mla_w_ukv": dukv[None],
        "mla_w_o": _unpad_heads_rows(dwo, HEAD_DIM)[None]}


_MIXERS = ((_pool_layer_fwd, _pool_layer_bwd), (_sb_layer_fwd, _sb_layer_bwd), (_mla_layer_fwd, _mla_layer_bwd), (_fox_layer_fwd, _fox_layer_bwd))


def _step_local(x, target, W):
    seq = x.shape[0]
    P = _kernel_weights(W)
    h = jnp.concatenate([jnp.zeros((PAD0, D_MODEL), F32), W["meta"], x], axis=0)
    tpad = jnp.pad(target, ((PAD0 + N_META, 0), (0, 0)))
    saved = []
    for i in range(4):
        h, s_mix = _MIXERS[i][0](h, P)
        h, s_ffn = _ffn_fwd(h, i, P)
        saved.append((s_mix, s_ffn))
    loss, dh, dfinal = _loss_head(h, W["final_norm"], tpad, "loss_head")
    grads = {"final_norm": dfinal.reshape(-1)}
    gains_mix, gains_ffn, dwg, dwu, dwd = [None] * 4, [None] * 4, [None] * 4, [None] * 4, [None] * 4
    for i in reversed(range(4)):
        s_mix, s_ffn = saved[i]
        dh, gains_ffn[i], dwg[i], dwu[i], dwd[i] = _ffn_bwd(dh, i, P, s_ffn)
        dh, g = _MIXERS[i][1](dh, P, s_mix)
        gains_mix[i] = g.pop("norm_mix%d" % i)
        grads.update(g)
    grads["norm_mix"] = jnp.concatenate(gains_mix, axis=0)
    grads["norm_ffn"] = jnp.concatenate(gains_ffn, axis=0)
    grads["ffn_w_gate"] = jnp.stack(dwg)
    grads["ffn_w_up"] = jnp.stack(dwu)
    grads["ffn_w_down"] = jnp.stack(dwd)
    grads["meta"] = dh[PAD0:PAD0 + N_META]
    return loss, dh[PAD0 + N_META:], grads


_WEIGHTS = ("meta", "norm_mix", "norm_ffn", "pool_w", "pool_scale", "sb_w_qkv", "sb_w_o", "mla_w_down", "mla_q_norm",
            "mla_kv_norm", "mla_w_uq", "mla_w_ukv", "mla_w_o", "fox_w_qkvf", "fox_b_f", "fox_w_o", "ffn_w_gate",
            "ffn_w_up", "ffn_w_down", "final_norm")
_SHARD_AXIS = {"meta": 1, "pool_w": 2, "sb_w_qkv": 2, "sb_w_o": 1, "mla_w_down": 1, "mla_q_norm": 1, "mla_kv_norm": 1,
               "mla_w_uq": 2, "mla_w_ukv": 2, "mla_w_o": 1, "fox_w_qkvf": 2, "fox_b_f": None, "fox_w_o": 1,
               "ffn_w_gate": 2, "ffn_w_up": 2, "ffn_w_down": 1}
_SHARDED = tuple(n for n in _WEIGHTS if _SHARD_AXIS.get(n) is not None)
_REPLICATED = tuple(n for n in _WEIGHTS if _SHARD_AXIS.get(n) is None)
_EXACT = ("meta", "mla_q_norm", "mla_kv_norm")
N_CHIPS = 4
GRAD_ROW_TILE = 512


PACK_ROWS = 16


def _piece_rows(t):
    return -(-t.size // (LANES * PACK_ROWS)) * PACK_ROWS


def _flat_rows(parts, dtype, row_multiple):
    pieces = []
    for p in parts:
        flat = p.astype(dtype).reshape(-1)
        pieces.append(jnp.pad(flat, (0, _piece_rows(p) * LANES - flat.shape[0])).reshape(-1, LANES))
    rows = sum(q.shape[0] for q in pieces)
    pad = -(-rows // row_multiple) * row_multiple - rows
    if pad:
        pieces.append(jnp.zeros((pad, LANES), dtype))
    return jnp.concatenate(pieces, axis=0)


def _split_flat(flat, like):
    out, off = [], 0
    for t in like:
        out.append(flat[off:off + _piece_rows(t)].reshape(-1)[:t.size].reshape(t.shape))
        off += _piece_rows(t)
    return out


def _gather_shards(local, names, dtype, name):
    blocks = [local[n] for n in names]
    recv = _exchange(_flat_rows(blocks, dtype, PACK_ROWS)[None], ("x", "y"), True, name)
    per_chip = [_split_flat(recv[s], blocks) for s in range(N_CHIPS)]
    return {n: jnp.concatenate([per_chip[s][k] for s in range(N_CHIPS)], axis=_SHARD_AXIS[n]) for k, n in enumerate(names)}


def _shard_of(g, n, s):
    w = g.shape[_SHARD_AXIS[n]] // N_CHIPS
    return lax.slice_in_dim(g, s * w, (s + 1) * w, axis=_SHARD_AXIS[n])


def _train_step(a):
    local = {n: a[n] for n in _WEIGHTS}
    full = {n: local[n] for n in _REPLICATED}
    full.update(_gather_shards(local, [n for n in _SHARDED if n not in _EXACT], BF16, "gather_weights"))
    full.update(_gather_shards(local, list(_EXACT), F32, "gather_exact"))

    loss, grad_x, grads = _step_local(a["x"][0], a["loss_target"][0], full)

    send = jnp.stack([
        _flat_rows([_shard_of(grads[n], n, s) for n in _SHARDED], BF16, 2 * GRAD_ROW_TILE).reshape(2, -1, LANES)
        for s in range(N_CHIPS)]).reshape(2 * N_CHIPS, -1, LANES)
    mine = _sum_chunks(_exchange(send, MESH_AXES, False, "scatter_grads"), "sum_grads", out_dtype=BF16)
    both = _exchange(mine[None], ("c",), True, "pair_grads").reshape(-1, LANES).astype(F32)
    reduced = dict(zip(_SHARDED, _split_flat(both, [local[n] for n in _SHARDED])))
    small = _flat_rows([grads[n] for n in _REPLICATED], F32, 8)
    small = _sum_chunks(_exchange(small[None], MESH_AXES, True, "gather_small_grads"), "sum_small_grads")
    reduced.update(zip(_REPLICATED, _split_flat(small, [local[n] for n in _REPLICATED])))

    deltas, new_m, new_v = {}, {}, {}
    for n in _WEIGHTS:
        deltas[n], new_m[n], new_v[n] = _adamw(local[n], reduced[n], a["m_" + n], a["v_" + n], "adamw")
    total = lax.psum(loss[0, 0], MESH_AXES)
    return (total, grad_x[None], *[reduced[n] for n in _WEIGHTS], *[deltas[n] for n in _WEIGHTS],
            *[new_m[n] for n in _WEIGHTS], *[new_v[n] for n in _WEIGHTS])


def kernel(x, meta, norm_mix, norm_ffn, pool_w, pool_scale, sb_w_qkv, sb_w_o, mla_w_down, mla_q_norm, mla_kv_norm, mla_w_uq, mla_w_ukv, mla_w_o, fox_w_qkvf, fox_b_f, fox_w_o, ffn_w_gate, ffn_w_up, ffn_w_down, final_norm, loss_target, m_meta, m_norm_mix, m_norm_ffn, m_pool_w, m_pool_scale, m_sb_w_qkv, m_sb_w_o, m_mla_w_down, m_mla_q_norm, m_mla_kv_norm, m_mla_w_uq, m_mla_w_ukv, m_mla_w_o, m_fox_w_qkvf, m_fox_b_f, m_fox_w_o, m_ffn_w_gate, m_ffn_w_up, m_ffn_w_down, m_final_norm, v_meta, v_norm_mix, v_norm_ffn, v_pool_w, v_pool_scale, v_sb_w_qkv, v_sb_w_o, v_mla_w_down, v_mla_q_norm, v_mla_kv_norm, v_mla_w_uq, v_mla_w_ukv, v_mla_w_o, v_fox_w_qkvf, v_fox_b_f, v_fox_w_o, v_ffn_w_gate, v_ffn_w_up, v_ffn_w_down, v_final_norm):
    return _train_step(dict(locals()))
```

```python
import functools

import jax
import jax.numpy as jnp
from jax import lax
from jax.experimental import pallas as pl
from jax.experimental.pallas import tpu as pltpu

F32 = jnp.float32
BF16 = jnp.bfloat16

D_MODEL = 1024
N_META = 16
PAD0 = 112
LANES = 128
N_HEADS = 16
HEAD_DIM = 64
POOL_WINDOWS = (2, 4, 8, 16)
POOL_GROUP = 256
POOL_HALO = 16
MLA_Q_RANK = 384
MLA_KV_RANK = 256
MLA_NOPE = 64
MLA_ROPE = 32
MLA_DOWN_PAD = 768
ROPE_THETA = 10000.0
D_FF = 2816
EPS = 1e-6
NEG = -1e30
ADAM_LR = 0.001
ADAM_B1 = 0.9
ADAM_B2 = 0.999
ADAM_EPS = 1e-08
ADAM_WD = 0.01
ADAM_STEP = 10
VMEM_LIMIT = 56 * 1024 * 1024
MESH_AXES = ("x", "y", "c")


def _cp(sem, **kw):
    return pltpu.CompilerParams(dimension_semantics=sem, vmem_limit_bytes=VMEM_LIMIT, **kw)


def _row_tile(m, target):
    best = None
    for t in range(16, min(m, target) + 1, 16):
        if m % t == 0:
            best = t
    return best or m


def _col_tile(n, target):
    best = None
    for t in range(LANES, min(n, target) + 1, LANES):
        if n % t == 0:
            best = t
    return best or n


def _mm(a, b, mode, name, out_dtype=F32, add=None, tm=640, tn=1536, tk=2048):
    if mode == "nn":
        (M, K), (K2, N) = a.shape, b.shape
    elif mode == "nt":
        (M, K), (N, K2) = a.shape, b.shape
    else:
        (K, M), (K2, N) = a.shape, b.shape
    assert K == K2, (mode, a.shape, b.shape)
    if mode == "tn":
        tm_ = _col_tile(M, 1408)
        tk_ = _row_tile(K, 1664)
    else:
        tm_ = _row_tile(M, tm)
        tk_ = _col_tile(K, tk) if K > tk else K
    tn_ = _col_tile(N, tn)
    nk = K // tk_
    if mode == "nn":
        a_spec = pl.BlockSpec((tm_, tk_), lambda i, j, k: (i, k))
        b_spec = pl.BlockSpec((tk_, tn_), lambda i, j, k: (k, j))
        dims = (((1,), (0,)), ((), ()))
    elif mode == "nt":
        a_spec = pl.BlockSpec((tm_, tk_), lambda i, j, k: (i, k))
        b_spec = pl.BlockSpec((tn_, tk_), lambda i, j, k: (j, k))
        dims = (((1,), (1,)), ((), ()))
    else:
        a_spec = pl.BlockSpec((tk_, tm_), lambda i, j, k: (k, i))
        b_spec = pl.BlockSpec((tk_, tn_), lambda i, j, k: (k, j))
        dims = (((0,), (0,)), ((), ()))
    o_spec = pl.BlockSpec((tm_, tn_), lambda i, j, k: (i, j))
    has_add = add is not None

    def body(*refs):
        if has_add:
            a_ref, b_ref, add_ref, o_ref, acc_ref = refs
        else:
            a_ref, b_ref, o_ref, acc_ref = refs
        k = pl.program_id(2)
        part = lax.dot_general(a_ref[...].astype(BF16), b_ref[...].astype(BF16), dims, preferred_element_type=F32)

        @pl.when(k == 0)
        def _():
            acc_ref[...] = part

        @pl.when(k > 0)
        def _():
            acc_ref[...] += part

        @pl.when(k == nk - 1)
        def _():
            r = acc_ref[...]
            if has_add:
                r = r + add_ref[...]
            o_ref[...] = r.astype(o_ref.dtype)

    ins = [a, b] + ([add] if has_add else [])
    in_specs = [a_spec, b_spec] + ([o_spec] if has_add else [])
    return pl.pallas_call(
        body,
        out_shape=jax.ShapeDtypeStruct((M, N), out_dtype),
        grid=(M // tm_, N // tn_, nk),
        in_specs=in_specs,
        out_specs=o_spec,
        scratch_shapes=[pltpu.VMEM((tm_, tn_), F32)],
        name=name,
        compiler_params=_cp(("parallel", "parallel", "arbitrary")),
    )(*ins)


def _rms_fwd(x, g, out_dtype, name):
    M, C = x.shape
    tm = _row_tile(M, 640)

    def body(x_ref, g_ref, o_ref):
        xf = x_ref[...]
        r = lax.rsqrt(jnp.mean(xf * xf, axis=-1, keepdims=True) + EPS)
        o_ref[...] = ((xf * r) * g_ref[...]).astype(o_ref.dtype)

    return pl.pallas_call(
        body,
        out_shape=jax.ShapeDtypeStruct((M, C), out_dtype),
        grid=(M // tm,),
        in_specs=[pl.BlockSpec((tm, C), lambda i: (i, 0)), pl.BlockSpec((1, C), lambda i: (0, 0))],
        out_specs=pl.BlockSpec((tm, C), lambda i: (i, 0)),
        name=name,
        compiler_params=_cp(("parallel",)),
    )(x, g.reshape(1, C))


def _rms_bwd(x, g, dy, dres, name):
    M, C = x.shape
    tm = _row_tile(M, 640)
    has_res = dres is not None

    def body(*refs):
        if has_res:
            x_ref, g_ref, dy_ref, dres_ref, dx_ref, dg_ref = refs
        else:
            x_ref, g_ref, dy_ref, dx_ref, dg_ref = refs
        xf = x_ref[...]
        r = lax.rsqrt(jnp.mean(xf * xf, axis=-1, keepdims=True) + EPS)
        xhat = xf * r
        dyf = dy_ref[...].astype(F32)

        @pl.when(pl.program_id(0) == 0)
        def _():
            dg_ref[...] = jnp.zeros_like(dg_ref)

        dg_ref[...] += jnp.sum(dyf * xhat, axis=0, keepdims=True)
        dxh = dyf * g_ref[...]
        dx = r * (dxh - xhat * jnp.mean(dxh * xhat, axis=-1, keepdims=True))
        if has_res:
            dx = dx + dres_ref[...]
        dx_ref[...] = dx

    row = pl.BlockSpec((tm, C), lambda i: (i, 0))
    vec = pl.BlockSpec((1, C), lambda i: (0, 0))
    ins = [x, g.reshape(1, C), dy] + ([dres] if has_res else [])
    return pl.pallas_call(
        body,
        out_shape=(jax.ShapeDtypeStruct((M, C), F32), jax.ShapeDtypeStruct((1, C), F32)),
        grid=(M // tm,),
        in_specs=[row, vec, row] + ([row] if has_res else []),
        out_specs=(row, vec),
        name=name,
        compiler_params=_cp(("arbitrary",)),
    )(*ins)


def _sigmoid(x):
    return 1.0 / (1.0 + jnp.exp(-x))


def _ffn_up(b, wg, wu, name):
    M, K = b.shape
    N = wg.shape[1]
    tm, tn = _row_tile(M, 640), _col_tile(N, 1536)

    def body(b_ref, wg_ref, wu_ref, g_ref, u_ref, h_ref):
        bv = b_ref[...]
        g = jnp.dot(bv, wg_ref[...], preferred_element_type=F32)
        u = jnp.dot(bv, wu_ref[...], preferred_element_type=F32)
        g_ref[...] = g
        u_ref[...] = u
        h_ref[...] = ((g * _sigmoid(g)) * u).astype(h_ref.dtype)

    w_spec = pl.BlockSpec((K, tn), lambda i, j: (0, j))
    o_spec = pl.BlockSpec((tm, tn), lambda i, j: (i, j))
    f32 = jax.ShapeDtypeStruct((M, N), F32)
    return pl.pallas_call(
        body,
        out_shape=(f32, f32, jax.ShapeDtypeStruct((M, N), BF16)),
        grid=(M // tm, N // tn),
        in_specs=[pl.BlockSpec((tm, K), lambda i, j: (i, 0)), w_spec, w_spec],
        out_specs=(o_spec, o_spec, o_spec),
        name=name,
        compiler_params=_cp(("parallel", "parallel")),
    )(b, wg, wu)


def _ffn_down_bwd(dh, wd, g, u, name):
    M, K = dh.shape
    N = wd.shape[0]
    tm, tn = _row_tile(M, 640), _col_tile(N, 1536)

    def body(dh_ref, wd_ref, g_ref, u_ref, dg_ref, du_ref):
        d = lax.dot_general(dh_ref[...].astype(BF16), wd_ref[...], _NT, preferred_element_type=F32)
        gv = g_ref[...]
        sg = _sigmoid(gv)
        du_ref[...] = (d * (gv * sg)).astype(du_ref.dtype)
        dg_ref[...] = ((d * u_ref[...]) * (sg * (1.0 + gv * (1.0 - sg)))).astype(dg_ref.dtype)

    o_spec = pl.BlockSpec((tm, tn), lambda i, j: (i, j))
    bf = jax.ShapeDtypeStruct((M, N), BF16)
    return pl.pallas_call(
        body,
        out_shape=(bf, bf),
        grid=(M // tm, N // tn),
        in_specs=[pl.BlockSpec((tm, K), lambda i, j: (i, 0)), pl.BlockSpec((tn, K), lambda i, j: (j, 0)), o_spec, o_spec],
        out_specs=(o_spec, o_spec),
        name=name,
        compiler_params=_cp(("parallel", "parallel")),
    )(dh, wd, g, u)


ATTN_GROUP = 3


def _attn_tile(lp):
    return _col_tile(lp, 640)


def _head_specs(lp, t, offs):
    q_spec = pl.BlockSpec((t, LANES), lambda h, i: (i, offs[0] + h))
    k_spec = pl.BlockSpec((lp, LANES), lambda h, i: (0, offs[1] + h))
    v_spec = pl.BlockSpec((lp, LANES), lambda h, i: (0, offs[2] + h))
    return q_spec, k_spec, v_spec


def _attn_fwd(qa, ka, va, kb, fq, *, nh, offs, name, tile=640):
    lp = qa.shape[0]
    t = _col_tile(lp, tile)
    nq = lp // t
    has_fq = fq is not None

    def body(*refs):
        if has_fq:
            q_ref, k_ref, v_ref, kb_ref, fq_ref, o_ref, lse_ref = refs
        else:
            q_ref, k_ref, v_ref, kb_ref, o_ref, lse_ref = refs
        i = pl.program_id(1)
        q = q_ref[...]
        fqc = fq_ref[:, 0:1] if has_fq else None
        causal = lax.broadcasted_iota(jnp.int32, (t, t), 1) <= lax.broadcasted_iota(jnp.int32, (t, t), 0)

        def step(j, carry, masked):
            rows = pl.ds(pl.multiple_of(j * t, t), t)
            s = biased(lax.dot_general(q, k_ref[rows, :], _NT, preferred_element_type=F32), j)
            if masked:
                s = jnp.where(causal, s, NEG)
            return update(s, v_ref[rows, :], carry)

        def update(s, v, carry):
            m, l, acc = carry
            m_new = jnp.maximum(m, jnp.max(s, axis=1, keepdims=True))
            p = jnp.exp(s - m_new)
            alpha = jnp.exp(m - m_new)
            l = alpha * l + jnp.sum(p, axis=1, keepdims=True)
            acc = alpha * acc + jnp.dot(p.astype(BF16), v, preferred_element_type=F32)
            return m_new, l, acc

        def biased(s, j):
            bias = kb_ref[j]
            if has_fq:
                bias = fqc + bias
            return s + bias

        def group(g, carry):
            js = [ATTN_GROUP * g + u for u in range(ATTN_GROUP)]
            rws = [pl.ds(pl.multiple_of(j * t, t), t) for j in js]
            scores = [lax.dot_general(q, k_ref[r, :], _NT, preferred_element_type=F32) for r in rws]
            for j, r, s in zip(js, rws, scores):
                carry = update(biased(s, j), v_ref[r, :], carry)
            return carry

        init = (jnp.full((t, 1), NEG, F32), jnp.zeros((t, 1), F32), jnp.zeros((t, LANES), F32))
        carry = lax.fori_loop(0, i // ATTN_GROUP, group, init)
        done = (i // ATTN_GROUP) * ATTN_GROUP
        carry = lax.fori_loop(0, i % ATTN_GROUP, lambda u, c: step(done + u, c, False), carry)
        m, l, acc = step(i, carry, True)
        valid = (i * t + lax.broadcasted_iota(jnp.int32, (t, 1), 0)) >= PAD0
        o_ref[...] = jnp.where(valid, acc / l, 0.0).astype(o_ref.dtype)
        lse_ref[...] = jnp.broadcast_to(m + jnp.log(l), (t, LANES))

    q_spec, k_spec, v_spec = _head_specs(lp, t, offs)
    kb_spec = pl.BlockSpec((None, nq, 1, t), lambda h, i: (h, 0, 0, 0))
    row_spec = pl.BlockSpec((t, LANES), lambda h, i: (i, h))
    ins = [qa, ka, va, kb] + ([fq] if has_fq else [])
    return pl.pallas_call(
        body,
        out_shape=(jax.ShapeDtypeStruct((lp, nh * LANES), BF16), jax.ShapeDtypeStruct((lp, nh * LANES), F32)),
        grid=(nh, nq),
        in_specs=[q_spec, k_spec, v_spec, kb_spec] + ([row_spec] if has_fq else []),
        out_specs=(row_spec, row_spec),
        name=name,
        compiler_params=_cp(("parallel", "arbitrary")),
    )(*ins)


def _attn_fwd_t(qa, ka, va, kb, *, nh, offs, name, tile=640):
    lp = qa.shape[0]
    t = _col_tile(lp, tile)
    nq = lp // t
    kbt = kb.reshape(nh, nq, t, 1)

    def body(q_ref, k_ref, v_ref, kb_ref, o_ref, lse_ref):
        i = pl.program_id(1)
        q = q_ref[...]
        causal = lax.broadcasted_iota(jnp.int32, (t, t), 0) <= lax.broadcasted_iota(jnp.int32, (t, t), 1)

        def update(s, v, carry):
            m, l, acc = carry
            m_new = jnp.maximum(m, jnp.max(s, axis=0, keepdims=True))
            p = jnp.exp(s - m_new)
            alpha = jnp.exp(m - m_new)
            l = alpha * l + jnp.sum(p, axis=0, keepdims=True)
            acc = alpha * acc + lax.dot_general(v, p.astype(BF16), _TN, preferred_element_type=F32)
            return m_new, l, acc

        def scores(j):
            rows = pl.ds(pl.multiple_of(j * t, t), t)
            return rows, lax.dot_general(k_ref[rows, :], q, _NT, preferred_element_type=F32)

        def step(j, carry, masked):
            rows, s = scores(j)
            s = s + kb_ref[j]
            if masked:
                s = jnp.where(causal, s, NEG)
            return update(s, v_ref[rows, :], carry)

        def group(g, carry):
            js = [ATTN_GROUP * g + u for u in range(ATTN_GROUP)]
            tiles = [scores(j) for j in js]
            for j, (rows, s) in zip(js, tiles):
                carry = update(s + kb_ref[j], v_ref[rows, :], carry)
            return carry

        init = (jnp.full((1, t), NEG, F32), jnp.zeros((1, t), F32), jnp.zeros((LANES, t), F32))
        carry = lax.fori_loop(0, i // ATTN_GROUP, group, init)
        done = (i // ATTN_GROUP) * ATTN_GROUP
        carry = lax.fori_loop(0, i % ATTN_GROUP, lambda u, c: step(done + u, c, False), carry)
        m, l, acc = step(i, carry, True)
        valid = (i * t + lax.broadcasted_iota(jnp.int32, (1, t), 1)) >= PAD0
        o_ref[...] = jnp.where(valid, acc / l, 0.0).T.astype(o_ref.dtype)
        lse_ref[...] = jnp.broadcast_to(m + jnp.log(l), (LANES, t)).T

    q_spec, k_spec, v_spec = _head_specs(lp, t, offs)
    kb_spec = pl.BlockSpec((None, nq, t, 1), lambda h, i: (h, 0, 0, 0))
    row_spec = pl.BlockSpec((t, LANES), lambda h, i: (i, h))
    return pl.pallas_call(
        body,
        out_shape=(jax.ShapeDtypeStruct((lp, nh * LANES), BF16), jax.ShapeDtypeStruct((lp, nh * LANES), F32)),
        grid=(nh, nq),
        in_specs=[q_spec, k_spec, v_spec, kb_spec],
        out_specs=(row_spec, row_spec),
        name=name,
        compiler_params=_cp(("parallel", "arbitrary")),
    )(qa, ka, va, kbt)


def _attn_bwd(qa, ka, va, kb, fq, o, do, lse, *, nh, offs, name, tile=640):
    lp = qa.shape[0]
    t = _col_tile(lp, tile)
    nq = lp // t
    has_fq = fq is not None

    def body(*refs):
        if has_fq:
            q_ref, k_ref, v_ref, kb_ref, fq_ref, o_ref, do_ref, lse_ref, dq_ref, dk_ref, dv_ref, dkb_ref, dqb_ref, dk_acc, dv_acc = refs
        else:
            q_ref, k_ref, v_ref, kb_ref, o_ref, do_ref, lse_ref, dq_ref, dk_ref, dv_ref, dk_acc, dv_acc = refs
        i = pl.program_id(1)

        @pl.when(i == 0)
        def _():
            dk_acc[...] = jnp.zeros_like(dk_acc)
            dv_acc[...] = jnp.zeros_like(dv_acc)
            if has_fq:
                dkb_ref[...] = jnp.zeros_like(dkb_ref)

        q = q_ref[...]
        dov = do_ref[...]
        delta = jnp.sum(o_ref[...].astype(F32) * dov.astype(F32), axis=1, keepdims=True)
        lse_c = lse_ref[:, 0:1]
        fqc = fq_ref[:, 0:1] if has_fq else None
        causal = lax.broadcasted_iota(jnp.int32, (t, t), 1) <= lax.broadcasted_iota(jnp.int32, (t, t), 0)

        def step(j, carry, masked):
            dq_acc, rs = carry
            st = pl.multiple_of(j * t, t)
            k = k_ref[pl.ds(st, t), :]
            v = v_ref[pl.ds(st, t), :]
            s = lax.dot_general(q, k, _NT, preferred_element_type=F32)
            bias = kb_ref[j]
            if has_fq:
                bias = fqc + bias
            s = s + bias
            if masked:
                s = jnp.where(causal, s, NEG)
            p = jnp.exp(s - lse_c)
            dp = lax.dot_general(dov, v, (((1,), (1,)), ((), ())), preferred_element_type=F32)
            ds = p * (dp - delta)
            dv_acc[pl.ds(st, t), :] += lax.dot_general(p.astype(BF16), dov, (((0,), (0,)), ((), ())), preferred_element_type=F32)
            dsb = ds.astype(BF16)
            dk_acc[pl.ds(st, t), :] += lax.dot_general(dsb, q, (((0,), (0,)), ((), ())), preferred_element_type=F32)
            if has_fq:
                dkb_ref[j] += jnp.sum(ds, axis=0, keepdims=True)
                rs = rs + jnp.sum(ds, axis=1, keepdims=True)
            return dq_acc + jnp.dot(dsb, k, preferred_element_type=F32), rs

        carry = (jnp.zeros((t, LANES), F32), jnp.zeros((t, 1), F32))
        carry = lax.fori_loop(0, i, lambda j, c: step(j, c, False), carry)
        dq_acc, rs = step(i, carry, True)
        dq_ref[...] = dq_acc.astype(dq_ref.dtype)
        if has_fq:
            dqb_ref[...] = jnp.broadcast_to(rs, (t, LANES))

        @pl.when(i == nq - 1)
        def _():
            dk_ref[...] = dk_acc[...].astype(dk_ref.dtype)
            dv_ref[...] = dv_acc[...].astype(dv_ref.dtype)

    q_spec, k_spec, v_spec = _head_specs(lp, t, offs)
    kb_spec = pl.BlockSpec((None, nq, 1, t), lambda h, i: (h, 0, 0, 0))
    row_spec = pl.BlockSpec((t, LANES), lambda h, i: (i, h))
    col_spec = pl.BlockSpec((lp, LANES), lambda h, i: (0, h))
    ins = [qa, ka, va, kb] + ([fq] if has_fq else []) + [o, do, lse]
    wide = jax.ShapeDtypeStruct((lp, nh * LANES), BF16)
    extra_shapes = (jax.ShapeDtypeStruct(kb.shape, F32), jax.ShapeDtypeStruct((lp, nh * LANES), F32)) if has_fq else ()
    extra_specs = (kb_spec, row_spec) if has_fq else ()
    return pl.pallas_call(
        body,
        out_shape=(wide, wide, wide) + extra_shapes,
        grid=(nh, nq),
        in_specs=[q_spec, k_spec, v_spec, kb_spec] + ([row_spec] if has_fq else []) + [row_spec, row_spec, row_spec],
        out_specs=(row_spec, col_spec, col_spec) + extra_specs,
        scratch_shapes=[pltpu.VMEM((lp, LANES), F32), pltpu.VMEM((lp, LANES), F32)],
        name=name,
        compiler_params=_cp(("parallel", "arbitrary")),
    )(*ins)


SB_TK = 128


def _split3(x):
    hi = x.astype(BF16)
    r1 = x - hi.astype(F32)
    mid = r1.astype(BF16)
    lo = (r1 - mid.astype(F32)).astype(BF16)
    return hi, mid, lo


SB_RC = 128
SB_UNROLL = 5


_NT = (((1,), (1,)), ((), ()))
_TN = (((0,), (0,)), ((), ()))


def _sb_logits(zraw, kbj, mask):
    z = kbj + zraw
    if mask is not None:
        z = jnp.where(mask, z, NEG)
    e = jnp.exp(-jnp.abs(z))
    g = jnp.minimum(z, 0.0) - jnp.log(1.0 + e)
    lk = g - z
    return z, e, g, lk


def _dot3_parts(parts, tri):
    d = functools.partial(jnp.dot, preferred_element_type=F32)
    return d(parts[0], tri) + d(parts[1], tri) + d(parts[2], tri)


def _split2(x):
    hi = x.astype(BF16)
    return hi, (x - hi.astype(F32)).astype(BF16)


def _dot_parts(parts, m):
    out = jnp.dot(parts[0], m, preferred_element_type=F32)
    for p in parts[1:]:
        out = out + jnp.dot(p, m, preferred_element_type=F32)
    return out


def _tri(n, pred):
    return pred(lax.broadcasted_iota(jnp.int32, (n, n), 0), lax.broadcasted_iota(jnp.int32, (n, n), 1)).astype(BF16)


def _sb_diag_chunks(jj, nrc, rc, tk):
    plan = []
    for r in range(nrc):
        lo_row, hi_row = r * rc, (r + 1) * rc - 1
        lo_col, hi_col = jj * tk, (jj + 1) * tk - 1
        if hi_row <= lo_col:
            plan.append(None)
        elif lo_row > hi_col:
            plan.append("all")
        else:
            plan.append(lo_col - lo_row)
    return plan


def _sb_fwd(qa, kb, *, nh, name, tq=640):
    lp = qa.shape[0]
    tq = _col_tile(lp, tq)
    tk = SB_TK
    rc = min(SB_RC, tq)
    nq, sub, nrc = lp // tq, tq // tk, tq // rc

    def body(q_ref, k_ref, v_ref, kb_ref, o_ref, c_scr, acc_scr):
        i = pl.program_id(1)
        c_scr[...] = jnp.zeros_like(c_scr)
        acc_scr[...] = jnp.zeros_like(acc_scr)
        tri = _tri(tk, lambda r, c: r > c)
        row_io = lax.broadcasted_iota(jnp.int32, (rc, tk), 0)
        col_io = lax.broadcasted_iota(jnp.int32, (rc, tk), 1)

        def scores(j, rows):
            k = k_ref[pl.ds(pl.multiple_of(j * tk, tk), tk), :]
            return [lax.dot_general(q_ref[rs, :], k, _NT, preferred_element_type=F32) for rs in rows]

        def weights(j, rows, masks, zs):
            kbj = kb_ref[j]
            gs, splits, firsts = [], [], []
            for mask, zraw in zip(masks, zs):
                _, _, g, lk = _sb_logits(zraw, kbj, mask)
                gs.append(g)
                firsts.append(lk[:, 0:1])
                splits.append(_split2(lk))
            sums = [_dot_parts(p, tri) for p in splits]
            avs = [jnp.exp(g + (sm + c_scr[rs, :])).astype(BF16) for g, sm, rs in zip(gs, sums, rows)]
            for rs, sm, first in zip(rows, sums, firsts):
                c_scr[rs, :] += jnp.broadcast_to(sm[:, 0:1] + first, (rc, tk))
            return avs

        def values(j, rows, avs):
            v = v_ref[pl.ds(pl.multiple_of(j * tk, tk), tk), :]
            pvs = [jnp.dot(a, v, preferred_element_type=F32) for a in avs]
            for rs, pv in zip(rows, pvs):
                acc_scr[rs, :] += pv

        diag = []
        for jj in reversed(range(sub)):
            plan = _sb_diag_chunks(jj, nrc, rc, tk)
            live = [r for r, what in enumerate(plan) if what is not None]
            diag.append((i * sub + jj, [pl.ds(r * rc, rc) for r in live],
                         [None if plan[r] == "all" else (col_io + plan[r]) < row_io for r in live]))
        diag_scores = [scores(j, rows) for j, rows, _ in diag]
        diag_avs = [weights(j, rows, masks, zs) for (j, rows, masks), zs in zip(diag, diag_scores)]
        for (j, rows, _), avs in zip(diag, diag_avs):
            values(j, rows, avs)

        n = i * sub
        rows = [pl.ds(r * rc, rc) for r in range(nrc)]
        nomask = [None] * nrc

        def left(m, carry):
            js = [n - 1 - SB_UNROLL * m - t for t in range(SB_UNROLL)]
            zss = [scores(j, rows) for j in js]
            avss = [weights(j, rows, nomask, zs) for j, zs in zip(js, zss)]
            for j, avs in zip(js, avss):
                values(j, rows, avs)
            return carry

        lax.fori_loop(0, n // SB_UNROLL, left, 0)

        def tail(t, carry):
            j = n % SB_UNROLL - 1 - t
            values(j, rows, weights(j, rows, nomask, scores(j, rows)))
            return carry

        lax.fori_loop(0, n % SB_UNROLL, tail, 0)

        o_ref[...] = acc_scr[...].astype(o_ref.dtype)

    q_spec, k_spec, v_spec = _head_specs(lp, tq, (0, nh, 2 * nh))
    kb_spec = pl.BlockSpec((None, lp // tk, 1, tk), lambda h, i: (h, 0, 0, 0))
    row_spec = pl.BlockSpec((tq, LANES), lambda h, i: (i, h))
    return pl.pallas_call(
        body,
        out_shape=jax.ShapeDtypeStruct((lp, nh * LANES), BF16),
        grid=(nh, nq),
        in_specs=[q_spec, k_spec, v_spec, kb_spec],
        out_specs=row_spec,
        scratch_shapes=[pltpu.VMEM((tq, tk), F32), pltpu.VMEM((tq, LANES), F32)],
        name=name,
        compiler_params=_cp(("parallel", "arbitrary")),
    )(qa, qa, qa, kb)


def _sb_bwd(qa, kb, do, *, nh, name, tq=640):
    lp = qa.shape[0]
    tq = _col_tile(lp, tq)
    tk = SB_TK
    rc = min(SB_RC, tq)
    nq, nk, sub, nrc = lp // tq, lp // tk, tq // tk, tq // rc

    def body(q_ref, k_ref, v_ref, kb_ref, do_ref, dq_ref, dk_ref, dv_ref, dkt_acc, dvt_acc, w_scr, b_scr, c_scr, u_scr, dq_scr):
        i = pl.program_id(1)

        @pl.when(i == 0)
        def _():
            dkt_acc[...] = jnp.zeros_like(dkt_acc)
            dvt_acc[...] = jnp.zeros_like(dvt_acc)

        c_scr[...] = jnp.zeros_like(c_scr)
        u_scr[...] = jnp.zeros_like(u_scr)
        dq_scr[...] = jnp.zeros_like(dq_scr)
        tri_gt = _tri(tk, lambda r, c: r > c)
        tri_lt = _tri(tk, lambda r, c: r < c)
        row_io = lax.broadcasted_iota(jnp.int32, (rc, tk), 0)
        col_io = lax.broadcasted_iota(jnp.int32, (rc, tk), 1)
        qt = q_ref[...].astype(F32).T.astype(BF16)
        dot_t = do_ref[...].astype(F32).T.astype(BF16)
        zero_blk = jnp.zeros((rc, tk), BF16)

        def full_rows(plan, parts):
            it = iter(parts)
            return jnp.concatenate([zero_blk if what is None else next(it) for what in plan], axis=0)

        def key_rows(j):
            return pl.ds(pl.multiple_of(j * tk, tk), tk)

        def scores(j, rows):
            k = k_ref[key_rows(j), :]
            v = v_ref[key_rows(j), :]
            zs = [lax.dot_general(q_ref[rs, :], k, _NT, preferred_element_type=F32) for rs in rows]
            das = [lax.dot_general(do_ref[rs, :], v, _NT, preferred_element_type=F32) for rs in rows]
            return zs, das

        def weights(j, rows, masks, zs, das):
            kbj = kb_ref[j]
            gs, splits, firsts = [], [], []
            for rs, mask, zraw in zip(rows, masks, zs):
                _, _, g, lk = _sb_logits(zraw, kbj, mask)
                b_scr[j, rs, :] = jnp.exp(g).astype(BF16)
                gs.append(g)
                firsts.append(lk[:, 0:1])
                splits.append(_split2(lk))
            sums = [_dot_parts(p, tri_gt) for p in splits]
            avs = []
            for rs, g, sm, da, first in zip(rows, gs, sums, das, firsts):
                a = jnp.exp(g + (sm + c_scr[rs, :]))
                w_scr[j, rs, :] = (a * da).astype(BF16)
                avs.append(a.astype(BF16))
                c_scr[rs, :] += jnp.broadcast_to(sm[:, 0:1] + first, (rc, tk))
            return avs

        def dv_update(j, plan, avs):
            dvt_acc[j] += jnp.dot(dot_t, full_rows(plan, avs), preferred_element_type=F32)

        diag = []
        for jj in reversed(range(sub)):
            plan = _sb_diag_chunks(jj, nrc, rc, tk)
            live = [r for r, what in enumerate(plan) if what is not None]
            diag.append((i * sub + jj, plan, [pl.ds(r * rc, rc) for r in live],
                         [None if plan[r] == "all" else (col_io + plan[r]) < row_io for r in live]))
        diag_scores = [scores(j, rows) for j, _, rows, _ in diag]
        diag_avs = [weights(j, rows, masks, *sc) for (j, _, rows, masks), sc in zip(diag, diag_scores)]
        for (j, plan, _, _), avs in zip(diag, diag_avs):
            dv_update(j, plan, avs)

        n = i * sub
        everything = ["all"] * nrc
        rows = [pl.ds(r * rc, rc) for r in range(nrc)]
        nomask = [None] * nrc
        def left1(m, carry):
            js = [n - 1 - SB_UNROLL * m - t for t in range(SB_UNROLL)]
            scs = [scores(j, rows) for j in js]
            avs = [weights(j, rows, nomask, *sc) for j, sc in zip(js, scs)]
            for j, av in zip(js, avs):
                dv_update(j, everything, av)
            return carry

        lax.fori_loop(0, n // SB_UNROLL, left1, 0)

        def tail1(t, carry):
            j = n % SB_UNROLL - 1 - t
            dv_update(j, everything, weights(j, rows, nomask, *scores(j, rows)))
            return carry

        lax.fori_loop(0, n % SB_UNROLL, tail1, 0)

        def prefix(j, rows):
            return [jnp.dot(w_scr[j, rs, :], tri_lt, preferred_element_type=F32) for rs in rows]

        def dlogits(j, rows, sums):
            dzs = []
            for rs, sm in zip(rows, sums):
                w = w_scr[j, rs, :].astype(F32)
                beta = b_scr[j, rs, :].astype(F32)
                dzs.append((w - beta * ((w + sm) + u_scr[rs, :])).astype(BF16))
                u_scr[rs, :] += jnp.broadcast_to(sm[:, tk - 1:tk] + w[:, tk - 1:tk], (rc, tk))
            return dzs

        def dqk_update(j, plan, rows, dzs):
            k = k_ref[key_rows(j), :]
            dqs = [jnp.dot(dz, k, preferred_element_type=F32) for dz in dzs]
            for rs, dq in zip(rows, dqs):
                dq_scr[rs, :] += dq
            dkt_acc[j] += jnp.dot(qt, full_rows(plan, dzs), preferred_element_type=F32)

        def left2(m, carry):
            js = [SB_UNROLL * m + t for t in range(SB_UNROLL)]
            sums = [prefix(j, rows) for j in js]
            dzs = [dlogits(j, rows, sm) for j, sm in zip(js, sums)]
            for j, dz in zip(js, dzs):
                dqk_update(j, everything, rows, dz)
            return carry

        lax.fori_loop(0, n // SB_UNROLL, left2, 0)

        def tail2(t, carry):
            j = (n // SB_UNROLL) * SB_UNROLL + t
            dqk_update(j, everything, rows, dlogits(j, rows, prefix(j, rows)))
            return carry

        lax.fori_loop(0, n % SB_UNROLL, tail2, 0)
        diag = []
        for jj in range(sub):
            plan = _sb_diag_chunks(jj, nrc, rc, tk)
            diag.append((i * sub + jj, plan, [pl.ds(r * rc, rc) for r, what in enumerate(plan) if what is not None]))
        diag_sums = [prefix(j, live_rows) for j, _, live_rows in diag]
        diag_dzs = [dlogits(j, live_rows, sm) for (j, _, live_rows), sm in zip(diag, diag_sums)]
        for (j, plan, live_rows), dzs in zip(diag, diag_dzs):
            dqk_update(j, plan, live_rows, dzs)
        dq_ref[...] = dq_scr[...].astype(dq_ref.dtype)

        @pl.when(i == nq - 1)
        def _():
            def flush(j, carry):
                dk_ref[key_rows(j), :] = dkt_acc[j].T.astype(dk_ref.dtype)
                dv_ref[key_rows(j), :] = dvt_acc[j].T.astype(dv_ref.dtype)
                return carry

            lax.fori_loop(0, nk, flush, 0)

    q_spec, k_spec, v_spec = _head_specs(lp, tq, (0, nh, 2 * nh))
    kb_spec = pl.BlockSpec((None, nk, 1, tk), lambda h, i: (h, 0, 0, 0))
    row_spec = pl.BlockSpec((tq, LANES), lambda h, i: (i, h))
    col_spec = pl.BlockSpec((lp, LANES), lambda h, i: (0, h))
    wide = jax.ShapeDtypeStruct((lp, nh * LANES), BF16)
    return pl.pallas_call(
        body,
        out_shape=(wide, wide, wide),
        grid=(nh, nq),
        in_specs=[q_spec, k_spec, v_spec, kb_spec, row_spec],
        out_specs=(row_spec, col_spec, col_spec),
        scratch_shapes=[
            pltpu.VMEM((nk, LANES, tk), F32),
            pltpu.VMEM((nk, LANES, tk), F32),
            pltpu.VMEM((nk, tq, tk), BF16),
            pltpu.VMEM((nk, tq, tk), BF16),
            pltpu.VMEM((tq, tk), F32),
            pltpu.VMEM((tq, tk), F32),
            pltpu.VMEM((tq, LANES), F32),
        ],
        name=name,
        compiler_params=_cp(("parallel", "arbitrary")),
    )(qa, qa, qa, kb, do)


def _pool_counts(pos, win):
    return jnp.clip(pos + 1, 1, win).astype(F32)


def _pool_fwd(a, name):
    lp, C = a.shape
    tm = _row_tile(lp, 640)
    hb = tm // POOL_HALO

    def body(prev_ref, cur_ref, o_ref, xs):
        i = pl.program_id(0)
        xs[pl.ds(0, POOL_HALO), :] = jnp.where(i > 0, prev_ref[...], 0.0)
        xs[pl.ds(POOL_HALO, tm), :] = cur_ref[...]
        pos = i * tm + lax.broadcasted_iota(jnp.int32, (tm, 1), 0) - PAD0
        for g, win in enumerate(POOL_WINDOWS):
            cols = pl.ds(g * POOL_GROUP, POOL_GROUP)
            s = xs[pl.ds(POOL_HALO, tm), cols]
            for k in range(1, win):
                s = s + xs[pl.ds(POOL_HALO - k, tm), cols]
            o_ref[:, cols] = (s / _pool_counts(pos, win) - xs[pl.ds(POOL_HALO, tm), cols]).astype(o_ref.dtype)

    return pl.pallas_call(
        body,
        out_shape=jax.ShapeDtypeStruct((lp, C), BF16),
        grid=(lp // tm,),
        in_specs=[
            pl.BlockSpec((POOL_HALO, C), lambda i: (jnp.maximum(i * hb - 1, 0), 0)),
            pl.BlockSpec((tm, C), lambda i: (i, 0)),
        ],
        out_specs=pl.BlockSpec((tm, C), lambda i: (i, 0)),
        scratch_shapes=[pltpu.VMEM((tm + POOL_HALO, C), F32)],
        name=name,
        compiler_params=_cp(("parallel",)),
    )(a, a)


def _pool_bwd(dp, name):
    lp, C = dp.shape
    tm = _row_tile(lp, 640)
    hb = tm // POOL_HALO
    nt = lp // tm
    last_halo = lp // POOL_HALO - 1

    def body(cur_ref, next_ref, o_ref, xs):
        i = pl.program_id(0)
        pos = i * tm + lax.broadcasted_iota(jnp.int32, (tm, 1), 0) - PAD0
        pos_h = (i + 1) * tm + lax.broadcasted_iota(jnp.int32, (POOL_HALO, 1), 0) - PAD0
        for g, win in enumerate(POOL_WINDOWS):
            cols = pl.ds(g * POOL_GROUP, POOL_GROUP)
            cur = cur_ref[:, cols]
            xs[pl.ds(0, tm), cols] = cur / _pool_counts(pos, win)
            xs[pl.ds(tm, POOL_HALO), cols] = jnp.where(i < nt - 1, next_ref[:, cols], 0.0) / _pool_counts(pos_h, win)
            s = xs[pl.ds(0, tm), cols]
            for k in range(1, win):
                s = s + xs[pl.ds(k, tm), cols]
            o_ref[:, cols] = jnp.where(pos >= 0, s - cur, 0.0)

    return pl.pallas_call(
        body,
        out_shape=jax.ShapeDtypeStruct((lp, C), F32),
        grid=(nt,),
        in_specs=[
            pl.BlockSpec((tm, C), lambda i: (i, 0)),
            pl.BlockSpec((POOL_HALO, C), lambda i: (jnp.minimum((i + 1) * hb, last_halo), 0)),
        ],
        out_specs=pl.BlockSpec((tm, C), lambda i: (i, 0)),
        scratch_shapes=[pltpu.VMEM((tm + POOL_HALO, C), F32)],
        name=name,
        compiler_params=_cp(("parallel",)),
    )(dp, dp)


def _scale_add(h, pre, scale, name):
    M, C = h.shape
    tm = _row_tile(M, 640)

    def body(h_ref, p_ref, s_ref, o_ref):
        o_ref[...] = h_ref[...] + p_ref[...] * s_ref[...]

    row = pl.BlockSpec((tm, C), lambda i: (i, 0))
    return pl.pallas_call(
        body,
        out_shape=jax.ShapeDtypeStruct((M, C), F32),
        grid=(M // tm,),
        in_specs=[row, row, pl.BlockSpec((1, C), lambda i: (0, 0))],
        out_specs=row,
        name=name,
        compiler_params=_cp(("parallel",)),
    )(h, pre, scale.reshape(1, C))


def _scale_bwd(dh, pre, scale, name):
    M, C = dh.shape
    tm = _row_tile(M, 640)

    def body(dh_ref, p_ref, s_ref, dp_ref, ds_ref):
        @pl.when(pl.program_id(0) == 0)
        def _():
            ds_ref[...] = jnp.zeros_like(ds_ref)

        d = dh_ref[...]
        ds_ref[...] += jnp.sum(d * p_ref[...], axis=0, keepdims=True)
        dp_ref[...] = (d * s_ref[...]).astype(dp_ref.dtype)

    row = pl.BlockSpec((tm, C), lambda i: (i, 0))
    vec = pl.BlockSpec((1, C), lambda i: (0, 0))
    return pl.pallas_call(
        body,
        out_shape=(jax.ShapeDtypeStruct((M, C), BF16), jax.ShapeDtypeStruct((1, C), F32)),
        grid=(M // tm,),
        in_specs=[row, row, vec],
        out_specs=(row, vec),
        name=name,
        compiler_params=_cp(("arbitrary",)),
    )(dh, pre, scale.reshape(1, C))


def _gate_parts(z):
    e = jnp.exp(-jnp.abs(z))
    return e, jnp.minimum(z, 0.0) - jnp.log(1.0 + e)


def _tri_dot3(tri, x):
    hi, mid, lo = _split3(x)
    d = functools.partial(jnp.dot, preferred_element_type=F32)
    return d(tri, hi) + d(tri, mid) + d(tri, lo)


def _gate_fwd(x, b, name):
    lp, C = x.shape
    tm = _row_tile(lp, 640)

    def body(x_ref, b_ref, o_ref, carry):
        i = pl.program_id(0)

        @pl.when(i == 0)
        def _():
            carry[...] = jnp.zeros_like(carry)

        _, ls = _gate_parts(x_ref[...] + b_ref[...])
        rows = i * tm + lax.broadcasted_iota(jnp.int32, (tm, 1), 0)
        ls = jnp.where(rows >= PAD0, ls, 0.0)
        tri = (lax.broadcasted_iota(jnp.int32, (tm, tm), 0) >= lax.broadcasted_iota(jnp.int32, (tm, tm), 1)).astype(BF16)
        f = _tri_dot3(tri, ls) + carry[...]
        o_ref[...] = f
        carry[...] = f[tm - 1:tm, :]

    return pl.pallas_call(
        body,
        out_shape=jax.ShapeDtypeStruct((lp, C), F32),
        grid=(lp // tm,),
        in_specs=[pl.BlockSpec((tm, C), lambda i: (i, 0)), pl.BlockSpec((1, C), lambda i: (0, 0))],
        out_specs=pl.BlockSpec((tm, C), lambda i: (i, 0)),
        scratch_shapes=[pltpu.VMEM((1, C), F32)],
        name=name,
        compiler_params=_cp(("arbitrary",)),
    )(x, b)


def _gate_bwd(x, b, df, name):
    lp, C = x.shape
    tm = _row_tile(lp, 640)
    nt = lp // tm

    def body(x_ref, b_ref, df_ref, dx_ref, db_ref, carry):
        i = pl.program_id(0)

        @pl.when(i == 0)
        def _():
            carry[...] = jnp.zeros_like(carry)
            db_ref[...] = jnp.zeros_like(db_ref)

        z = x_ref[...] + b_ref[...]
        e, _ = _gate_parts(z)
        tri = (lax.broadcasted_iota(jnp.int32, (tm, tm), 0) <= lax.broadcasted_iota(jnp.int32, (tm, tm), 1)).astype(BF16)
        r = _tri_dot3(tri, df_ref[...]) + carry[...]
        carry[...] = r[0:1, :]
        rows = (nt - 1 - i) * tm + lax.broadcasted_iota(jnp.int32, (tm, 1), 0)
        dx = jnp.where(rows >= PAD0, r * (jnp.where(z >= 0.0, e, 1.0) / (1.0 + e)), 0.0)
        dx_ref[...] = dx
        db_ref[...] += jnp.sum(dx, axis=0, keepdims=True)

    rev = pl.BlockSpec((tm, C), lambda i: (nt - 1 - i, 0))
    vec = pl.BlockSpec((1, C), lambda i: (0, 0))
    return pl.pallas_call(
        body,
        out_shape=(jax.ShapeDtypeStruct((lp, C), F32), jax.ShapeDtypeStruct((1, C), F32)),
        grid=(nt,),
        in_specs=[rev, vec, rev],
        out_specs=(rev, vec),
        scratch_shapes=[pltpu.VMEM((1, C), F32)],
        name=name,
        compiler_params=_cp(("arbitrary",)),
    )(x, b, df)


MLA_SCALE = (MLA_NOPE + MLA_ROPE) ** -0.5


def _rope_apply(x, c, a, b):
    return x * c + pltpu.roll(x, LANES - 16, 1) * a + pltpu.roll(x, 16, 1) * b


def _rope_transpose(dy, c, a, b):
    return dy * c + pltpu.roll(dy * a, 16, 1) + pltpu.roll(dy * b, LANES - 16, 1)


def _mla_prep_fwd(q, kmat, kr, c, a, b, name):
    lp, W = q.shape
    nh = W // LANES
    tm = _row_tile(lp, 640)

    def body(q_ref, k_ref, kr_ref, c_ref, a_ref, b_ref, qo_ref, ko_ref):
        cv, av, bv = c_ref[...], a_ref[...], b_ref[...]
        kr_roped = _rope_apply(kr_ref[...], cv, av, bv)
        for h in range(nh):
            cols = pl.ds(h * LANES, LANES)
            qo_ref[:, cols] = (_rope_apply(q_ref[:, cols], cv, av, bv) * MLA_SCALE).astype(qo_ref.dtype)
            ko_ref[:, cols] = (k_ref[:, cols] + kr_roped).astype(ko_ref.dtype)

    heads = pl.BlockSpec((tm, W), lambda i: (i, 0))
    tab = pl.BlockSpec((tm, LANES), lambda i: (i, 0))
    wide = jax.ShapeDtypeStruct((lp, W), BF16)
    return pl.pallas_call(
        body,
        out_shape=(wide, wide),
        grid=(lp // tm,),
        in_specs=[heads, heads, tab, tab, tab, tab],
        out_specs=(heads, heads),
        name=name,
        compiler_params=_cp(("parallel",)),
    )(q, kmat, kr, c, a, b)


def _mla_prep_bwd(dq, dk, c, a, b, name):
    lp, W = dq.shape
    nh = W // LANES
    tm = _row_tile(lp, 640)

    def body(dq_ref, dk_ref, c_ref, a_ref, b_ref, dqo_ref, dkr_ref):
        cv, av, bv = c_ref[...], a_ref[...], b_ref[...]
        ksum = jnp.zeros((tm, LANES), F32)
        for h in range(nh):
            cols = pl.ds(h * LANES, LANES)
            dqo_ref[:, cols] = _rope_transpose(dq_ref[:, cols].astype(F32) * MLA_SCALE, cv, av, bv).astype(dqo_ref.dtype)
            ksum = ksum + dk_ref[:, cols].astype(F32)
        dkr_ref[...] = _rope_transpose(ksum, cv, av, bv)

    wide = pl.BlockSpec((tm, W), lambda i: (i, 0))
    tab = pl.BlockSpec((tm, LANES), lambda i: (i, 0))
    return pl.pallas_call(
        body,
        out_shape=(jax.ShapeDtypeStruct((lp, W), BF16), jax.ShapeDtypeStruct((lp, LANES), F32)),
        grid=(lp // tm,),
        in_specs=[wide, wide, tab, tab, tab],
        out_specs=(wide, tab),
        name=name,
        compiler_params=_cp(("parallel",)),
    )(dq, dk, c, a, b)


def _loss_head(h, g, target, name):
    lp, C = h.shape
    tm = _row_tile(lp, 640)
    nt = lp // tm

    def body(h_ref, g_ref, t_ref, loss_ref, dh_ref, dg_ref, sq):
        i = pl.program_id(0)

        @pl.when(i == 0)
        def _():
            dg_ref[...] = jnp.zeros_like(dg_ref)
            sq[...] = jnp.zeros_like(sq)

        xf = h_ref[...]
        gv = g_ref[...]
        r = lax.rsqrt(jnp.mean(xf * xf, axis=-1, keepdims=True) + EPS)
        xhat = xf * r
        rows = i * tm + lax.broadcasted_iota(jnp.int32, (tm, 1), 0)
        err = jnp.where(rows >= PAD0 + N_META, xhat * gv - t_ref[...], 0.0)
        sq[...] += jnp.sum(err * err, axis=0, keepdims=True)
        dy = err * (1.0 / C)
        dg_ref[...] += jnp.sum(dy * xhat, axis=0, keepdims=True)
        dxh = dy * gv
        dh_ref[...] = r * (dxh - xhat * jnp.mean(dxh * xhat, axis=-1, keepdims=True))

        @pl.when(i == nt - 1)
        def _():
            loss_ref[...] = jnp.broadcast_to(jnp.sum(sq[...], axis=1, keepdims=True) * (0.5 / C), (1, LANES))

    row = pl.BlockSpec((tm, C), lambda i: (i, 0))
    vec = pl.BlockSpec((1, C), lambda i: (0, 0))
    return pl.pallas_call(
        body,
        out_shape=(jax.ShapeDtypeStruct((1, LANES), F32), jax.ShapeDtypeStruct((lp, C), F32), jax.ShapeDtypeStruct((1, C), F32)),
        grid=(nt,),
        in_specs=[row, vec, row],
        out_specs=(pl.BlockSpec((1, LANES), lambda i: (0, 0)), row, vec),
        scratch_shapes=[pltpu.VMEM((1, C), F32)],
        name=name,
        compiler_params=_cp(("arbitrary",)),
    )(h, g.reshape(1, C), target)


def _adamw(w, g, m, v, name):
    shape = w.shape
    C = shape[-1]
    R = w.size // C
    tr = R
    if R % 8 == 0:
        for cand in range(8, R + 1, 8):
            if R % cand == 0 and cand * C * 4 <= (1 << 20):
                tr = cand
    c1 = 1.0 - ADAM_B1 ** ADAM_STEP
    c2 = 1.0 - ADAM_B2 ** ADAM_STEP

    def body(w_ref, g_ref, m_ref, v_ref, d_ref, nm_ref, nv_ref):
        gv = g_ref[...]
        nm = ADAM_B1 * m_ref[...] + (1.0 - ADAM_B1) * gv
        nv = ADAM_B2 * v_ref[...] + (1.0 - ADAM_B2) * (gv * gv)
        nm_ref[...] = nm
        nv_ref[...] = nv
        d_ref[...] = -ADAM_LR * ((nm / c1) / (jnp.sqrt(nv / c2) + ADAM_EPS) + ADAM_WD * w_ref[...])

    blk = pl.BlockSpec((tr, C), lambda i: (i, 0))
    out = jax.ShapeDtypeStruct((R, C), F32)
    outs = pl.pallas_call(
        body,
        out_shape=(out, out, out),
        grid=(R // tr,),
        in_specs=[blk] * 4,
        out_specs=(blk, blk, blk),
        name=name,
        compiler_params=_cp(("parallel",)),
    )(*(t.reshape(R, C) for t in (w, g, m, v)))
    return tuple(t.reshape(shape) for t in outs)


def _exchange(send, axes, same, name):
    na = len(axes)
    n = 1 << na
    _, R, C = send.shape
    parts = max(p for p in (8, 4, 2, 1) if R % (16 * p) == 0 or p == 1)
    pr = R // parts

    def body(send_ref, recv_ref, send_sems, recv_sems, local_sem):
        coords = {ax: lax.axis_index(ax) for ax in MESH_AXES}
        me = 0
        for ax in axes:
            me = me * 2 + coords[ax]

        def member(r):
            dev = dict(coords)
            for b, ax in enumerate(axes):
                if (r >> (na - 1 - b)) & 1:
                    dev[ax] = 1 - dev[ax]
            return tuple(dev[ax] for ax in MESH_AXES)

        def chunk(j, p):
            return (send_ref.at[0] if same else send_ref.at[j]).at[pl.ds(p * pr, pr)]

        def slot(j, p):
            return recv_ref.at[j].at[pl.ds(p * pr, pr)]

        own = pltpu.make_async_copy(send_ref.at[0] if same else send_ref.at[me], recv_ref.at[me], local_sem)
        own.start()
        copies = []
        for r in range(1, n):
            peer = me ^ r
            for p in range(parts):
                cp = pltpu.make_async_remote_copy(
                    src_ref=chunk(peer, p), dst_ref=slot(me, p), send_sem=send_sems.at[r, p], recv_sem=recv_sems.at[r, p],
                    device_id=member(r), device_id_type=pl.DeviceIdType.MESH)
                cp.start()
                copies.append(cp)
        for r in range(1, n):
            for p in range(parts):
                arrival = pltpu.make_async_remote_copy(
                    src_ref=chunk(me, p), dst_ref=slot(me ^ r, p), send_sem=send_sems.at[r, p], recv_sem=recv_sems.at[r, p],
                    device_id=member(r), device_id_type=pl.DeviceIdType.MESH)
                arrival.wait_recv()
        for cp in copies:
            cp.wait_send()
        own.wait()

    any_spec = pl.BlockSpec(memory_space=pl.ANY)
    return pl.pallas_call(
        body,
        out_shape=jax.ShapeDtypeStruct((n, R, C), send.dtype),
        in_specs=[any_spec],
        out_specs=any_spec,
        scratch_shapes=[pltpu.SemaphoreType.DMA((n, parts)), pltpu.SemaphoreType.DMA((n, parts)), pltpu.SemaphoreType.DMA],
        name=name,
        compiler_params=pltpu.CompilerParams(has_side_effects=True),
    )(send)


def _sum_chunks(x, name, out_dtype=F32):
    n, R, C = x.shape
    tr = _row_tile(R, 512)

    def body(x_ref, o_ref):
        acc = x_ref[0].astype(F32)
        for j in range(1, n):
            acc = acc + x_ref[j].astype(F32)
        o_ref[...] = acc.astype(o_ref.dtype)

    return pl.pallas_call(
        body,
        out_shape=jax.ShapeDtypeStruct((R, C), out_dtype),
        grid=(R // tr,),
        in_specs=[pl.BlockSpec((n, tr, C), lambda i: (0, i, 0))],
        out_specs=pl.BlockSpec((tr, C), lambda i: (i, 0)),
        name=name,
        compiler_params=_cp(("parallel",)),
    )(x)


def _pad_heads_cols(w, groups, d):
    k = w.shape[0]
    w = w.reshape(k, groups * N_HEADS, d)
    return jnp.pad(w, ((0, 0), (0, 0), (0, LANES - d))).reshape(k, groups * N_HEADS * LANES)


def _unpad_heads_cols(w, groups, d):
    k = w.shape[0]
    return w.reshape(k, groups * N_HEADS, LANES)[:, :, :d].reshape(k, groups * N_HEADS * d)


def _pad_heads_rows(w, d):
    n = w.shape[1]
    return jnp.pad(w.reshape(N_HEADS, d, n), ((0, 0), (0, LANES - d), (0, 0))).reshape(N_HEADS * LANES, n)


def _unpad_heads_rows(w, d):
    n = w.shape[1]
    return w.reshape(N_HEADS, LANES, n)[:, :d].reshape(N_HEADS * d, n)


Q_SCALE = HEAD_DIM ** -0.5


def _scale_q_cols(w):
    nq = N_HEADS * LANES
    return jnp.concatenate([w[:, :nq] * Q_SCALE, w[:, nq:]], axis=1)


def _kernel_weights(W):
    P = dict(W)
    pw = W["pool_w"][0]
    bd = jnp.zeros((D_MODEL, D_MODEL), pw.dtype)
    for g in range(len(POOL_WINDOWS)):
        bd = lax.dynamic_update_slice(bd, pw[g], (g * POOL_GROUP, g * POOL_GROUP))
    P["pool_bd"] = bd
    P["sb_qkv"] = _scale_q_cols(_pad_heads_cols(W["sb_w_qkv"][0], 3, HEAD_DIM))
    P["sb_o"] = _pad_heads_rows(W["sb_w_o"][0], HEAD_DIM)
    nq = 3 * N_HEADS * HEAD_DIM
    P["fox_qkv"] = _scale_q_cols(_pad_heads_cols(W["fox_w_qkvf"][0][:, :nq], 3, HEAD_DIM))
    P["fox_f"] = jnp.pad(W["fox_w_qkvf"][0][:, nq:], ((0, 0), (0, LANES - N_HEADS)))
    P["fox_o"] = _pad_heads_rows(W["fox_w_o"][0], HEAD_DIM)
    P["fox_b"] = jnp.pad(W["fox_b_f"], ((0, 0), (0, LANES - N_HEADS)))
    P["mla_down"] = jnp.pad(W["mla_w_down"][0], ((0, 0), (0, MLA_DOWN_PAD - W["mla_w_down"].shape[2])))
    P["mla_uq"] = _pad_heads_cols(W["mla_w_uq"][0], 1, MLA_NOPE + MLA_ROPE)
    ukv = W["mla_w_ukv"][0].reshape(MLA_KV_RANK, N_HEADS, 2 * HEAD_DIM)
    padk = ((0, 0), (0, 0), (0, LANES - HEAD_DIM))
    P["mla_ukv"] = jnp.concatenate(
        [jnp.pad(ukv[:, :, :MLA_NOPE], padk).reshape(MLA_KV_RANK, -1), jnp.pad(ukv[:, :, MLA_NOPE:], padk).reshape(MLA_KV_RANK, -1)], axis=1)
    P["mla_o"] = _pad_heads_rows(W["mla_w_o"][0], HEAD_DIM)
    return P


def _rope_tables(lp):
    pos = (jnp.arange(lp) - PAD0).astype(F32)
    inv = ROPE_THETA ** (-jnp.arange(0, MLA_ROPE, 2, dtype=F32) / MLA_ROPE)
    ang = pos[:, None] * inv[None, :]
    cos, sin = jnp.cos(ang), jnp.sin(ang)
    half = MLA_ROPE // 2
    z = lambda n: jnp.zeros((lp, n), F32)
    c = jnp.concatenate([jnp.ones((lp, MLA_NOPE), F32), cos, cos, z(LANES - MLA_NOPE - MLA_ROPE)], axis=1)
    a = jnp.concatenate([z(MLA_NOPE), -sin, z(LANES - MLA_NOPE - half)], axis=1)
    b = jnp.concatenate([z(MLA_NOPE + half), sin, z(LANES - MLA_NOPE - MLA_ROPE)], axis=1)
    return c, a, b


def _key_bias(lp, t, per_head=None):
    pad = jnp.arange(lp)[None, :] < PAD0
    body = jnp.zeros((N_HEADS, lp), F32) if per_head is None else per_head
    return jnp.where(pad, NEG, body).reshape(N_HEADS, lp // t, 1, t)


def _ffn_fwd(h, i, P):
    b = _rms_fwd(h, P["norm_ffn"][i], BF16, "ffn_norm")
    g, u, hd = _ffn_up(b, P["ffn_w_gate"][i], P["ffn_w_up"][i], "ffn_gate_up")
    return _mm(hd, P["ffn_w_down"][i], "nn", "ffn_down", add=h), (h, b, g, u, hd)


def _ffn_bwd(dh, i, P, saved):
    h, b, g, u, hd = saved
    dwd = _mm(hd, dh, "tn", "ffn_down_dw")
    dg, du = _ffn_down_bwd(dh, P["ffn_w_down"][i], g, u, "ffn_down_dx")
    dwg = _mm(b, dg, "tn", "ffn_gate_dw")
    dwu = _mm(b, du, "tn", "ffn_up_dw")
    db = _mm(dg, P["ffn_w_gate"][i], "nt", "ffn_gate_dx")
    db = _mm(du, P["ffn_w_up"][i], "nt", "ffn_up_dx", add=db)
    dh_in, dgain = _rms_bwd(h, P["norm_ffn"][i], db, dh, "ffn_norm_bwd")
    return dh_in, dgain, dwg, dwu, dwd


def _pool_layer_fwd(h, P):
    a = _rms_fwd(h, P["norm_mix"][0], F32, "pool_norm")
    pooled = _pool_fwd(a, "pool_window")
    pre = _mm(pooled, P["pool_bd"], "nn", "pool_mix")
    return _scale_add(h, pre, P["pool_scale"][0], "pool_scale_add"), (h, pooled, pre)


def _pool_layer_bwd(dh, P, saved):
    h, pooled, pre = saved
    dpre, dscale = _scale_bwd(dh, pre, P["pool_scale"][0], "pool_scale_bwd")
    dbd = _mm(pooled, dpre, "tn", "pool_mix_dw")
    dpooled = _mm(dpre, P["pool_bd"], "nt", "pool_mix_dx")
    da = _pool_bwd(dpooled, "pool_window_bwd")
    dh_in, dgain = _rms_bwd(h, P["norm_mix"][0], da, dh, "mix_norm_bwd")
    dw = jnp.stack([dbd[g * POOL_GROUP:(g + 1) * POOL_GROUP, g * POOL_GROUP:(g + 1) * POOL_GROUP] for g in range(len(POOL_WINDOWS))])
    return dh_in, {"norm_mix0": dgain, "pool_w": dw[None], "pool_scale": dscale}


def _out_proj_bwd(o, dh, wo, tag):
    return _mm(o, dh, "tn", tag + "_o_dw"), _mm(dh, wo, "nt", tag + "_o_dx", out_dtype=BF16)


def _sb_layer_fwd(h, P):
    lp = h.shape[0]
    a = _rms_fwd(h, P["norm_mix"][1], BF16, "mix_norm")
    qkv = _mm(a, P["sb_qkv"], "nn", "sb_qkv", out_dtype=BF16)
    kb = _key_bias(lp, SB_TK)
    o = _sb_fwd(qkv, kb, nh=N_HEADS, name="sb_attn")
    return _mm(o, P["sb_o"], "nn", "attn_out", add=h), (h, a, qkv, kb, o)


def _sb_layer_bwd(dh, P, saved):
    h, a, qkv, kb, o = saved
    dwo, do = _out_proj_bwd(o, dh, P["sb_o"], "attn")
    dq, dk, dv = _sb_bwd(qkv, kb, do, nh=N_HEADS, name="sb_attn_bwd")
    dqkv = jnp.concatenate([dq, dk, dv], axis=1)
    dw = _scale_q_cols(_mm(a, dqkv, "tn", "qkv_dw"))
    da = _mm(dqkv, P["sb_qkv"], "nt", "qkv_dx")
    dh_in, dgain = _rms_bwd(h, P["norm_mix"][1], da, dh, "mix_norm_bwd")
    return dh_in, {"norm_mix1": dgain, "sb_w_qkv": _unpad_heads_cols(dw, 3, HEAD_DIM)[None], "sb_w_o": _unpad_heads_rows(dwo, HEAD_DIM)[None]}


def _fox_layer_fwd(h, P):
    lp = h.shape[0]
    t = _attn_tile(lp)
    a = _rms_fwd(h, P["norm_mix"][3], BF16, "mix_norm")
    qkv = _mm(a, P["fox_qkv"], "nn", "sb_qkv", out_dtype=BF16)
    f = _mm(a, P["fox_f"], "nn", "fox_gate_proj")
    fc = _gate_fwd(f, P["fox_b"], "fox_gate")[:, :N_HEADS]
    kb = _key_bias(lp, t, -fc.T)
    fq = jnp.broadcast_to(fc[:, :, None], (lp, N_HEADS, LANES)).reshape(lp, N_HEADS * LANES)
    o, lse = _attn_fwd(qkv, qkv, qkv, kb, fq, nh=N_HEADS, offs=(0, N_HEADS, 2 * N_HEADS), name="fox_attn")
    return _mm(o, P["fox_o"], "nn", "attn_out", add=h), (h, a, qkv, f, kb, fq, o, lse)


def _fox_layer_bwd(dh, P, saved):
    h, a, qkv, f, kb, fq, o, lse = saved
    lp = h.shape[0]
    dwo, do = _out_proj_bwd(o, dh, P["fox_o"], "attn")
    dq, dk, dv, dkb, dqb = _attn_bwd(qkv, qkv, qkv, kb, fq, o, do, lse, nh=N_HEADS, offs=(0, N_HEADS, 2 * N_HEADS),
                                     name="fox_attn_bwd")
    dfc = jnp.pad(dqb.reshape(lp, N_HEADS, LANES)[:, :, 0] - dkb.reshape(N_HEADS, lp).T, ((0, 0), (0, LANES - N_HEADS)))
    df, dbf = _gate_bwd(f, P["fox_b"], dfc, "fox_gate_bwd")
    dqkv = jnp.concatenate([dq, dk, dv], axis=1)
    dw = _scale_q_cols(_mm(a, dqkv, "tn", "qkv_dw"))
    dwf = _mm(a, df, "tn", "fox_gate_dw")
    da = _mm(dqkv, P["fox_qkv"], "nt", "qkv_dx")
    da = _mm(df, P["fox_f"], "nt", "fox_gate_dx", add=da)
    dh_in, dgain = _rms_bwd(h, P["norm_mix"][3], da, dh, "mix_norm_bwd")
    dwqkvf = jnp.concatenate([_unpad_heads_cols(dw, 3, HEAD_DIM), dwf[:, :N_HEADS]], axis=1)
    return dh_in, {"norm_mix3": dgain, "fox_w_qkvf": dwqkvf[None], "fox_b_f": dbf[:, :N_HEADS], "fox_w_o": _unpad_heads_rows(dwo, HEAD_DIM)[None]}


def _mla_layer_fwd(h, P):
    lp = h.shape[0]
    a = _rms_fwd(h, P["norm_mix"][2], BF16, "mix_norm")
    down = _mm(a, P["mla_down"], "nn", "mla_down")
    cq_pre = down[:, :MLA_Q_RANK]
    ckv_pre = down[:, MLA_Q_RANK:MLA_Q_RANK + MLA_KV_RANK]
    kr = jnp.pad(down[:, MLA_Q_RANK + MLA_KV_RANK:MLA_Q_RANK + MLA_KV_RANK + MLA_ROPE], ((0, 0), (MLA_NOPE, LANES - MLA_NOPE - MLA_ROPE)))
    cq = _rms_fwd(cq_pre, P["mla_q_norm"][0], BF16, "mla_q_norm")
    ckv = _rms_fwd(ckv_pre, P["mla_kv_norm"][0], BF16, "mla_kv_norm")
    q = _mm(cq, P["mla_uq"], "nn", "mla_uq")
    kv = _mm(ckv, P["mla_ukv"], "nn", "mla_ukv", out_dtype=BF16)
    tabs = _rope_tables(lp)
    qr, kc = _mla_prep_fwd(q, kv, kr, *tabs, "mla_rope")
    kb = _key_bias(lp, _attn_tile(lp))
    o, lse = _attn_fwd_t(qr, kc, kv, kb, nh=N_HEADS, offs=(0, 0, N_HEADS), name="mla_attn")
    return _mm(o, P["mla_o"], "nn", "attn_out", add=h), (h, a, cq_pre, ckv_pre, cq, ckv, qr, kc, kv, tabs, kb, o, lse)


def _mla_layer_bwd(dh, P, saved):
    h, a, cq_pre, ckv_pre, cq, ckv, qr, kc, kv, tabs, kb, o, lse = saved
    lp = h.shape[0]
    dwo, do = _out_proj_bwd(o, dh, P["mla_o"], "attn")
    dqr, dkc, dv = _attn_bwd(qr, kc, kv, kb, None, o, do, lse, nh=N_HEADS, offs=(0, 0, N_HEADS),
                             name="mla_attn_bwd")
    dq, dkr = _mla_prep_bwd(dqr, dkc, *tabs, "mla_rope_bwd")
    dkv = jnp.concatenate([dkc, dv], axis=1)
    dwuq = _mm(cq, dq, "tn", "mla_uq_dw")
    dcq = _mm(dq, P["mla_uq"], "nt", "mla_uq_dx")
    dwukv = _mm(ckv, dkv, "tn", "mla_ukv_dw")
    dckv = _mm(dkv, P["mla_ukv"], "nt", "mla_ukv_dx")
    dcq_pre, dqn = _rms_bwd(cq_pre, P["mla_q_norm"][0], dcq, None, "mla_q_norm_bwd")
    dckv_pre, dkvn = _rms_bwd(ckv_pre, P["mla_kv_norm"][0], dckv, None, "mla_kv_norm_bwd")
    used = MLA_Q_RANK + MLA_KV_RANK + MLA_ROPE
    ddown = jnp.concatenate([dcq_pre, dckv_pre, dkr[:, MLA_NOPE:MLA_NOPE + MLA_ROPE], jnp.zeros((lp, MLA_DOWN_PAD - used), F32)], axis=1)
    dwdown = _mm(a, ddown, "tn", "mla_down_dw")
    da = _mm(ddown, P["mla_down"], "nt", "mla_down_dx")
    dh_in, dgain = _rms_bwd(h, P["norm_mix"][2], da, dh, "mix_norm_bwd")
    dukv = dwukv.reshape(MLA_KV_RANK, 2, N_HEADS, LANES)[:, :, :, :HEAD_DIM]
    dukv = jnp.concatenate([dukv[:, 0], dukv[:, 1]], axis=-1).reshape(MLA_KV_RANK, N_HEADS * 2 * HEAD_DIM)
    return dh_in, {
        "norm_mix2": dgain, "mla_w_down": dwdown[:, :used][None], "mla_q_norm": dqn, "mla_kv_norm": dkvn,
        "mla_w_uq": _unpad_heads_cols(dwuq, 1, MLA_NOPE + MLA_ROPE)[None], "mla_w_ukv": dukv[None],
        "mla_w_o": _unpad_heads_rows(dwo, HEAD_DIM)[None]}


_MIXERS = ((_pool_layer_fwd, _pool_layer_bwd), (_sb_layer_fwd, _sb_layer_bwd), (_mla_layer_fwd, _mla_layer_bwd), (_fox_layer_fwd, _fox_layer_bwd))


def _step_local(x, target, W):
    seq = x.shape[0]
    P = _kernel_weights(W)
    h = jnp.concatenate([jnp.zeros((PAD0, D_MODEL), F32), W["meta"], x], axis=0)
    tpad = jnp.pad(target, ((PAD0 + N_META, 0), (0, 0)))
    saved = []
    for i in range(4):
        h, s_mix = _MIXERS[i][0](h, P)
        h, s_ffn = _ffn_fwd(h, i, P)
        saved.append((s_mix, s_ffn))
    loss, dh, dfinal = _loss_head(h, W["final_norm"], tpad, "loss_head")
    grads = {"final_norm": dfinal.reshape(-1)}
    gains_mix, gains_ffn, dwg, dwu, dwd = [None] * 4, [None] * 4, [None] * 4, [None] * 4, [None] * 4
    for i in reversed(range(4)):
        s_mix, s_ffn = saved[i]
        dh, gains_ffn[i], dwg[i], dwu[i], dwd[i] = _ffn_bwd(dh, i, P, s_ffn)
        dh, g = _MIXERS[i][1](dh, P, s_mix)
        gains_mix[i] = g.pop("norm_mix%d" % i)
        grads.update(g)
    grads["norm_mix"] = jnp.concatenate(gains_mix, axis=0)
    grads["norm_ffn"] = jnp.concatenate(gains_ffn, axis=0)
    grads["ffn_w_gate"] = jnp.stack(dwg)
    grads["ffn_w_up"] = jnp.stack(dwu)
    grads["ffn_w_down"] = jnp.stack(dwd)
    grads["meta"] = dh[PAD0:PAD0 + N_META]
    return loss, dh[PAD0 + N_META:], grads


_WEIGHTS = ("meta", "norm_mix", "norm_ffn", "pool_w", "pool_scale", "sb_w_qkv", "sb_w_o", "mla_w_down", "mla_q_norm",
            "mla_kv_norm", "mla_w_uq", "mla_w_ukv", "mla_w_o", "fox_w_qkvf", "fox_b_f", "fox_w_o", "ffn_w_gate",
            "ffn_w_up", "ffn_w_down", "final_norm")
_SHARD_AXIS = {"meta": 1, "pool_w": 2, "sb_w_qkv": 2, "sb_w_o": 1, "mla_w_down": 1, "mla_q_norm": 1, "mla_kv_norm": 1,
               "mla_w_uq": 2, "mla_w_ukv": 2, "mla_w_o": 1, "fox_w_qkvf": 2, "fox_b_f": None, "fox_w_o": 1,
               "ffn_w_gate": 2, "ffn_w_up": 2, "ffn_w_down": 1}
_SHARDED = tuple(n for n in _WEIGHTS if _SHARD_AXIS.get(n) is not None)
_REPLICATED = tuple(n for n in _WEIGHTS if _SHARD_AXIS.get(n) is None)
_EXACT = ("meta", "mla_q_norm", "mla_kv_norm")
N_CHIPS = 4
GRAD_ROW_TILE = 512


PACK_ROWS = 16


def _piece_rows(t):
    return -(-t.size // (LANES * PACK_ROWS)) * PACK_ROWS


def _flat_rows(parts, dtype, row_multiple):
    pieces = []
    for p in parts:
        flat = p.astype(dtype).reshape(-1)
        pieces.append(jnp.pad(flat, (0, _piece_rows(p) * LANES - flat.shape[0])).reshape(-1, LANES))
    rows = sum(q.shape[0] for q in pieces)
    pad = -(-rows // row_multiple) * row_multiple - rows
    if pad:
        pieces.append(jnp.zeros((pad, LANES), dtype))
    return jnp.concatenate(pieces, axis=0)


def _split_flat(flat, like):
    out, off = [], 0
    for t in like:
        out.append(flat[off:off + _piece_rows(t)].reshape(-1)[:t.size].reshape(t.shape))
        off += _piece_rows(t)
    return out


def _gather_shards(local, names, dtype, name):
    blocks = [local[n] for n in names]
    recv = _exchange(_flat_rows(blocks, dtype, PACK_ROWS)[None], ("x", "y"), True, name)
    per_chip = [_split_flat(recv[s], blocks) for s in range(N_CHIPS)]
    return {n: jnp.concatenate([per_chip[s][k] for s in range(N_CHIPS)], axis=_SHARD_AXIS[n]) for k, n in enumerate(names)}


def _shard_of(g, n, s):
    w = g.shape[_SHARD_AXIS[n]] // N_CHIPS
    return lax.slice_in_dim(g, s * w, (s + 1) * w, axis=_SHARD_AXIS[n])


def _train_step(a):
    local = {n: a[n] for n in _WEIGHTS}
    full = {n: local[n] for n in _REPLICATED}
    full.update(_gather_shards(local, [n for n in _SHARDED if n not in _EXACT], BF16, "gather_weights"))
    full.update(_gather_shards(local, list(_EXACT), F32, "gather_exact"))

    loss, grad_x, grads = _step_local(a["x"][0], a["loss_target"][0], full)

    send = jnp.stack([
        _flat_rows([_shard_of(grads[n], n, s) for n in _SHARDED], BF16, 2 * GRAD_ROW_TILE).reshape(2, -1, LANES)
        for s in range(N_CHIPS)]).reshape(2 * N_CHIPS, -1, LANES)
    mine = _sum_chunks(_exchange(send, MESH_AXES, False, "scatter_grads"), "sum_grads", out_dtype=BF16)
    both = _exchange(mine[None], ("c",), True, "pair_grads").reshape(-1, LANES).astype(F32)
    reduced = dict(zip(_SHARDED, _split_flat(both, [local[n] for n in _SHARDED])))
    small = _flat_rows([grads[n] for n in _REPLICATED], F32, 8)
    small = _sum_chunks(_exchange(small[None], MESH_AXES, True, "gather_small_grads"), "sum_small_grads")
    reduced.update(zip(_REPLICATED, _split_flat(small, [local[n] for n in _REPLICATED])))

    deltas, new_m, new_v = {}, {}, {}
    for n in _WEIGHTS:
        deltas[n], new_m[n], new_v[n] = _adamw(local[n], reduced[n], a["m_" + n], a["v_" + n], "adamw")
    total = lax.psum(loss[0, 0], MESH_AXES)
    return (total, grad_x[None], *[reduced[n] for n in _WEIGHTS], *[deltas[n] for n in _WEIGHTS],
            *[new_m[n] for n in _WEIGHTS], *[new_v[n] for n in _WEIGHTS])


def kernel(x, meta, norm_mix, norm_ffn, pool_w, pool_scale, sb_w_qkv, sb_w_o, mla_w_down, mla_q_norm, mla_kv_norm, mla_w_uq, mla_w_ukv, mla_w_o, fox_w_qkvf, fox_b_f, fox_w_o, ffn_w_gate, ffn_w_up, ffn_w_down, final_norm, loss_target, m_meta, m_norm_mix, m_norm_ffn, m_pool_w, m_pool_scale, m_sb_w_qkv, m_sb_w_o, m_mla_w_down, m_mla_q_norm, m_mla_kv_norm, m_mla_w_uq, m_mla_w_ukv, m_mla_w_o, m_fox_w_qkvf, m_fox_b_f, m_fox_w_o, m_ffn_w_gate, m_ffn_w_up, m_ffn_w_down, m_final_norm, v_meta, v_norm_mix, v_norm_ffn, v_pool_w, v_pool_scale, v_sb_w_qkv, v_sb_w_o, v_mla_w_down, v_mla_q_norm, v_mla_kv_norm, v_mla_w_uq, v_mla_w_ukv, v_mla_w_o, v_fox_w_qkvf, v_fox_b_f, v_fox_w_o, v_ffn_w_gate, v_ffn_w_up, v_ffn_w_down, v_final_norm):
    return _train_step(dict(locals()))
```
